```python
import jax, jax.numpy as jnp
from jax import lax
import numpy as np

D_MODEL = 1024
BATCH = 8
SEQ = 4096
DEPTH = 2
DEC_BATCH = 32
DEC_SEQ = 32
PAST_LEN = 2048

CHUNK = 64
N_MIXERS = 2
N_POOL_LAYERS = (DEPTH + 1) // 2
N_SB_LAYERS = DEPTH // 2
POOL_WINDOWS = (2, 4, 8, 16)
N_POOL_GROUPS = len(POOL_WINDOWS)
POOL_GROUP = D_MODEL // N_POOL_GROUPS
POOL_HIST = max(POOL_WINDOWS) - 1
N_HEADS = 16
HEAD_DIM = D_MODEL // N_HEADS
Q_BLOCK = 128
N_EXPERTS = 32
TOP_K = 4
D_FF = D_MODEL
SWIGLU_LIMIT = 7.0
SWIGLU_ALPHA = 1.702
DISPATCH_BLOCK = 128
EPS = 1e-5

kernel_name = "hybrid_pool_stickbreak_moe_stream_step"


def rms_norm(x, g):
    x32 = x.astype(jnp.float32)
    y = x32 * lax.rsqrt(jnp.mean(x32 * x32, axis=-1, keepdims=True) + EPS) * g.astype(jnp.float32)
    return y.astype(x.dtype)


def multi_scale_pool(h, pos, w, scale):
    bsz, length, _ = h.shape
    hg = h.astype(jnp.float32).reshape(bsz, length, N_POOL_GROUPS, POOL_GROUP)
    cs = jnp.cumsum(hg, axis=1)
    diffs = []
    for g, win in enumerate(POOL_WINDOWS):
        z = jnp.pad(cs[:, :, g], ((0, 0), (win, 0), (0, 0)))
        wsum = z[:, win:] - z[:, :length]
        cnt = jnp.minimum(pos + 1, win).astype(jnp.float32)
        diffs.append(wsum / cnt[None, :, None] - hg[:, :, g])
    d = jnp.stack(diffs, axis=2)
    y = jnp.einsum('blgc,gcd->blgd', d, w.astype(jnp.float32)).reshape(bsz, length, D_MODEL)
    return (y * scale.astype(jnp.float32)).astype(h.dtype)


def split_heads(qkv):
    q, k, v = jnp.split(qkv, 3, axis=-1)
    shp = qkv.shape[:-1] + (N_HEADS, HEAD_DIM)
    return q.reshape(shp), k.reshape(shp), v.reshape(shp)


def stick_breaking(q, k, v, q_pos, k_pos):
    z = jnp.einsum('bqhd,bkhd->bhqk', q.astype(jnp.float32), k.astype(jnp.float32)) * (HEAD_DIM ** -0.5)
    mask = k_pos[None, :] < q_pos[:, None]
    log_keep = jnp.where(mask, jax.nn.log_sigmoid(-z), 0.0)
    after = lax.cumsum(log_keep, axis=3, reverse=True) - log_keep
    a = jnp.where(mask, jnp.exp(jax.nn.log_sigmoid(z) + after), 0.0)
    return jnp.einsum('bhqk,bkhd->bqhd', a, v.astype(jnp.float32)).astype(q.dtype)


def stick_breaking_prompt(q, k, v):
    bsz, s, h, dh = q.shape
    nb = s // Q_BLOCK
    qb = q.reshape(bsz, nb, Q_BLOCK, h, dh).transpose(1, 0, 2, 3, 4)
    k_pos = jnp.arange(s)

    def block(args):
        qi, bi = args
        return stick_breaking(qi, k, v, bi * Q_BLOCK + jnp.arange(Q_BLOCK), k_pos)

    o = lax.map(block, (qb, jnp.arange(nb)))
    return o.transpose(1, 0, 2, 3, 4).reshape(bsz, s, h, dh)


def moe_ffn(h, w_r, b_r, w1, b1, w2, b2):
    n, d = h.shape
    logits = h.astype(jnp.float32) @ w_r.astype(jnp.float32) + b_r.astype(jnp.float32)
    top_val, top_idx = lax.top_k(logits, TOP_K)
    gates = jax.nn.softmax(top_val, axis=-1)
    e_flat = top_idx.reshape(-1)
    order = jnp.argsort(e_flat, stable=True)
    e_sorted = e_flat[order]
    tok_sorted = order // TOP_K
    gate_sorted = gates.reshape(-1)[order]
    counts = jnp.zeros((N_EXPERTS,), jnp.int32).at[e_flat].add(1)
    padded = (counts + DISPATCH_BLOCK - 1) // DISPATCH_BLOCK * DISPATCH_BLOCK
    pad_end = jnp.cumsum(padded)
    pad_start = pad_end - padded
    start = jnp.cumsum(counts) - counts
    dest = pad_start[e_sorted] + (jnp.arange(n * TOP_K) - start[e_sorted])
    n_blocks = -(-(n * TOP_K) // DISPATCH_BLOCK) + N_EXPERTS
    buf = jnp.zeros((n_blocks * DISPATCH_BLOCK, d), h.dtype).at[dest].set(h[tok_sorted])
    block_expert = jnp.minimum(
        jnp.searchsorted(pad_end, jnp.arange(n_blocks) * DISPATCH_BLOCK, side='right'), N_EXPERTS - 1)

    def expert_block(args):
        xb, e = args
        gu = xb @ w1[e] + b1[e]
        gate = jnp.minimum(gu[:, :D_FF], SWIGLU_LIMIT)
        up = jnp.clip(gu[:, D_FF:], -SWIGLU_LIMIT, SWIGLU_LIMIT)
        hid = (up + 1.0) * (gate * jax.nn.sigmoid(gate * SWIGLU_ALPHA))
        return hid @ w2[e] + b2[e]

    ybuf = lax.map(expert_block, (buf.reshape(n_blocks, DISPATCH_BLOCK, d), block_expert)).reshape(-1, d)
    y = ybuf[dest].astype(jnp.float32) * gate_sorted[:, None]
    out = jnp.zeros((n, d), jnp.float32).at[tok_sorted].add(y)
    return out.astype(h.dtype)


def setup_inputs(seed: int = 0) -> dict:
    key = jax.random.key(seed)
    ks = jax.random.split(key, 18)

    def nrm(k, shape, scale=1.0):
        return scale * jax.random.normal(k, shape, jnp.float32)

    return {
        "x_prompt": nrm(ks[0], (BATCH, SEQ, D_MODEL)),
        "x_sample": nrm(ks[1], (DEC_BATCH, DEC_SEQ, D_MODEL)),
        "state_pool": nrm(ks[2], (N_POOL_LAYERS, DEC_BATCH, POOL_HIST, D_MODEL)),
        "cache_k": nrm(ks[3], (N_SB_LAYERS, DEC_BATCH, PAST_LEN, N_HEADS, HEAD_DIM)),
        "cache_v": nrm(ks[4], (N_SB_LAYERS, DEC_BATCH, PAST_LEN, N_HEADS, HEAD_DIM)),
        "norm_mix": 1.0 + nrm(ks[5], (DEPTH, D_MODEL), 0.02),
        "norm_ffn": 1.0 + nrm(ks[6], (DEPTH, D_MODEL), 0.02),
        "pool_w": nrm(ks[7], (N_POOL_LAYERS, N_POOL_GROUPS, POOL_GROUP, POOL_GROUP), POOL_GROUP ** -0.5),
        "pool_scale": 1.0 + nrm(ks[8], (N_POOL_LAYERS, D_MODEL), 0.1),
        "w_qkv": nrm(ks[9], (N_SB_LAYERS, D_MODEL, 3 * D_MODEL), D_MODEL ** -0.5),
        "w_o": nrm(ks[10], (N_SB_LAYERS, D_MODEL, D_MODEL), D_MODEL ** -0.5),
        "router_w": nrm(ks[11], (DEPTH, D_MODEL, N_EXPERTS), D_MODEL ** -0.5),
        "router_b": nrm(ks[12], (DEPTH, N_EXPERTS), 0.01),
        "moe_w1": nrm(ks[13], (DEPTH, N_EXPERTS, D_MODEL, 2 * D_FF), D_MODEL ** -0.5),
        "moe_b1": nrm(ks[14], (DEPTH, N_EXPERTS, 2 * D_FF), 0.01),
        "moe_w2": nrm(ks[15], (DEPTH, N_EXPERTS, D_FF, D_MODEL), D_FF ** -0.5),
        "moe_b2": nrm(ks[16], (DEPTH, N_EXPERTS, D_MODEL), 0.01),
        "final_norm": 1.0 + nrm(ks[17], (D_MODEL,), 0.02),
    }


def reference(x_prompt, x_sample, state_pool, cache_k, cache_v, norm_mix, norm_ffn, pool_w, pool_scale,
              w_qkv, w_o, router_w, router_b, moe_w1, moe_b1, moe_w2, moe_b2, final_norm):
    xp, xs = x_prompt, x_sample
    b, s, d = xp.shape
    db, t, _ = xs.shape
    past = cache_k.shape[2]
    pool_p, pool_s, k_p, v_p, k_s, v_s = [], [], [], [], [], []
    for i in range(DEPTH):
        j = i // N_MIXERS
        hp = rms_norm(xp, norm_mix[i])
        hs = rms_norm(xs, norm_mix[i])
        if i % N_MIXERS == 0:
            mp = multi_scale_pool(hp, jnp.arange(s), pool_w[j], pool_scale[j])
            ext = jnp.concatenate([state_pool[j].astype(hs.dtype), hs], axis=1)
            ext_pos = past - POOL_HIST + jnp.arange(POOL_HIST + t)
            ms = multi_scale_pool(ext, ext_pos, pool_w[j], pool_scale[j])[:, POOL_HIST:]
            pool_p.append(hp[:, s - POOL_HIST:])
            pool_s.append(ext[:, t:])
        else:
            qp, kp, vp = split_heads(hp @ w_qkv[j])
            qs, ks, vs = split_heads(hs @ w_qkv[j])
            o_p = stick_breaking_prompt(qp, kp, vp)
            k_all = jnp.concatenate([cache_k[j].astype(ks.dtype), ks], axis=1)
            v_all = jnp.concatenate([cache_v[j].astype(vs.dtype), vs], axis=1)
            o_s = stick_breaking(qs, k_all, v_all, past + jnp.arange(t), jnp.arange(past + t))
            mp = o_p.reshape(b, s, d) @ w_o[j]
            ms = o_s.reshape(db, t, d) @ w_o[j]
            k_p.append(kp)
            v_p.append(vp)
            k_s.append(ks)
            v_s.append(vs)
        xp = xp + mp
        xs = xs + ms
        tokens = jnp.concatenate([rms_norm(xp, norm_ffn[i]).reshape(-1, d),
                                  rms_norm(xs, norm_ffn[i]).reshape(-1, d)], axis=0)
        f = moe_ffn(tokens, router_w[i], router_b[i], moe_w1[i], moe_b1[i], moe_w2[i], moe_b2[i])
        xp = xp + f[:b * s].reshape(b, s, d)
        xs = xs + f[b * s:].reshape(db, t, d)
    y_prompt = rms_norm(xp, final_norm)
    y_sample = rms_norm(xs, final_norm)
    return (y_prompt, y_sample, jnp.stack(pool_p), jnp.stack(k_p), jnp.stack(v_p),
            jnp.stack(pool_s), jnp.stack(k_s), jnp.stack(v_s))
```

```python
import functools

import jax
import jax.numpy as jnp
from jax import lax
from jax.experimental import pallas as pl
from jax.experimental.pallas import tpu as pltpu

EPS = 1e-5
POOL_WINDOWS = (2, 4, 8, 16)
POOL_HIST_ROWS = 16
N_HEADS = 16
TOP_K = 4
SWIGLU_LIMIT = 7.0
SWIGLU_ALPHA = 1.702
LANES = 128
ROW_TILE = 512
COMBINE_TILE = 256
ATTN_TQ = 128
ATTN_TK = 128
ATTN_EXIT = 104.0
VMEM_LIMIT = 56 * 1024 * 1024

F32 = jnp.float32
BF16 = jnp.bfloat16


def _rms(x, g):
    ms = jnp.mean(x * x, axis=-1, keepdims=True)
    return x * lax.rsqrt(ms + EPS) * g


def _dot(a, b):
    return jnp.dot(a, b, preferred_element_type=F32)


def _dot_nt(a, b, precision=None):
    return lax.dot_general(a, b, (((1,), (1,)), ((), ())), preferred_element_type=F32, precision=precision)


def _route_tail(xn, gffn_ref, wr_ref, br_ref, cnt_ref, tok_ref, idx_ref, gate_ref, rank_ref):
    r = xn.shape[0]
    n_exp = wr_ref.shape[0]
    tok = _rms(xn, gffn_ref[...])
    tok_ref[...] = tok.astype(tok_ref.dtype)
    logits = _dot_nt(wr_ref[...], tok, precision=lax.Precision.HIGHEST) + br_ref[...]
    eidx = lax.broadcasted_iota(jnp.int32, logits.shape, 0).astype(F32)
    vals, idxs = [], []
    l = logits
    for _ in range(TOP_K):
        m = jnp.max(l, axis=0, keepdims=True)
        i = jnp.min(jnp.where(l == m, eidx, float(n_exp)), axis=0, keepdims=True)
        vals.append(m)
        idxs.append(i)
        l = jnp.where(eidx == i, -jnp.inf, l)
    es = [jnp.exp(v - vals[0]) for v in vals]
    den = es[0]
    for e in es[1:]:
        den = den + e
    gate_ref[...] = jnp.concatenate([e / den for e in es], axis=0)
    idx_ref[...] = jnp.concatenate(idxs, axis=0).astype(jnp.int32)
    member = jnp.zeros(logits.shape, F32)
    for i in idxs:
        member = member + jnp.where(eidx == i, 1.0, 0.0)
    tri = jnp.where(lax.broadcasted_iota(jnp.int32, (r, r), 0) < lax.broadcasted_iota(jnp.int32, (r, r), 1),
                    1.0, 0.0).astype(BF16)
    before = _dot(member.astype(BF16), tri) + cnt_ref[:, :1]
    ranks = [jnp.sum(jnp.where(eidx == i, before, 0.0), axis=0, keepdims=True) for i in idxs]
    rank_ref[...] = jnp.concatenate(ranks, axis=0).astype(jnp.int32)
    cnt_ref[...] = cnt_ref[...] + jnp.sum(member, axis=1, keepdims=True)


def _pool_kernel(pos0, nt, x_ref, hist_ref, cnt_in_ref, gmix_ref, pw_ref, ps_ref, gffn_ref, wr_ref, br_ref,
                 xn_in, tok_in, idx_in, gate_in, rank_in,
                 xn_ref, tok_ref, idx_ref, gate_ref, rank_ref, hlast_ref, cnt_out_ref, ext_ref, cnt_ref):
    del xn_in, tok_in, idx_in, gate_in, rank_in
    b = pl.program_id(0)
    t = pl.program_id(1)
    bb, ts, d = x_ref.shape
    hist = POOL_HIST_ROWS
    group = d // len(POOL_WINDOWS)

    @pl.when((b == 0) & (t == 0))
    def _():
        cnt_ref[...] = cnt_in_ref[...]

    @pl.when(t == 0)
    def _():
        ext_ref[:, 0:hist, :] = hist_ref[...]

    x = x_ref[...]
    h = _rms(x, gmix_ref[...])
    ext_ref[:, hist:hist + ts, :] = h

    @pl.when(t == nt - 1)
    def _():
        hlast_ref[...] = h[:, ts - hist:, :]

    pos = pos0 + t * ts + lax.broadcasted_iota(jnp.int32, (1, ts, 1), 1)
    ys = []
    for g, win in enumerate(POOL_WINDOWS):
        cols = slice(g * group, (g + 1) * group)
        hg = h[:, :, cols]
        acc = hg
        for j in range(1, win):
            acc = acc + ext_ref[:, hist - j:hist - j + ts, cols]
        cnt = jnp.minimum(pos + 1, win).astype(F32)
        dg = acc / cnt - hg
        ys.append(_dot(dg.reshape(bb * ts, group).astype(BF16), pw_ref[g]))
    y = jnp.concatenate(ys, axis=-1) * ps_ref[...]
    xn = x.reshape(bb * ts, d) + y
    xn_ref[...] = xn
    ext_ref[:, 0:hist, :] = ext_ref[:, ts:ts + hist, :]
    _route_tail(xn, gffn_ref, wr_ref, br_ref, cnt_ref, tok_ref, idx_ref, gate_ref, rank_ref)
    cnt_out_ref[...] = cnt_ref[...]


def _pool_layer(x, hist, pos0, bb, ts, row_off, n_rows, cnt_in, prev, gmix, pw, ps, gffn, wr, br):
    b, l, d = x.shape
    n_exp = wr.shape[0]
    nb, nt = b // bb, l // ts
    r = bb * ts
    off = row_off // r
    out_shape = [jax.ShapeDtypeStruct((n_rows, d), F32), jax.ShapeDtypeStruct((n_rows, d), F32),
                 jax.ShapeDtypeStruct((TOP_K, n_rows), jnp.int32), jax.ShapeDtypeStruct((TOP_K, n_rows), F32),
                 jax.ShapeDtypeStruct((TOP_K, n_rows), jnp.int32),
                 jax.ShapeDtypeStruct((b, POOL_HIST_ROWS, d), F32), jax.ShapeDtypeStruct((n_exp, LANES), F32)]
    if prev is None:
        prev = [jnp.zeros((8, LANES), s.dtype) for s in out_shape[:5]]
        aliases = {}
    else:
        aliases = {9 + i: i for i in range(5)}
    full = lambda shape: pl.BlockSpec(shape, lambda i, j: (0,) * len(shape))
    rows = lambda i, j: (off + i * nt + j, 0)
    lanes = lambda i, j: (0, off + i * nt + j)
    any_spec = pl.BlockSpec(memory_space=pl.ANY)
    return pl.pallas_call(
        functools.partial(_pool_kernel, pos0, nt),
        grid=(nb, nt),
        in_specs=[pl.BlockSpec((bb, ts, d), lambda i, j: (i, j, 0)),
                  pl.BlockSpec((bb, POOL_HIST_ROWS, d), lambda i, j: (i, 0, 0)),
                  full((n_exp, LANES)), full((1, d)), full(pw.shape), full((1, d)), full((1, d)),
                  full((n_exp, d)), full((n_exp, 1))] + [any_spec] * 5,
        out_specs=[pl.BlockSpec((r, d), rows), pl.BlockSpec((r, d), rows),
                   pl.BlockSpec((TOP_K, r), lanes), pl.BlockSpec((TOP_K, r), lanes), pl.BlockSpec((TOP_K, r), lanes),
                   pl.BlockSpec((bb, POOL_HIST_ROWS, d), lambda i, j: (i, 0, 0)), full((n_exp, LANES))],
        out_shape=out_shape,
        scratch_shapes=[pltpu.VMEM((bb, POOL_HIST_ROWS + ts, d), F32), pltpu.VMEM((n_exp, LANES), F32)],
        input_output_aliases=aliases,
        compiler_params=pltpu.CompilerParams(dimension_semantics=("arbitrary", "arbitrary"),
                                             vmem_limit_bytes=VMEM_LIMIT),
    )(x, hist, cnt_in, gmix, pw, ps, gffn, wr, br, *prev)


def _proj_kernel(o_ref, x_ref, wo_ref, cnt_in_ref, gffn_ref, wr_ref, br_ref,
                 xn_ref, tok_ref, idx_ref, gate_ref, rank_ref, cnt_out_ref, cnt_ref):
    @pl.when(pl.program_id(0) == 0)
    def _():
        cnt_ref[...] = cnt_in_ref[...]

    xn = x_ref[...] + _dot(o_ref[...], wo_ref[...])
    xn_ref[...] = xn
    _route_tail(xn, gffn_ref, wr_ref, br_ref, cnt_ref, tok_ref, idx_ref, gate_ref, rank_ref)
    cnt_out_ref[...] = cnt_ref[...]


def _proj_layer(o, x, wo, cnt_in, gffn, wr, br):
    n, d = x.shape
    n_exp = wr.shape[0]
    r = ROW_TILE
    full = lambda shape: pl.BlockSpec(shape, lambda i: (0,) * len(shape))
    rows = pl.BlockSpec((r, d), lambda i: (i, 0))
    lanes = pl.BlockSpec((TOP_K, r), lambda i: (0, i))
    return pl.pallas_call(
        _proj_kernel,
        grid=(n // r,),
        in_specs=[rows, rows, full((d, d)), full((n_exp, LANES)), full((1, d)), full((n_exp, d)), full((n_exp, 1))],
        out_specs=[rows, rows, lanes, lanes, lanes, full((n_exp, LANES))],
        out_shape=[jax.ShapeDtypeStruct((n, d), F32), jax.ShapeDtypeStruct((n, d), F32),
                   jax.ShapeDtypeStruct((TOP_K, n), jnp.int32), jax.ShapeDtypeStruct((TOP_K, n), F32),
                   jax.ShapeDtypeStruct((TOP_K, n), jnp.int32), jax.ShapeDtypeStruct((n_exp, LANES), F32)],
        scratch_shapes=[pltpu.VMEM((n_exp, LANES), F32)],
        compiler_params=pltpu.CompilerParams(dimension_semantics=("arbitrary",), vmem_limit_bytes=VMEM_LIMIT),
    )(o, x, wo, cnt_in, gffn, wr, br)


def _row_copy(src, dst, sem):
    return pltpu.make_async_copy(src, dst, sem)


def _dispatch_kernel(dest_hbm, tok_ref, buf_in, buf_ref, idx_smem, sem_idx, sem_rows):
    del buf_in
    i = pl.program_id(0)
    r = tok_ref.shape[0]
    cp = pltpu.make_async_copy(dest_hbm.at[i], idx_smem, sem_idx)
    cp.start()
    cp.wait()

    def body(row, carry):
        for k in range(TOP_K):
            dst = idx_smem[k * r + row]
            _row_copy(tok_ref.at[pl.ds(row, 1), :], buf_ref.at[pl.ds(dst, 1), :], sem_rows).start()
        return carry

    lax.fori_loop(0, r, body, 0, unroll=8)
    for k in range(TOP_K):
        _row_copy(tok_ref, buf_ref.at[pl.ds(0, r), :], sem_rows).wait()


def _dispatch(tok, dest_tiles, n_buf_rows):
    n, d = tok.shape
    r = ROW_TILE
    buf0 = jnp.zeros((n_buf_rows, d), tok.dtype)
    any_spec = pl.BlockSpec(memory_space=pl.ANY)
    return pl.pallas_call(
        _dispatch_kernel,
        grid=(n // r,),
        in_specs=[any_spec, pl.BlockSpec((r, d), lambda i: (i, 0)), any_spec],
        out_specs=any_spec,
        out_shape=jax.ShapeDtypeStruct((n_buf_rows, d), tok.dtype),
        scratch_shapes=[pltpu.SMEM((TOP_K * r,), jnp.int32), pltpu.SemaphoreType.DMA, pltpu.SemaphoreType.DMA],
        input_output_aliases={2: 0},
        compiler_params=pltpu.CompilerParams(dimension_semantics=("arbitrary",), has_side_effects=True),
    )(dest_tiles, tok, buf0)


def _expert_kernel(be_ref, nused_ref, x_ref, w1_ref, b1_ref, w2_ref, b2_ref, o_ref):
    @pl.when(pl.program_id(0) < nused_ref[0])
    def _():
        f = w2_ref.shape[1]
        gu = _dot(x_ref[...].astype(BF16), w1_ref[0]) + b1_ref[0]
        gate = jnp.minimum(gu[:, :f], SWIGLU_LIMIT)
        up = jnp.clip(gu[:, f:], -SWIGLU_LIMIT, SWIGLU_LIMIT)
        hid = (up + 1.0) * (gate * jax.nn.sigmoid(gate * SWIGLU_ALPHA))
        o_ref[...] = _dot(hid.astype(BF16), w2_ref[0]) + b2_ref[0]


def _experts(buf, block_expert, nused, w1, b1, w2, b2, tm):
    rows, d = buf.shape
    n_exp, _, f2 = w1.shape
    f = w2.shape[1]
    row_block = lambda i, be, nu: (jnp.minimum(i, nu[0] - 1), 0)
    by_expert = lambda i, be, nu: (be[i], 0, 0)
    return pl.pallas_call(
        _expert_kernel,
        grid_spec=pltpu.PrefetchScalarGridSpec(
            num_scalar_prefetch=2,
            grid=(rows // tm,),
            in_specs=[pl.BlockSpec((tm, d), row_block),
                      pl.BlockSpec((1, d, f2), by_expert), pl.BlockSpec((1, 1, f2), by_expert),
                      pl.BlockSpec((1, f, d), by_expert), pl.BlockSpec((1, 1, d), by_expert)],
            out_specs=pl.BlockSpec((tm, d), row_block)),
        out_shape=jax.ShapeDtypeStruct((rows, d), F32),
        compiler_params=pltpu.CompilerParams(dimension_semantics=("arbitrary",), vmem_limit_bytes=VMEM_LIMIT),
    )(block_expert, nused, buf, w1, b1.reshape(n_exp, 1, f2), w2, b2.reshape(n_exp, 1, d))


def _combine_kernel(dest_hbm, y_hbm, gate_ref, x_ref, g_ref, xo_ref, hn_ref, idx_smem, rows_ref, sem_idx, sem_rows):
    i = pl.program_id(0)
    r = x_ref.shape[0]
    cp = pltpu.make_async_copy(dest_hbm.at[i], idx_smem, sem_idx)
    cp.start()
    cp.wait()

    def body(row, carry):
        for k in range(TOP_K):
            src = idx_smem[k * r + row]
            _row_copy(y_hbm.at[pl.ds(src, 1), :], rows_ref.at[k, pl.ds(row, 1), :], sem_rows).start()
        return carry

    lax.fori_loop(0, r, body, 0, unroll=8)
    gates = jnp.concatenate([gate_ref[...], jnp.zeros((LANES - TOP_K, r), F32)], axis=0)
    gates_t = jnp.transpose(gates)
    for k in range(TOP_K):
        _row_copy(y_hbm.at[pl.ds(0, r), :], rows_ref.at[k], sem_rows).wait()
    f = rows_ref[0] * gates_t[:, 0:1]
    for k in range(1, TOP_K):
        f = f + rows_ref[k] * gates_t[:, k:k + 1]
    xo = x_ref[...] + f
    xo_ref[...] = xo
    hn_ref[...] = _rms(xo, g_ref[...]).astype(hn_ref.dtype)


def _combine(ybuf, dest_tiles, gates, x, g, hn_dtype):
    n, d = x.shape
    r = COMBINE_TILE
    any_spec = pl.BlockSpec(memory_space=pl.ANY)
    rows = pl.BlockSpec((r, d), lambda i: (i, 0))
    return pl.pallas_call(
        _combine_kernel,
        grid=(n // r,),
        in_specs=[any_spec, any_spec, pl.BlockSpec((TOP_K, r), lambda i: (0, i)), rows,
                  pl.BlockSpec((1, d), lambda i: (0, 0))],
        out_specs=[rows, rows],
        out_shape=[jax.ShapeDtypeStruct((n, d), F32), jax.ShapeDtypeStruct((n, d), hn_dtype)],
        scratch_shapes=[pltpu.SMEM((TOP_K * r,), jnp.int32), pltpu.VMEM((TOP_K, r, d), F32),
                        pltpu.SemaphoreType.DMA, pltpu.SemaphoreType.DMA],
        compiler_params=pltpu.CompilerParams(dimension_semantics=("arbitrary",), vmem_limit_bytes=VMEM_LIMIT),
    )(dest_tiles, ybuf, gates, x, g)


def _moe(tok, idx, gates, rank, cnt, x, g_next, hn_dtype, w1, b1, w2, b2):
    n, d = tok.shape
    n_exp = w1.shape[0]
    pairs = n * TOP_K
    tm = 128
    for cand in (512, 256):
        if pairs >= 4 * cand * n_exp and pairs % cand == 0:
            tm = cand
            break
    n_blocks = -(-pairs // tm) + n_exp
    counts = cnt[:, 0].astype(jnp.int32)
    padded = (counts + tm - 1) // tm * tm
    pad_end = jnp.cumsum(padded)
    pad_start = pad_end - padded
    dest = jnp.take(pad_start, idx, axis=0) + rank
    nused = (pad_end[-1:] // tm).astype(jnp.int32)
    block_expert = jnp.minimum(jnp.searchsorted(pad_end, jnp.arange(n_blocks, dtype=jnp.int32) * tm, side='right'),
                               n_exp - 1).astype(jnp.int32)

    def tiles(r):
        return dest.reshape(TOP_K, n // r, r).transpose(1, 0, 2).reshape(n // r, TOP_K * r)

    buf = _dispatch(tok, tiles(ROW_TILE), n_blocks * tm)
    ybuf = _experts(buf, block_expert, nused, w1, b1, w2, b2, tm)
    return _combine(ybuf, tiles(COMBINE_TILE), gates, x, g_next, hn_dtype)


def _qkv_kernel(h_ref, w_ref, q_ref, kb_ref, vb_ref, kf_ref, vf_ref):
    d = h_ref.shape[1]
    qkv = _dot(h_ref[...], w_ref[...])
    hd = d // N_HEADS
    q_ref[...] = (qkv[:, :d] * (hd ** -0.5)).astype(BF16)
    k = qkv[:, d:2 * d]
    v = qkv[:, 2 * d:]
    kf_ref[...] = k
    vf_ref[...] = v
    kb_ref[...] = k.astype(BF16)
    vb_ref[...] = v.astype(BF16)


def _qkv(h, w):
    n, d = h.shape
    r = ROW_TILE
    rows = pl.BlockSpec((r, d), lambda i: (i, 0))
    return pl.pallas_call(
        _qkv_kernel,
        grid=(n // r,),
        in_specs=[rows, pl.BlockSpec((d, 3 * d), lambda i: (0, 0))],
        out_specs=[rows] * 5,
        out_shape=[jax.ShapeDtypeStruct((n, d), BF16)] * 3 + [jax.ShapeDtypeStruct((n, d), F32)] * 2,
        compiler_params=pltpu.CompilerParams(dimension_semantics=("arbitrary",), vmem_limit_bytes=VMEM_LIMIT),
    )(h, w)


def _attn_kernel(q_pos0, q_ref, k_ref, v_ref, o_in, o_ref, qm_ref, carry_ref, acc_ref):
    del o_in
    qi = pl.program_id(1)
    tq, d = q_ref.shape
    tk = ATTN_TK
    n_pairs = d // LANES
    half = LANES // 2
    lane = lax.broadcasted_iota(jnp.int32, (tq, LANES), 1)
    for p in range(n_pairs):
        qp = q_ref[:, p * LANES:(p + 1) * LANES]
        qm_ref[2 * p] = jnp.where(lane < half, qp, jnp.zeros_like(qp))
        qm_ref[2 * p + 1] = jnp.where(lane >= half, qp, jnp.zeros_like(qp))
    carry_ref[...] = jnp.zeros(carry_ref.shape, F32)
    acc_ref[...] = jnp.zeros(acc_ref.shape, F32)
    row_pos = q_pos0 + qi * tq + lax.broadcasted_iota(jnp.int32, (tq, tk), 0)
    col = lax.broadcasted_iota(jnp.int32, (tq, tk), 1)
    src = lax.broadcasted_iota(jnp.int32, (tk, 2 * tk), 0)
    dst = lax.broadcasted_iota(jnp.int32, (tk, 2 * tk), 1)
    sums = jnp.where((dst >= tk) | (src > dst), 1.0, 0.0).astype(BF16)
    j0 = (q_pos0 + (qi + 1) * tq - 2) // tk

    def cond(state):
        j, top = state
        return (j >= 0) & (top > -ATTN_EXIT)

    def body(state):
        j, _ = state
        ks = pl.multiple_of(j * tk, tk)
        mask = (j * tk + col) < row_pos
        top = jnp.full((tq, tk), -jnp.inf, F32)
        for p in range(n_pairs):
            cols = slice(p * LANES, (p + 1) * LANES)
            kp = k_ref[pl.ds(ks, tk), cols]
            vp = v_ref[pl.ds(ks, tk), cols]
            outs = []
            for par in range(2):
                h = 2 * p + par
                z = _dot_nt(qm_ref[h], kp)
                sp = jnp.maximum(z, 0.0) + jnp.log(1.0 + jnp.exp(-jnp.abs(z)))
                log_keep = jnp.where(mask, -sp, 0.0)
                hi = log_keep.astype(BF16)
                lo = (log_keep - hi.astype(F32)).astype(BF16)
                cs = _dot(hi, sums) + _dot(lo, sums)
                carry = carry_ref[h]
                a = jnp.where(mask, jnp.exp(z - sp + cs[:, :tk] + carry), 0.0)
                outs.append(_dot(a.astype(BF16), vp))
                carry = carry + cs[:, tk:]
                carry_ref[h] = carry
                top = jnp.maximum(top, carry)
            acc_ref[:, cols] += jnp.where(lane < half, outs[0], outs[1])
        return j - 1, jnp.max(top)

    lax.while_loop(cond, body, (j0, jnp.float32(0.0)))
    o_ref[...] = acc_ref[...].astype(o_ref.dtype)


def _attention(q, k, v, q_pos0, n_streams, lq, lk, tq, q_row_off, kv_block_off, n_rows, prev):
    d = q.shape[1]
    nq = lq // tq
    off = q_row_off // tq
    rows = lambda b, i: (off + b * nq + i, 0)
    kv = pl.BlockSpec((lk, d), lambda b, i: (kv_block_off + b, 0))
    if prev is None:
        prev = jnp.zeros((8, LANES), BF16)
        aliases = {}
    else:
        aliases = {3: 0}
    return pl.pallas_call(
        functools.partial(_attn_kernel, q_pos0),
        grid=(n_streams, nq),
        in_specs=[pl.BlockSpec((tq, d), rows), kv, kv, pl.BlockSpec(memory_space=pl.ANY)],
        out_specs=pl.BlockSpec((tq, d), rows),
        out_shape=jax.ShapeDtypeStruct((n_rows, d), BF16),
        scratch_shapes=[pltpu.VMEM((N_HEADS, tq, LANES), BF16), pltpu.VMEM((N_HEADS, tq, ATTN_TK), F32),
                        pltpu.VMEM((tq, d), F32)],
        input_output_aliases=aliases,
        compiler_params=pltpu.CompilerParams(dimension_semantics=("arbitrary", "arbitrary"),
                                             vmem_limit_bytes=VMEM_LIMIT),
    )(q, k, v, prev)


def kernel(x_prompt, x_sample, state_pool, cache_k, cache_v, norm_mix, norm_ffn, pool_w, pool_scale, w_qkv, w_o, router_w, router_b, moe_w1, moe_b1, moe_w2, moe_b2, final_norm):
    b, s, d = x_prompt.shape
    db, t, _ = x_sample.shape
    past = cache_k.shape[2]
    n_exp = router_w.shape[2]
    hd = d // N_HEADS
    hist = POOL_HIST_ROWS
    n_p, n_s = b * s, db * t
    n = n_p + n_s
    assert t >= hist and ROW_TILE % t == 0 and s % ROW_TILE == 0 and n_s % ROW_TILE == 0
    row = lambda a: a.reshape(1, -1)
    wr = [router_w[i].T for i in range(2)]
    br = [router_b[i].reshape(n_exp, 1) for i in range(2)]
    w1 = [moe_w1[i].astype(BF16) for i in range(2)]
    w2 = [moe_w2[i].astype(BF16) for i in range(2)]

    ts_p = min(s, ROW_TILE)
    bb_s = max(1, min(db, ROW_TILE // t))
    cnt0 = jnp.zeros((n_exp, LANES), F32)
    pool_args = (row(norm_mix[0]), pool_w[0].astype(BF16), row(pool_scale[0]), row(norm_ffn[0]), wr[0], br[0])
    outs_p = _pool_layer(x_prompt, jnp.zeros((b, hist, d), F32), 0, 1, ts_p, 0, n, cnt0, None, *pool_args)
    hist_s = jnp.concatenate([jnp.zeros((db, 1, d), F32), state_pool[0]], axis=1)
    outs_s = _pool_layer(x_sample, hist_s, past, bb_s, t, n_p, n, outs_p[6], outs_p[:5], *pool_args)
    xn, tok, idx, gates, rank, hlast_s, cnt = outs_s
    new_pool_prompt = outs_p[5][None, :, 1:, :]
    new_pool_sample = hlast_s[None, :, 1:, :]
    x1, h1 = _moe(tok, idx, gates, rank, cnt, xn, row(norm_mix[1]), BF16, w1[0], moe_b1[0], w2[0], moe_b2[0])

    q, kb, vb, kf, vf = _qkv(h1, w_qkv[0].astype(BF16))
    o = _attention(q, kb, vb, 0, b, s, s, min(s, ATTN_TQ), 0, 0, n, None)
    lk_s = -(-(past + t) // ATTN_TK) * ATTN_TK

    def sample_keys(cache, new):
        allk = jnp.concatenate([cache[0].reshape(db, past, d).astype(BF16), new[n_p:].reshape(db, t, d)], axis=1)
        return jnp.pad(allk, ((0, 0), (0, lk_s - past - t), (0, 0))).reshape(db * lk_s, d)

    o = _attention(q, sample_keys(cache_k, kb), sample_keys(cache_v, vb), past, db, t, lk_s, t, n_p, 0, n, o)
    xn, tok, idx, gates, rank, cnt = _proj_layer(o, x1, w_o[0].astype(BF16), cnt0, row(norm_ffn[1]), wr[1], br[1])
    _, y = _moe(tok, idx, gates, rank, cnt, xn, row(final_norm), F32, w1[1], moe_b1[1], w2[1], moe_b2[1])

    heads = lambda a, nb, l: a.reshape(1, nb, l, N_HEADS, hd)
    return (y[:n_p].reshape(b, s, d), y[n_p:].reshape(db, t, d), new_pool_prompt,
            heads(kf[:n_p], b, s), heads(vf[:n_p], b, s), new_pool_sample,
            heads(kf[n_p:], db, t), heads(vf[n_p:], db, t))
```

```python
import functools

import jax
import jax.numpy as jnp
from jax import lax
from jax.experimental import pallas as pl
from jax.experimental.pallas import tpu as pltpu

EPS = 1e-5
POOL_WINDOWS = (2, 4, 8, 16)
POOL_HIST_ROWS = 16
N_HEADS = 16
TOP_K = 4
SWIGLU_LIMIT = 7.0
SWIGLU_ALPHA = 1.702
LANES = 128
ROW_TILE = 512
ATTN_TQ = 128
ATTN_TK = 128
ATTN_EXIT = 104.0
VMEM_LIMIT = 56 * 1024 * 1024

F32 = jnp.float32
BF16 = jnp.bfloat16


def _rms(x, g):
    ms = jnp.mean(x * x, axis=-1, keepdims=True)
    return x * lax.rsqrt(ms + EPS) * g


def _dot(a, b):
    return jnp.dot(a, b, preferred_element_type=F32)


def _dot_nt(a, b, precision=None):
    return lax.dot_general(a, b, (((1,), (1,)), ((), ())), preferred_element_type=F32, precision=precision)


def _params(semantics):
    return pltpu.CompilerParams(dimension_semantics=semantics, vmem_limit_bytes=VMEM_LIMIT)


def _full(shape):
    return pl.BlockSpec(shape, lambda i, *_: (0,) * len(shape))


def _route_tail(xn, gffn_ref, wr_ref, br_ref, cnt_ref, tok_ref, idx_ref, gate_ref, rank_ref):
    r = xn.shape[0]
    n_exp = wr_ref.shape[0]
    tok = _rms(xn, gffn_ref[...])
    tok_ref[...] = tok.astype(tok_ref.dtype)
    logits = _dot_nt(wr_ref[...], tok, precision=lax.Precision.HIGHEST) + br_ref[...]
    eidx = lax.broadcasted_iota(jnp.int32, logits.shape, 0).astype(F32)
    vals, idxs = [], []
    l = logits
    for _ in range(TOP_K):
        m = jnp.max(l, axis=0, keepdims=True)
        i = jnp.min(jnp.where(l == m, eidx, float(n_exp)), axis=0, keepdims=True)
        vals.append(m)
        idxs.append(i)
        l = jnp.where(eidx == i, -jnp.inf, l)
    es = [jnp.exp(v - vals[0]) for v in vals]
    den = es[0]
    for e in es[1:]:
        den = den + e
    gate_ref[...] = jnp.concatenate([e / den for e in es], axis=0)
    idx_ref[...] = jnp.concatenate(idxs, axis=0).astype(jnp.int32)
    member = jnp.zeros(logits.shape, F32)
    for i in idxs:
        member = member + jnp.where(eidx == i, 1.0, 0.0)
    tri = jnp.where(lax.broadcasted_iota(jnp.int32, (r, r), 0) < lax.broadcasted_iota(jnp.int32, (r, r), 1),
                    1.0, 0.0).astype(BF16)
    before = _dot(member.astype(BF16), tri) + cnt_ref[:, :1]
    ranks = [jnp.sum(jnp.where(eidx == i, before, 0.0), axis=0, keepdims=True) for i in idxs]
    rank_ref[...] = jnp.concatenate(ranks, axis=0).astype(jnp.int32)
    cnt_ref[...] = cnt_ref[...] + jnp.sum(member, axis=1, keepdims=True)


def _pool_mix(h, ext_ref, pos, pw_ref, ps_ref):
    ts, d = h.shape[-2:]
    hist = POOL_HIST_ROWS
    group = d // len(POOL_WINDOWS)
    pre = (slice(None),) * (h.ndim - 2)
    ys = []
    for g, win in enumerate(POOL_WINDOWS):
        cols = slice(g * group, (g + 1) * group)
        hg = h[pre + (slice(None), cols)]
        acc = hg
        for j in range(1, win):
            acc = acc + ext_ref[pre + (slice(hist - j, hist - j + ts), cols)]
        cnt = jnp.minimum(pos + 1, win).astype(F32)
        dg = acc / cnt - hg
        ys.append(_dot(dg.reshape(-1, group).astype(BF16), pw_ref[g]))
    return jnp.concatenate(ys, axis=-1) * ps_ref[...]


def _pool_kernel(ntp, nt, pos0_s, xp_ref, xs_ref, hist_ref, gmix_ref, pw_ref, ps_ref, gffn_ref, wr_ref, br_ref,
                 xn_ref, tok_ref, idx_ref, gate_ref, rank_ref, hlast_p_ref, hlast_s_ref, cnt_out_ref,
                 ext_p, ext_s, cnt_ref):
    i = pl.program_id(0)
    hist = POOL_HIST_ROWS

    @pl.when(i == 0)
    def _():
        cnt_ref[...] = jnp.zeros(cnt_ref.shape, F32)

    @pl.when(i < ntp)
    def _():
        ts, d = xp_ref.shape
        t = lax.rem(i, nt)

        @pl.when(t == 0)
        def _():
            ext_p[0:hist, :] = jnp.zeros((hist, d), F32)

        x = xp_ref[...]
        h = _rms(x, gmix_ref[...])
        ext_p[hist:hist + ts, :] = h

        @pl.when(t == nt - 1)
        def _():
            hlast_p_ref[0] = h[ts - hist:, :]

        pos = t * ts + lax.broadcasted_iota(jnp.int32, (ts, 1), 0)
        xn_ref[...] = x + _pool_mix(h, ext_p, pos, pw_ref, ps_ref)
        ext_p[0:hist, :] = ext_p[ts:ts + hist, :]

    @pl.when(i >= ntp)
    def _():
        bb, ts, d = xs_ref.shape
        ext_s[:, 0:hist, :] = hist_ref[...]
        x = xs_ref[...]
        h = _rms(x, gmix_ref[...])
        ext_s[:, hist:hist + ts, :] = h
        hlast_s_ref[...] = h[:, ts - hist:, :]
        pos = pos0_s + lax.broadcasted_iota(jnp.int32, (1, ts, 1), 1)
        xn_ref[...] = x.reshape(bb * ts, d) + _pool_mix(h, ext_s, pos, pw_ref, ps_ref)

    _route_tail(xn_ref[...], gffn_ref, wr_ref, br_ref, cnt_ref, tok_ref, idx_ref, gate_ref, rank_ref)
    cnt_out_ref[...] = cnt_ref[...]


def _pool_layer(x_prompt, x_sample, hist_s, pos0_s, gmix, pw, ps, gffn, wr, br):
    b, s, d = x_prompt.shape
    db, t, _ = x_sample.shape
    n_exp = wr.shape[0]
    r = ROW_TILE
    nt = s // r
    ntp = b * nt
    bb = r // t
    nts = db // bb
    n = b * s + db * t
    hist = POOL_HIST_ROWS
    rows = pl.BlockSpec((r, d), lambda i: (i, 0))
    lanes = pl.BlockSpec((TOP_K, r), lambda i: (0, i))
    sample_blk = lambda i: (jnp.maximum(i - ntp, 0), 0, 0)
    return pl.pallas_call(
        functools.partial(_pool_kernel, ntp, nt, pos0_s),
        grid=(ntp + nts,),
        in_specs=[pl.BlockSpec((r, d), lambda i: (jnp.minimum(i, ntp - 1), 0)),
                  pl.BlockSpec((bb, t, d), sample_blk), pl.BlockSpec((bb, hist, d), sample_blk),
                  _full((1, d)), _full(pw.shape), _full((1, d)), _full((1, d)), _full((n_exp, d)), _full((n_exp, 1))],
        out_specs=[rows, rows, lanes, lanes, lanes,
                   pl.BlockSpec((1, hist, d), lambda i: (jnp.minimum(i // nt, b - 1), 0, 0)),
                   pl.BlockSpec((bb, hist, d), sample_blk), _full((n_exp, LANES))],
        out_shape=[jax.ShapeDtypeStruct((n, d), F32), jax.ShapeDtypeStruct((n, d), F32),
                   jax.ShapeDtypeStruct((TOP_K, n), jnp.int32), jax.ShapeDtypeStruct((TOP_K, n), F32),
                   jax.ShapeDtypeStruct((TOP_K, n), jnp.int32),
                   jax.ShapeDtypeStruct((b, hist, d), F32), jax.ShapeDtypeStruct((db, hist, d), F32),
                   jax.ShapeDtypeStruct((n_exp, LANES), F32)],
        scratch_shapes=[pltpu.VMEM((hist + r, d), F32), pltpu.VMEM((bb, hist + t, d), F32),
                        pltpu.VMEM((n_exp, LANES), F32)],
        compiler_params=_params(("arbitrary",)),
        name="pool_route",
    )(x_prompt.reshape(b * s, d), x_sample, hist_s, gmix, pw, ps, gffn, wr, br)


def _proj_kernel(ntp, op_ref, os_ref, x_ref, wo_ref, gffn_ref, wr_ref, br_ref,
                 xn_ref, tok_ref, idx_ref, gate_ref, rank_ref, cnt_out_ref, cnt_ref):
    i = pl.program_id(0)

    @pl.when(i == 0)
    def _():
        cnt_ref[...] = jnp.zeros(cnt_ref.shape, F32)

    o = jnp.where(i < ntp, op_ref[...], os_ref[...])
    xn = x_ref[...] + _dot(o, wo_ref[...])
    xn_ref[...] = xn
    _route_tail(xn, gffn_ref, wr_ref, br_ref, cnt_ref, tok_ref, idx_ref, gate_ref, rank_ref)
    cnt_out_ref[...] = cnt_ref[...]


def _proj_layer(o_p, o_s, x, wo, gffn, wr, br):
    n, d = x.shape
    n_exp = wr.shape[0]
    r = ROW_TILE
    ntp = o_p.shape[0] // r
    rows = pl.BlockSpec((r, d), lambda i: (i, 0))
    lanes = pl.BlockSpec((TOP_K, r), lambda i: (0, i))
    return pl.pallas_call(
        functools.partial(_proj_kernel, ntp),
        grid=(n // r,),
        in_specs=[pl.BlockSpec((r, d), lambda i: (jnp.minimum(i, ntp - 1), 0)),
                  pl.BlockSpec((r, d), lambda i: (jnp.maximum(i - ntp, 0), 0)),
                  rows, _full((d, d)), _full((1, d)), _full((n_exp, d)), _full((n_exp, 1))],
        out_specs=[rows, rows, lanes, lanes, lanes, _full((n_exp, LANES))],
        out_shape=[jax.ShapeDtypeStruct((n, d), F32), jax.ShapeDtypeStruct((n, d), F32),
                   jax.ShapeDtypeStruct((TOP_K, n), jnp.int32), jax.ShapeDtypeStruct((TOP_K, n), F32),
                   jax.ShapeDtypeStruct((TOP_K, n), jnp.int32), jax.ShapeDtypeStruct((n_exp, LANES), F32)],
        scratch_shapes=[pltpu.VMEM((n_exp, LANES), F32)],
        compiler_params=_params(("arbitrary",)),
        name="proj_route",
    )(o_p, o_s, x, wo, gffn, wr, br)


def _dispatch_kernel(tm, zflag_ref, dest_hbm, tok_ref, buf_ref, idx_smem, zeros_ref, sem_idx, sem_rows, sem_zero):
    i = pl.program_id(0)
    r = tok_ref.shape[0]
    n_blocks = zflag_ref.shape[0]

    def zero_copy(blk):
        return pltpu.make_async_copy(zeros_ref, buf_ref.at[pl.ds(pl.multiple_of(blk * tm, tm), tm), :], sem_zero)

    @pl.when(i == 0)
    def _():
        zeros_ref[...] = jnp.zeros(zeros_ref.shape, zeros_ref.dtype)

        def start(blk, carry):
            @pl.when(zflag_ref[blk] != 0)
            def _():
                zero_copy(blk).start()
            return carry

        def wait(blk, carry):
            @pl.when(zflag_ref[blk] != 0)
            def _():
                zero_copy(blk).wait()
            return carry

        lax.fori_loop(0, n_blocks, start, 0)
        lax.fori_loop(0, n_blocks, wait, 0)

    cp = pltpu.make_async_copy(dest_hbm.at[i], idx_smem, sem_idx)
    cp.start()
    cp.wait()

    def body(row, carry):
        for k in range(TOP_K):
            dst = idx_smem[k * r + row]
            pltpu.make_async_copy(tok_ref.at[pl.ds(row, 1), :], buf_ref.at[pl.ds(dst, 1), :],
                                  sem_rows).start(priority=k % 2)
        return carry

    lax.fori_loop(0, r, body, 0, unroll=8)
    for k in range(TOP_K):
        pltpu.make_async_copy(tok_ref, buf_ref.at[pl.ds(0, r), :], sem_rows).wait()


def _dispatch(tok, dest_tiles, zflag, tm):
    n, d = tok.shape
    r = ROW_TILE
    n_blocks = zflag.shape[0]
    any_spec = pl.BlockSpec(memory_space=pl.ANY)
    return pl.pallas_call(
        functools.partial(_dispatch_kernel, tm),
        grid_spec=pltpu.PrefetchScalarGridSpec(
            num_scalar_prefetch=1,
            grid=(n // r,),
            in_specs=[any_spec, pl.BlockSpec((r, d), lambda i, zf: (i, 0))],
            out_specs=any_spec,
            scratch_shapes=[pltpu.SMEM((TOP_K * r,), jnp.int32), pltpu.VMEM((tm, d), tok.dtype),
                            pltpu.SemaphoreType.DMA, pltpu.SemaphoreType.DMA, pltpu.SemaphoreType.DMA]),
        out_shape=jax.ShapeDtypeStruct((n_blocks * tm, d), tok.dtype),
        compiler_params=_params(("arbitrary",)),
        name="moe_dispatch",
    )(zflag, dest_tiles, tok)


def _expert_kernel(be_ref, nused_ref, x_ref, w1_ref, b1_ref, w2_ref, b2_ref, o_ref, w1b_ref, w2b_ref):
    i = pl.program_id(0)
    used = i < nused_ref[0]

    @pl.when(used & ((i == 0) | (be_ref[i] != be_ref[jnp.maximum(i - 1, 0)])))
    def _():
        w1b_ref[...] = w1_ref[0].astype(BF16)
        w2b_ref[...] = w2_ref[0].astype(BF16)

    @pl.when(used)
    def _():
        f = w2b_ref.shape[0]
        gu = _dot(x_ref[...].astype(BF16), w1b_ref[...]) + b1_ref[0]
        gate = jnp.minimum(gu[:, :f], SWIGLU_LIMIT)
        up = jnp.clip(gu[:, f:], -SWIGLU_LIMIT, SWIGLU_LIMIT)
        hid = (up + 1.0) * (gate * jax.nn.sigmoid(gate * SWIGLU_ALPHA))
        o_ref[...] = _dot(hid.astype(BF16), w2b_ref[...]) + b2_ref[0]

    @pl.when(jnp.logical_not(used))
    def _():
        o_ref[...] = jnp.zeros(o_ref.shape, o_ref.dtype)


def _experts(buf, block_expert, nused, w1, b1, w2, b2, tm):
    rows, d = buf.shape
    n_exp, _, f2 = w1.shape
    f = w2.shape[1]
    in_rows = lambda i, be, nu: (jnp.minimum(i, nu[0] - 1), 0)
    by_expert = lambda i, be, nu: (be[i], 0, 0)
    return pl.pallas_call(
        _expert_kernel,
        grid_spec=pltpu.PrefetchScalarGridSpec(
            num_scalar_prefetch=2,
            grid=(rows // tm,),
            in_specs=[pl.BlockSpec((tm, d), in_rows),
                      pl.BlockSpec((1, d, f2), by_expert), pl.BlockSpec((1, 1, f2), by_expert),
                      pl.BlockSpec((1, f, d), by_expert), pl.BlockSpec((1, 1, d), by_expert)],
            out_specs=pl.BlockSpec((tm, d), lambda i, be, nu: (i, 0)),
            scratch_shapes=[pltpu.VMEM((d, f2), BF16), pltpu.VMEM((f, d), BF16)]),
        out_shape=jax.ShapeDtypeStruct((rows, d), F32),
        compiler_params=_params(("arbitrary",)),
        name="moe_experts",
    )(block_expert, nused, buf, w1, b1.reshape(n_exp, 1, f2), w2, b2.reshape(n_exp, 1, d))


def _combine_kernel(ntp, dest_hbm, y_hbm, gate_ref, x_ref, g_ref, *refs):
    outs, (idx_smem, rows_ref, sem_idx, sem_rows) = refs[:-4], refs[-4:]
    i = pl.program_id(0)
    r = x_ref.shape[0]
    cp = pltpu.make_async_copy(dest_hbm.at[i], idx_smem, sem_idx)
    cp.start()
    cp.wait()

    def body(row, carry):
        for k in range(TOP_K):
            src = idx_smem[k * r + row]
            pltpu.make_async_copy(y_hbm.at[pl.ds(src, 1), :], rows_ref.at[k, pl.ds(row, 1), :],
                                  sem_rows).start(priority=k % 2)
        return carry

    lax.fori_loop(0, r, body, 0, unroll=8)
    gates = jnp.concatenate([gate_ref[...], jnp.zeros((LANES - TOP_K, r), F32)], axis=0)
    gates_t = jnp.transpose(gates)
    for k in range(TOP_K):
        pltpu.make_async_copy(y_hbm.at[pl.ds(0, r), :], rows_ref.at[k], sem_rows).wait()
    f = rows_ref[0] * gates_t[:, 0:1]
    for k in range(1, TOP_K):
        f = f + rows_ref[k] * gates_t[:, k:k + 1]
    xo = x_ref[...] + f
    hn = _rms(xo, g_ref[...])
    if ntp is None:
        xo_ref, hn_ref = outs
        xo_ref[...] = xo
        hn_ref[...] = hn.astype(hn_ref.dtype)
    else:
        hp_ref, hs_ref = outs

        @pl.when(i < ntp)
        def _():
            hp_ref[...] = hn

        @pl.when(i >= ntp)
        def _():
            hs_ref[...] = hn


def _combine(ybuf, dest_tiles, gates, x, g, n_p):
    n, d = x.shape
    r = ROW_TILE
    any_spec = pl.BlockSpec(memory_space=pl.ANY)
    rows = pl.BlockSpec((r, d), lambda i: (i, 0))
    if n_p is None:
        ntp = None
        out_specs = [rows, rows]
        out_shape = [jax.ShapeDtypeStruct((n, d), F32), jax.ShapeDtypeStruct((n, d), BF16)]
    else:
        ntp = n_p // r
        out_specs = [pl.BlockSpec((r, d), lambda i: (jnp.minimum(i, ntp - 1), 0)),
                     pl.BlockSpec((r, d), lambda i: (jnp.maximum(i - ntp, 0), 0))]
        out_shape = [jax.ShapeDtypeStruct((n_p, d), F32), jax.ShapeDtypeStruct((n - n_p, d), F32)]
    return pl.pallas_call(
        functools.partial(_combine_kernel, ntp),
        grid=(n // r,),
        in_specs=[any_spec, any_spec, pl.BlockSpec((TOP_K, r), lambda i: (0, i)), rows, _full((1, d))],
        out_specs=out_specs,
        out_shape=out_shape,
        scratch_shapes=[pltpu.SMEM((TOP_K * r,), jnp.int32), pltpu.VMEM((TOP_K, r, d), F32),
                        pltpu.SemaphoreType.DMA, pltpu.SemaphoreType.DMA],
        compiler_params=_params(("arbitrary",)),
        name="moe_combine",
    )(dest_tiles, ybuf, gates, x, g)


def _moe(tok, idx, gates, rank, cnt, x, g_next, n_p, w1, b1, w2, b2):
    n, d = tok.shape
    n_exp = w1.shape[0]
    pairs = n * TOP_K
    tm = 128
    for cand in (512, 256):
        if pairs >= 4 * cand * n_exp and pairs % cand == 0:
            tm = cand
            break
    n_blocks = -(-pairs // tm) + n_exp
    counts = cnt[:, 0].astype(jnp.int32)
    padded = (counts + tm - 1) // tm * tm
    pad_end = jnp.cumsum(padded)
    pad_start = pad_end - padded
    experts = jnp.arange(n_exp, dtype=jnp.int32)
    dest = rank + jnp.sum(jnp.where(idx[:, :, None] == experts, pad_start, 0), axis=-1)
    starts = jnp.arange(n_blocks, dtype=jnp.int32) * tm
    nused = (pad_end[-1:] // tm).astype(jnp.int32)
    block_expert = jnp.minimum(jnp.sum((pad_end[None, :] <= starts[:, None]).astype(jnp.int32), axis=1), n_exp - 1)
    last_of_expert = jnp.any((starts[:, None] + tm == pad_end[None, :]) & (padded[None, :] > 0), axis=1)
    zflag = (last_of_expert | (starts >= pad_end[-1])).astype(jnp.int32)
    r = ROW_TILE
    dest_tiles = dest.reshape(TOP_K, n // r, r).transpose(1, 0, 2).reshape(n // r, TOP_K * r)
    buf = _dispatch(tok, dest_tiles, zflag, tm)
    ybuf = _experts(buf, block_expert, nused, w1, b1, w2, b2, tm)
    return _combine(ybuf, dest_tiles, gates, x, g_next, n_p)


def _qkv_kernel(ntp, h_ref, w_ref, q_ref, kb_ref, vb_ref, kp_ref, vp_ref, ks_ref, vs_ref):
    i = pl.program_id(0)
    d = h_ref.shape[1]
    qkv = _dot(h_ref[...], w_ref[...])
    hd = d // N_HEADS
    q_ref[...] = (qkv[:, :d] * (hd ** -0.5)).astype(BF16)
    k = qkv[:, d:2 * d]
    v = qkv[:, 2 * d:]
    kb_ref[...] = k.astype(BF16)
    vb_ref[...] = v.astype(BF16)

    @pl.when(i < ntp)
    def _():
        kp_ref[...] = k
        vp_ref[...] = v

    @pl.when(i >= ntp)
    def _():
        ks_ref[...] = k
        vs_ref[...] = v


def _qkv(h, w, n_p):
    n, d = h.shape
    r = ROW_TILE
    ntp = n_p // r
    rows = pl.BlockSpec((r, d), lambda i: (i, 0))
    prompt = pl.BlockSpec((r, d), lambda i: (jnp.minimum(i, ntp - 1), 0))
    sample = pl.BlockSpec((r, d), lambda i: (jnp.maximum(i - ntp, 0), 0))
    return pl.pallas_call(
        functools.partial(_qkv_kernel, ntp),
        grid=(n // r,),
        in_specs=[rows, _full((d, 3 * d))],
        out_specs=[rows] * 3 + [prompt, prompt, sample, sample],
        out_shape=[jax.ShapeDtypeStruct((n, d), BF16)] * 3 + [jax.ShapeDtypeStruct((n_p, d), F32)] * 2
        + [jax.ShapeDtypeStruct((n - n_p, d), F32)] * 2,
        compiler_params=_params(("arbitrary",)),
        name="qkv",
    )(h, w)


def _attn_kernel(q_pos0, q_ref, k_ref, v_ref, o_ref, qm_ref, carry_ref, acc_ref):
    qi = pl.program_id(1)
    tq, d = q_ref.shape
    tk = ATTN_TK
    n_pairs = d // LANES
    half = LANES // 2
    lane = lax.broadcasted_iota(jnp.int32, (tq, LANES), 1)
    for p in range(n_pairs):
        qp = q_ref[:, p * LANES:(p + 1) * LANES]
        qm_ref[p, 0:tq, :] = jnp.where(lane < half, qp, jnp.zeros_like(qp))
        qm_ref[p, tq:2 * tq, :] = jnp.where(lane >= half, qp, jnp.zeros_like(qp))
    carry_ref[...] = jnp.zeros(carry_ref.shape, F32)
    acc_ref[...] = jnp.zeros(acc_ref.shape, F32)
    row_pos = q_pos0 + qi * tq + lax.rem(lax.broadcasted_iota(jnp.int32, (2 * tq, tk), 0), tq)
    col = lax.broadcasted_iota(jnp.int32, (2 * tq, tk), 1)
    src = lax.rem(lax.broadcasted_iota(jnp.int32, (2 * tk, 2 * tk), 0), tk)
    dst = lax.broadcasted_iota(jnp.int32, (2 * tk, 2 * tk), 1)
    sums = jnp.where((dst >= tk) | (src > dst), 1.0, 0.0).astype(BF16)
    j0 = (q_pos0 + (qi + 1) * tq - 2) // tk

    def cond(state):
        j, top = state
        return (j >= 0) & (top > -ATTN_EXIT)

    def body(state):
        j, _ = state
        ks = pl.multiple_of(j * tk, tk)
        mask = (j * tk + col) < row_pos
        pairs = [slice(p * LANES, (p + 1) * LANES) for p in range(n_pairs)]
        zs = [_dot_nt(qm_ref[p], k_ref[pl.ds(ks, tk), pairs[p]]) for p in range(n_pairs)]
        log_beta, parts = [], []
        for z in zs:
            sp = jnp.maximum(z, 0.0) + jnp.log(1.0 + jnp.exp(-jnp.abs(z)))
            log_keep = jnp.where(mask, -sp, 0.0)
            hi = log_keep.astype(BF16)
            lo = (log_keep - hi.astype(F32)).astype(BF16)
            parts.append(jnp.concatenate([hi, lo], axis=1))
            log_beta.append(z - sp)
        sums_out = [_dot(part, sums) for part in parts]
        weights = []
        top = jnp.full((2 * tq, tk), -jnp.inf, F32)
        for p in range(n_pairs):
            carry = carry_ref[p]
            a = jnp.where(mask, jnp.exp(log_beta[p] + sums_out[p][:, :tk] + carry), 0.0)
            weights.append(a.astype(BF16))
            carry = carry + sums_out[p][:, tk:]
            carry_ref[p] = carry
            top = jnp.maximum(top, carry)
        for p in range(n_pairs):
            out = _dot(weights[p], v_ref[pl.ds(ks, tk), pairs[p]])
            acc_ref[:, pairs[p]] += jnp.where(lane < half, out[:tq], out[tq:])
        return j - 1, jnp.max(top)

    lax.while_loop(cond, body, (j0, jnp.float32(0.0)))
    o_ref[...] = acc_ref[...].astype(o_ref.dtype)


def _attention(q, k, v, q_pos0, n_streams, lq, lk, tq, q_row_off):
    d = q.shape[1]
    nq = lq // tq
    off = q_row_off // tq
    kv = pl.BlockSpec((lk, d), lambda b, i: (b, 0))
    return pl.pallas_call(
        functools.partial(_attn_kernel, q_pos0),
        grid=(n_streams, nq),
        in_specs=[pl.BlockSpec((tq, d), lambda b, i: (off + b * nq + i, 0)), kv, kv],
        out_specs=pl.BlockSpec((tq, d), lambda b, i: (b * nq + i, 0)),
        out_shape=jax.ShapeDtypeStruct((n_streams * lq, d), BF16),
        scratch_shapes=[pltpu.VMEM((d // LANES, 2 * tq, LANES), BF16), pltpu.VMEM((d // LANES, 2 * tq, ATTN_TK), F32),
                        pltpu.VMEM((tq, d), F32)],
        compiler_params=_params(("arbitrary", "arbitrary")),
        name="stick_breaking",
    )(q, k, v)


def kernel(x_prompt, x_sample, state_pool, cache_k, cache_v, norm_mix, norm_ffn, pool_w, pool_scale, w_qkv, w_o, router_w, router_b, moe_w1, moe_b1, moe_w2, moe_b2, final_norm):
    b, s, d = x_prompt.shape
    db, t, _ = x_sample.shape
    past = cache_k.shape[2]
    n_exp = router_w.shape[2]
    hd = d // N_HEADS
    hist = POOL_HIST_ROWS
    n_p, n_s = b * s, db * t
    assert t >= hist and ROW_TILE % t == 0 and s % ROW_TILE == 0 and n_s % ROW_TILE == 0
    row = lambda a: a.reshape(1, -1)
    wr = [router_w[i].T for i in range(2)]
    br = [router_b[i].reshape(n_exp, 1) for i in range(2)]

    hist_s = jnp.concatenate([jnp.zeros((db, 1, d), F32), state_pool[0]], axis=1)
    xn, tok, idx, gates, rank, hlast_p, hlast_s, cnt = _pool_layer(
        x_prompt, x_sample, hist_s, past, row(norm_mix[0]), pool_w[0].astype(BF16), row(pool_scale[0]),
        row(norm_ffn[0]), wr[0], br[0])
    x1, h1 = _moe(tok, idx, gates, rank, cnt, xn, row(norm_mix[1]), None, moe_w1[0], moe_b1[0], moe_w2[0], moe_b2[0])

    q, kb, vb, kf_p, vf_p, kf_s, vf_s = _qkv(h1, w_qkv[0].astype(BF16), n_p)
    o_p = _attention(q, kb, vb, 0, b, s, s, min(s, ATTN_TQ), 0)
    lk_s = -(-(past + t) // ATTN_TK) * ATTN_TK

    def sample_keys(cache, new):
        allk = jnp.concatenate([cache[0].reshape(db, past, d).astype(BF16), new[n_p:].reshape(db, t, d)], axis=1)
        return jnp.pad(allk, ((0, 0), (0, lk_s - past - t), (0, 0))).reshape(db * lk_s, d)

    o_s = _attention(q, sample_keys(cache_k, kb), sample_keys(cache_v, vb), past, db, t, lk_s, t, n_p)
    xn, tok, idx, gates, rank, cnt = _proj_layer(o_p, o_s, x1, w_o[0].astype(BF16), row(norm_ffn[1]), wr[1], br[1])
    y_p, y_s = _moe(tok, idx, gates, rank, cnt, xn, row(final_norm), n_p, moe_w1[1], moe_b1[1], moe_w2[1], moe_b2[1])

    heads = lambda a, nb, l: a.reshape(1, nb, l, N_HEADS, hd)
    return (y_p.reshape(b, s, d), y_s.reshape(db, t, d), hlast_p[None, :, 1:, :],
            heads(kf_p, b, s), heads(vf_p, b, s), hlast_s[None, :, 1:, :],
            heads(kf_s, db, t), heads(vf_s, db, t))
```

```python
import functools

import jax
import jax.numpy as jnp
from jax import lax
from jax.experimental import pallas as pl
from jax.experimental.pallas import tpu as pltpu

EPS = 1e-5
POOL_WINDOWS = (2, 4, 8, 16)
POOL_HIST_ROWS = 16
N_HEADS = 16
TOP_K = 4
SWIGLU_LIMIT = 7.0
SWIGLU_ALPHA = 1.702
LANES = 128
ROW_TILE = 512
QKV_TILE = 256
ATTN_TQ = 128
ATTN_TK = 128
ATTN_EXIT = 104.0
VMEM_LIMIT = 56 * 1024 * 1024

F32 = jnp.float32
BF16 = jnp.bfloat16


def _rms(x, g):
    ms = jnp.mean(x * x, axis=-1, keepdims=True)
    return x * lax.rsqrt(ms + EPS) * g


def _dot(a, b):
    return jnp.dot(a, b, preferred_element_type=F32)


def _dot_nt(a, b, precision=None):
    return lax.dot_general(a, b, (((1,), (1,)), ((), ())), preferred_element_type=F32, precision=precision)


def _params(semantics):
    return pltpu.CompilerParams(dimension_semantics=semantics, vmem_limit_bytes=VMEM_LIMIT)


def _full(shape):
    return pl.BlockSpec(shape, lambda i, *_: (0,) * len(shape))


def _route_tail(xn, gffn_ref, wr_ref, br_ref, cnt_ref, tok_ref, idx_ref, gate_ref, rank_ref):
    r = xn.shape[0]
    n_exp = wr_ref.shape[0]
    tok = _rms(xn, gffn_ref[...])
    tok_ref[...] = tok.astype(tok_ref.dtype)
    logits = _dot_nt(wr_ref[...], tok, precision=lax.Precision.HIGHEST) + br_ref[...]
    eidx = lax.broadcasted_iota(jnp.int32, logits.shape, 0).astype(F32)
    vals, idxs = [], []
    l = logits
    for _ in range(TOP_K):
        m = jnp.max(l, axis=0, keepdims=True)
        i = jnp.min(jnp.where(l == m, eidx, float(n_exp)), axis=0, keepdims=True)
        vals.append(m)
        idxs.append(i)
        l = jnp.where(eidx == i, -jnp.inf, l)
    es = [jnp.exp(v - vals[0]) for v in vals]
    den = es[0]
    for e in es[1:]:
        den = den + e
    gate_ref[...] = jnp.concatenate([e / den for e in es], axis=0)
    idx_ref[...] = jnp.concatenate(idxs, axis=0).astype(jnp.int32)
    member = jnp.zeros(logits.shape, F32)
    for i in idxs:
        member = member + jnp.where(eidx == i, 1.0, 0.0)
    tri = jnp.where(lax.broadcasted_iota(jnp.int32, (r, r), 0) < lax.broadcasted_iota(jnp.int32, (r, r), 1),
                    1.0, 0.0).astype(BF16)
    before = _dot(member.astype(BF16), tri) + cnt_ref[:, :1]
    ranks = [jnp.sum(jnp.where(eidx == i, before, 0.0), axis=0, keepdims=True) for i in idxs]
    rank_ref[...] = jnp.concatenate(ranks, axis=0).astype(jnp.int32)
    cnt_ref[...] = cnt_ref[...] + jnp.sum(member, axis=1, keepdims=True)


def _pool_mix(h, ext_ref, pos, pw_ref, ps_ref):
    ts, d = h.shape[-2:]
    hist = POOL_HIST_ROWS
    group = d // len(POOL_WINDOWS)
    pre = (slice(None),) * (h.ndim - 2)
    ys = []
    for g, win in enumerate(POOL_WINDOWS):
        cols = slice(g * group, (g + 1) * group)
        hg = h[pre + (slice(None), cols)]
        acc = hg
        for j in range(1, win):
            acc = acc + ext_ref[pre + (slice(hist - j, hist - j + ts), cols)]
        cnt = jnp.minimum(pos + 1, win).astype(F32)
        dg = acc / cnt - hg
        ys.append(_dot(dg.reshape(-1, group).astype(BF16), pw_ref[g]))
    return jnp.concatenate(ys, axis=-1) * ps_ref[...]


def _pool_kernel(ntp, nt, pos0_s, xp_ref, xs_ref, hist_ref, gmix_ref, pw_ref, ps_ref, gffn_ref, wr_ref, br_ref,
                 xn_ref, tok_ref, idx_ref, gate_ref, rank_ref, hlast_p_ref, hlast_s_ref, cnt_out_ref,
                 ext_p, ext_s, cnt_ref):
    i = pl.program_id(0)
    hist = POOL_HIST_ROWS

    @pl.when(i == 0)
    def _():
        cnt_ref[...] = jnp.zeros(cnt_ref.shape, F32)

    @pl.when(i < ntp)
    def _():
        ts, d = xp_ref.shape
        t = lax.rem(i, nt)

        @pl.when(t == 0)
        def _():
            ext_p[0:hist, :] = jnp.zeros((hist, d), F32)

        x = xp_ref[...]
        h = _rms(x, gmix_ref[...])
        ext_p[hist:hist + ts, :] = h

        @pl.when(t == nt - 1)
        def _():
            hlast_p_ref[0] = h[ts - hist:, :]

        pos = t * ts + lax.broadcasted_iota(jnp.int32, (ts, 1), 0)
        xn_ref[...] = x + _pool_mix(h, ext_p, pos, pw_ref, ps_ref)
        ext_p[0:hist, :] = ext_p[ts:ts + hist, :]

    @pl.when(i >= ntp)
    def _():
        bb, ts, d = xs_ref.shape
        ext_s[:, 0:hist, :] = hist_ref[...]
        x = xs_ref[...]
        h = _rms(x, gmix_ref[...])
        ext_s[:, hist:hist + ts, :] = h
        hlast_s_ref[...] = h[:, ts - hist:, :]
        pos = pos0_s + lax.broadcasted_iota(jnp.int32, (1, ts, 1), 1)
        xn_ref[...] = x.reshape(bb * ts, d) + _pool_mix(h, ext_s, pos, pw_ref, ps_ref)

    _route_tail(xn_ref[...], gffn_ref, wr_ref, br_ref, cnt_ref, tok_ref, idx_ref, gate_ref, rank_ref)
    cnt_out_ref[...] = cnt_ref[...]


def _pool_layer(x_prompt, x_sample, hist_s, pos0_s, gmix, pw, ps, gffn, wr, br):
    b, s, d = x_prompt.shape
    db, t, _ = x_sample.shape
    n_exp = wr.shape[0]
    r = ROW_TILE
    nt = s // r
    ntp = b * nt
    bb = r // t
    nts = db // bb
    n = b * s + db * t
    hist = POOL_HIST_ROWS
    rows = pl.BlockSpec((r, d), lambda i: (i, 0))
    lanes = pl.BlockSpec((TOP_K, r), lambda i: (0, i))
    sample_blk = lambda i: (jnp.maximum(i - ntp, 0), 0, 0)
    return pl.pallas_call(
        functools.partial(_pool_kernel, ntp, nt, pos0_s),
        grid=(ntp + nts,),
        in_specs=[pl.BlockSpec((r, d), lambda i: (jnp.minimum(i, ntp - 1), 0)),
                  pl.BlockSpec((bb, t, d), sample_blk), pl.BlockSpec((bb, hist, d), sample_blk),
                  _full((1, d)), _full(pw.shape), _full((1, d)), _full((1, d)), _full((n_exp, d)), _full((n_exp, 1))],
        out_specs=[rows, rows, lanes, lanes, lanes,
                   pl.BlockSpec((1, hist, d), lambda i: (jnp.minimum(i // nt, b - 1), 0, 0)),
                   pl.BlockSpec((bb, hist, d), sample_blk), _full((n_exp, LANES))],
        out_shape=[jax.ShapeDtypeStruct((n, d), F32), jax.ShapeDtypeStruct((n, d), F32),
                   jax.ShapeDtypeStruct((TOP_K, n), jnp.int32), jax.ShapeDtypeStruct((TOP_K, n), F32),
                   jax.ShapeDtypeStruct((TOP_K, n), jnp.int32),
                   jax.ShapeDtypeStruct((b, hist, d), F32), jax.ShapeDtypeStruct((db, hist, d), F32),
                   jax.ShapeDtypeStruct((n_exp, LANES), F32)],
        scratch_shapes=[pltpu.VMEM((hist + r, d), F32), pltpu.VMEM((bb, hist + t, d), F32),
                        pltpu.VMEM((n_exp, LANES), F32)],
        compiler_params=_params(("arbitrary",)),
        name="pool_route",
    )(x_prompt.reshape(b * s, d), x_sample, hist_s, gmix, pw, ps, gffn, wr, br)


def _proj_kernel(ntp, op_ref, os_ref, x_ref, wo_ref, gffn_ref, wr_ref, br_ref,
                 xn_ref, tok_ref, idx_ref, gate_ref, rank_ref, cnt_out_ref, cnt_ref):
    i = pl.program_id(0)

    @pl.when(i == 0)
    def _():
        cnt_ref[...] = jnp.zeros(cnt_ref.shape, F32)

    o = jnp.where(i < ntp, op_ref[...], os_ref[...])
    xn = x_ref[...] + _dot(o, wo_ref[...])
    xn_ref[...] = xn
    _route_tail(xn, gffn_ref, wr_ref, br_ref, cnt_ref, tok_ref, idx_ref, gate_ref, rank_ref)
    cnt_out_ref[...] = cnt_ref[...]


def _proj_layer(o_p, o_s, x, wo, gffn, wr, br):
    n, d = x.shape
    n_exp = wr.shape[0]
    r = ROW_TILE
    ntp = o_p.shape[0] // r
    rows = pl.BlockSpec((r, d), lambda i: (i, 0))
    lanes = pl.BlockSpec((TOP_K, r), lambda i: (0, i))
    return pl.pallas_call(
        functools.partial(_proj_kernel, ntp),
        grid=(n // r,),
        in_specs=[pl.BlockSpec((r, d), lambda i: (jnp.minimum(i, ntp - 1), 0)),
                  pl.BlockSpec((r, d), lambda i: (jnp.maximum(i - ntp, 0), 0)),
                  rows, _full((d, d)), _full((1, d)), _full((n_exp, d)), _full((n_exp, 1))],
        out_specs=[rows, rows, lanes, lanes, lanes, _full((n_exp, LANES))],
        out_shape=[jax.ShapeDtypeStruct((n, d), F32), jax.ShapeDtypeStruct((n, d), F32),
                   jax.ShapeDtypeStruct((TOP_K, n), jnp.int32), jax.ShapeDtypeStruct((TOP_K, n), F32),
                   jax.ShapeDtypeStruct((TOP_K, n), jnp.int32), jax.ShapeDtypeStruct((n_exp, LANES), F32)],
        scratch_shapes=[pltpu.VMEM((n_exp, LANES), F32)],
        compiler_params=_params(("arbitrary",)),
        name="proj_route",
    )(o_p, o_s, x, wo, gffn, wr, br)


def _dispatch_kernel(tm, zflag_ref, dest_hbm, tok_ref, buf_ref, idx_smem, zeros_ref, sem_idx, sem_rows, sem_zero):
    i = pl.program_id(0)
    r = tok_ref.shape[0]
    n_blocks = zflag_ref.shape[0]

    def zero_copy(blk):
        return pltpu.make_async_copy(zeros_ref, buf_ref.at[pl.ds(pl.multiple_of(blk * tm, tm), tm), :], sem_zero)

    @pl.when(i == 0)
    def _():
        zeros_ref[...] = jnp.zeros(zeros_ref.shape, zeros_ref.dtype)

        def start(blk, carry):
            @pl.when(zflag_ref[blk] != 0)
            def _():
                zero_copy(blk).start()
            return carry

        def wait(blk, carry):
            @pl.when(zflag_ref[blk] != 0)
            def _():
                zero_copy(blk).wait()
            return carry

        lax.fori_loop(0, n_blocks, start, 0)
        lax.fori_loop(0, n_blocks, wait, 0)

    cp = pltpu.make_async_copy(dest_hbm.at[i], idx_smem, sem_idx)
    cp.start()
    cp.wait()

    def body(row, carry):
        for k in range(TOP_K):
            dst = idx_smem[k * r + row]
            pltpu.make_async_copy(tok_ref.at[pl.ds(row, 1), :], buf_ref.at[pl.ds(dst, 1), :],
                                  sem_rows).start(priority=k % 2)
        return carry

    lax.fori_loop(0, r, body, 0, unroll=8)
    for k in range(TOP_K):
        pltpu.make_async_copy(tok_ref, buf_ref.at[pl.ds(0, r), :], sem_rows).wait()


def _dispatch(tok, dest_tiles, zflag, tm):
    n, d = tok.shape
    r = ROW_TILE
    n_blocks = zflag.shape[0]
    any_spec = pl.BlockSpec(memory_space=pl.ANY)
    return pl.pallas_call(
        functools.partial(_dispatch_kernel, tm),
        grid_spec=pltpu.PrefetchScalarGridSpec(
            num_scalar_prefetch=1,
            grid=(n // r,),
            in_specs=[any_spec, pl.BlockSpec((r, d), lambda i, zf: (i, 0))],
            out_specs=any_spec,
            scratch_shapes=[pltpu.SMEM((TOP_K * r,), jnp.int32), pltpu.VMEM((tm, d), tok.dtype),
                            pltpu.SemaphoreType.DMA, pltpu.SemaphoreType.DMA, pltpu.SemaphoreType.DMA]),
        out_shape=jax.ShapeDtypeStruct((n_blocks * tm, d), tok.dtype),
        compiler_params=_params(("arbitrary",)),
        name="moe_dispatch",
    )(zflag, dest_tiles, tok)


def _expert_kernel(be_ref, nused_ref, x_ref, w1_ref, b1_ref, w2_ref, b2_ref, o_ref, w1b_ref, w2b_ref):
    i = pl.program_id(0)
    used = i < nused_ref[0]

    @pl.when(used & ((i == 0) | (be_ref[i] != be_ref[jnp.maximum(i - 1, 0)])))
    def _():
        w1b_ref[...] = w1_ref[0, 0].astype(BF16)
        w2b_ref[...] = w2_ref[0, 0].astype(BF16)

    @pl.when(used)
    def _():
        f = w2b_ref.shape[0]
        gu = _dot(x_ref[...].astype(BF16), w1b_ref[...]) + b1_ref[0]
        gate = jnp.minimum(gu[:, :f], SWIGLU_LIMIT)
        up = jnp.clip(gu[:, f:], -SWIGLU_LIMIT, SWIGLU_LIMIT)
        hid = (up + 1.0) * (gate * jax.nn.sigmoid(gate * SWIGLU_ALPHA))
        o_ref[...] = _dot(hid.astype(BF16), w2b_ref[...]) + b2_ref[0]

    @pl.when(jnp.logical_not(used))
    def _():
        o_ref[...] = jnp.zeros(o_ref.shape, o_ref.dtype)


def _experts(buf, block_expert, nused, layer, w1, b1, w2, b2, tm):
    rows, d = buf.shape
    _, n_exp, _, f2 = w1.shape
    f = w2.shape[2]
    in_rows = lambda i, be, nu: (jnp.minimum(i, nu[0] - 1), 0)
    by_expert = lambda i, be, nu: (be[i], 0, 0)
    by_layer_expert = lambda i, be, nu: (layer, be[i], 0, 0)
    return pl.pallas_call(
        _expert_kernel,
        grid_spec=pltpu.PrefetchScalarGridSpec(
            num_scalar_prefetch=2,
            grid=(rows // tm,),
            in_specs=[pl.BlockSpec((tm, d), in_rows),
                      pl.BlockSpec((1, 1, d, f2), by_layer_expert), pl.BlockSpec((1, 1, f2), by_expert),
                      pl.BlockSpec((1, 1, f, d), by_layer_expert), pl.BlockSpec((1, 1, d), by_expert)],
            out_specs=pl.BlockSpec((tm, d), lambda i, be, nu: (i, 0)),
            scratch_shapes=[pltpu.VMEM((d, f2), BF16), pltpu.VMEM((f, d), BF16)]),
        out_shape=jax.ShapeDtypeStruct((rows, d), F32),
        compiler_params=_params(("arbitrary",)),
        name="moe_experts",
    )(block_expert, nused, buf, w1, b1.reshape(n_exp, 1, f2), w2, b2.reshape(n_exp, 1, d))


def _combine_kernel(ntp, dest_hbm, y_hbm, gate_ref, x_ref, g_ref, *refs):
    outs, (idx_smem, rows_ref, sem_idx, sem_rows) = refs[:-4], refs[-4:]
    i = pl.program_id(0)
    r = x_ref.shape[0]
    cp = pltpu.make_async_copy(dest_hbm.at[i], idx_smem, sem_idx)
    cp.start()
    cp.wait()

    def body(row, carry):
        for k in range(TOP_K):
            src = idx_smem[k * r + row]
            pltpu.make_async_copy(y_hbm.at[pl.ds(src, 1), :], rows_ref.at[k, pl.ds(row, 1), :],
                                  sem_rows).start(priority=k % 2)
        return carry

    lax.fori_loop(0, r, body, 0, unroll=8)
    gates = jnp.concatenate([gate_ref[...], jnp.zeros((LANES - TOP_K, r), F32)], axis=0)
    gates_t = jnp.transpose(gates)
    for k in range(TOP_K):
        pltpu.make_async_copy(y_hbm.at[pl.ds(0, r), :], rows_ref.at[k], sem_rows).wait()
    f = rows_ref[0] * gates_t[:, 0:1]
    for k in range(1, TOP_K):
        f = f + rows_ref[k] * gates_t[:, k:k + 1]
    xo = x_ref[...] + f
    hn = _rms(xo, g_ref[...])
    if ntp is None:
        xo_ref, hn_ref = outs
        xo_ref[...] = xo
        hn_ref[...] = hn.astype(hn_ref.dtype)
    else:
        hp_ref, hs_ref = outs

        @pl.when(i < ntp)
        def _():
            hp_ref[...] = hn

        @pl.when(i >= ntp)
        def _():
            hs_ref[...] = hn


def _combine(ybuf, dest_tiles, gates, x, g, n_p):
    n, d = x.shape
    r = ROW_TILE
    any_spec = pl.BlockSpec(memory_space=pl.ANY)
    rows = pl.BlockSpec((r, d), lambda i: (i, 0))
    if n_p is None:
        ntp = None
        out_specs = [rows, rows]
        out_shape = [jax.ShapeDtypeStruct((n, d), F32), jax.ShapeDtypeStruct((n, d), BF16)]
    else:
        ntp = n_p // r
        out_specs = [pl.BlockSpec((r, d), lambda i: (jnp.minimum(i, ntp - 1), 0)),
                     pl.BlockSpec((r, d), lambda i: (jnp.maximum(i - ntp, 0), 0))]
        out_shape = [jax.ShapeDtypeStruct((n_p, d), F32), jax.ShapeDtypeStruct((n - n_p, d), F32)]
    return pl.pallas_call(
        functools.partial(_combine_kernel, ntp),
        grid=(n // r,),
        in_specs=[any_spec, any_spec, pl.BlockSpec((TOP_K, r), lambda i: (0, i)), rows, _full((1, d))],
        out_specs=out_specs,
        out_shape=out_shape,
        scratch_shapes=[pltpu.SMEM((TOP_K * r,), jnp.int32), pltpu.VMEM((TOP_K, r, d), F32),
                        pltpu.SemaphoreType.DMA, pltpu.SemaphoreType.DMA],
        compiler_params=_params(("arbitrary",)),
        name="moe_combine",
    )(dest_tiles, ybuf, gates, x, g)


def _moe(tok, idx, gates, rank, cnt, x, g_next, n_p, layer, w1, b1, w2, b2):
    n, d = tok.shape
    n_exp = w1.shape[1]
    pairs = n * TOP_K
    tm = 128
    for cand in (512, 256):
        if pairs >= 4 * cand * n_exp and pairs % cand == 0:
            tm = cand
            break
    n_blocks = -(-pairs // tm) + n_exp
    counts = cnt[:, 0].astype(jnp.int32)
    padded = (counts + tm - 1) // tm * tm
    pad_end = jnp.cumsum(padded)
    pad_start = pad_end - padded
    experts = jnp.arange(n_exp, dtype=jnp.int32)
    dest = rank + jnp.sum(jnp.where(idx[:, :, None] == experts, pad_start, 0), axis=-1)
    starts = jnp.arange(n_blocks, dtype=jnp.int32) * tm
    nused = (pad_end[-1:] // tm).astype(jnp.int32)
    block_expert = jnp.minimum(jnp.sum((pad_end[None, :] <= starts[:, None]).astype(jnp.int32), axis=1), n_exp - 1)
    last_of_expert = jnp.any((starts[:, None] + tm == pad_end[None, :]) & (padded[None, :] > 0), axis=1)
    zflag = (last_of_expert | (starts >= pad_end[-1])).astype(jnp.int32)
    r = ROW_TILE
    dest_tiles = dest.reshape(TOP_K, n // r, r).transpose(1, 0, 2).reshape(n // r, TOP_K * r)
    buf = _dispatch(tok, dest_tiles, zflag, tm)
    ybuf = _experts(buf, block_expert, nused, layer, w1, b1, w2, b2, tm)
    return _combine(ybuf, dest_tiles, gates, x, g_next, n_p)


def _qkv_kernel(ntp, h_ref, w_ref, q_ref, kb_ref, vb_ref, kp_ref, vp_ref, ks_ref, vs_ref):
    i = pl.program_id(0)
    r, d = h_ref.shape
    qkv = _dot(h_ref[...], w_ref[...])
    hd = d // N_HEADS
    q_ref[...] = (qkv[:, :d] * (hd ** -0.5)).astype(BF16)
    k = qkv[:, d:2 * d]
    v = qkv[:, 2 * d:]
    kb_ref[...] = k.astype(BF16)
    vb_ref[...] = v.astype(BF16)

    @pl.when(i < ntp)
    def _():
        for h in range(N_HEADS):
            kp_ref[pl.ds(h, r, stride=N_HEADS), :] = k[:, h * hd:(h + 1) * hd]
            vp_ref[pl.ds(h, r, stride=N_HEADS), :] = v[:, h * hd:(h + 1) * hd]

    @pl.when(i >= ntp)
    def _():
        ks_ref[...] = k
        vs_ref[...] = v


def _qkv(h, w, n_p):
    n, d = h.shape
    r = QKV_TILE
    hd = d // N_HEADS
    ntp = n_p // r
    rows = pl.BlockSpec((r, d), lambda i: (i, 0))
    prompt = pl.BlockSpec((r * N_HEADS, hd), lambda i: (jnp.minimum(i, ntp - 1), 0))
    sample = pl.BlockSpec((r, d), lambda i: (jnp.maximum(i - ntp, 0), 0))
    return pl.pallas_call(
        functools.partial(_qkv_kernel, ntp),
        grid=(n // r,),
        in_specs=[rows, _full((d, 3 * d))],
        out_specs=[rows] * 3 + [prompt, prompt, sample, sample],
        out_shape=[jax.ShapeDtypeStruct((n, d), BF16)] * 3 + [jax.ShapeDtypeStruct((n_p * N_HEADS, hd), F32)] * 2
        + [jax.ShapeDtypeStruct((n - n_p, d), F32)] * 2,
        compiler_params=_params(("arbitrary",)),
        name="qkv",
    )(h, w)


def _attn_setup(q_ref, qm_ref, carry_ref, acc_ref):
    tq, d = q_ref.shape
    lane = lax.broadcasted_iota(jnp.int32, (tq, LANES), 1)
    for p in range(d // LANES):
        qp = q_ref[:, p * LANES:(p + 1) * LANES]
        qm_ref[p, 0:tq, :] = jnp.where(lane < LANES // 2, qp, jnp.zeros_like(qp))
        qm_ref[p, tq:2 * tq, :] = jnp.where(lane >= LANES // 2, qp, jnp.zeros_like(qp))
    carry_ref[...] = jnp.zeros(carry_ref.shape, F32)
    acc_ref[...] = jnp.zeros(acc_ref.shape, F32)


def _suffix_sum_matrix(tk):
    src = lax.rem(lax.broadcasted_iota(jnp.int32, (2 * tk, 2 * tk), 0), tk)
    dst = lax.broadcasted_iota(jnp.int32, (2 * tk, 2 * tk), 1)
    return jnp.where((dst >= tk) | (src > dst), 1.0, 0.0).astype(BF16)


def _attn_block(qm_ref, carry_ref, acc_ref, keys, values, mask, sums):
    n_pairs, tq2, tk = carry_ref.shape
    tq = tq2 // 2
    lane = lax.broadcasted_iota(jnp.int32, (tq, LANES), 1)
    visible = (lambda x: x) if mask is None else (lambda x: jnp.where(mask, x, 0.0))
    zs = [_dot_nt(qm_ref[p], keys[p]) for p in range(n_pairs)]
    log_beta, parts = [], []
    for z in zs:
        sp = jnp.maximum(z, 0.0) + jnp.log(1.0 + jnp.exp(-jnp.abs(z)))
        log_keep = visible(-sp)
        hi = log_keep.astype(BF16)
        lo = (log_keep - hi.astype(F32)).astype(BF16)
        parts.append(jnp.concatenate([hi, lo], axis=1))
        log_beta.append(z - sp)
    sums_out = [_dot(part, sums) for part in parts]
    weights = []
    top = jnp.full((tq2, tk), -jnp.inf, F32)
    for p in range(n_pairs):
        carry = carry_ref[p]
        a = visible(jnp.exp(log_beta[p] + sums_out[p][:, :tk] + carry))
        weights.append(a.astype(BF16))
        carry = carry + sums_out[p][:, tk:]
        carry_ref[p] = carry
        top = jnp.maximum(top, carry)
    for p in range(n_pairs):
        out = _dot(weights[p], values[p])
        acc_ref[:, p * LANES:(p + 1) * LANES] += jnp.where(lane < LANES // 2, out[:tq], out[tq:])
    return jnp.max(top)


def _keep_sweeping(state):
    j, top = state
    return (j >= 0) & (top > -ATTN_EXIT)


def _attn_prompt_kernel(q_ref, k_ref, v_ref, o_ref, qm_ref, carry_ref, acc_ref):
    qi = pl.program_id(1)
    tq, d = q_ref.shape
    tk = ATTN_TK
    n_pairs = d // LANES
    _attn_setup(q_ref, qm_ref, carry_ref, acc_ref)
    row_pos = qi * tq + lax.rem(lax.broadcasted_iota(jnp.int32, (2 * tq, tk), 0), tq)
    col = lax.broadcasted_iota(jnp.int32, (2 * tq, tk), 1)
    sums = _suffix_sum_matrix(tk)

    def block(j, mask):
        rows = pl.ds(pl.multiple_of(j * tk, tk), tk)
        keys = [k_ref[rows, p * LANES:(p + 1) * LANES] for p in range(n_pairs)]
        values = [v_ref[rows, p * LANES:(p + 1) * LANES] for p in range(n_pairs)]
        return _attn_block(qm_ref, carry_ref, acc_ref, keys, values, mask, sums)

    j0 = ((qi + 1) * tq - 2) // tk
    top = block(j0, (j0 * tk + col) < row_pos)
    lax.while_loop(_keep_sweeping, lambda state: (state[0] - 1, block(state[0], None)), (j0 - 1, top))
    o_ref[...] = acc_ref[...].astype(o_ref.dtype)


def _attention_prompt(q, k, v, n_streams, length):
    d = q.shape[1]
    tq = min(length, ATTN_TQ)
    assert ATTN_TK % tq == 0 and length % ATTN_TK == 0
    nq = length // tq
    kv = pl.BlockSpec((length, d), lambda b, i: (b, 0))
    return pl.pallas_call(
        _attn_prompt_kernel,
        grid=(n_streams, nq),
        in_specs=[pl.BlockSpec((tq, d), lambda b, i: (b * nq + i, 0)), kv, kv],
        out_specs=pl.BlockSpec((tq, d), lambda b, i: (b * nq + i, 0)),
        out_shape=jax.ShapeDtypeStruct((n_streams * length, d), BF16),
        scratch_shapes=[pltpu.VMEM((d // LANES, 2 * tq, LANES), BF16), pltpu.VMEM((d // LANES, 2 * tq, ATTN_TK), F32),
                        pltpu.VMEM((tq, d), F32)],
        compiler_params=_params(("arbitrary", "arbitrary")),
        name="stick_breaking_prompt",
    )(q, k, v)


def _attn_sample_kernel(n_cache_blocks, q_ref, kn_ref, vn_ref, ck_hbm, cv_hbm, o_ref,
                        qm_ref, carry_ref, acc_ref, kbuf, vbuf, sem):
    b = pl.program_id(0)
    t, d = q_ref.shape
    tk = ATTN_TK
    n_pairs = d // LANES
    blk_rows = tk * N_HEADS

    def fetch(j, slot):
        rows = pl.ds(pl.multiple_of((b * n_cache_blocks + j) * blk_rows, blk_rows), blk_rows)
        return (pltpu.make_async_copy(ck_hbm.at[rows, :], kbuf.at[slot], sem.at[0, slot]),
                pltpu.make_async_copy(cv_hbm.at[rows, :], vbuf.at[slot], sem.at[1, slot]))

    for cp in fetch(n_cache_blocks - 1, (n_cache_blocks - 1) % 2):
        cp.start()
    _attn_setup(q_ref, qm_ref, carry_ref, acc_ref)
    row = lax.rem(lax.broadcasted_iota(jnp.int32, (2 * t, tk), 0), t)
    col = lax.broadcasted_iota(jnp.int32, (2 * t, tk), 1)
    sums = _suffix_sum_matrix(tk)
    pad = jnp.zeros((tk - t, LANES), BF16)
    keys = [jnp.concatenate([kn_ref[:, p * LANES:(p + 1) * LANES], pad], axis=0) for p in range(n_pairs)]
    values = [jnp.concatenate([vn_ref[:, p * LANES:(p + 1) * LANES], pad], axis=0) for p in range(n_pairs)]
    top = _attn_block(qm_ref, carry_ref, acc_ref, keys, values, col < row, sums)

    def head_pair(buf, slot, p):
        even = buf[slot, pl.ds(2 * p, tk, stride=N_HEADS), :]
        odd = buf[slot, pl.ds(2 * p + 1, tk, stride=N_HEADS), :]
        return jnp.concatenate([even, odd], axis=1).astype(BF16)

    def body(state):
        j, _ = state
        slot = lax.rem(j, 2)
        for cp in fetch(j, slot):
            cp.wait()

        @pl.when(j > 0)
        def _():
            for cp in fetch(j - 1, 1 - slot):
                cp.start()

        keys = [head_pair(kbuf, slot, p) for p in range(n_pairs)]
        values = [head_pair(vbuf, slot, p) for p in range(n_pairs)]
        return j - 1, _attn_block(qm_ref, carry_ref, acc_ref, keys, values, None, sums)

    j_end, _ = lax.while_loop(_keep_sweeping, body, (jnp.int32(n_cache_blocks - 1), top))

    @pl.when(j_end >= 0)
    def _():
        for cp in fetch(j_end, lax.rem(j_end, 2)):
            cp.wait()

    o_ref[...] = acc_ref[...].astype(o_ref.dtype)


def _attention_sample(q, k_new, v_new, cache_k, cache_v, n_streams, t, past, row_off):
    d = q.shape[1]
    hd = d // N_HEADS
    assert past % ATTN_TK == 0 and t <= ATTN_TK
    rows = pl.BlockSpec((t, d), lambda b: (row_off // t + b, 0))
    any_spec = pl.BlockSpec(memory_space=pl.ANY)
    return pl.pallas_call(
        functools.partial(_attn_sample_kernel, past // ATTN_TK),
        grid=(n_streams,),
        in_specs=[rows, rows, rows, any_spec, any_spec],
        out_specs=pl.BlockSpec((t, d), lambda b: (b, 0)),
        out_shape=jax.ShapeDtypeStruct((n_streams * t, d), BF16),
        scratch_shapes=[pltpu.VMEM((d // LANES, 2 * t, LANES), BF16), pltpu.VMEM((d // LANES, 2 * t, ATTN_TK), F32),
                        pltpu.VMEM((t, d), F32),
                        pltpu.VMEM((2, ATTN_TK * N_HEADS, hd), F32), pltpu.VMEM((2, ATTN_TK * N_HEADS, hd), F32),
                        pltpu.SemaphoreType.DMA((2, 2))],
        compiler_params=_params(("arbitrary",)),
        name="stick_breaking_sample",
    )(q, k_new, v_new, cache_k, cache_v)


def kernel(x_prompt, x_sample, state_pool, cache_k, cache_v, norm_mix, norm_ffn, pool_w, pool_scale, w_qkv, w_o, router_w, router_b, moe_w1, moe_b1, moe_w2, moe_b2, final_norm):
    b, s, d = x_prompt.shape
    db, t, _ = x_sample.shape
    past = cache_k.shape[2]
    n_exp = router_w.shape[2]
    hd = d // N_HEADS
    hist = POOL_HIST_ROWS
    n_p, n_s = b * s, db * t
    assert t >= hist and ROW_TILE % t == 0 and s % ROW_TILE == 0 and n_s % ROW_TILE == 0 and ROW_TILE % QKV_TILE == 0
    row = lambda a: a.reshape(1, -1)
    wr = [router_w[i].T for i in range(2)]
    br = [router_b[i].reshape(n_exp, 1) for i in range(2)]

    hist_s = jnp.concatenate([jnp.zeros((db, 1, d), F32), state_pool[0]], axis=1)
    xn, tok, idx, gates, rank, hlast_p, hlast_s, cnt = _pool_layer(
        x_prompt, x_sample, hist_s, past, row(norm_mix[0]), pool_w[0].astype(BF16), row(pool_scale[0]),
        row(norm_ffn[0]), wr[0], br[0])
    x1, h1 = _moe(tok, idx, gates, rank, cnt, xn, row(norm_mix[1]), None, 0, moe_w1, moe_b1[0], moe_w2, moe_b2[0])

    q, kb, vb, kf_p, vf_p, kf_s, vf_s = _qkv(h1, w_qkv[0].astype(BF16), n_p)
    o_p = _attention_prompt(q, kb, vb, b, s)
    head_major = lambda cache: cache.reshape(db * past * N_HEADS, hd)
    o_s = _attention_sample(q, kb, vb, head_major(cache_k), head_major(cache_v), db, t, past, n_p)
    xn, tok, idx, gates, rank, cnt = _proj_layer(o_p, o_s, x1, w_o[0].astype(BF16), row(norm_ffn[1]), wr[1], br[1])
    y_p, y_s = _moe(tok, idx, gates, rank, cnt, xn, row(final_norm), n_p, 1, moe_w1, moe_b1[1], moe_w2, moe_b2[1])

    heads = lambda a, nb, l: a.reshape(1, nb, l, N_HEADS, hd)
    return (y_p.reshape(b, s, d), y_s.reshape(db, t, d), hlast_p[None, :, 1:, :],
            heads(kf_p, b, s), heads(vf_p, b, s), hlast_s[None, :, 1:, :],
            heads(kf_s, db, t), heads(vf_s, db, t))
```

```python
import functools

import jax
import jax.numpy as jnp
from jax import lax
from jax.experimental import pallas as pl
from jax.experimental.pallas import tpu as pltpu

EPS = 1e-5
POOL_WINDOWS = (2, 4, 8, 16)
POOL_HIST_ROWS = 16
N_HEADS = 16
TOP_K = 4
SWIGLU_LIMIT = 7.0
SWIGLU_ALPHA = 1.702
LANES = 128
ROW_TILE = 512
QKV_TILE = 256
ATTN_TQ = 128
ATTN_TK = 128
ATTN_EXIT = 104.0
VMEM_LIMIT = 56 * 1024 * 1024

F32 = jnp.float32
BF16 = jnp.bfloat16


def _rms(x, g):
    ms = jnp.mean(x * x, axis=-1, keepdims=True)
    return x * lax.rsqrt(ms + EPS) * g


def _dot(a, b):
    return jnp.dot(a, b, preferred_element_type=F32)


def _dot_nt(a, b, precision=None):
    return lax.dot_general(a, b, (((1,), (1,)), ((), ())), preferred_element_type=F32, precision=precision)


def _params(semantics):
    return pltpu.CompilerParams(dimension_semantics=semantics, vmem_limit_bytes=VMEM_LIMIT)


def _full(shape):
    return pl.BlockSpec(shape, lambda i, *_: (0,) * len(shape))


def _route_tail(xn, gffn_ref, wr_ref, br_ref, cnt_ref, tok_ref, idx_ref, gate_ref, rank_ref):
    r = xn.shape[0]
    n_exp = wr_ref.shape[0]
    tok = _rms(xn, gffn_ref[...])
    tok_ref[...] = tok.astype(tok_ref.dtype)
    logits = _dot_nt(wr_ref[...], tok, precision=lax.Precision.HIGHEST) + br_ref[...]
    eidx = lax.broadcasted_iota(jnp.int32, logits.shape, 0).astype(F32)
    vals, idxs = [], []
    l = logits
    for _ in range(TOP_K):
        m = jnp.max(l, axis=0, keepdims=True)
        i = jnp.min(jnp.where(l == m, eidx, float(n_exp)), axis=0, keepdims=True)
        vals.append(m)
        idxs.append(i)
        l = jnp.where(eidx == i, -jnp.inf, l)
    es = [jnp.exp(v - vals[0]) for v in vals]
    den = es[0]
    for e in es[1:]:
        den = den + e
    gate_ref[...] = jnp.concatenate([e / den for e in es], axis=0)
    idx_ref[...] = jnp.concatenate(idxs, axis=0).astype(jnp.int32)
    member = jnp.zeros(logits.shape, F32)
    for i in idxs:
        member = member + jnp.where(eidx == i, 1.0, 0.0)
    tri = jnp.where(lax.broadcasted_iota(jnp.int32, (r, r), 0) < lax.broadcasted_iota(jnp.int32, (r, r), 1),
                    1.0, 0.0).astype(BF16)
    before = _dot(member.astype(BF16), tri) + cnt_ref[:, :1]
    ranks = [jnp.sum(jnp.where(eidx == i, before, 0.0), axis=0, keepdims=True) for i in idxs]
    rank_ref[...] = jnp.concatenate(ranks, axis=0).astype(jnp.int32)
    cnt_ref[...] = cnt_ref[...] + jnp.sum(member, axis=1, keepdims=True)


def _pool_mix(h, ext_ref, pos, pw_ref, ps_ref):
    ts, d = h.shape[-2:]
    hist = POOL_HIST_ROWS
    group = d // len(POOL_WINDOWS)
    pre = (slice(None),) * (h.ndim - 2)
    ys = []
    for g, win in enumerate(POOL_WINDOWS):
        cols = slice(g * group, (g + 1) * group)
        hg = h[pre + (slice(None), cols)]
        acc = hg
        for j in range(1, win):
            acc = acc + ext_ref[pre + (slice(hist - j, hist - j + ts), cols)]
        cnt = jnp.minimum(pos + 1, win).astype(F32)
        dg = acc / cnt - hg
        ys.append(_dot(dg.reshape(-1, group).astype(BF16), pw_ref[g]))
    return jnp.concatenate(ys, axis=-1) * ps_ref[...]


def _pool_kernel(ntp, nt, pos0_s, xp_ref, xs_ref, hist_ref, gmix_ref, pw_ref, ps_ref, gffn_ref, wr_ref, br_ref,
                 xn_ref, tok_ref, idx_ref, gate_ref, rank_ref, hlast_p_ref, hlast_s_ref, cnt_out_ref,
                 ext_p, ext_s, cnt_ref):
    i = pl.program_id(0)
    hist = POOL_HIST_ROWS

    @pl.when(i == 0)
    def _():
        cnt_ref[...] = jnp.zeros(cnt_ref.shape, F32)

    @pl.when(i < ntp)
    def _():
        ts, d = xp_ref.shape
        t = lax.rem(i, nt)

        @pl.when(t == 0)
        def _():
            ext_p[0:hist, :] = jnp.zeros((hist, d), F32)

        x = xp_ref[...]
        h = _rms(x, gmix_ref[...])
        ext_p[hist:hist + ts, :] = h

        @pl.when(t == nt - 1)
        def _():
            hlast_p_ref[0] = h[ts - hist:, :]

        pos = t * ts + lax.broadcasted_iota(jnp.int32, (ts, 1), 0)
        xn_ref[...] = x + _pool_mix(h, ext_p, pos, pw_ref, ps_ref)
        ext_p[0:hist, :] = ext_p[ts:ts + hist, :]

    @pl.when(i >= ntp)
    def _():
        bb, ts, d = xs_ref.shape
        ext_s[:, 0:hist, :] = hist_ref[...]
        x = xs_ref[...]
        h = _rms(x, gmix_ref[...])
        ext_s[:, hist:hist + ts, :] = h
        hlast_s_ref[...] = h[:, ts - hist:, :]
        pos = pos0_s + lax.broadcasted_iota(jnp.int32, (1, ts, 1), 1)
        xn_ref[...] = x.reshape(bb * ts, d) + _pool_mix(h, ext_s, pos, pw_ref, ps_ref)

    _route_tail(xn_ref[...], gffn_ref, wr_ref, br_ref, cnt_ref, tok_ref, idx_ref, gate_ref, rank_ref)
    cnt_out_ref[...] = cnt_ref[...]


def _pool_layer(x_prompt, x_sample, hist_s, pos0_s, gmix, pw, ps, gffn, wr, br):
    b, s, d = x_prompt.shape
    db, t, _ = x_sample.shape
    n_exp = wr.shape[0]
    r = ROW_TILE
    nt = s // r
    ntp = b * nt
    bb = r // t
    nts = db // bb
    n = b * s + db * t
    hist = POOL_HIST_ROWS
    rows = pl.BlockSpec((r, d), lambda i: (i, 0))
    lanes = pl.BlockSpec((TOP_K, r), lambda i: (0, i))
    sample_blk = lambda i: (jnp.maximum(i - ntp, 0), 0, 0)
    return pl.pallas_call(
        functools.partial(_pool_kernel, ntp, nt, pos0_s),
        grid=(ntp + nts,),
        in_specs=[pl.BlockSpec((r, d), lambda i: (jnp.minimum(i, ntp - 1), 0)),
                  pl.BlockSpec((bb, t, d), sample_blk), pl.BlockSpec((bb, hist, d), sample_blk),
                  _full((1, d)), _full(pw.shape), _full((1, d)), _full((1, d)), _full((n_exp, d)), _full((n_exp, 1))],
        out_specs=[rows, rows, lanes, lanes, lanes,
                   pl.BlockSpec((1, hist, d), lambda i: (jnp.minimum(i // nt, b - 1), 0, 0)),
                   pl.BlockSpec((bb, hist, d), sample_blk), _full((n_exp, LANES))],
        out_shape=[jax.ShapeDtypeStruct((n, d), F32), jax.ShapeDtypeStruct((n, d), F32),
                   jax.ShapeDtypeStruct((TOP_K, n), jnp.int32), jax.ShapeDtypeStruct((TOP_K, n), F32),
                   jax.ShapeDtypeStruct((TOP_K, n), jnp.int32),
                   jax.ShapeDtypeStruct((b, hist, d), F32), jax.ShapeDtypeStruct((db, hist, d), F32),
                   jax.ShapeDtypeStruct((n_exp, LANES), F32)],
        scratch_shapes=[pltpu.VMEM((hist + r, d), F32), pltpu.VMEM((bb, hist + t, d), F32),
                        pltpu.VMEM((n_exp, LANES), F32)],
        compiler_params=_params(("arbitrary",)),
        name="pool_route",
    )(x_prompt.reshape(b * s, d), x_sample, hist_s, gmix, pw, ps, gffn, wr, br)


def _proj_kernel(ntp, op_ref, os_ref, x_ref, wo_ref, gffn_ref, wr_ref, br_ref,
                 xn_ref, tok_ref, idx_ref, gate_ref, rank_ref, cnt_out_ref, cnt_ref):
    i = pl.program_id(0)

    @pl.when(i == 0)
    def _():
        cnt_ref[...] = jnp.zeros(cnt_ref.shape, F32)

    o = jnp.where(i < ntp, op_ref[...], os_ref[...])
    xn = x_ref[...] + _dot(o, wo_ref[...])
    xn_ref[...] = xn
    _route_tail(xn, gffn_ref, wr_ref, br_ref, cnt_ref, tok_ref, idx_ref, gate_ref, rank_ref)
    cnt_out_ref[...] = cnt_ref[...]


def _proj_layer(o_p, o_s, x, wo, gffn, wr, br):
    n, d = x.shape
    n_exp = wr.shape[0]
    r = ROW_TILE
    ntp = o_p.shape[0] // r
    rows = pl.BlockSpec((r, d), lambda i: (i, 0))
    lanes = pl.BlockSpec((TOP_K, r), lambda i: (0, i))
    return pl.pallas_call(
        functools.partial(_proj_kernel, ntp),
        grid=(n // r,),
        in_specs=[pl.BlockSpec((r, d), lambda i: (jnp.minimum(i, ntp - 1), 0)),
                  pl.BlockSpec((r, d), lambda i: (jnp.maximum(i - ntp, 0), 0)),
                  rows, _full((d, d)), _full((1, d)), _full((n_exp, d)), _full((n_exp, 1))],
        out_specs=[rows, rows, lanes, lanes, lanes, _full((n_exp, LANES))],
        out_shape=[jax.ShapeDtypeStruct((n, d), F32), jax.ShapeDtypeStruct((n, d), F32),
                   jax.ShapeDtypeStruct((TOP_K, n), jnp.int32), jax.ShapeDtypeStruct((TOP_K, n), F32),
                   jax.ShapeDtypeStruct((TOP_K, n), jnp.int32), jax.ShapeDtypeStruct((n_exp, LANES), F32)],
        scratch_shapes=[pltpu.VMEM((n_exp, LANES), F32)],
        compiler_params=_params(("arbitrary",)),
        name="proj_route",
    )(o_p, o_s, x, wo, gffn, wr, br)


def _dispatch_kernel(tm, zflag_ref, dest_hbm, tok_ref, buf_ref, idx_smem, zeros_ref, sem_idx, sem_rows, sem_zero):
    i = pl.program_id(0)
    r = tok_ref.shape[0]
    n_blocks = zflag_ref.shape[0]

    def zero_copy(blk):
        return pltpu.make_async_copy(zeros_ref, buf_ref.at[pl.ds(pl.multiple_of(blk * tm, tm), tm), :], sem_zero)

    @pl.when(i == 0)
    def _():
        zeros_ref[...] = jnp.zeros(zeros_ref.shape, zeros_ref.dtype)

        def start(blk, carry):
            @pl.when(zflag_ref[blk] != 0)
            def _():
                zero_copy(blk).start()
            return carry

        def wait(blk, carry):
            @pl.when(zflag_ref[blk] != 0)
            def _():
                zero_copy(blk).wait()
            return carry

        lax.fori_loop(0, n_blocks, start, 0)
        lax.fori_loop(0, n_blocks, wait, 0)

    cp = pltpu.make_async_copy(dest_hbm.at[i], idx_smem, sem_idx)
    cp.start()
    cp.wait()

    def body(row, carry):
        for k in range(TOP_K):
            dst = idx_smem[k * r + row]
            pltpu.make_async_copy(tok_ref.at[pl.ds(row, 1), :], buf_ref.at[pl.ds(dst, 1), :],
                                  sem_rows).start(priority=k % 2)
        return carry

    lax.fori_loop(0, r, body, 0, unroll=8)
    for k in range(TOP_K):
        pltpu.make_async_copy(tok_ref, buf_ref.at[pl.ds(0, r), :], sem_rows).wait()


def _dispatch(tok, dest_tiles, zflag, tm):
    n, d = tok.shape
    r = ROW_TILE
    n_blocks = zflag.shape[0]
    any_spec = pl.BlockSpec(memory_space=pl.ANY)
    return pl.pallas_call(
        functools.partial(_dispatch_kernel, tm),
        grid_spec=pltpu.PrefetchScalarGridSpec(
            num_scalar_prefetch=1,
            grid=(n // r,),
            in_specs=[any_spec, pl.BlockSpec((r, d), lambda i, zf: (i, 0))],
            out_specs=any_spec,
            scratch_shapes=[pltpu.SMEM((TOP_K * r,), jnp.int32), pltpu.VMEM((tm, d), tok.dtype),
                            pltpu.SemaphoreType.DMA, pltpu.SemaphoreType.DMA, pltpu.SemaphoreType.DMA]),
        out_shape=jax.ShapeDtypeStruct((n_blocks * tm, d), tok.dtype),
        compiler_params=_params(("arbitrary",)),
        name="moe_dispatch",
    )(zflag, dest_tiles, tok)


def _expert_kernel(be_ref, nused_ref, x_ref, w1_ref, b1_ref, w2_ref, b2_ref, o_ref, w1b_ref, w2b_ref):
    i = pl.program_id(0)
    used = i < nused_ref[0]

    @pl.when(used & ((i == 0) | (be_ref[i] != be_ref[jnp.maximum(i - 1, 0)])))
    def _():
        w1b_ref[...] = w1_ref[0, 0].astype(BF16)
        w2b_ref[...] = w2_ref[0, 0].astype(BF16)

    @pl.when(used)
    def _():
        f = w2b_ref.shape[0]
        gu = _dot(x_ref[...].astype(BF16), w1b_ref[...]) + b1_ref[0]
        gate = jnp.minimum(gu[:, :f], SWIGLU_LIMIT)
        up = jnp.clip(gu[:, f:], -SWIGLU_LIMIT, SWIGLU_LIMIT)
        hid = (up + 1.0) * (gate * jax.nn.sigmoid(gate * SWIGLU_ALPHA))
        o_ref[...] = _dot(hid.astype(BF16), w2b_ref[...]) + b2_ref[0]

    @pl.when(jnp.logical_not(used))
    def _():
        o_ref[...] = jnp.zeros(o_ref.shape, o_ref.dtype)


def _experts(buf, block_expert, nused, layer, w1, b1, w2, b2, tm):
    rows, d = buf.shape
    _, n_exp, _, f2 = w1.shape
    f = w2.shape[2]
    in_rows = lambda i, be, nu: (jnp.minimum(i, nu[0] - 1), 0)
    by_expert = lambda i, be, nu: (be[i], 0, 0)
    by_layer_expert = lambda i, be, nu: (layer, be[i], 0, 0)
    return pl.pallas_call(
        _expert_kernel,
        grid_spec=pltpu.PrefetchScalarGridSpec(
            num_scalar_prefetch=2,
            grid=(rows // tm,),
            in_specs=[pl.BlockSpec((tm, d), in_rows),
                      pl.BlockSpec((1, 1, d, f2), by_layer_expert), pl.BlockSpec((1, 1, f2), by_expert),
                      pl.BlockSpec((1, 1, f, d), by_layer_expert), pl.BlockSpec((1, 1, d), by_expert)],
            out_specs=pl.BlockSpec((tm, d), lambda i, be, nu: (i, 0)),
            scratch_shapes=[pltpu.VMEM((d, f2), BF16), pltpu.VMEM((f, d), BF16)]),
        out_shape=jax.ShapeDtypeStruct((rows, d), F32),
        compiler_params=_params(("arbitrary",)),
        name="moe_experts",
    )(block_expert, nused, buf, w1, b1.reshape(n_exp, 1, f2), w2, b2.reshape(n_exp, 1, d))


def _combine_kernel(ntp, dest_hbm, y_hbm, gate_ref, x_ref, g_ref, *refs):
    outs, (idx_smem, rows_ref, sem_idx, sem_rows) = refs[:-4], refs[-4:]
    i = pl.program_id(0)
    r = x_ref.shape[0]
    cp = pltpu.make_async_copy(dest_hbm.at[i], idx_smem, sem_idx)
    cp.start()
    cp.wait()

    def body(row, carry):
        for k in range(TOP_K):
            src = idx_smem[k * r + row]
            pltpu.make_async_copy(y_hbm.at[pl.ds(src, 1), :], rows_ref.at[k, pl.ds(row, 1), :],
                                  sem_rows).start(priority=k % 2)
        return carry

    lax.fori_loop(0, r, body, 0, unroll=8)
    gates = jnp.concatenate([gate_ref[...], jnp.zeros((LANES - TOP_K, r), F32)], axis=0)
    gates_t = jnp.transpose(gates)
    for k in range(TOP_K):
        pltpu.make_async_copy(y_hbm.at[pl.ds(0, r), :], rows_ref.at[k], sem_rows).wait()
    f = rows_ref[0] * gates_t[:, 0:1]
    for k in range(1, TOP_K):
        f = f + rows_ref[k] * gates_t[:, k:k + 1]
    xo = x_ref[...] + f
    hn = _rms(xo, g_ref[...])
    if ntp is None:
        xo_ref, hn_ref = outs
        xo_ref[...] = xo
        hn_ref[...] = hn.astype(hn_ref.dtype)
    else:
        hp_ref, hs_ref = outs

        @pl.when(i < ntp)
        def _():
            hp_ref[...] = hn

        @pl.when(i >= ntp)
        def _():
            hs_ref[...] = hn


def _combine(ybuf, dest_tiles, gates, x, g, n_p):
    n, d = x.shape
    r = ROW_TILE
    any_spec = pl.BlockSpec(memory_space=pl.ANY)
    rows = pl.BlockSpec((r, d), lambda i: (i, 0))
    if n_p is None:
        ntp = None
        out_specs = [rows, rows]
        out_shape = [jax.ShapeDtypeStruct((n, d), F32), jax.ShapeDtypeStruct((n, d), BF16)]
    else:
        ntp = n_p // r
        out_specs = [pl.BlockSpec((r, d), lambda i: (jnp.minimum(i, ntp - 1), 0)),
                     pl.BlockSpec((r, d), lambda i: (jnp.maximum(i - ntp, 0), 0))]
        out_shape = [jax.ShapeDtypeStruct((n_p, d), F32), jax.ShapeDtypeStruct((n - n_p, d), F32)]
    return pl.pallas_call(
        functools.partial(_combine_kernel, ntp),
        grid=(n // r,),
        in_specs=[any_spec, any_spec, pl.BlockSpec((TOP_K, r), lambda i: (0, i)), rows, _full((1, d))],
        out_specs=out_specs,
        out_shape=out_shape,
        scratch_shapes=[pltpu.SMEM((TOP_K * r,), jnp.int32), pltpu.VMEM((TOP_K, r, d), F32),
                        pltpu.SemaphoreType.DMA, pltpu.SemaphoreType.DMA],
        compiler_params=_params(("arbitrary",)),
        name="moe_combine",
    )(dest_tiles, ybuf, gates, x, g)


def _moe(tok, idx, gates, rank, cnt, x, g_next, n_p, layer, w1, b1, w2, b2):
    n, d = tok.shape
    n_exp = w1.shape[1]
    pairs = n * TOP_K
    tm = 128
    for cand in (512, 256):
        if pairs >= 4 * cand * n_exp and pairs % cand == 0:
            tm = cand
            break
    n_blocks = -(-pairs // tm) + n_exp
    counts = cnt[:, 0].astype(jnp.int32)
    padded = (counts + tm - 1) // tm * tm
    pad_end = jnp.cumsum(padded)
    pad_start = pad_end - padded
    experts = jnp.arange(n_exp, dtype=jnp.int32)
    dest = rank + jnp.sum(jnp.where(idx[:, :, None] == experts, pad_start, 0), axis=-1)
    starts = jnp.arange(n_blocks, dtype=jnp.int32) * tm
    nused = (pad_end[-1:] // tm).astype(jnp.int32)
    block_expert = jnp.minimum(jnp.sum((pad_end[None, :] <= starts[:, None]).astype(jnp.int32), axis=1), n_exp - 1)
    last_of_expert = jnp.any((starts[:, None] + tm == pad_end[None, :]) & (padded[None, :] > 0), axis=1)
    zflag = (last_of_expert | (starts >= pad_end[-1])).astype(jnp.int32)
    r = ROW_TILE
    dest_tiles = dest.reshape(TOP_K, n // r, r).transpose(1, 0, 2).reshape(n // r, TOP_K * r)
    buf = _dispatch(tok, dest_tiles, zflag, tm)
    ybuf = _experts(buf, block_expert, nused, layer, w1, b1, w2, b2, tm)
    return _combine(ybuf, dest_tiles, gates, x, g_next, n_p)


def _qkv_kernel(ntp, h_ref, wq_ref, wkt_ref, wvt_ref, q_ref, ktb_ref, vtb_ref, ktf_ref, vtf_ref,
                ksb_ref, vsb_ref, ksf_ref, vsf_ref):
    i = pl.program_id(0)
    r, d = h_ref.shape
    tk = ktb_ref.shape[-1]
    h = h_ref[...]
    q_ref[...] = (_dot(h, wq_ref[...]) * ((d // N_HEADS) ** -0.5)).astype(BF16)

    @pl.when(i < ntp)
    def _():
        for wt_ref, tb_ref, tf_ref in ((wkt_ref, ktb_ref, ktf_ref), (wvt_ref, vtb_ref, vtf_ref)):
            xt = _dot_nt(wt_ref[...], h)
            tf_ref[...] = xt
            for c in range(r // tk):
                tb_ref[0, c] = xt[:, c * tk:(c + 1) * tk].astype(BF16)

    @pl.when(i >= ntp)
    def _():
        for wt_ref, sb_ref, sf_ref in ((wkt_ref, ksb_ref, ksf_ref), (wvt_ref, vsb_ref, vsf_ref)):
            x = _dot_nt(h, wt_ref[...])
            sf_ref[...] = x
            sb_ref[...] = x.astype(BF16)


def _qkv(h, wq, wkt, wvt, n_streams, length):
    n, d = h.shape
    r = QKV_TILE
    tk = ATTN_TK
    n_p = n_streams * length
    nt = length // r
    ntp = n_p // r
    rows = pl.BlockSpec((r, d), lambda i: (i, 0))
    prompt_blk = lambda i: (jnp.minimum(i, ntp - 1) // nt, lax.rem(jnp.minimum(i, ntp - 1), nt))
    t_blocks = pl.BlockSpec((1, r // tk, d, tk), lambda i: prompt_blk(i) + (0, 0))
    t_full = pl.BlockSpec((d, r), prompt_blk)
    sample = pl.BlockSpec((r, d), lambda i: (jnp.maximum(i - ntp, 0), 0))
    w_spec = _full((d, d))
    return pl.pallas_call(
        functools.partial(_qkv_kernel, ntp),
        grid=(n // r,),
        in_specs=[rows, w_spec, w_spec, w_spec],
        out_specs=[rows, t_blocks, t_blocks, t_full, t_full, sample, sample, sample, sample],
        out_shape=[jax.ShapeDtypeStruct((n, d), BF16)]
        + [jax.ShapeDtypeStruct((n_streams, length // tk, d, tk), BF16)] * 2
        + [jax.ShapeDtypeStruct((n_streams * d, length), F32)] * 2
        + [jax.ShapeDtypeStruct((n - n_p, d), BF16)] * 2 + [jax.ShapeDtypeStruct((n - n_p, d), F32)] * 2,
        compiler_params=_params(("arbitrary",)),
        name="qkv",
    )(h, wq, wkt, wvt)


def _attn_setup(q_ref, qm_ref, carry_ref, acc_ref):
    tq, d = q_ref.shape
    lane = lax.broadcasted_iota(jnp.int32, (tq, LANES), 1)
    for p in range(d // LANES):
        qp = q_ref[:, p * LANES:(p + 1) * LANES]
        qm_ref[p, 0:tq, :] = jnp.where(lane < LANES // 2, qp, jnp.zeros_like(qp))
        qm_ref[p, tq:2 * tq, :] = jnp.where(lane >= LANES // 2, qp, jnp.zeros_like(qp))
    carry_ref[...] = jnp.zeros(carry_ref.shape, F32)
    acc_ref[...] = jnp.zeros(acc_ref.shape, F32)


def _suffix_sum_matrix(tk):
    src = lax.rem(lax.broadcasted_iota(jnp.int32, (2 * tk, 2 * tk), 0), tk)
    dst = lax.broadcasted_iota(jnp.int32, (2 * tk, 2 * tk), 1)
    return jnp.where((dst >= tk) | (src > dst), 1.0, 0.0).astype(BF16)


def _attn_block(qm_ref, carry_ref, acc_ref, keys, values, transposed, mask, sums):
    n_pairs, tq2, tk = carry_ref.shape
    tq = tq2 // 2
    lane = lax.broadcasted_iota(jnp.int32, (tq, LANES), 1)
    visible = (lambda x: x) if mask is None else (lambda x: jnp.where(mask, x, 0.0))
    scores = _dot if transposed else _dot_nt
    mix = _dot_nt if transposed else _dot
    zs = [scores(qm_ref[p], keys[p]) for p in range(n_pairs)]
    log_beta, parts = [], []
    for z in zs:
        sp = jnp.maximum(z, 0.0) + jnp.log(1.0 + jnp.exp(-jnp.abs(z)))
        log_keep = visible(-sp)
        hi = log_keep.astype(BF16)
        lo = (log_keep - hi.astype(F32)).astype(BF16)
        parts.append(jnp.concatenate([hi, lo], axis=1))
        log_beta.append(z - sp)
    sums_out = [_dot(part, sums) for part in parts]
    weights = []
    top = jnp.full((tq2, tk), -jnp.inf, F32)
    for p in range(n_pairs):
        carry = carry_ref[p]
        a = visible(jnp.exp(log_beta[p] + sums_out[p][:, :tk] + carry))
        weights.append(a.astype(BF16))
        carry = carry + sums_out[p][:, tk:]
        carry_ref[p] = carry
        top = jnp.maximum(top, carry)
    for p in range(n_pairs):
        out = mix(weights[p], values[p])
        acc_ref[:, p * LANES:(p + 1) * LANES] += jnp.where(lane < LANES // 2, out[:tq], out[tq:])
    return jnp.max(top)


def _keep_sweeping(state):
    j, top = state
    return (j >= 0) & (top > -ATTN_EXIT)


def _attn_prompt_kernel(q_ref, kt_ref, vt_ref, o_ref, qm_ref, carry_ref, acc_ref):
    qi = pl.program_id(1)
    tq, d = q_ref.shape
    tk = kt_ref.shape[-1]
    n_pairs = d // LANES
    _attn_setup(q_ref, qm_ref, carry_ref, acc_ref)
    row_pos = qi * tq + lax.rem(lax.broadcasted_iota(jnp.int32, (2 * tq, tk), 0), tq)
    col = lax.broadcasted_iota(jnp.int32, (2 * tq, tk), 1)
    sums = _suffix_sum_matrix(tk)

    def block(j, mask):
        keys = [kt_ref[0, j, p * LANES:(p + 1) * LANES, :] for p in range(n_pairs)]
        values = [vt_ref[0, j, p * LANES:(p + 1) * LANES, :] for p in range(n_pairs)]
        return _attn_block(qm_ref, carry_ref, acc_ref, keys, values, True, mask, sums)

    j0 = ((qi + 1) * tq - 2) // tk
    top = block(j0, (j0 * tk + col) < row_pos)
    lax.while_loop(_keep_sweeping, lambda state: (state[0] - 1, block(state[0], None)), (j0 - 1, top))
    o_ref[...] = acc_ref[...].astype(o_ref.dtype)


def _attention_prompt(q, kt, vt):
    n_streams, n_kblocks, d, tk = kt.shape
    length = n_kblocks * tk
    tq = min(length, ATTN_TQ)
    assert tk % tq == 0
    nq = length // tq
    kv = pl.BlockSpec((1, n_kblocks, d, tk), lambda b, i: (b, 0, 0, 0))
    rows = pl.BlockSpec((tq, d), lambda b, i: (b * nq + i, 0))
    return pl.pallas_call(
        _attn_prompt_kernel,
        grid=(n_streams, nq),
        in_specs=[rows, kv, kv],
        out_specs=rows,
        out_shape=jax.ShapeDtypeStruct((n_streams * length, d), BF16),
        scratch_shapes=[pltpu.VMEM((d // LANES, 2 * tq, LANES), BF16), pltpu.VMEM((d // LANES, 2 * tq, tk), F32),
                        pltpu.VMEM((tq, d), F32)],
        compiler_params=_params(("arbitrary", "arbitrary")),
        name="stick_breaking_prompt",
    )(q, kt, vt)


def _attn_sample_kernel(q_ref, kn_ref, vn_ref, ck_hbm, cv_hbm, o_ref, qm_ref, carry_ref, acc_ref, kbuf, vbuf, sem):
    b = pl.program_id(0)
    t, d = q_ref.shape
    tk = kbuf.shape[-1]
    n_pairs = d // LANES
    n_cache_blocks = ck_hbm.shape[1] // tk

    def fetch(j, slot):
        src = (pl.ds(pl.multiple_of(b * d, d), d), pl.ds(pl.multiple_of(j * tk, tk), tk))
        return (pltpu.make_async_copy(ck_hbm.at[src], kbuf.at[slot], sem.at[0, slot]),
                pltpu.make_async_copy(cv_hbm.at[src], vbuf.at[slot], sem.at[1, slot]))

    for cp in fetch(n_cache_blocks - 1, (n_cache_blocks - 1) % 2):
        cp.start()
    _attn_setup(q_ref, qm_ref, carry_ref, acc_ref)
    row = lax.rem(lax.broadcasted_iota(jnp.int32, (2 * t, tk), 0), t)
    col = lax.broadcasted_iota(jnp.int32, (2 * t, tk), 1)
    sums = _suffix_sum_matrix(tk)
    pad = jnp.zeros((tk - t, LANES), BF16)
    keys = [jnp.concatenate([kn_ref[:, p * LANES:(p + 1) * LANES], pad], axis=0) for p in range(n_pairs)]
    values = [jnp.concatenate([vn_ref[:, p * LANES:(p + 1) * LANES], pad], axis=0) for p in range(n_pairs)]
    top = _attn_block(qm_ref, carry_ref, acc_ref, keys, values, False, col < row, sums)

    def body(state):
        j, _ = state
        slot = lax.rem(j, 2)
        for cp in fetch(j, slot):
            cp.wait()

        @pl.when(j > 0)
        def _():
            for cp in fetch(j - 1, 1 - slot):
                cp.start()

        keys = [kbuf[slot, p * LANES:(p + 1) * LANES, :].astype(BF16) for p in range(n_pairs)]
        values = [vbuf[slot, p * LANES:(p + 1) * LANES, :].astype(BF16) for p in range(n_pairs)]
        return j - 1, _attn_block(qm_ref, carry_ref, acc_ref, keys, values, True, None, sums)

    j_end, _ = lax.while_loop(_keep_sweeping, body, (jnp.int32(n_cache_blocks - 1), top))

    @pl.when(j_end >= 0)
    def _():
        for cp in fetch(j_end, lax.rem(j_end, 2)):
            cp.wait()

    o_ref[...] = acc_ref[...].astype(o_ref.dtype)


def _attention_sample(q, k_new, v_new, cache_kt, cache_vt, t, q_row_off):
    d = q.shape[1]
    n_streams = k_new.shape[0] // t
    tk = ATTN_TK
    assert cache_kt.shape[1] % tk == 0 and t <= tk
    any_spec = pl.BlockSpec(memory_space=pl.ANY)
    new = pl.BlockSpec((t, d), lambda b: (b, 0))
    return pl.pallas_call(
        _attn_sample_kernel,
        grid=(n_streams,),
        in_specs=[pl.BlockSpec((t, d), lambda b: (q_row_off // t + b, 0)), new, new, any_spec, any_spec],
        out_specs=new,
        out_shape=jax.ShapeDtypeStruct((n_streams * t, d), BF16),
        scratch_shapes=[pltpu.VMEM((d // LANES, 2 * t, LANES), BF16), pltpu.VMEM((d // LANES, 2 * t, tk), F32),
                        pltpu.VMEM((t, d), F32), pltpu.VMEM((2, d, tk), F32), pltpu.VMEM((2, d, tk), F32),
                        pltpu.SemaphoreType.DMA((2, 2))],
        compiler_params=_params(("arbitrary",)),
        name="stick_breaking_sample",
    )(q, k_new, v_new, cache_kt, cache_vt)


def kernel(x_prompt, x_sample, state_pool, cache_k, cache_v, norm_mix, norm_ffn, pool_w, pool_scale, w_qkv, w_o, router_w, router_b, moe_w1, moe_b1, moe_w2, moe_b2, final_norm):
    b, s, d = x_prompt.shape
    db, t, _ = x_sample.shape
    past = cache_k.shape[2]
    n_exp = router_w.shape[2]
    hd = d // N_HEADS
    hist = POOL_HIST_ROWS
    n_p, n_s = b * s, db * t
    assert t >= hist and ROW_TILE % t == 0 and s % ROW_TILE == 0 and n_s % ROW_TILE == 0 and ROW_TILE % QKV_TILE == 0
    row = lambda a: a.reshape(1, -1)
    wr = [router_w[i].T for i in range(2)]
    br = [router_b[i].reshape(n_exp, 1) for i in range(2)]

    hist_s = jnp.concatenate([jnp.zeros((db, 1, d), F32), state_pool[0]], axis=1)
    xn, tok, idx, gates, rank, hlast_p, hlast_s, cnt = _pool_layer(
        x_prompt, x_sample, hist_s, past, row(norm_mix[0]), pool_w[0].astype(BF16), row(pool_scale[0]),
        row(norm_ffn[0]), wr[0], br[0])
    x1, h1 = _moe(tok, idx, gates, rank, cnt, xn, row(norm_mix[1]), None, 0, moe_w1, moe_b1[0], moe_w2, moe_b2[0])

    wqkv = w_qkv[0].astype(BF16)
    q, ktb, vtb, ktf, vtf, ksb, vsb, ksf, vsf = _qkv(h1, wqkv[:, :d], wqkv[:, d:2 * d].T, wqkv[:, 2 * d:].T, b, s)
    o_p = _attention_prompt(q, ktb, vtb)
    transposed = lambda cache: cache.transpose(0, 1, 3, 4, 2).reshape(db * d, past)
    o_s = _attention_sample(q, ksb, vsb, transposed(cache_k), transposed(cache_v), t, n_p)
    xn, tok, idx, gates, rank, cnt = _proj_layer(o_p, o_s, x1, w_o[0].astype(BF16), row(norm_ffn[1]), wr[1], br[1])
    y_p, y_s = _moe(tok, idx, gates, rank, cnt, xn, row(final_norm), n_p, 1, moe_w1, moe_b1[1], moe_w2, moe_b2[1])

    frames_major = lambda xt: xt.reshape(1, b, N_HEADS, hd, s).transpose(0, 1, 4, 2, 3)
    heads = lambda a: a.reshape(1, db, t, N_HEADS, hd)
    return (y_p.reshape(b, s, d), y_s.reshape(db, t, d), hlast_p[None, :, 1:, :],
            frames_major(ktf), frames_major(vtf), hlast_s[None, :, 1:, :], heads(ksf), heads(vsf))
```

```python
import functools

import jax
import jax.numpy as jnp
from jax import lax
from jax.experimental import pallas as pl
from jax.experimental.pallas import tpu as pltpu
from jax.experimental.pallas import tpu_sc as plsc

EPS = 1e-5
POOL_WINDOWS = (2, 4, 8, 16)
POOL_HIST_ROWS = 16
N_HEADS = 16
TOP_K = 4
SWIGLU_LIMIT = 7.0
SWIGLU_ALPHA = 1.702
LANES = 128
ROW_TILE = 512
QKV_TILE = 256
ATTN_TQ = 128
ATTN_TK = 128
ATTN_EXIT = 104.0
VMEM_LIMIT = 56 * 1024 * 1024
SC_CORES = 2
SC_SUBCORES = 16
SC_CHUNK = 96

F32 = jnp.float32
BF16 = jnp.bfloat16


def _rms(x, g):
    ms = jnp.mean(x * x, axis=-1, keepdims=True)
    return x * lax.rsqrt(ms + EPS) * g


def _dot(a, b):
    return jnp.dot(a, b, preferred_element_type=F32)


def _dot_nt(a, b, precision=None):
    return lax.dot_general(a, b, (((1,), (1,)), ((), ())), preferred_element_type=F32, precision=precision)


def _params(semantics):
    return pltpu.CompilerParams(dimension_semantics=semantics, vmem_limit_bytes=VMEM_LIMIT)


def _full(shape):
    return pl.BlockSpec(shape, lambda i, *_: (0,) * len(shape))


def _route_tail(xn, gffn_ref, wr_ref, br_ref, cnt_ref, tok_ref, idx_ref, gate_ref, rank_ref):
    r = xn.shape[0]
    n_exp = wr_ref.shape[0]
    tok = _rms(xn, gffn_ref[...])
    tok_ref[...] = tok.astype(tok_ref.dtype)
    logits = _dot_nt(wr_ref[...], tok, precision=lax.Precision.HIGHEST) + br_ref[...]
    eidx = lax.broadcasted_iota(jnp.int32, logits.shape, 0).astype(F32)
    vals, idxs = [], []
    l = logits
    for _ in range(TOP_K):
        m = jnp.max(l, axis=0, keepdims=True)
        i = jnp.min(jnp.where(l == m, eidx, float(n_exp)), axis=0, keepdims=True)
        vals.append(m)
        idxs.append(i)
        l = jnp.where(eidx == i, -jnp.inf, l)
    es = [jnp.exp(v - vals[0]) for v in vals]
    den = es[0]
    for e in es[1:]:
        den = den + e
    gate_ref[...] = jnp.concatenate([e / den for e in es], axis=0)
    idx_ref[...] = jnp.concatenate(idxs, axis=0).astype(jnp.int32)
    member = jnp.zeros(logits.shape, F32)
    for i in idxs:
        member = member + jnp.where(eidx == i, 1.0, 0.0)
    tri = jnp.where(lax.broadcasted_iota(jnp.int32, (r, r), 0) < lax.broadcasted_iota(jnp.int32, (r, r), 1),
                    1.0, 0.0).astype(BF16)
    before = _dot(member.astype(BF16), tri) + cnt_ref[:, :1]
    ranks = [jnp.sum(jnp.where(eidx == i, before, 0.0), axis=0, keepdims=True) for i in idxs]
    rank_ref[...] = jnp.concatenate(ranks, axis=0).astype(jnp.int32)
    cnt_ref[...] = cnt_ref[...] + jnp.sum(member, axis=1, keepdims=True)


def _pool_mix(h, ext_ref, pos, pw_ref, ps_ref):
    ts, d = h.shape[-2:]
    hist = POOL_HIST_ROWS
    group = d // len(POOL_WINDOWS)
    pre = (slice(None),) * (h.ndim - 2)
    ys = []
    for g, win in enumerate(POOL_WINDOWS):
        cols = slice(g * group, (g + 1) * group)
        hg = h[pre + (slice(None), cols)]
        acc = hg
        for j in range(1, win):
            acc = acc + ext_ref[pre + (slice(hist - j, hist - j + ts), cols)]
        cnt = jnp.minimum(pos + 1, win).astype(F32)
        dg = acc / cnt - hg
        ys.append(_dot(dg.reshape(-1, group).astype(BF16), pw_ref[g]))
    return jnp.concatenate(ys, axis=-1) * ps_ref[...]


def _pool_kernel(ntp, nt, pos0_s, xp_ref, xs_ref, hist_ref, gmix_ref, pw_ref, ps_ref, gffn_ref, wr_ref, br_ref,
                 xn_ref, tok_ref, idx_ref, gate_ref, rank_ref, hlast_p_ref, hlast_s_ref, cnt_out_ref,
                 ext_p, ext_s, cnt_ref):
    i = pl.program_id(0)
    hist = POOL_HIST_ROWS

    @pl.when(i == 0)
    def _():
        cnt_ref[...] = jnp.zeros(cnt_ref.shape, F32)

    @pl.when(i < ntp)
    def _():
        ts, d = xp_ref.shape
        t = lax.rem(i, nt)

        @pl.when(t == 0)
        def _():
            ext_p[0:hist, :] = jnp.zeros((hist, d), F32)

        x = xp_ref[...]
        h = _rms(x, gmix_ref[...])
        ext_p[hist:hist + ts, :] = h

        @pl.when(t == nt - 1)
        def _():
            hlast_p_ref[0] = h[ts - hist:, :]

        pos = t * ts + lax.broadcasted_iota(jnp.int32, (ts, 1), 0)
        xn_ref[...] = x + _pool_mix(h, ext_p, pos, pw_ref, ps_ref)
        ext_p[0:hist, :] = ext_p[ts:ts + hist, :]

    @pl.when(i >= ntp)
    def _():
        bb, ts, d = xs_ref.shape
        ext_s[:, 0:hist, :] = hist_ref[...]
        x = xs_ref[...]
        h = _rms(x, gmix_ref[...])
        ext_s[:, hist:hist + ts, :] = h
        hlast_s_ref[...] = h[:, ts - hist:, :]
        pos = pos0_s + lax.broadcasted_iota(jnp.int32, (1, ts, 1), 1)
        xn_ref[...] = x.reshape(bb * ts, d) + _pool_mix(h, ext_s, pos, pw_ref, ps_ref)

    _route_tail(xn_ref[...], gffn_ref, wr_ref, br_ref, cnt_ref, tok_ref, idx_ref, gate_ref, rank_ref)
    cnt_out_ref[...] = cnt_ref[...]


def _pool_layer(x_prompt, x_sample, hist_s, pos0_s, gmix, pw, ps, gffn, wr, br):
    b, s, d = x_prompt.shape
    db, t, _ = x_sample.shape
    n_exp = wr.shape[0]
    r = ROW_TILE
    nt = s // r
    ntp = b * nt
    bb = r // t
    nts = db // bb
    n = b * s + db * t
    hist = POOL_HIST_ROWS
    rows = pl.BlockSpec((r, d), lambda i: (i, 0))
    lanes = pl.BlockSpec((TOP_K, r), lambda i: (0, i))
    sample_blk = lambda i: (jnp.maximum(i - ntp, 0), 0, 0)
    return pl.pallas_call(
        functools.partial(_pool_kernel, ntp, nt, pos0_s),
        grid=(ntp + nts,),
        in_specs=[pl.BlockSpec((r, d), lambda i: (jnp.minimum(i, ntp - 1), 0)),
                  pl.BlockSpec((bb, t, d), sample_blk), pl.BlockSpec((bb, hist, d), sample_blk),
                  _full((1, d)), _full(pw.shape), _full((1, d)), _full((1, d)), _full((n_exp, d)), _full((n_exp, 1))],
        out_specs=[rows, rows, lanes, lanes, lanes,
                   pl.BlockSpec((1, hist, d), lambda i: (jnp.minimum(i // nt, b - 1), 0, 0)),
                   pl.BlockSpec((bb, hist, d), sample_blk), _full((n_exp, LANES))],
        out_shape=[jax.ShapeDtypeStruct((n, d), F32), jax.ShapeDtypeStruct((n, d), F32),
                   jax.ShapeDtypeStruct((TOP_K, n), jnp.int32), jax.ShapeDtypeStruct((TOP_K, n), F32),
                   jax.ShapeDtypeStruct((TOP_K, n), jnp.int32),
                   jax.ShapeDtypeStruct((b, hist, d), F32), jax.ShapeDtypeStruct((db, hist, d), F32),
                   jax.ShapeDtypeStruct((n_exp, LANES), F32)],
        scratch_shapes=[pltpu.VMEM((hist + r, d), F32), pltpu.VMEM((bb, hist + t, d), F32),
                        pltpu.VMEM((n_exp, LANES), F32)],
        compiler_params=_params(("arbitrary",)),
        name="pool_route",
    )(x_prompt.reshape(b * s, d), x_sample, hist_s, gmix, pw, ps, gffn, wr, br)


def _proj_kernel(ntp, op_ref, os_ref, x_ref, wo_ref, gffn_ref, wr_ref, br_ref,
                 xn_ref, tok_ref, idx_ref, gate_ref, rank_ref, cnt_out_ref, cnt_ref):
    i = pl.program_id(0)

    @pl.when(i == 0)
    def _():
        cnt_ref[...] = jnp.zeros(cnt_ref.shape, F32)

    o = jnp.where(i < ntp, op_ref[...], os_ref[...])
    xn = x_ref[...] + _dot(o, wo_ref[...])
    xn_ref[...] = xn
    _route_tail(xn, gffn_ref, wr_ref, br_ref, cnt_ref, tok_ref, idx_ref, gate_ref, rank_ref)
    cnt_out_ref[...] = cnt_ref[...]


def _proj_layer(o_p, o_s, x, wo, gffn, wr, br):
    n, d = x.shape
    n_exp = wr.shape[0]
    r = ROW_TILE
    ntp = o_p.shape[0] // r
    rows = pl.BlockSpec((r, d), lambda i: (i, 0))
    lanes = pl.BlockSpec((TOP_K, r), lambda i: (0, i))
    return pl.pallas_call(
        functools.partial(_proj_kernel, ntp),
        grid=(n // r,),
        in_specs=[pl.BlockSpec((r, d), lambda i: (jnp.minimum(i, ntp - 1), 0)),
                  pl.BlockSpec((r, d), lambda i: (jnp.maximum(i - ntp, 0), 0)),
                  rows, _full((d, d)), _full((1, d)), _full((n_exp, d)), _full((n_exp, 1))],
        out_specs=[rows, rows, lanes, lanes, lanes, _full((n_exp, LANES))],
        out_shape=[jax.ShapeDtypeStruct((n, d), F32), jax.ShapeDtypeStruct((n, d), F32),
                   jax.ShapeDtypeStruct((TOP_K, n), jnp.int32), jax.ShapeDtypeStruct((TOP_K, n), F32),
                   jax.ShapeDtypeStruct((TOP_K, n), jnp.int32), jax.ShapeDtypeStruct((n_exp, LANES), F32)],
        scratch_shapes=[pltpu.VMEM((n_exp, LANES), F32)],
        compiler_params=_params(("arbitrary",)),
        name="proj_route",
    )(o_p, o_s, x, wo, gffn, wr, br)


def _sc_chunks(dest, w):
    n = dest.shape[1]
    return dest.reshape(TOP_K, n // w, w).transpose(1, 0, 2)


def _dispatch(tok, dest, n_buf_rows):
    n, d = tok.shape
    w = SC_CHUNK
    n_workers = SC_CORES * SC_SUBCORES
    per_worker = n // (w * n_workers)
    assert per_worker * w * n_workers == n
    mesh = plsc.VectorSubcoreMesh(core_axis_name="core", subcore_axis_name="subcore",
                                  num_cores=SC_CORES, num_subcores=SC_SUBCORES)

    @functools.partial(
        pl.kernel, mesh=mesh, out_type=jax.ShapeDtypeStruct((n_buf_rows, d), tok.dtype),
        scratch_types=[pltpu.VMEM((TOP_K, w), jnp.int32), pltpu.VMEM((w, d), tok.dtype), pltpu.SemaphoreType.DMA],
        name="moe_dispatch_sc")
    def scatter_rows(tok_hbm, dest_hbm, buf_hbm, idx_v, rows_v, sem):
        worker = lax.axis_index("subcore") * SC_CORES + lax.axis_index("core")

        @pl.loop(0, per_worker)
        def _(c):
            chunk = worker * per_worker + c
            pltpu.sync_copy(dest_hbm.at[chunk], idx_v)
            pltpu.sync_copy(tok_hbm.at[pl.ds(pl.multiple_of(chunk * w, w), w)], rows_v)
            copies = [pltpu.async_copy(rows_v, buf_hbm.at[idx_v.at[k]], sem) for k in range(TOP_K)]
            for cp in copies:
                cp.wait()

    return scatter_rows(tok, _sc_chunks(dest, w))


def _expert_kernel(be_ref, valid_ref, nused_ref, x_ref, w1_ref, b1_ref, w2_ref, b2_ref, o_ref, w1b_ref, w2b_ref):
    i = pl.program_id(0)
    used = valid_ref[i] > 0

    @pl.when(used & ((i == 0) | (be_ref[i] != be_ref[jnp.maximum(i - 1, 0)])))
    def _():
        w1b_ref[...] = w1_ref[0, 0].astype(BF16)
        w2b_ref[...] = w2_ref[0, 0].astype(BF16)

    @pl.when(used)
    def _():
        f = w2b_ref.shape[0]
        row = lax.broadcasted_iota(jnp.int32, (x_ref.shape[0], 1), 0)
        x = jnp.where(row < valid_ref[i], x_ref[...], 0.0)
        gu = _dot(x.astype(BF16), w1b_ref[...]) + b1_ref[0]
        gate = jnp.minimum(gu[:, :f], SWIGLU_LIMIT)
        up = jnp.clip(gu[:, f:], -SWIGLU_LIMIT, SWIGLU_LIMIT)
        hid = (up + 1.0) * (gate * jax.nn.sigmoid(gate * SWIGLU_ALPHA))
        o_ref[...] = _dot(hid.astype(BF16), w2b_ref[...]) + b2_ref[0]

    @pl.when(jnp.logical_not(used))
    def _():
        o_ref[...] = jnp.zeros(o_ref.shape, o_ref.dtype)


def _experts(buf, block_expert, valid, nused, layer, w1, b1, w2, b2, tm):
    rows, d = buf.shape
    _, n_exp, _, f2 = w1.shape
    f = w2.shape[2]
    in_rows = lambda i, be, va, nu: (jnp.minimum(i, nu[0] - 1), 0)
    by_expert = lambda i, be, va, nu: (be[i], 0, 0)
    by_layer_expert = lambda i, be, va, nu: (layer, be[i], 0, 0)
    return pl.pallas_call(
        _expert_kernel,
        grid_spec=pltpu.PrefetchScalarGridSpec(
            num_scalar_prefetch=3,
            grid=(rows // tm,),
            in_specs=[pl.BlockSpec((tm, d), in_rows),
                      pl.BlockSpec((1, 1, d, f2), by_layer_expert), pl.BlockSpec((1, 1, f2), by_expert),
                      pl.BlockSpec((1, 1, f, d), by_layer_expert), pl.BlockSpec((1, 1, d), by_expert)],
            out_specs=pl.BlockSpec((tm, d), lambda i, be, va, nu: (i, 0)),
            scratch_shapes=[pltpu.VMEM((d, f2), BF16), pltpu.VMEM((f, d), BF16)]),
        out_shape=jax.ShapeDtypeStruct((rows, d), F32),
        compiler_params=_params(("arbitrary",)),
        name="moe_experts",
    )(block_expert, valid, nused, buf, w1, b1.reshape(n_exp, 1, f2), w2, b2.reshape(n_exp, 1, d))


def _combine_kernel(ntp, dest_hbm, y_hbm, gate_ref, x_ref, g_ref, *refs):
    outs, (idx_smem, rows_ref, sem_idx, sem_rows) = refs[:-4], refs[-4:]
    i = pl.program_id(0)
    r = x_ref.shape[0]
    cp = pltpu.make_async_copy(dest_hbm.at[i], idx_smem, sem_idx)
    cp.start()
    cp.wait()

    def body(row, carry):
        for k in range(TOP_K):
            src = idx_smem[k * r + row]
            pltpu.make_async_copy(y_hbm.at[pl.ds(src, 1), :], rows_ref.at[k, pl.ds(row, 1), :],
                                  sem_rows).start(priority=k % 2)
        return carry

    lax.fori_loop(0, r, body, 0, unroll=8)
    gates = jnp.concatenate([gate_ref[...], jnp.zeros((LANES - TOP_K, r), F32)], axis=0)
    gates_t = jnp.transpose(gates)
    for k in range(TOP_K):
        pltpu.make_async_copy(y_hbm.at[pl.ds(0, r), :], rows_ref.at[k], sem_rows).wait()
    f = rows_ref[0] * gates_t[:, 0:1]
    for k in range(1, TOP_K):
        f = f + rows_ref[k] * gates_t[:, k:k + 1]
    xo = x_ref[...] + f
    hn = _rms(xo, g_ref[...])
    if ntp is None:
        xo_ref, hn_ref = outs
        xo_ref[...] = xo
        hn_ref[...] = hn.astype(hn_ref.dtype)
    else:
        hp_ref, hs_ref = outs

        @pl.when(i < ntp)
        def _():
            hp_ref[...] = hn

        @pl.when(i >= ntp)
        def _():
            hs_ref[...] = hn


def _combine(ybuf, dest_tiles, gates, x, g, n_p):
    n, d = x.shape
    r = ROW_TILE
    any_spec = pl.BlockSpec(memory_space=pl.ANY)
    rows = pl.BlockSpec((r, d), lambda i: (i, 0))
    if n_p is None:
        ntp = None
        out_specs = [rows, rows]
        out_shape = [jax.ShapeDtypeStruct((n, d), F32), jax.ShapeDtypeStruct((n, d), BF16)]
    else:
        ntp = n_p // r
        out_specs = [pl.BlockSpec((r, d), lambda i: (jnp.minimum(i, ntp - 1), 0)),
                     pl.BlockSpec((r, d), lambda i: (jnp.maximum(i - ntp, 0), 0))]
        out_shape = [jax.ShapeDtypeStruct((n_p, d), F32), jax.ShapeDtypeStruct((n - n_p, d), F32)]
    return pl.pallas_call(
        functools.partial(_combine_kernel, ntp),
        grid=(n // r,),
        in_specs=[any_spec, any_spec, pl.BlockSpec((TOP_K, r), lambda i: (0, i)), rows, _full((1, d))],
        out_specs=out_specs,
        out_shape=out_shape,
        scratch_shapes=[pltpu.SMEM((TOP_K * r,), jnp.int32), pltpu.VMEM((TOP_K, r, d), F32),
                        pltpu.SemaphoreType.DMA, pltpu.SemaphoreType.DMA],
        compiler_params=_params(("arbitrary",)),
        name="moe_combine",
    )(dest_tiles, ybuf, gates, x, g)


def _moe(tok, idx, gates, rank, cnt, x, g_next, n_p, layer, w1, b1, w2, b2):
    n, d = tok.shape
    n_exp = w1.shape[1]
    pairs = n * TOP_K
    tm = 128
    for cand in (512, 256):
        if pairs >= 4 * cand * n_exp and pairs % cand == 0:
            tm = cand
            break
    n_blocks = -(-pairs // tm) + n_exp
    counts = cnt[:, 0].astype(jnp.int32)
    padded = (counts + tm - 1) // tm * tm
    pad_end = jnp.cumsum(padded)
    pad_start = pad_end - padded
    experts = jnp.arange(n_exp, dtype=jnp.int32)
    dest = rank + jnp.sum(jnp.where(idx[:, :, None] == experts, pad_start, 0), axis=-1)
    starts = jnp.arange(n_blocks, dtype=jnp.int32) * tm
    nused = (pad_end[-1:] // tm).astype(jnp.int32)
    block_expert = jnp.minimum(jnp.sum((pad_end[None, :] <= starts[:, None]).astype(jnp.int32), axis=1), n_exp - 1)
    tokens_end = jnp.sum(jnp.where(block_expert[:, None] == experts, pad_start + counts, 0), axis=-1)
    valid = jnp.clip(tokens_end - starts, 0, tm).astype(jnp.int32)
    r = ROW_TILE
    dest_tiles = dest.reshape(TOP_K, n // r, r).transpose(1, 0, 2).reshape(n // r, TOP_K * r)
    buf = _dispatch(tok, dest, n_blocks * tm)
    ybuf = _experts(buf, block_expert, valid, nused, layer, w1, b1, w2, b2, tm)
    return _combine(ybuf, dest_tiles, gates, x, g_next, n_p)


def _qkv_kernel(ntp, h_ref, wq_ref, wkt_ref, wvt_ref, q_ref, ktb_ref, vtb_ref, ktf_ref, vtf_ref,
                ksb_ref, vsb_ref, ksf_ref, vsf_ref):
    i = pl.program_id(0)
    r, d = h_ref.shape
    tk = ktb_ref.shape[-1]
    h = h_ref[...]
    q_ref[...] = (_dot(h, wq_ref[...]) * ((d // N_HEADS) ** -0.5)).astype(BF16)

    @pl.when(i < ntp)
    def _():
        for wt_ref, tb_ref, tf_ref in ((wkt_ref, ktb_ref, ktf_ref), (wvt_ref, vtb_ref, vtf_ref)):
            xt = _dot_nt(wt_ref[...], h)
            tf_ref[...] = xt
            for c in range(r // tk):
                tb_ref[0, c] = xt[:, c * tk:(c + 1) * tk].astype(BF16)

    @pl.when(i >= ntp)
    def _():
        for wt_ref, sb_ref, sf_ref in ((wkt_ref, ksb_ref, ksf_ref), (wvt_ref, vsb_ref, vsf_ref)):
            x = _dot_nt(h, wt_ref[...])
            sf_ref[...] = x
            sb_ref[...] = x.astype(BF16)


def _qkv(h, wq, wkt, wvt, n_streams, length):
    n, d = h.shape
    r = QKV_TILE
    tk = ATTN_TK
    n_p = n_streams * length
    nt = length // r
    ntp = n_p // r
    rows = pl.BlockSpec((r, d), lambda i: (i, 0))
    prompt_blk = lambda i: (jnp.minimum(i, ntp - 1) // nt, lax.rem(jnp.minimum(i, ntp - 1), nt))
    t_blocks = pl.BlockSpec((1, r // tk, d, tk), lambda i: prompt_blk(i) + (0, 0))
    t_full = pl.BlockSpec((d, r), prompt_blk)
    sample = pl.BlockSpec((r, d), lambda i: (jnp.maximum(i - ntp, 0), 0))
    w_spec = _full((d, d))
    return pl.pallas_call(
        functools.partial(_qkv_kernel, ntp),
        grid=(n // r,),
        in_specs=[rows, w_spec, w_spec, w_spec],
        out_specs=[rows, t_blocks, t_blocks, t_full, t_full, sample, sample, sample, sample],
        out_shape=[jax.ShapeDtypeStruct((n, d), BF16)]
        + [jax.ShapeDtypeStruct((n_streams, length // tk, d, tk), BF16)] * 2
        + [jax.ShapeDtypeStruct((n_streams * d, length), F32)] * 2
        + [jax.ShapeDtypeStruct((n - n_p, d), BF16)] * 2 + [jax.ShapeDtypeStruct((n - n_p, d), F32)] * 2,
        compiler_params=_params(("arbitrary",)),
        name="qkv",
    )(h, wq, wkt, wvt)


def _attn_setup(q_ref, qm_ref, carry_ref, acc_ref):
    tq, d = q_ref.shape
    lane = lax.broadcasted_iota(jnp.int32, (tq, LANES), 1)
    for p in range(d // LANES):
        qp = q_ref[:, p * LANES:(p + 1) * LANES]
        qm_ref[p, 0:tq, :] = jnp.where(lane < LANES // 2, qp, jnp.zeros_like(qp))
        qm_ref[p, tq:2 * tq, :] = jnp.where(lane >= LANES // 2, qp, jnp.zeros_like(qp))
    carry_ref[...] = jnp.zeros(carry_ref.shape, F32)
    acc_ref[...] = jnp.zeros(acc_ref.shape, F32)


def _suffix_sum_matrix(tk):
    src = lax.rem(lax.broadcasted_iota(jnp.int32, (2 * tk, 2 * tk), 0), tk)
    dst = lax.broadcasted_iota(jnp.int32, (2 * tk, 2 * tk), 1)
    return jnp.where((dst >= tk) | (src > dst), 1.0, 0.0).astype(BF16)


def _attn_block(qm_ref, carry_ref, acc_ref, keys, values, transposed, mask, sums):
    n_pairs, tq2, tk = carry_ref.shape
    tq = tq2 // 2
    lane = lax.broadcasted_iota(jnp.int32, (tq, LANES), 1)
    visible = (lambda x: x) if mask is None else (lambda x: jnp.where(mask, x, 0.0))
    scores = _dot if transposed else _dot_nt
    mix = _dot_nt if transposed else _dot
    zs = [scores(qm_ref[p], keys[p]) for p in range(n_pairs)]
    log_beta, parts = [], []
    for z in zs:
        sp = jnp.maximum(z, 0.0) + jnp.log(1.0 + jnp.exp(-jnp.abs(z)))
        log_keep = visible(-sp)
        hi = log_keep.astype(BF16)
        lo = (log_keep - hi.astype(F32)).astype(BF16)
        parts.append(jnp.concatenate([hi, lo], axis=1))
        log_beta.append(z - sp)
    sums_out = [_dot(part, sums) for part in parts]
    weights = []
    top = jnp.full((tq2, tk), -jnp.inf, F32)
    for p in range(n_pairs):
        carry = carry_ref[p]
        a = visible(jnp.exp(log_beta[p] + sums_out[p][:, :tk] + carry))
        weights.append(a.astype(BF16))
        carry = carry + sums_out[p][:, tk:]
        carry_ref[p] = carry
        top = jnp.maximum(top, carry)
    for p in range(n_pairs):
        out = mix(weights[p], values[p])
        acc_ref[:, p * LANES:(p + 1) * LANES] += jnp.where(lane < LANES // 2, out[:tq], out[tq:])
    return jnp.max(top)


def _keep_sweeping(state):
    j, top = state
    return (j >= 0) & (top > -ATTN_EXIT)


def _attn_prompt_kernel(q_ref, kt_ref, vt_ref, o_ref, qm_ref, carry_ref, acc_ref):
    qi = pl.program_id(1)
    tq, d = q_ref.shape
    tk = kt_ref.shape[-1]
    n_pairs = d // LANES
    _attn_setup(q_ref, qm_ref, carry_ref, acc_ref)
    row_pos = qi * tq + lax.rem(lax.broadcasted_iota(jnp.int32, (2 * tq, tk), 0), tq)
    col = lax.broadcasted_iota(jnp.int32, (2 * tq, tk), 1)
    sums = _suffix_sum_matrix(tk)

    def block(j, mask):
        keys = [kt_ref[0, j, p * LANES:(p + 1) * LANES, :] for p in range(n_pairs)]
        values = [vt_ref[0, j, p * LANES:(p + 1) * LANES, :] for p in range(n_pairs)]
        return _attn_block(qm_ref, carry_ref, acc_ref, keys, values, True, mask, sums)

    j0 = ((qi + 1) * tq - 2) // tk
    top = block(j0, (j0 * tk + col) < row_pos)
    lax.while_loop(_keep_sweeping, lambda state: (state[0] - 1, block(state[0], None)), (j0 - 1, top))
    o_ref[...] = acc_ref[...].astype(o_ref.dtype)


def _attention_prompt(q, kt, vt):
    n_streams, n_kblocks, d, tk = kt.shape
    length = n_kblocks * tk
    tq = min(length, ATTN_TQ)
    assert tk % tq == 0
    nq = length // tq
    kv = pl.BlockSpec((1, n_kblocks, d, tk), lambda b, i: (b, 0, 0, 0))
    rows = pl.BlockSpec((tq, d), lambda b, i: (b * nq + i, 0))
    return pl.pallas_call(
        _attn_prompt_kernel,
        grid=(n_streams, nq),
        in_specs=[rows, kv, kv],
        out_specs=rows,
        out_shape=jax.ShapeDtypeStruct((n_streams * length, d), BF16),
        scratch_shapes=[pltpu.VMEM((d // LANES, 2 * tq, LANES), BF16), pltpu.VMEM((d // LANES, 2 * tq, tk), F32),
                        pltpu.VMEM((tq, d), F32)],
        compiler_params=_params(("arbitrary", "arbitrary")),
        name="stick_breaking_prompt",
    )(q, kt, vt)


def _attn_sample_kernel(q_ref, kn_ref, vn_ref, ck_hbm, cv_hbm, o_ref, qm_ref, carry_ref, acc_ref, kbuf, vbuf, sem):
    b = pl.program_id(0)
    t, d = q_ref.shape
    tk = kbuf.shape[-1]
    n_pairs = d // LANES
    n_cache_blocks = ck_hbm.shape[1] // tk

    def fetch(j, slot):
        src = (pl.ds(pl.multiple_of(b * d, d), d), pl.ds(pl.multiple_of(j * tk, tk), tk))
        return (pltpu.make_async_copy(ck_hbm.at[src], kbuf.at[slot], sem.at[0, slot]),
                pltpu.make_async_copy(cv_hbm.at[src], vbuf.at[slot], sem.at[1, slot]))

    for cp in fetch(n_cache_blocks - 1, (n_cache_blocks - 1) % 2):
        cp.start()
    _attn_setup(q_ref, qm_ref, carry_ref, acc_ref)
    row = lax.rem(lax.broadcasted_iota(jnp.int32, (2 * t, tk), 0), t)
    col = lax.broadcasted_iota(jnp.int32, (2 * t, tk), 1)
    sums = _suffix_sum_matrix(tk)
    pad = jnp.zeros((tk - t, LANES), BF16)
    keys = [jnp.concatenate([kn_ref[:, p * LANES:(p + 1) * LANES], pad], axis=0) for p in range(n_pairs)]
    values = [jnp.concatenate([vn_ref[:, p * LANES:(p + 1) * LANES], pad], axis=0) for p in range(n_pairs)]
    top = _attn_block(qm_ref, carry_ref, acc_ref, keys, values, False, col < row, sums)

    def body(state):
        j, _ = state
        slot = lax.rem(j, 2)
        for cp in fetch(j, slot):
            cp.wait()

        @pl.when(j > 0)
        def _():
            for cp in fetch(j - 1, 1 - slot):
                cp.start()

        keys = [kbuf[slot, p * LANES:(p + 1) * LANES, :].astype(BF16) for p in range(n_pairs)]
        values = [vbuf[slot, p * LANES:(p + 1) * LANES, :].astype(BF16) for p in range(n_pairs)]
        return j - 1, _attn_block(qm_ref, carry_ref, acc_ref, keys, values, True, None, sums)

    j_end, _ = lax.while_loop(_keep_sweeping, body, (jnp.int32(n_cache_blocks - 1), top))

    @pl.when(j_end >= 0)
    def _():
        for cp in fetch(j_end, lax.rem(j_end, 2)):
            cp.wait()

    o_ref[...] = acc_ref[...].astype(o_ref.dtype)


def _attention_sample(q, k_new, v_new, cache_kt, cache_vt, t, q_row_off):
    d = q.shape[1]
    n_streams = k_new.shape[0] // t
    tk = ATTN_TK
    assert cache_kt.shape[1] % tk == 0 and t <= tk
    any_spec = pl.BlockSpec(memory_space=pl.ANY)
    new = pl.BlockSpec((t, d), lambda b: (b, 0))
    return pl.pallas_call(
        _attn_sample_kernel,
        grid=(n_streams,),
        in_specs=[pl.BlockSpec((t, d), lambda b: (q_row_off // t + b, 0)), new, new, any_spec, any_spec],
        out_specs=new,
        out_shape=jax.ShapeDtypeStruct((n_streams * t, d), BF16),
        scratch_shapes=[pltpu.VMEM((d // LANES, 2 * t, LANES), BF16), pltpu.VMEM((d // LANES, 2 * t, tk), F32),
                        pltpu.VMEM((t, d), F32), pltpu.VMEM((2, d, tk), F32), pltpu.VMEM((2, d, tk), F32),
                        pltpu.SemaphoreType.DMA((2, 2))],
        compiler_params=_params(("arbitrary",)),
        name="stick_breaking_sample",
    )(q, k_new, v_new, cache_kt, cache_vt)


def kernel(x_prompt, x_sample, state_pool, cache_k, cache_v, norm_mix, norm_ffn, pool_w, pool_scale, w_qkv, w_o, router_w, router_b, moe_w1, moe_b1, moe_w2, moe_b2, final_norm):
    b, s, d = x_prompt.shape
    db, t, _ = x_sample.shape
    past = cache_k.shape[2]
    n_exp = router_w.shape[2]
    hd = d // N_HEADS
    hist = POOL_HIST_ROWS
    n_p, n_s = b * s, db * t
    assert t >= hist and ROW_TILE % t == 0 and s % ROW_TILE == 0 and n_s % ROW_TILE == 0 and ROW_TILE % QKV_TILE == 0
    row = lambda a: a.reshape(1, -1)
    wr = [router_w[i].T for i in range(2)]
    br = [router_b[i].reshape(n_exp, 1) for i in range(2)]

    hist_s = jnp.concatenate([jnp.zeros((db, 1, d), F32), state_pool[0]], axis=1)
    xn, tok, idx, gates, rank, hlast_p, hlast_s, cnt = _pool_layer(
        x_prompt, x_sample, hist_s, past, row(norm_mix[0]), pool_w[0].astype(BF16), row(pool_scale[0]),
        row(norm_ffn[0]), wr[0], br[0])
    x1, h1 = _moe(tok, idx, gates, rank, cnt, xn, row(norm_mix[1]), None, 0, moe_w1, moe_b1[0], moe_w2, moe_b2[0])

    wqkv = w_qkv[0].astype(BF16)
    q, ktb, vtb, ktf, vtf, ksb, vsb, ksf, vsf = _qkv(h1, wqkv[:, :d], wqkv[:, d:2 * d].T, wqkv[:, 2 * d:].T, b, s)
    o_p = _attention_prompt(q, ktb, vtb)
    transposed = lambda cache: cache.transpose(0, 1, 3, 4, 2).reshape(db * d, past)
    o_s = _attention_sample(q, ksb, vsb, transposed(cache_k), transposed(cache_v), t, n_p)
    xn, tok, idx, gates, rank, cnt = _proj_layer(o_p, o_s, x1, w_o[0].astype(BF16), row(norm_ffn[1]), wr[1], br[1])
    y_p, y_s = _moe(tok, idx, gates, rank, cnt, xn, row(final_norm), n_p, 1, moe_w1, moe_b1[1], moe_w2, moe_b2[1])

    frames_major = lambda xt: xt.reshape(1, b, N_HEADS, hd, s).transpose(0, 1, 4, 2, 3)
    heads = lambda a: a.reshape(1, db, t, N_HEADS, hd)
    return (y_p.reshape(b, s, d), y_s.reshape(db, t, d), hlast_p[None, :, 1:, :],
            frames_major(ktf), frames_major(vtf), hlast_s[None, :, 1:, :], heads(ksf), heads(vsf))
```

```python
import functools

import jax
import jax.numpy as jnp
from jax import lax
from jax.experimental import pallas as pl
from jax.experimental.pallas import tpu as pltpu
from jax.experimental.pallas import tpu_sc as plsc

EPS = 1e-5
POOL_WINDOWS = (2, 4, 8, 16)
POOL_HIST_ROWS = 16
N_HEADS = 16
TOP_K = 4
SWIGLU_LIMIT = 7.0
SWIGLU_ALPHA = 1.702
LANES = 128
ROW_TILE = 512
QKV_TILE = 256
ATTN_TQ = 128
ATTN_TK = 128
ATTN_EXIT = 104.0
VMEM_LIMIT = 56 * 1024 * 1024
SC_CORES = 2
SC_SUBCORES = 16
SC_CHUNK = 96
SC_GATHER_CHUNK = 48
COMBINE_TILE = 256

F32 = jnp.float32
BF16 = jnp.bfloat16


def _rms(x, g):
    ms = jnp.mean(x * x, axis=-1, keepdims=True)
    return x * lax.rsqrt(ms + EPS) * g


def _dot(a, b):
    return jnp.dot(a, b, preferred_element_type=F32)


def _dot_nt(a, b, precision=None):
    return lax.dot_general(a, b, (((1,), (1,)), ((), ())), preferred_element_type=F32, precision=precision)


def _params(semantics):
    return pltpu.CompilerParams(dimension_semantics=semantics, vmem_limit_bytes=VMEM_LIMIT)


def _full(shape):
    return pl.BlockSpec(shape, lambda i, *_: (0,) * len(shape))


def _route_tail(xn, gffn_ref, wr_ref, br_ref, cnt_ref, tok_ref, idx_ref, gate_ref, rank_ref):
    r = xn.shape[0]
    n_exp = wr_ref.shape[0]
    tok = _rms(xn, gffn_ref[...])
    tok_ref[...] = tok.astype(tok_ref.dtype)
    logits = _dot_nt(wr_ref[...], tok, precision=lax.Precision.HIGHEST) + br_ref[...]
    eidx = lax.broadcasted_iota(jnp.int32, logits.shape, 0).astype(F32)
    vals, idxs = [], []
    l = logits
    for _ in range(TOP_K):
        m = jnp.max(l, axis=0, keepdims=True)
        i = jnp.min(jnp.where(l == m, eidx, float(n_exp)), axis=0, keepdims=True)
        vals.append(m)
        idxs.append(i)
        l = jnp.where(eidx == i, -jnp.inf, l)
    es = [jnp.exp(v - vals[0]) for v in vals]
    den = es[0]
    for e in es[1:]:
        den = den + e
    gate_ref[...] = jnp.concatenate([e / den for e in es], axis=0)
    idx_ref[...] = jnp.concatenate(idxs, axis=0).astype(jnp.int32)
    member = jnp.zeros(logits.shape, F32)
    for i in idxs:
        member = member + jnp.where(eidx == i, 1.0, 0.0)
    tri = jnp.where(lax.broadcasted_iota(jnp.int32, (r, r), 0) < lax.broadcasted_iota(jnp.int32, (r, r), 1),
                    1.0, 0.0).astype(BF16)
    before = _dot(member.astype(BF16), tri) + cnt_ref[:, :1]
    ranks = [jnp.sum(jnp.where(eidx == i, before, 0.0), axis=0, keepdims=True) for i in idxs]
    rank_ref[...] = jnp.concatenate(ranks, axis=0).astype(jnp.int32)
    cnt_ref[...] = cnt_ref[...] + jnp.sum(member, axis=1, keepdims=True)


def _pool_mix(h, ext_ref, pos, pw_ref, ps_ref):
    ts, d = h.shape[-2:]
    hist = POOL_HIST_ROWS
    group = d // len(POOL_WINDOWS)
    pre = (slice(None),) * (h.ndim - 2)
    ys = []
    for g, win in enumerate(POOL_WINDOWS):
        cols = slice(g * group, (g + 1) * group)
        hg = h[pre + (slice(None), cols)]
        acc = hg
        for j in range(1, win):
            acc = acc + ext_ref[pre + (slice(hist - j, hist - j + ts), cols)]
        cnt = jnp.minimum(pos + 1, win).astype(F32)
        dg = acc / cnt - hg
        ys.append(_dot(dg.reshape(-1, group).astype(BF16), pw_ref[g]))
    return jnp.concatenate(ys, axis=-1) * ps_ref[...]


def _pool_kernel(ntp, nt, pos0_s, xp_ref, xs_ref, hist_ref, gmix_ref, pw_ref, ps_ref, gffn_ref, wr_ref, br_ref,
                 xn_ref, tok_ref, idx_ref, gate_ref, rank_ref, hlast_p_ref, hlast_s_ref, cnt_out_ref,
                 ext_p, ext_s, cnt_ref):
    i = pl.program_id(0)
    hist = POOL_HIST_ROWS

    @pl.when(i == 0)
    def _():
        cnt_ref[...] = jnp.zeros(cnt_ref.shape, F32)

    @pl.when(i < ntp)
    def _():
        ts, d = xp_ref.shape
        t = lax.rem(i, nt)

        @pl.when(t == 0)
        def _():
            ext_p[0:hist, :] = jnp.zeros((hist, d), F32)

        x = xp_ref[...]
        h = _rms(x, gmix_ref[...])
        ext_p[hist:hist + ts, :] = h

        @pl.when(t == nt - 1)
        def _():
            hlast_p_ref[0] = h[ts - hist:, :]

        pos = t * ts + lax.broadcasted_iota(jnp.int32, (ts, 1), 0)
        xn_ref[...] = x + _pool_mix(h, ext_p, pos, pw_ref, ps_ref)
        ext_p[0:hist, :] = ext_p[ts:ts + hist, :]

    @pl.when(i >= ntp)
    def _():
        bb, ts, d = xs_ref.shape
        ext_s[:, 0:hist, :] = hist_ref[...]
        x = xs_ref[...]
        h = _rms(x, gmix_ref[...])
        ext_s[:, hist:hist + ts, :] = h
        hlast_s_ref[...] = h[:, ts - hist:, :]
        pos = pos0_s + lax.broadcasted_iota(jnp.int32, (1, ts, 1), 1)
        xn_ref[...] = x.reshape(bb * ts, d) + _pool_mix(h, ext_s, pos, pw_ref, ps_ref)

    _route_tail(xn_ref[...], gffn_ref, wr_ref, br_ref, cnt_ref, tok_ref, idx_ref, gate_ref, rank_ref)
    cnt_out_ref[...] = cnt_ref[...]


def _pool_layer(x_prompt, x_sample, hist_s, pos0_s, gmix, pw, ps, gffn, wr, br):
    b, s, d = x_prompt.shape
    db, t, _ = x_sample.shape
    n_exp = wr.shape[0]
    r = ROW_TILE
    nt = s // r
    ntp = b * nt
    bb = r // t
    nts = db // bb
    n = b * s + db * t
    hist = POOL_HIST_ROWS
    rows = pl.BlockSpec((r, d), lambda i: (i, 0))
    lanes = pl.BlockSpec((TOP_K, r), lambda i: (0, i))
    sample_blk = lambda i: (jnp.maximum(i - ntp, 0), 0, 0)
    return pl.pallas_call(
        functools.partial(_pool_kernel, ntp, nt, pos0_s),
        grid=(ntp + nts,),
        in_specs=[pl.BlockSpec((r, d), lambda i: (jnp.minimum(i, ntp - 1), 0)),
                  pl.BlockSpec((bb, t, d), sample_blk), pl.BlockSpec((bb, hist, d), sample_blk),
                  _full((1, d)), _full(pw.shape), _full((1, d)), _full((1, d)), _full((n_exp, d)), _full((n_exp, 1))],
        out_specs=[rows, rows, lanes, lanes, lanes,
                   pl.BlockSpec((1, hist, d), lambda i: (jnp.minimum(i // nt, b - 1), 0, 0)),
                   pl.BlockSpec((bb, hist, d), sample_blk), _full((n_exp, LANES))],
        out_shape=[jax.ShapeDtypeStruct((n, d), F32), jax.ShapeDtypeStruct((n, d), F32),
                   jax.ShapeDtypeStruct((TOP_K, n), jnp.int32), jax.ShapeDtypeStruct((TOP_K, n), F32),
                   jax.ShapeDtypeStruct((TOP_K, n), jnp.int32),
                   jax.ShapeDtypeStruct((b, hist, d), F32), jax.ShapeDtypeStruct((db, hist, d), F32),
                   jax.ShapeDtypeStruct((n_exp, LANES), F32)],
        scratch_shapes=[pltpu.VMEM((hist + r, d), F32), pltpu.VMEM((bb, hist + t, d), F32),
                        pltpu.VMEM((n_exp, LANES), F32)],
        compiler_params=_params(("arbitrary",)),
        name="pool_route",
    )(x_prompt.reshape(b * s, d), x_sample, hist_s, gmix, pw, ps, gffn, wr, br)


def _proj_kernel(ntp, op_ref, os_ref, x_ref, wo_ref, gffn_ref, wr_ref, br_ref,
                 xn_ref, tok_ref, idx_ref, gate_ref, rank_ref, cnt_out_ref, cnt_ref):
    i = pl.program_id(0)

    @pl.when(i == 0)
    def _():
        cnt_ref[...] = jnp.zeros(cnt_ref.shape, F32)

    o = jnp.where(i < ntp, op_ref[...], os_ref[...])
    xn = x_ref[...] + _dot(o, wo_ref[...])
    xn_ref[...] = xn
    _route_tail(xn, gffn_ref, wr_ref, br_ref, cnt_ref, tok_ref, idx_ref, gate_ref, rank_ref)
    cnt_out_ref[...] = cnt_ref[...]


def _proj_layer(o_p, o_s, x, wo, gffn, wr, br):
    n, d = x.shape
    n_exp = wr.shape[0]
    r = ROW_TILE
    ntp = o_p.shape[0] // r
    rows = pl.BlockSpec((r, d), lambda i: (i, 0))
    lanes = pl.BlockSpec((TOP_K, r), lambda i: (0, i))
    return pl.pallas_call(
        functools.partial(_proj_kernel, ntp),
        grid=(n // r,),
        in_specs=[pl.BlockSpec((r, d), lambda i: (jnp.minimum(i, ntp - 1), 0)),
                  pl.BlockSpec((r, d), lambda i: (jnp.maximum(i - ntp, 0), 0)),
                  rows, _full((d, d)), _full((1, d)), _full((n_exp, d)), _full((n_exp, 1))],
        out_specs=[rows, rows, lanes, lanes, lanes, _full((n_exp, LANES))],
        out_shape=[jax.ShapeDtypeStruct((n, d), F32), jax.ShapeDtypeStruct((n, d), F32),
                   jax.ShapeDtypeStruct((TOP_K, n), jnp.int32), jax.ShapeDtypeStruct((TOP_K, n), F32),
                   jax.ShapeDtypeStruct((TOP_K, n), jnp.int32), jax.ShapeDtypeStruct((n_exp, LANES), F32)],
        scratch_shapes=[pltpu.VMEM((n_exp, LANES), F32)],
        compiler_params=_params(("arbitrary",)),
        name="proj_route",
    )(o_p, o_s, x, wo, gffn, wr, br)


def _sc_chunks(dest, w):
    n = dest.shape[1]
    return dest.reshape(TOP_K, n // w, w).transpose(1, 0, 2)


def _dispatch(tok, dest, n_buf_rows):
    n, d = tok.shape
    w = SC_CHUNK
    n_workers = SC_CORES * SC_SUBCORES
    per_worker = n // (w * n_workers)
    assert per_worker * w * n_workers == n
    mesh = plsc.VectorSubcoreMesh(core_axis_name="core", subcore_axis_name="subcore",
                                  num_cores=SC_CORES, num_subcores=SC_SUBCORES)

    @functools.partial(
        pl.kernel, mesh=mesh, out_type=jax.ShapeDtypeStruct((n_buf_rows, d), tok.dtype),
        scratch_types=[pltpu.VMEM((TOP_K, w), jnp.int32), pltpu.VMEM((w, d), tok.dtype), pltpu.SemaphoreType.DMA],
        name="moe_dispatch_sc")
    def scatter_rows(tok_hbm, dest_hbm, buf_hbm, idx_v, rows_v, sem):
        worker = lax.axis_index("subcore") * SC_CORES + lax.axis_index("core")

        @pl.loop(0, per_worker)
        def _(c):
            chunk = worker * per_worker + c
            pltpu.sync_copy(dest_hbm.at[chunk], idx_v)
            pltpu.sync_copy(tok_hbm.at[pl.ds(pl.multiple_of(chunk * w, w), w)], rows_v)
            copies = [pltpu.async_copy(rows_v, buf_hbm.at[idx_v.at[k]], sem) for k in range(TOP_K)]
            for cp in copies:
                cp.wait()

    return scatter_rows(tok, _sc_chunks(dest, w))


def _expert_kernel(be_ref, valid_ref, nused_ref, x_ref, w1_ref, b1_ref, w2_ref, b2_ref, o_ref, w1b_ref, w2b_ref):
    i = pl.program_id(0)
    used = valid_ref[i] > 0

    @pl.when(used & ((i == 0) | (be_ref[i] != be_ref[jnp.maximum(i - 1, 0)])))
    def _():
        w1b_ref[...] = w1_ref[0, 0].astype(BF16)
        w2b_ref[...] = w2_ref[0, 0].astype(BF16)

    @pl.when(used)
    def _():
        f = w2b_ref.shape[0]
        row = lax.broadcasted_iota(jnp.int32, (x_ref.shape[0], 1), 0)
        x = jnp.where(row < valid_ref[i], x_ref[...], 0.0)
        gu = _dot(x.astype(BF16), w1b_ref[...]) + b1_ref[0]
        gate = jnp.minimum(gu[:, :f], SWIGLU_LIMIT)
        up = jnp.clip(gu[:, f:], -SWIGLU_LIMIT, SWIGLU_LIMIT)
        hid = (up + 1.0) * (gate * jax.nn.sigmoid(gate * SWIGLU_ALPHA))
        o_ref[...] = _dot(hid.astype(BF16), w2b_ref[...]) + b2_ref[0]

    @pl.when(jnp.logical_not(used))
    def _():
        o_ref[...] = jnp.zeros(o_ref.shape, o_ref.dtype)


def _experts(buf, block_expert, valid, nused, layer, w1, b1, w2, b2, tm):
    rows, d = buf.shape
    _, n_exp, _, f2 = w1.shape
    f = w2.shape[2]
    in_rows = lambda i, be, va, nu: (jnp.minimum(i, nu[0] - 1), 0)
    by_expert = lambda i, be, va, nu: (be[i], 0, 0)
    by_layer_expert = lambda i, be, va, nu: (layer, be[i], 0, 0)
    return pl.pallas_call(
        _expert_kernel,
        grid_spec=pltpu.PrefetchScalarGridSpec(
            num_scalar_prefetch=3,
            grid=(rows // tm,),
            in_specs=[pl.BlockSpec((tm, d), in_rows),
                      pl.BlockSpec((1, 1, d, f2), by_layer_expert), pl.BlockSpec((1, 1, f2), by_expert),
                      pl.BlockSpec((1, 1, f, d), by_layer_expert), pl.BlockSpec((1, 1, d), by_expert)],
            out_specs=pl.BlockSpec((tm, d), lambda i, be, va, nu: (i, 0)),
            scratch_shapes=[pltpu.VMEM((d, f2), BF16), pltpu.VMEM((f, d), BF16)]),
        out_shape=jax.ShapeDtypeStruct((rows, d), F32),
        compiler_params=_params(("arbitrary",)),
        name="moe_experts",
    )(block_expert, valid, nused, buf, w1, b1.reshape(n_exp, 1, f2), w2, b2.reshape(n_exp, 1, d))


def _gather_rows(ybuf, dest):
    _, d = ybuf.shape
    n = dest.shape[1]
    w = SC_GATHER_CHUNK
    n_workers = SC_CORES * SC_SUBCORES
    per_worker = n // (w * n_workers)
    assert per_worker * w * n_workers == n and TOP_K % 2 == 0
    mesh = plsc.VectorSubcoreMesh(core_axis_name="core", subcore_axis_name="subcore",
                                  num_cores=SC_CORES, num_subcores=SC_SUBCORES)

    @functools.partial(
        pl.kernel, mesh=mesh, out_type=jax.ShapeDtypeStruct((TOP_K, n, d), ybuf.dtype),
        scratch_types=[pltpu.VMEM((TOP_K, w), jnp.int32), pltpu.VMEM((2, w, d), ybuf.dtype),
                       pltpu.SemaphoreType.DMA((2,)), pltpu.SemaphoreType.DMA((2,))],
        name="moe_gather_sc")
    def gather_rows(y_hbm, dest_hbm, out_hbm, idx_v, rows_v, sem_in, sem_out):
        worker = lax.axis_index("subcore") * SC_CORES + lax.axis_index("core")

        @pl.loop(0, per_worker)
        def _(c):
            chunk = worker * per_worker + c
            tokens = pl.ds(pl.multiple_of(chunk * w, w), w)
            pltpu.sync_copy(dest_hbm.at[chunk], idx_v)
            fetch = lambda k: pltpu.async_copy(y_hbm.at[idx_v.at[k]], rows_v.at[k % 2], sem_in.at[k % 2])
            store = lambda k: pltpu.async_copy(rows_v.at[k % 2], out_hbm.at[k, tokens], sem_out.at[k % 2])
            fetches = [fetch(0), fetch(1)]
            stores = []
            for k in range(TOP_K):
                fetches[k].wait()
                stores.append(store(k))
                if k + 2 < TOP_K:
                    stores[k].wait()
                    fetches.append(fetch(k + 2))
            for k in range(TOP_K - 2, TOP_K):
                stores[k].wait()

    return gather_rows(ybuf, _sc_chunks(dest, w))


def _combine_kernel(ntp, rows_ref, gate_ref, x_ref, g_ref, *outs):
    i = pl.program_id(0)
    r = x_ref.shape[0]
    gates = jnp.concatenate([gate_ref[...], jnp.zeros((LANES - TOP_K, r), F32)], axis=0)
    gates_t = jnp.transpose(gates)
    f = rows_ref[0] * gates_t[:, 0:1]
    for k in range(1, TOP_K):
        f = f + rows_ref[k] * gates_t[:, k:k + 1]
    xo = x_ref[...] + f
    hn = _rms(xo, g_ref[...])
    if ntp is None:
        xo_ref, hn_ref = outs
        xo_ref[...] = xo
        hn_ref[...] = hn.astype(hn_ref.dtype)
    else:
        hp_ref, hs_ref = outs

        @pl.when(i < ntp)
        def _():
            hp_ref[...] = hn

        @pl.when(i >= ntp)
        def _():
            hs_ref[...] = hn


def _combine(rows4, gates, x, g, n_p):
    n, d = x.shape
    r = COMBINE_TILE
    rows = pl.BlockSpec((r, d), lambda i: (i, 0))
    if n_p is None:
        ntp = None
        out_specs = [rows, rows]
        out_shape = [jax.ShapeDtypeStruct((n, d), F32), jax.ShapeDtypeStruct((n, d), BF16)]
    else:
        ntp = n_p // r
        out_specs = [pl.BlockSpec((r, d), lambda i: (jnp.minimum(i, ntp - 1), 0)),
                     pl.BlockSpec((r, d), lambda i: (jnp.maximum(i - ntp, 0), 0))]
        out_shape = [jax.ShapeDtypeStruct((n_p, d), F32), jax.ShapeDtypeStruct((n - n_p, d), F32)]
    return pl.pallas_call(
        functools.partial(_combine_kernel, ntp),
        grid=(n // r,),
        in_specs=[pl.BlockSpec((TOP_K, r, d), lambda i: (0, i, 0)), pl.BlockSpec((TOP_K, r), lambda i: (0, i)),
                  rows, _full((1, d))],
        out_specs=out_specs,
        out_shape=out_shape,
        compiler_params=_params(("arbitrary",)),
        name="moe_combine",
    )(rows4, gates, x, g)


def _moe(tok, idx, gates, rank, cnt, x, g_next, n_p, layer, w1, b1, w2, b2):
    n, d = tok.shape
    n_exp = w1.shape[1]
    pairs = n * TOP_K
    tm = 128
    for cand in (512, 256):
        if pairs >= 4 * cand * n_exp and pairs % cand == 0:
            tm = cand
            break
    n_blocks = -(-pairs // tm) + n_exp
    counts = cnt[:, 0].astype(jnp.int32)
    padded = (counts + tm - 1) // tm * tm
    pad_end = jnp.cumsum(padded)
    pad_start = pad_end - padded
    experts = jnp.arange(n_exp, dtype=jnp.int32)
    dest = rank + jnp.sum(jnp.where(idx[:, :, None] == experts, pad_start, 0), axis=-1)
    starts = jnp.arange(n_blocks, dtype=jnp.int32) * tm
    nused = (pad_end[-1:] // tm).astype(jnp.int32)
    block_expert = jnp.minimum(jnp.sum((pad_end[None, :] <= starts[:, None]).astype(jnp.int32), axis=1), n_exp - 1)
    tokens_end = jnp.sum(jnp.where(block_expert[:, None] == experts, pad_start + counts, 0), axis=-1)
    valid = jnp.clip(tokens_end - starts, 0, tm).astype(jnp.int32)
    buf = _dispatch(tok, dest, n_blocks * tm)
    ybuf = _experts(buf, block_expert, valid, nused, layer, w1, b1, w2, b2, tm)
    return _combine(_gather_rows(ybuf, dest), gates, x, g_next, n_p)


def _qkv_kernel(ntp, h_ref, wq_ref, wkt_ref, wvt_ref, q_ref, ktb_ref, vtb_ref, ktf_ref, vtf_ref,
                ksb_ref, vsb_ref, ksf_ref, vsf_ref):
    i = pl.program_id(0)
    r, d = h_ref.shape
    tk = ktb_ref.shape[-1]
    h = h_ref[...]
    q_ref[...] = (_dot(h, wq_ref[...]) * ((d // N_HEADS) ** -0.5)).astype(BF16)

    @pl.when(i < ntp)
    def _():
        for wt_ref, tb_ref, tf_ref in ((wkt_ref, ktb_ref, ktf_ref), (wvt_ref, vtb_ref, vtf_ref)):
            xt = _dot_nt(wt_ref[...], h)
            tf_ref[...] = xt
            for c in range(r // tk):
                tb_ref[0, c] = xt[:, c * tk:(c + 1) * tk].astype(BF16)

    @pl.when(i >= ntp)
    def _():
        for wt_ref, sb_ref, sf_ref in ((wkt_ref, ksb_ref, ksf_ref), (wvt_ref, vsb_ref, vsf_ref)):
            x = _dot_nt(h, wt_ref[...])
            sf_ref[...] = x
            sb_ref[...] = x.astype(BF16)


def _qkv(h, wq, wkt, wvt, n_streams, length):
    n, d = h.shape
    r = QKV_TILE
    tk = ATTN_TK
    n_p = n_streams * length
    nt = length // r
    ntp = n_p // r
    rows = pl.BlockSpec((r, d), lambda i: (i, 0))
    prompt_blk = lambda i: (jnp.minimum(i, ntp - 1) // nt, lax.rem(jnp.minimum(i, ntp - 1), nt))
    t_blocks = pl.BlockSpec((1, r // tk, d, tk), lambda i: prompt_blk(i) + (0, 0))
    t_full = pl.BlockSpec((d, r), prompt_blk)
    sample = pl.BlockSpec((r, d), lambda i: (jnp.maximum(i - ntp, 0), 0))
    w_spec = _full((d, d))
    return pl.pallas_call(
        functools.partial(_qkv_kernel, ntp),
        grid=(n // r,),
        in_specs=[rows, w_spec, w_spec, w_spec],
        out_specs=[rows, t_blocks, t_blocks, t_full, t_full, sample, sample, sample, sample],
        out_shape=[jax.ShapeDtypeStruct((n, d), BF16)]
        + [jax.ShapeDtypeStruct((n_streams, length // tk, d, tk), BF16)] * 2
        + [jax.ShapeDtypeStruct((n_streams * d, length), F32)] * 2
        + [jax.ShapeDtypeStruct((n - n_p, d), BF16)] * 2 + [jax.ShapeDtypeStruct((n - n_p, d), F32)] * 2,
        compiler_params=_params(("arbitrary",)),
        name="qkv",
    )(h, wq, wkt, wvt)


def _attn_setup(q_ref, qm_ref, carry_ref, acc_ref):
    tq, d = q_ref.shape
    lane = lax.broadcasted_iota(jnp.int32, (tq, LANES), 1)
    for p in range(d // LANES):
        qp = q_ref[:, p * LANES:(p + 1) * LANES]
        qm_ref[p, 0:tq, :] = jnp.where(lane < LANES // 2, qp, jnp.zeros_like(qp))
        qm_ref[p, tq:2 * tq, :] = jnp.where(lane >= LANES // 2, qp, jnp.zeros_like(qp))
    carry_ref[...] = jnp.zeros(carry_ref.shape, F32)
    acc_ref[...] = jnp.zeros(acc_ref.shape, F32)


def _suffix_sum_matrix(tk):
    src = lax.rem(lax.broadcasted_iota(jnp.int32, (2 * tk, 2 * tk), 0), tk)
    dst = lax.broadcasted_iota(jnp.int32, (2 * tk, 2 * tk), 1)
    return jnp.where((dst >= tk) | (src > dst), 1.0, 0.0).astype(BF16)


def _attn_block(qm_ref, carry_ref, acc_ref, keys, values, transposed, mask, sums):
    n_pairs, tq2, tk = carry_ref.shape
    tq = tq2 // 2
    lane = lax.broadcasted_iota(jnp.int32, (tq, LANES), 1)
    visible = (lambda x: x) if mask is None else (lambda x: jnp.where(mask, x, 0.0))
    scores = _dot if transposed else _dot_nt
    mix = _dot_nt if transposed else _dot
    zs = [scores(qm_ref[p], keys[p]) for p in range(n_pairs)]
    log_beta, parts = [], []
    for z in zs:
        sp = jnp.maximum(z, 0.0) + jnp.log(1.0 + jnp.exp(-jnp.abs(z)))
        log_keep = visible(-sp)
        hi = log_keep.astype(BF16)
        lo = (log_keep - hi.astype(F32)).astype(BF16)
        parts.append(jnp.concatenate([hi, lo], axis=1))
        log_beta.append(z - sp)
    sums_out = [_dot(part, sums) for part in parts]
    weights = []
    top = jnp.full((tq2, tk), -jnp.inf, F32)
    for p in range(n_pairs):
        carry = carry_ref[p]
        a = visible(jnp.exp(log_beta[p] + sums_out[p][:, :tk] + carry))
        weights.append(a.astype(BF16))
        carry = carry + sums_out[p][:, tk:]
        carry_ref[p] = carry
        top = jnp.maximum(top, carry)
    for p in range(n_pairs):
        out = mix(weights[p], values[p])
        acc_ref[:, p * LANES:(p + 1) * LANES] += jnp.where(lane < LANES // 2, out[:tq], out[tq:])
    return jnp.max(top)


def _keep_sweeping(state):
    j, top = state
    return (j >= 0) & (top > -ATTN_EXIT)


def _attn_prompt_kernel(q_ref, kt_ref, vt_ref, o_ref, qm_ref, carry_ref, acc_ref):
    qi = pl.program_id(1)
    tq, d = q_ref.shape
    tk = kt_ref.shape[-1]
    n_pairs = d // LANES
    _attn_setup(q_ref, qm_ref, carry_ref, acc_ref)
    row_pos = qi * tq + lax.rem(lax.broadcasted_iota(jnp.int32, (2 * tq, tk), 0), tq)
    col = lax.broadcasted_iota(jnp.int32, (2 * tq, tk), 1)
    sums = _suffix_sum_matrix(tk)

    def block(j, mask):
        keys = [kt_ref[0, j, p * LANES:(p + 1) * LANES, :] for p in range(n_pairs)]
        values = [vt_ref[0, j, p * LANES:(p + 1) * LANES, :] for p in range(n_pairs)]
        return _attn_block(qm_ref, carry_ref, acc_ref, keys, values, True, mask, sums)

    j0 = ((qi + 1) * tq - 2) // tk
    top = block(j0, (j0 * tk + col) < row_pos)
    lax.while_loop(_keep_sweeping, lambda state: (state[0] - 1, block(state[0], None)), (j0 - 1, top))
    o_ref[...] = acc_ref[...].astype(o_ref.dtype)


def _attention_prompt(q, kt, vt):
    n_streams, n_kblocks, d, tk = kt.shape
    length = n_kblocks * tk
    tq = min(length, ATTN_TQ)
    assert tk % tq == 0
    nq = length // tq
    kv = pl.BlockSpec((1, n_kblocks, d, tk), lambda b, i: (b, 0, 0, 0))
    rows = pl.BlockSpec((tq, d), lambda b, i: (b * nq + i, 0))
    return pl.pallas_call(
        _attn_prompt_kernel,
        grid=(n_streams, nq),
        in_specs=[rows, kv, kv],
        out_specs=rows,
        out_shape=jax.ShapeDtypeStruct((n_streams * length, d), BF16),
        scratch_shapes=[pltpu.VMEM((d // LANES, 2 * tq, LANES), BF16), pltpu.VMEM((d // LANES, 2 * tq, tk), F32),
                        pltpu.VMEM((tq, d), F32)],
        compiler_params=_params(("arbitrary", "arbitrary")),
        name="stick_breaking_prompt",
    )(q, kt, vt)


def _attn_sample_kernel(q_ref, kn_ref, vn_ref, ck_hbm, cv_hbm, o_ref, qm_ref, carry_ref, acc_ref, kbuf, vbuf, sem):
    b = pl.program_id(0)
    t, d = q_ref.shape
    tk = kbuf.shape[-1]
    n_pairs = d // LANES
    n_cache_blocks = ck_hbm.shape[1] // tk

    def fetch(j, slot):
        src = (pl.ds(pl.multiple_of(b * d, d), d), pl.ds(pl.multiple_of(j * tk, tk), tk))
        return (pltpu.make_async_copy(ck_hbm.at[src], kbuf.at[slot], sem.at[0, slot]),
                pltpu.make_async_copy(cv_hbm.at[src], vbuf.at[slot], sem.at[1, slot]))

    for cp in fetch(n_cache_blocks - 1, (n_cache_blocks - 1) % 2):
        cp.start()
    _attn_setup(q_ref, qm_ref, carry_ref, acc_ref)
    row = lax.rem(lax.broadcasted_iota(jnp.int32, (2 * t, tk), 0), t)
    col = lax.broadcasted_iota(jnp.int32, (2 * t, tk), 1)
    sums = _suffix_sum_matrix(tk)
    pad = jnp.zeros((tk - t, LANES), BF16)
    keys = [jnp.concatenate([kn_ref[:, p * LANES:(p + 1) * LANES], pad], axis=0) for p in range(n_pairs)]
    values = [jnp.concatenate([vn_ref[:, p * LANES:(p + 1) * LANES], pad], axis=0) for p in range(n_pairs)]
    top = _attn_block(qm_ref, carry_ref, acc_ref, keys, values, False, col < row, sums)

    def body(state):
        j, _ = state
        slot = lax.rem(j, 2)
        for cp in fetch(j, slot):
            cp.wait()

        @pl.when(j > 0)
        def _():
            for cp in fetch(j - 1, 1 - slot):
                cp.start()

        keys = [kbuf[slot, p * LANES:(p + 1) * LANES, :].astype(BF16) for p in range(n_pairs)]
        values = [vbuf[slot, p * LANES:(p + 1) * LANES, :].astype(BF16) for p in range(n_pairs)]
        return j - 1, _attn_block(qm_ref, carry_ref, acc_ref, keys, values, True, None, sums)

    j_end, _ = lax.while_loop(_keep_sweeping, body, (jnp.int32(n_cache_blocks - 1), top))

    @pl.when(j_end >= 0)
    def _():
        for cp in fetch(j_end, lax.rem(j_end, 2)):
            cp.wait()

    o_ref[...] = acc_ref[...].astype(o_ref.dtype)


def _attention_sample(q, k_new, v_new, cache_kt, cache_vt, t, q_row_off):
    d = q.shape[1]
    n_streams = k_new.shape[0] // t
    tk = ATTN_TK
    assert cache_kt.shape[1] % tk == 0 and t <= tk
    any_spec = pl.BlockSpec(memory_space=pl.ANY)
    new = pl.BlockSpec((t, d), lambda b: (b, 0))
    return pl.pallas_call(
        _attn_sample_kernel,
        grid=(n_streams,),
        in_specs=[pl.BlockSpec((t, d), lambda b: (q_row_off // t + b, 0)), new, new, any_spec, any_spec],
        out_specs=new,
        out_shape=jax.ShapeDtypeStruct((n_streams * t, d), BF16),
        scratch_shapes=[pltpu.VMEM((d // LANES, 2 * t, LANES), BF16), pltpu.VMEM((d // LANES, 2 * t, tk), F32),
                        pltpu.VMEM((t, d), F32), pltpu.VMEM((2, d, tk), F32), pltpu.VMEM((2, d, tk), F32),
                        pltpu.SemaphoreType.DMA((2, 2))],
        compiler_params=_params(("arbitrary",)),
        name="stick_breaking_sample",
    )(q, k_new, v_new, cache_kt, cache_vt)


def kernel(x_prompt, x_sample, state_pool, cache_k, cache_v, norm_mix, norm_ffn, pool_w, pool_scale, w_qkv, w_o, router_w, router_b, moe_w1, moe_b1, moe_w2, moe_b2, final_norm):
    b, s, d = x_prompt.shape
    db, t, _ = x_sample.shape
    past = cache_k.shape[2]
    n_exp = router_w.shape[2]
    hd = d // N_HEADS
    hist = POOL_HIST_ROWS
    n_p, n_s = b * s, db * t
    assert t >= hist and ROW_TILE % t == 0 and s % ROW_TILE == 0 and n_s % ROW_TILE == 0 and ROW_TILE % QKV_TILE == 0
    row = lambda a: a.reshape(1, -1)
    wr = [router_w[i].T for i in range(2)]
    br = [router_b[i].reshape(n_exp, 1) for i in range(2)]

    hist_s = jnp.concatenate([jnp.zeros((db, 1, d), F32), state_pool[0]], axis=1)
    xn, tok, idx, gates, rank, hlast_p, hlast_s, cnt = _pool_layer(
        x_prompt, x_sample, hist_s, past, row(norm_mix[0]), pool_w[0].astype(BF16), row(pool_scale[0]),
        row(norm_ffn[0]), wr[0], br[0])
    x1, h1 = _moe(tok, idx, gates, rank, cnt, xn, row(norm_mix[1]), None, 0, moe_w1, moe_b1[0], moe_w2, moe_b2[0])

    wqkv = w_qkv[0].astype(BF16)
    q, ktb, vtb, ktf, vtf, ksb, vsb, ksf, vsf = _qkv(h1, wqkv[:, :d], wqkv[:, d:2 * d].T, wqkv[:, 2 * d:].T, b, s)
    o_p = _attention_prompt(q, ktb, vtb)
    transposed = lambda cache: cache.transpose(0, 1, 3, 4, 2).reshape(db * d, past)
    o_s = _attention_sample(q, ksb, vsb, transposed(cache_k), transposed(cache_v), t, n_p)
    xn, tok, idx, gates, rank, cnt = _proj_layer(o_p, o_s, x1, w_o[0].astype(BF16), row(norm_ffn[1]), wr[1], br[1])
    y_p, y_s = _moe(tok, idx, gates, rank, cnt, xn, row(final_norm), n_p, 1, moe_w1, moe_b1[1], moe_w2, moe_b2[1])

    frames_major = lambda xt: xt.reshape(1, b, N_HEADS, hd, s).transpose(0, 1, 4, 2, 3)
    heads = lambda a: a.reshape(1, db, t, N_HEADS, hd)
    return (y_p.reshape(b, s, d), y_s.reshape(db, t, d), hlast_p[None, :, 1:, :],
            frames_major(ktf), frames_major(vtf), hlast_s[None, :, 1:, :], heads(ksf), heads(vsf))
```

```python
import functools

import jax
import jax.numpy as jnp
from jax import lax
from jax.experimental import pallas as pl
from jax.experimental.pallas import tpu as pltpu
from jax.experimental.pallas import tpu_sc as plsc

EPS = 1e-5
POOL_WINDOWS = (2, 4, 8, 16)
POOL_HIST_ROWS = 16
N_HEADS = 16
TOP_K = 4
SWIGLU_LIMIT = 7.0
SWIGLU_ALPHA = 1.702
LANES = 128
ROW_TILE = 512
QKV_TILE = 256
ATTN_TQ = 128
ATTN_TK = 128
ATTN_EXIT = 104.0
VMEM_LIMIT = 56 * 1024 * 1024
SC_CORES = 2
SC_SUBCORES = 16
SC_CHUNK = 96
SC_GATHER_CHUNK = 48
COMBINE_TILE = 256

F32 = jnp.float32
BF16 = jnp.bfloat16


def _rms(x, g):
    ms = jnp.mean(x * x, axis=-1, keepdims=True)
    return x * lax.rsqrt(ms + EPS) * g


def _dot(a, b):
    return jnp.dot(a, b, preferred_element_type=F32)


def _dot_nt(a, b, precision=None):
    return lax.dot_general(a, b, (((1,), (1,)), ((), ())), preferred_element_type=F32, precision=precision)


def _pack_halves(x):
    c = x.shape[1] // 2
    bits = lax.bitcast_convert_type(x.astype(BF16).astype(F32), jnp.uint32)
    return bits[:, :c] | (bits[:, c:] >> 16)


def _unpack_halves(w):
    return (lax.bitcast_convert_type(w & jnp.uint32(0xFFFF0000), F32), lax.bitcast_convert_type(w << 16, F32))


def _params(semantics):
    return pltpu.CompilerParams(dimension_semantics=semantics, vmem_limit_bytes=VMEM_LIMIT)


def _full(shape):
    return pl.BlockSpec(shape, lambda i, *_: (0,) * len(shape))


def _route_tail(xn, gffn_ref, wr_ref, br_ref, cnt_ref, tok_ref, idx_ref, gate_ref, rank_ref):
    r = xn.shape[0]
    n_exp = wr_ref.shape[0]
    tok = _rms(xn, gffn_ref[...])
    tok_ref[...] = _pack_halves(tok)
    logits = _dot_nt(wr_ref[...], tok, precision=lax.Precision.HIGHEST) + br_ref[...]
    eidx = lax.broadcasted_iota(jnp.int32, logits.shape, 0).astype(F32)
    vals, idxs = [], []
    l = logits
    for _ in range(TOP_K):
        m = jnp.max(l, axis=0, keepdims=True)
        i = jnp.min(jnp.where(l == m, eidx, float(n_exp)), axis=0, keepdims=True)
        vals.append(m)
        idxs.append(i)
        l = jnp.where(eidx == i, -jnp.inf, l)
    es = [jnp.exp(v - vals[0]) for v in vals]
    den = es[0]
    for e in es[1:]:
        den = den + e
    gate_ref[...] = jnp.concatenate([e / den for e in es], axis=0)
    idx_ref[...] = jnp.concatenate(idxs, axis=0).astype(jnp.int32)
    member = jnp.zeros(logits.shape, F32)
    for i in idxs:
        member = member + jnp.where(eidx == i, 1.0, 0.0)
    tri = jnp.where(lax.broadcasted_iota(jnp.int32, (r, r), 0) < lax.broadcasted_iota(jnp.int32, (r, r), 1),
                    1.0, 0.0).astype(BF16)
    before = _dot(member.astype(BF16), tri) + cnt_ref[:, :1]
    ranks = [jnp.sum(jnp.where(eidx == i, before, 0.0), axis=0, keepdims=True) for i in idxs]
    rank_ref[...] = jnp.concatenate(ranks, axis=0).astype(jnp.int32)
    cnt_ref[...] = cnt_ref[...] + jnp.sum(member, axis=1, keepdims=True)


def _pool_mix(h, ext_ref, pos, pw_ref, ps_ref):
    ts, d = h.shape[-2:]
    hist = POOL_HIST_ROWS
    group = d // len(POOL_WINDOWS)
    pre = (slice(None),) * (h.ndim - 2)
    ys = []
    for g, win in enumerate(POOL_WINDOWS):
        cols = slice(g * group, (g + 1) * group)
        hg = h[pre + (slice(None), cols)]
        acc = hg
        for j in range(1, win):
            acc = acc + ext_ref[pre + (slice(hist - j, hist - j + ts), cols)]
        cnt = jnp.minimum(pos + 1, win).astype(F32)
        dg = acc / cnt - hg
        ys.append(_dot(dg.reshape(-1, group).astype(BF16), pw_ref[g]))
    return jnp.concatenate(ys, axis=-1) * ps_ref[...]


def _pool_kernel(ntp, nt, pos0_s, xp_ref, xs_ref, hist_ref, gmix_ref, pw_ref, ps_ref, gffn_ref, wr_ref, br_ref,
                 xn_ref, tok_ref, idx_ref, gate_ref, rank_ref, hlast_p_ref, hlast_s_ref, cnt_out_ref,
                 ext_p, ext_s, cnt_ref):
    i = pl.program_id(0)
    hist = POOL_HIST_ROWS

    @pl.when(i == 0)
    def _():
        cnt_ref[...] = jnp.zeros(cnt_ref.shape, F32)

    @pl.when(i < ntp)
    def _():
        ts, d = xp_ref.shape
        t = lax.rem(i, nt)

        @pl.when(t == 0)
        def _():
            ext_p[0:hist, :] = jnp.zeros((hist, d), F32)

        x = xp_ref[...]
        h = _rms(x, gmix_ref[...])
        ext_p[hist:hist + ts, :] = h

        @pl.when(t == nt - 1)
        def _():
            hlast_p_ref[0] = h[ts - hist:, :]

        pos = t * ts + lax.broadcasted_iota(jnp.int32, (ts, 1), 0)
        xn_ref[...] = x + _pool_mix(h, ext_p, pos, pw_ref, ps_ref)
        ext_p[0:hist, :] = ext_p[ts:ts + hist, :]

    @pl.when(i >= ntp)
    def _():
        bb, ts, d = xs_ref.shape
        ext_s[:, 0:hist, :] = hist_ref[...]
        x = xs_ref[...]
        h = _rms(x, gmix_ref[...])
        ext_s[:, hist:hist + ts, :] = h
        hlast_s_ref[...] = h[:, ts - hist:, :]
        pos = pos0_s + lax.broadcasted_iota(jnp.int32, (1, ts, 1), 1)
        xn_ref[...] = x.reshape(bb * ts, d) + _pool_mix(h, ext_s, pos, pw_ref, ps_ref)

    _route_tail(xn_ref[...], gffn_ref, wr_ref, br_ref, cnt_ref, tok_ref, idx_ref, gate_ref, rank_ref)
    cnt_out_ref[...] = cnt_ref[...]


def _pool_layer(x_prompt, x_sample, hist_s, pos0_s, gmix, pw, ps, gffn, wr, br):
    b, s, d = x_prompt.shape
    db, t, _ = x_sample.shape
    n_exp = wr.shape[0]
    r = ROW_TILE
    nt = s // r
    ntp = b * nt
    bb = r // t
    nts = db // bb
    n = b * s + db * t
    hist = POOL_HIST_ROWS
    rows = pl.BlockSpec((r, d), lambda i: (i, 0))
    lanes = pl.BlockSpec((TOP_K, r), lambda i: (0, i))
    sample_blk = lambda i: (jnp.maximum(i - ntp, 0), 0, 0)
    return pl.pallas_call(
        functools.partial(_pool_kernel, ntp, nt, pos0_s),
        grid=(ntp + nts,),
        in_specs=[pl.BlockSpec((r, d), lambda i: (jnp.minimum(i, ntp - 1), 0)),
                  pl.BlockSpec((bb, t, d), sample_blk), pl.BlockSpec((bb, hist, d), sample_blk),
                  _full((1, d)), _full(pw.shape), _full((1, d)), _full((1, d)), _full((n_exp, d)), _full((n_exp, 1))],
        out_specs=[rows, pl.BlockSpec((r, d // 2), lambda i: (i, 0)), lanes, lanes, lanes,
                   pl.BlockSpec((1, hist, d), lambda i: (jnp.minimum(i // nt, b - 1), 0, 0)),
                   pl.BlockSpec((bb, hist, d), sample_blk), _full((n_exp, LANES))],
        out_shape=[jax.ShapeDtypeStruct((n, d), F32), jax.ShapeDtypeStruct((n, d // 2), jnp.uint32),
                   jax.ShapeDtypeStruct((TOP_K, n), jnp.int32), jax.ShapeDtypeStruct((TOP_K, n), F32),
                   jax.ShapeDtypeStruct((TOP_K, n), jnp.int32),
                   jax.ShapeDtypeStruct((b, hist, d), F32), jax.ShapeDtypeStruct((db, hist, d), F32),
                   jax.ShapeDtypeStruct((n_exp, LANES), F32)],
        scratch_shapes=[pltpu.VMEM((hist + r, d), F32), pltpu.VMEM((bb, hist + t, d), F32),
                        pltpu.VMEM((n_exp, LANES), F32)],
        compiler_params=_params(("arbitrary",)),
        name="pool_route",
    )(x_prompt.reshape(b * s, d), x_sample, hist_s, gmix, pw, ps, gffn, wr, br)


def _proj_kernel(ntp, op_ref, os_ref, x_ref, wo_ref, gffn_ref, wr_ref, br_ref,
                 xn_ref, tok_ref, idx_ref, gate_ref, rank_ref, cnt_out_ref, cnt_ref):
    i = pl.program_id(0)

    @pl.when(i == 0)
    def _():
        cnt_ref[...] = jnp.zeros(cnt_ref.shape, F32)

    o = jnp.where(i < ntp, op_ref[...], os_ref[...])
    xn = x_ref[...] + _dot(o, wo_ref[...])
    xn_ref[...] = xn
    _route_tail(xn, gffn_ref, wr_ref, br_ref, cnt_ref, tok_ref, idx_ref, gate_ref, rank_ref)
    cnt_out_ref[...] = cnt_ref[...]


def _proj_layer(o_p, o_s, x, wo, gffn, wr, br):
    n, d = x.shape
    n_exp = wr.shape[0]
    r = ROW_TILE
    ntp = o_p.shape[0] // r
    rows = pl.BlockSpec((r, d), lambda i: (i, 0))
    lanes = pl.BlockSpec((TOP_K, r), lambda i: (0, i))
    return pl.pallas_call(
        functools.partial(_proj_kernel, ntp),
        grid=(n // r,),
        in_specs=[pl.BlockSpec((r, d), lambda i: (jnp.minimum(i, ntp - 1), 0)),
                  pl.BlockSpec((r, d), lambda i: (jnp.maximum(i - ntp, 0), 0)),
                  rows, _full((d, d)), _full((1, d)), _full((n_exp, d)), _full((n_exp, 1))],
        out_specs=[rows, pl.BlockSpec((r, d // 2), lambda i: (i, 0)), lanes, lanes, lanes, _full((n_exp, LANES))],
        out_shape=[jax.ShapeDtypeStruct((n, d), F32), jax.ShapeDtypeStruct((n, d // 2), jnp.uint32),
                   jax.ShapeDtypeStruct((TOP_K, n), jnp.int32), jax.ShapeDtypeStruct((TOP_K, n), F32),
                   jax.ShapeDtypeStruct((TOP_K, n), jnp.int32), jax.ShapeDtypeStruct((n_exp, LANES), F32)],
        scratch_shapes=[pltpu.VMEM((n_exp, LANES), F32)],
        compiler_params=_params(("arbitrary",)),
        name="proj_route",
    )(o_p, o_s, x, wo, gffn, wr, br)


def _sc_chunks(dest, w):
    n = dest.shape[1]
    return dest.reshape(TOP_K, n // w, w).transpose(1, 0, 2)


def _dispatch(tok, dest, n_buf_rows):
    n, d = tok.shape
    w = SC_CHUNK
    n_workers = SC_CORES * SC_SUBCORES
    per_worker = n // (w * n_workers)
    assert per_worker * w * n_workers == n
    mesh = plsc.VectorSubcoreMesh(core_axis_name="core", subcore_axis_name="subcore",
                                  num_cores=SC_CORES, num_subcores=SC_SUBCORES)

    @functools.partial(
        pl.kernel, mesh=mesh, out_type=jax.ShapeDtypeStruct((n_buf_rows, d), tok.dtype),
        scratch_types=[pltpu.VMEM((TOP_K, w), jnp.int32), pltpu.VMEM((w, d), tok.dtype), pltpu.SemaphoreType.DMA],
        name="moe_dispatch_sc")
    def scatter_rows(tok_hbm, dest_hbm, buf_hbm, idx_v, rows_v, sem):
        worker = lax.axis_index("subcore") * SC_CORES + lax.axis_index("core")

        @pl.loop(0, per_worker)
        def _(c):
            chunk = worker * per_worker + c
            pltpu.sync_copy(dest_hbm.at[chunk], idx_v)
            pltpu.sync_copy(tok_hbm.at[pl.ds(pl.multiple_of(chunk * w, w), w)], rows_v)
            copies = [pltpu.async_copy(rows_v, buf_hbm.at[idx_v.at[k]], sem) for k in range(TOP_K)]
            for cp in copies:
                cp.wait()

    return scatter_rows(tok, _sc_chunks(dest, w))


def _expert_kernel(be_ref, valid_ref, nused_ref, x_ref, w1_ref, b1_ref, w2_ref, b2_ref, o_ref, w1b_ref, w2b_ref):
    i = pl.program_id(0)
    used = valid_ref[i] > 0

    @pl.when(used & ((i == 0) | (be_ref[i] != be_ref[jnp.maximum(i - 1, 0)])))
    def _():
        w1b_ref[...] = w1_ref[0, 0].astype(BF16)
        w2b_ref[...] = w2_ref[0, 0].astype(BF16)

    @pl.when(used)
    def _():
        f = w2b_ref.shape[0]
        row = lax.broadcasted_iota(jnp.int32, (x_ref.shape[0], 1), 0)
        x = jnp.where(row < valid_ref[i], x_ref[...], jnp.uint32(0))
        xa, xb = _unpack_halves(x)
        half = x.shape[1]
        gu = (_dot(xa.astype(BF16), w1b_ref[:half, :]) + _dot(xb.astype(BF16), w1b_ref[half:, :])) + b1_ref[0]
        gate = jnp.minimum(gu[:, :f], SWIGLU_LIMIT)
        up = jnp.clip(gu[:, f:], -SWIGLU_LIMIT, SWIGLU_LIMIT)
        hid = (up + 1.0) * (gate * jax.nn.sigmoid(gate * SWIGLU_ALPHA))
        o_ref[...] = _pack_halves(_dot(hid.astype(BF16), w2b_ref[...]) + b2_ref[0])

    @pl.when(jnp.logical_not(used))
    def _():
        o_ref[...] = jnp.zeros(o_ref.shape, o_ref.dtype)


def _experts(buf, block_expert, valid, nused, layer, w1, b1, w2, b2, tm):
    rows, dw = buf.shape
    _, n_exp, d, f2 = w1.shape
    f = w2.shape[2]
    in_rows = lambda i, be, va, nu: (jnp.minimum(i, nu[0] - 1), 0)
    by_expert = lambda i, be, va, nu: (be[i], 0, 0)
    by_layer_expert = lambda i, be, va, nu: (layer, be[i], 0, 0)
    return pl.pallas_call(
        _expert_kernel,
        grid_spec=pltpu.PrefetchScalarGridSpec(
            num_scalar_prefetch=3,
            grid=(rows // tm,),
            in_specs=[pl.BlockSpec((tm, dw), in_rows),
                      pl.BlockSpec((1, 1, d, f2), by_layer_expert), pl.BlockSpec((1, 1, f2), by_expert),
                      pl.BlockSpec((1, 1, f, d), by_layer_expert), pl.BlockSpec((1, 1, d), by_expert)],
            out_specs=pl.BlockSpec((tm, dw), lambda i, be, va, nu: (i, 0)),
            scratch_shapes=[pltpu.VMEM((d, f2), BF16), pltpu.VMEM((f, d), BF16)]),
        out_shape=jax.ShapeDtypeStruct((rows, dw), jnp.uint32),
        compiler_params=_params(("arbitrary",)),
        name="moe_experts",
    )(block_expert, valid, nused, buf, w1, b1.reshape(n_exp, 1, f2), w2, b2.reshape(n_exp, 1, d))


def _gather_rows(ybuf, dest):
    _, d = ybuf.shape
    n = dest.shape[1]
    w = SC_GATHER_CHUNK
    n_workers = SC_CORES * SC_SUBCORES
    per_worker = n // (w * n_workers)
    assert per_worker * w * n_workers == n and TOP_K % 2 == 0
    mesh = plsc.VectorSubcoreMesh(core_axis_name="core", subcore_axis_name="subcore",
                                  num_cores=SC_CORES, num_subcores=SC_SUBCORES)

    @functools.partial(
        pl.kernel, mesh=mesh, out_type=jax.ShapeDtypeStruct((TOP_K, n, d), ybuf.dtype),
        scratch_types=[pltpu.VMEM((TOP_K, w), jnp.int32), pltpu.VMEM((2, w, d), ybuf.dtype),
                       pltpu.SemaphoreType.DMA((2,)), pltpu.SemaphoreType.DMA((2,))],
        name="moe_gather_sc")
    def gather_rows(y_hbm, dest_hbm, out_hbm, idx_v, rows_v, sem_in, sem_out):
        worker = lax.axis_index("subcore") * SC_CORES + lax.axis_index("core")

        @pl.loop(0, per_worker)
        def _(c):
            chunk = worker * per_worker + c
            tokens = pl.ds(pl.multiple_of(chunk * w, w), w)
            pltpu.sync_copy(dest_hbm.at[chunk], idx_v)
            fetch = lambda k: pltpu.async_copy(y_hbm.at[idx_v.at[k]], rows_v.at[k % 2], sem_in.at[k % 2])
            store = lambda k: pltpu.async_copy(rows_v.at[k % 2], out_hbm.at[k, tokens], sem_out.at[k % 2])
            fetches = [fetch(0), fetch(1)]
            stores = []
            for k in range(TOP_K):
                fetches[k].wait()
                stores.append(store(k))
                if k + 2 < TOP_K:
                    stores[k].wait()
                    fetches.append(fetch(k + 2))
            for k in range(TOP_K - 2, TOP_K):
                stores[k].wait()

    return gather_rows(ybuf, _sc_chunks(dest, w))


def _combine_kernel(ntp, rows_ref, gate_ref, x_ref, g_ref, *outs):
    i = pl.program_id(0)
    r = x_ref.shape[0]
    gates = jnp.concatenate([gate_ref[...], jnp.zeros((LANES - TOP_K, r), F32)], axis=0)
    gates_t = jnp.transpose(gates)
    halves = [_unpack_halves(rows_ref[k]) for k in range(TOP_K)]
    f = []
    for side in range(2):
        acc = halves[0][side] * gates_t[:, 0:1]
        for k in range(1, TOP_K):
            acc = acc + halves[k][side] * gates_t[:, k:k + 1]
        f.append(acc)
    xo = x_ref[...] + jnp.concatenate(f, axis=1)
    hn = _rms(xo, g_ref[...])
    if ntp is None:
        xo_ref, hn_ref = outs
        xo_ref[...] = xo
        hn_ref[...] = hn.astype(hn_ref.dtype)
    else:
        hp_ref, hs_ref = outs

        @pl.when(i < ntp)
        def _():
            hp_ref[...] = hn

        @pl.when(i >= ntp)
        def _():
            hs_ref[...] = hn


def _combine(rows4, gates, x, g, n_p):
    n, d = x.shape
    r = COMBINE_TILE
    rows = pl.BlockSpec((r, d), lambda i: (i, 0))
    if n_p is None:
        ntp = None
        out_specs = [rows, rows]
        out_shape = [jax.ShapeDtypeStruct((n, d), F32), jax.ShapeDtypeStruct((n, d), BF16)]
    else:
        ntp = n_p // r
        out_specs = [pl.BlockSpec((r, d), lambda i: (jnp.minimum(i, ntp - 1), 0)),
                     pl.BlockSpec((r, d), lambda i: (jnp.maximum(i - ntp, 0), 0))]
        out_shape = [jax.ShapeDtypeStruct((n_p, d), F32), jax.ShapeDtypeStruct((n - n_p, d), F32)]
    return pl.pallas_call(
        functools.partial(_combine_kernel, ntp),
        grid=(n // r,),
        in_specs=[pl.BlockSpec((TOP_K, r, d // 2), lambda i: (0, i, 0)), pl.BlockSpec((TOP_K, r), lambda i: (0, i)),
                  rows, _full((1, d))],
        out_specs=out_specs,
        out_shape=out_shape,
        compiler_params=_params(("arbitrary",)),
        name="moe_combine",
    )(rows4, gates, x, g)


def _moe(tok, idx, gates, rank, cnt, x, g_next, n_p, layer, w1, b1, w2, b2):
    n, d = tok.shape
    n_exp = w1.shape[1]
    pairs = n * TOP_K
    tm = 128
    for cand in (512, 256):
        if pairs >= 4 * cand * n_exp and pairs % cand == 0:
            tm = cand
            break
    n_blocks = -(-pairs // tm) + n_exp
    counts = cnt[:, 0].astype(jnp.int32)
    padded = (counts + tm - 1) // tm * tm
    pad_end = jnp.cumsum(padded)
    pad_start = pad_end - padded
    experts = jnp.arange(n_exp, dtype=jnp.int32)
    dest = rank + jnp.sum(jnp.where(idx[:, :, None] == experts, pad_start, 0), axis=-1)
    starts = jnp.arange(n_blocks, dtype=jnp.int32) * tm
    nused = (pad_end[-1:] // tm).astype(jnp.int32)
    block_expert = jnp.minimum(jnp.sum((pad_end[None, :] <= starts[:, None]).astype(jnp.int32), axis=1), n_exp - 1)
    tokens_end = jnp.sum(jnp.where(block_expert[:, None] == experts, pad_start + counts, 0), axis=-1)
    valid = jnp.clip(tokens_end - starts, 0, tm).astype(jnp.int32)
    buf = _dispatch(tok, dest, n_blocks * tm)
    ybuf = _experts(buf, block_expert, valid, nused, layer, w1, b1, w2, b2, tm)
    return _combine(_gather_rows(ybuf, dest), gates, x, g_next, n_p)


def _qkv_kernel(ntp, h_ref, wq_ref, wkt_ref, wvt_ref, q_ref, ktb_ref, vtb_ref, ktf_ref, vtf_ref,
                ksb_ref, vsb_ref, ksf_ref, vsf_ref):
    i = pl.program_id(0)
    r, d = h_ref.shape
    tk = ktb_ref.shape[-1]
    h = h_ref[...]
    q_ref[...] = (_dot(h, wq_ref[...]) * ((d // N_HEADS) ** -0.5)).astype(BF16)

    @pl.when(i < ntp)
    def _():
        for wt_ref, tb_ref, tf_ref in ((wkt_ref, ktb_ref, ktf_ref), (wvt_ref, vtb_ref, vtf_ref)):
            xt = _dot_nt(wt_ref[...], h)
            tf_ref[...] = xt
            for c in range(r // tk):
                tb_ref[0, c] = xt[:, c * tk:(c + 1) * tk].astype(BF16)

    @pl.when(i >= ntp)
    def _():
        for wt_ref, sb_ref, sf_ref in ((wkt_ref, ksb_ref, ksf_ref), (wvt_ref, vsb_ref, vsf_ref)):
            x = _dot_nt(h, wt_ref[...])
            sf_ref[...] = x
            sb_ref[...] = x.astype(BF16)


def _qkv(h, wq, wkt, wvt, n_streams, length):
    n, d = h.shape
    r = QKV_TILE
    tk = ATTN_TK
    n_p = n_streams * length
    nt = length // r
    ntp = n_p // r
    rows = pl.BlockSpec((r, d), lambda i: (i, 0))
    prompt_blk = lambda i: (jnp.minimum(i, ntp - 1) // nt, lax.rem(jnp.minimum(i, ntp - 1), nt))
    t_blocks = pl.BlockSpec((1, r // tk, d, tk), lambda i: prompt_blk(i) + (0, 0))
    t_full = pl.BlockSpec((d, r), prompt_blk)
    sample = pl.BlockSpec((r, d), lambda i: (jnp.maximum(i - ntp, 0), 0))
    w_spec = _full((d, d))
    return pl.pallas_call(
        functools.partial(_qkv_kernel, ntp),
        grid=(n // r,),
        in_specs=[rows, w_spec, w_spec, w_spec],
        out_specs=[rows, t_blocks, t_blocks, t_full, t_full, sample, sample, sample, sample],
        out_shape=[jax.ShapeDtypeStruct((n, d), BF16)]
        + [jax.ShapeDtypeStruct((n_streams, length // tk, d, tk), BF16)] * 2
        + [jax.ShapeDtypeStruct((n_streams * d, length), F32)] * 2
        + [jax.ShapeDtypeStruct((n - n_p, d), BF16)] * 2 + [jax.ShapeDtypeStruct((n - n_p, d), F32)] * 2,
        compiler_params=_params(("arbitrary",)),
        name="qkv",
    )(h, wq, wkt, wvt)


def _attn_setup(q_ref, qm_ref, carry_ref, acc_ref):
    tq, d = q_ref.shape
    lane = lax.broadcasted_iota(jnp.int32, (tq, LANES), 1)
    for p in range(d // LANES):
        qp = q_ref[:, p * LANES:(p + 1) * LANES]
        qm_ref[p, 0:tq, :] = jnp.where(lane < LANES // 2, qp, jnp.zeros_like(qp))
        qm_ref[p, tq:2 * tq, :] = jnp.where(lane >= LANES // 2, qp, jnp.zeros_like(qp))
    carry_ref[...] = jnp.zeros(carry_ref.shape, F32)
    acc_ref[...] = jnp.zeros(acc_ref.shape, F32)


def _suffix_sum_matrix(tk):
    src = lax.rem(lax.broadcasted_iota(jnp.int32, (2 * tk, 2 * tk), 0), tk)
    dst = lax.broadcasted_iota(jnp.int32, (2 * tk, 2 * tk), 1)
    return jnp.where((dst >= tk) | (src > dst), 1.0, 0.0).astype(BF16)


def _attn_block(qm_ref, carry_ref, acc_ref, keys, values, transposed, mask, sums):
    n_pairs, tq2, tk = carry_ref.shape
    tq = tq2 // 2
    lane = lax.broadcasted_iota(jnp.int32, (tq, LANES), 1)
    visible = (lambda x: x) if mask is None else (lambda x: jnp.where(mask, x, 0.0))
    scores = _dot if transposed else _dot_nt
    mix = _dot_nt if transposed else _dot
    zs = [scores(qm_ref[p], keys[p]) for p in range(n_pairs)]
    log_beta, parts = [], []
    for z in zs:
        sp = jnp.maximum(z, 0.0) + jnp.log(1.0 + jnp.exp(-jnp.abs(z)))
        log_keep = visible(-sp)
        hi = log_keep.astype(BF16)
        lo = (log_keep - hi.astype(F32)).astype(BF16)
        parts.append(jnp.concatenate([hi, lo], axis=1))
        log_beta.append(z - sp)
    sums_out = [_dot(part, sums) for part in parts]
    weights = []
    top = jnp.full((tq2, tk), -jnp.inf, F32)
    for p in range(n_pairs):
        carry = carry_ref[p]
        a = visible(jnp.exp(log_beta[p] + sums_out[p][:, :tk] + carry))
        weights.append(a.astype(BF16))
        carry = carry + sums_out[p][:, tk:]
        carry_ref[p] = carry
        top = jnp.maximum(top, carry)
    for p in range(n_pairs):
        out = mix(weights[p], values[p])
        acc_ref[:, p * LANES:(p + 1) * LANES] += jnp.where(lane < LANES // 2, out[:tq], out[tq:])
    return jnp.max(top)


def _keep_sweeping(state):
    j, top = state
    return (j >= 0) & (top > -ATTN_EXIT)


def _attn_prompt_kernel(q_ref, kt_ref, vt_ref, o_ref, qm_ref, carry_ref, acc_ref):
    qi = pl.program_id(1)
    tq, d = q_ref.shape
    tk = kt_ref.shape[-1]
    n_pairs = d // LANES
    _attn_setup(q_ref, qm_ref, carry_ref, acc_ref)
    row_pos = qi * tq + lax.rem(lax.broadcasted_iota(jnp.int32, (2 * tq, tk), 0), tq)
    col = lax.broadcasted_iota(jnp.int32, (2 * tq, tk), 1)
    sums = _suffix_sum_matrix(tk)

    def block(j, mask):
        keys = [kt_ref[0, j, p * LANES:(p + 1) * LANES, :] for p in range(n_pairs)]
        values = [vt_ref[0, j, p * LANES:(p + 1) * LANES, :] for p in range(n_pairs)]
        return _attn_block(qm_ref, carry_ref, acc_ref, keys, values, True, mask, sums)

    j0 = ((qi + 1) * tq - 2) // tk
    top = block(j0, (j0 * tk + col) < row_pos)
    lax.while_loop(_keep_sweeping, lambda state: (state[0] - 1, block(state[0], None)), (j0 - 1, top))
    o_ref[...] = acc_ref[...].astype(o_ref.dtype)


def _attention_prompt(q, kt, vt):
    n_streams, n_kblocks, d, tk = kt.shape
    length = n_kblocks * tk
    tq = min(length, ATTN_TQ)
    assert tk % tq == 0
    nq = length // tq
    kv = pl.BlockSpec((1, n_kblocks, d, tk), lambda b, i: (b, 0, 0, 0))
    rows = pl.BlockSpec((tq, d), lambda b, i: (b * nq + i, 0))
    return pl.pallas_call(
        _attn_prompt_kernel,
        grid=(n_streams, nq),
        in_specs=[rows, kv, kv],
        out_specs=rows,
        out_shape=jax.ShapeDtypeStruct((n_streams * length, d), BF16),
        scratch_shapes=[pltpu.VMEM((d // LANES, 2 * tq, LANES), BF16), pltpu.VMEM((d // LANES, 2 * tq, tk), F32),
                        pltpu.VMEM((tq, d), F32)],
        compiler_params=_params(("arbitrary", "arbitrary")),
        name="stick_breaking_prompt",
    )(q, kt, vt)


def _attn_sample_kernel(q_ref, kn_ref, vn_ref, ck_hbm, cv_hbm, o_ref, qm_ref, carry_ref, acc_ref, kbuf, vbuf, sem):
    b = pl.program_id(0)
    t, d = q_ref.shape
    tk = kbuf.shape[-1]
    n_pairs = d // LANES
    n_cache_blocks = ck_hbm.shape[1] // tk

    def fetch(j, slot):
        src = (pl.ds(pl.multiple_of(b * d, d), d), pl.ds(pl.multiple_of(j * tk, tk), tk))
        return (pltpu.make_async_copy(ck_hbm.at[src], kbuf.at[slot], sem.at[0, slot]),
                pltpu.make_async_copy(cv_hbm.at[src], vbuf.at[slot], sem.at[1, slot]))

    for cp in fetch(n_cache_blocks - 1, (n_cache_blocks - 1) % 2):
        cp.start()
    _attn_setup(q_ref, qm_ref, carry_ref, acc_ref)
    row = lax.rem(lax.broadcasted_iota(jnp.int32, (2 * t, tk), 0), t)
    col = lax.broadcasted_iota(jnp.int32, (2 * t, tk), 1)
    sums = _suffix_sum_matrix(tk)
    pad = jnp.zeros((tk - t, LANES), BF16)
    keys = [jnp.concatenate([kn_ref[:, p * LANES:(p + 1) * LANES], pad], axis=0) for p in range(n_pairs)]
    values = [jnp.concatenate([vn_ref[:, p * LANES:(p + 1) * LANES], pad], axis=0) for p in range(n_pairs)]
    top = _attn_block(qm_ref, carry_ref, acc_ref, keys, values, False, col < row, sums)

    def body(state):
        j, _ = state
        slot = lax.rem(j, 2)
        for cp in fetch(j, slot):
            cp.wait()

        @pl.when(j > 0)
        def _():
            for cp in fetch(j - 1, 1 - slot):
                cp.start()

        keys = [kbuf[slot, p * LANES:(p + 1) * LANES, :].astype(BF16) for p in range(n_pairs)]
        values = [vbuf[slot, p * LANES:(p + 1) * LANES, :].astype(BF16) for p in range(n_pairs)]
        return j - 1, _attn_block(qm_ref, carry_ref, acc_ref, keys, values, True, None, sums)

    j_end, _ = lax.while_loop(_keep_sweeping, body, (jnp.int32(n_cache_blocks - 1), top))

    @pl.when(j_end >= 0)
    def _():
        for cp in fetch(j_end, lax.rem(j_end, 2)):
            cp.wait()

    o_ref[...] = acc_ref[...].astype(o_ref.dtype)


def _attention_sample(q, k_new, v_new, cache_kt, cache_vt, t, q_row_off):
    d = q.shape[1]
    n_streams = k_new.shape[0] // t
    tk = ATTN_TK
    assert cache_kt.shape[1] % tk == 0 and t <= tk
    any_spec = pl.BlockSpec(memory_space=pl.ANY)
    new = pl.BlockSpec((t, d), lambda b: (b, 0))
    return pl.pallas_call(
        _attn_sample_kernel,
        grid=(n_streams,),
        in_specs=[pl.BlockSpec((t, d), lambda b: (q_row_off // t + b, 0)), new, new, any_spec, any_spec],
        out_specs=new,
        out_shape=jax.ShapeDtypeStruct((n_streams * t, d), BF16),
        scratch_shapes=[pltpu.VMEM((d // LANES, 2 * t, LANES), BF16), pltpu.VMEM((d // LANES, 2 * t, tk), F32),
                        pltpu.VMEM((t, d), F32), pltpu.VMEM((2, d, tk), F32), pltpu.VMEM((2, d, tk), F32),
                        pltpu.SemaphoreType.DMA((2, 2))],
        compiler_params=_params(("arbitrary",)),
        name="stick_breaking_sample",
    )(q, k_new, v_new, cache_kt, cache_vt)


def kernel(x_prompt, x_sample, state_pool, cache_k, cache_v, norm_mix, norm_ffn, pool_w, pool_scale, w_qkv, w_o, router_w, router_b, moe_w1, moe_b1, moe_w2, moe_b2, final_norm):
    b, s, d = x_prompt.shape
    db, t, _ = x_sample.shape
    past = cache_k.shape[2]
    n_exp = router_w.shape[2]
    hd = d // N_HEADS
    hist = POOL_HIST_ROWS
    n_p, n_s = b * s, db * t
    assert t >= hist and ROW_TILE % t == 0 and s % ROW_TILE == 0 and n_s % ROW_TILE == 0 and ROW_TILE % QKV_TILE == 0
    row = lambda a: a.reshape(1, -1)
    wr = [router_w[i].T for i in range(2)]
    br = [router_b[i].reshape(n_exp, 1) for i in range(2)]

    hist_s = jnp.concatenate([jnp.zeros((db, 1, d), F32), state_pool[0]], axis=1)
    xn, tok, idx, gates, rank, hlast_p, hlast_s, cnt = _pool_layer(
        x_prompt, x_sample, hist_s, past, row(norm_mix[0]), pool_w[0].astype(BF16), row(pool_scale[0]),
        row(norm_ffn[0]), wr[0], br[0])
    x1, h1 = _moe(tok, idx, gates, rank, cnt, xn, row(norm_mix[1]), None, 0, moe_w1, moe_b1[0], moe_w2, moe_b2[0])

    wqkv = w_qkv[0].astype(BF16)
    q, ktb, vtb, ktf, vtf, ksb, vsb, ksf, vsf = _qkv(h1, wqkv[:, :d], wqkv[:, d:2 * d].T, wqkv[:, 2 * d:].T, b, s)
    o_p = _attention_prompt(q, ktb, vtb)
    transposed = lambda cache: cache.transpose(0, 1, 3, 4, 2).reshape(db * d, past)
    o_s = _attention_sample(q, ksb, vsb, transposed(cache_k), transposed(cache_v), t, n_p)
    xn, tok, idx, gates, rank, cnt = _proj_layer(o_p, o_s, x1, w_o[0].astype(BF16), row(norm_ffn[1]), wr[1], br[1])
    y_p, y_s = _moe(tok, idx, gates, rank, cnt, xn, row(final_norm), n_p, 1, moe_w1, moe_b1[1], moe_w2, moe_b2[1])

    frames_major = lambda xt: xt.reshape(1, b, N_HEADS, hd, s).transpose(0, 1, 4, 2, 3)
    heads = lambda a: a.reshape(1, db, t, N_HEADS, hd)
    return (y_p.reshape(b, s, d), y_s.reshape(db, t, d), hlast_p[None, :, 1:, :],
            frames_major(ktf), frames_major(vtf), hlast_s[None, :, 1:, :], heads(ksf), heads(vsf))
```

```python
import functools

import jax
import jax.numpy as jnp
from jax import lax
from jax.experimental import pallas as pl
from jax.experimental.pallas import tpu as pltpu
from jax.experimental.pallas import tpu_sc as plsc

EPS = 1e-5
POOL_WINDOWS = (2, 4, 8, 16)
POOL_HIST_ROWS = 16
N_HEADS = 16
TOP_K = 4
SWIGLU_LIMIT = 7.0
SWIGLU_ALPHA = 1.702
LANES = 128
ROW_TILE = 512
QKV_TILE = 256
ATTN_TQ = 128
ATTN_TK = 128
ATTN_EXIT = 104.0
VMEM_LIMIT = 56 * 1024 * 1024
SC_CORES = 2
SC_SUBCORES = 16
SC_CHUNK = 96
SC_GATHER_CHUNK = 48
COMBINE_TILE = 512

F32 = jnp.float32
BF16 = jnp.bfloat16


def _rms(x, g):
    ms = jnp.mean(x * x, axis=-1, keepdims=True)
    return x * lax.rsqrt(ms + EPS) * g


def _dot(a, b):
    return jnp.dot(a, b, preferred_element_type=F32)


def _dot_nt(a, b, precision=None):
    return lax.dot_general(a, b, (((1,), (1,)), ((), ())), preferred_element_type=F32, precision=precision)


def _pack_halves(x):
    c = x.shape[1] // 2
    bits = lax.bitcast_convert_type(x.astype(BF16).astype(F32), jnp.uint32)
    return bits[:, :c] | (bits[:, c:] >> 16)


def _unpack_halves(w):
    return (lax.bitcast_convert_type(w & jnp.uint32(0xFFFF0000), F32), lax.bitcast_convert_type(w << 16, F32))


def _params(semantics):
    return pltpu.CompilerParams(dimension_semantics=semantics, vmem_limit_bytes=VMEM_LIMIT)


def _full(shape):
    return pl.BlockSpec(shape, lambda i, *_: (0,) * len(shape))


def _route_init(cnt_ref, tri_ref):
    r = tri_ref.shape[0]
    cnt_ref[...] = jnp.zeros(cnt_ref.shape, F32)
    tri_ref[...] = jnp.where(lax.broadcasted_iota(jnp.int32, (r, r), 0) < lax.broadcasted_iota(jnp.int32, (r, r), 1),
                             1.0, 0.0).astype(BF16)


def _route_tail(xn, gffn_ref, wr_ref, br_ref, cnt_ref, tri_ref, tok_ref, idx_ref, gate_ref, rank_ref):
    n_exp = wr_ref.shape[0]
    tok = _rms(xn, gffn_ref[...])
    tok_ref[...] = _pack_halves(tok)
    logits = _dot_nt(wr_ref[...], tok, precision=lax.Precision.HIGHEST) + br_ref[...]
    eidx = lax.broadcasted_iota(jnp.int32, logits.shape, 0).astype(F32)
    vals, idxs = [], []
    l = logits
    for _ in range(TOP_K):
        m = jnp.max(l, axis=0, keepdims=True)
        i = jnp.min(jnp.where(l == m, eidx, float(n_exp)), axis=0, keepdims=True)
        vals.append(m)
        idxs.append(i)
        l = jnp.where(eidx == i, -jnp.inf, l)
    es = [jnp.exp(v - vals[0]) for v in vals]
    den = es[0]
    for e in es[1:]:
        den = den + e
    gate_ref[...] = jnp.concatenate([e / den for e in es], axis=0)
    idx_ref[...] = jnp.concatenate(idxs, axis=0).astype(jnp.int32)
    member = jnp.zeros(logits.shape, F32)
    for i in idxs:
        member = member + jnp.where(eidx == i, 1.0, 0.0)
    before = _dot(member.astype(BF16), tri_ref[...]) + cnt_ref[:, :1]
    ranks = [jnp.sum(jnp.where(eidx == i, before, 0.0), axis=0, keepdims=True) for i in idxs]
    rank_ref[...] = jnp.concatenate(ranks, axis=0).astype(jnp.int32)
    cnt_ref[...] = cnt_ref[...] + jnp.sum(member, axis=1, keepdims=True)


def _pool_mix(h, ext_ref, pos, pw_ref, ps_ref):
    ts, d = h.shape[-2:]
    hist = POOL_HIST_ROWS
    group = d // len(POOL_WINDOWS)
    pre = (slice(None),) * (h.ndim - 2)
    ys = []
    for g, win in enumerate(POOL_WINDOWS):
        cols = slice(g * group, (g + 1) * group)
        hg = h[pre + (slice(None), cols)]
        acc = hg
        for j in range(1, win):
            acc = acc + ext_ref[pre + (slice(hist - j, hist - j + ts), cols)]
        cnt = jnp.minimum(pos + 1, win).astype(F32)
        dg = acc / cnt - hg
        ys.append(_dot(dg.reshape(-1, group).astype(BF16), pw_ref[g]))
    return jnp.concatenate(ys, axis=-1) * ps_ref[...]


def _pool_mix_tiled(h, ext_ref, lvl_ref, pos, pw_ref, ps_ref):
    ts, d = h.shape
    hist = POOL_HIST_ROWS
    group = d // len(POOL_WINDOWS)
    prev_ref, prev_col0 = ext_ref, 0
    ys = []
    for g, win in enumerate(POOL_WINDOWS):
        assert win == 2 * (POOL_WINDOWS[g - 1] if g else 1)
        col0 = g * group
        cols = slice(col0 - prev_col0, d - prev_col0)
        level = prev_ref[hist:hist + ts, cols] + prev_ref[hist - win // 2:hist - win // 2 + ts, cols]
        if g + 1 < len(POOL_WINDOWS):
            lvl_ref[g, hist:hist + ts, 0:d - col0 - group] = level[:, group:]
            prev_ref, prev_col0 = lvl_ref.at[g], col0 + group
        cnt = jnp.minimum(pos + 1, win).astype(F32)
        dg = level[:, :group] / cnt - h[:, col0:col0 + group]
        ys.append(_dot(dg.astype(BF16), pw_ref[g]))
    return jnp.concatenate(ys, axis=-1) * ps_ref[...]


def _pool_kernel(ntp, nt, pos0_s, xp_ref, xs_ref, hist_ref, gmix_ref, pw_ref, ps_ref, gffn_ref, wr_ref, br_ref,
                 xn_ref, tok_ref, idx_ref, gate_ref, rank_ref, hlast_p_ref, hlast_s_ref, cnt_out_ref,
                 ext_p, lvl_p, ext_s, cnt_ref, tri_ref):
    i = pl.program_id(0)
    hist = POOL_HIST_ROWS

    @pl.when(i == 0)
    def _():
        _route_init(cnt_ref, tri_ref)

    @pl.when(i < ntp)
    def _():
        ts, d = xp_ref.shape
        t = lax.rem(i, nt)

        @pl.when(t == 0)
        def _():
            ext_p[0:hist, :] = jnp.zeros((hist, d), F32)
            lvl_p[:, 0:hist, :] = jnp.zeros((lvl_p.shape[0], hist, lvl_p.shape[2]), F32)

        x = xp_ref[...]
        h = _rms(x, gmix_ref[...])
        ext_p[hist:hist + ts, :] = h

        @pl.when(t == nt - 1)
        def _():
            hlast_p_ref[0] = h[ts - hist:, :]

        pos = t * ts + lax.broadcasted_iota(jnp.int32, (ts, 1), 0)
        xn_ref[...] = x + _pool_mix_tiled(h, ext_p, lvl_p, pos, pw_ref, ps_ref)
        ext_p[0:hist, :] = ext_p[ts:ts + hist, :]
        lvl_p[:, 0:hist, :] = lvl_p[:, ts:ts + hist, :]

    @pl.when(i >= ntp)
    def _():
        bb, ts, d = xs_ref.shape
        ext_s[:, 0:hist, :] = hist_ref[...]
        x = xs_ref[...]
        h = _rms(x, gmix_ref[...])
        ext_s[:, hist:hist + ts, :] = h
        hlast_s_ref[...] = h[:, ts - hist:, :]
        pos = pos0_s + lax.broadcasted_iota(jnp.int32, (1, ts, 1), 1)
        xn_ref[...] = x.reshape(bb * ts, d) + _pool_mix(h, ext_s, pos, pw_ref, ps_ref)

    _route_tail(xn_ref[...], gffn_ref, wr_ref, br_ref, cnt_ref, tri_ref, tok_ref, idx_ref, gate_ref, rank_ref)
    cnt_out_ref[...] = cnt_ref[...]


def _pool_layer(x_prompt, x_sample, hist_s, pos0_s, gmix, pw, ps, gffn, wr, br):
    b, s, d = x_prompt.shape
    db, t, _ = x_sample.shape
    n_exp = wr.shape[0]
    r = ROW_TILE
    nt = s // r
    ntp = b * nt
    bb = r // t
    nts = db // bb
    n = b * s + db * t
    hist = POOL_HIST_ROWS
    rows = pl.BlockSpec((r, d), lambda i: (i, 0))
    lanes = pl.BlockSpec((TOP_K, r), lambda i: (0, i))
    sample_blk = lambda i: (jnp.maximum(i - ntp, 0), 0, 0)
    return pl.pallas_call(
        functools.partial(_pool_kernel, ntp, nt, pos0_s),
        grid=(ntp + nts,),
        in_specs=[pl.BlockSpec((r, d), lambda i: (jnp.minimum(i, ntp - 1), 0)),
                  pl.BlockSpec((bb, t, d), sample_blk), pl.BlockSpec((bb, hist, d), sample_blk),
                  _full((1, d)), _full(pw.shape), _full((1, d)), _full((1, d)), _full((n_exp, d)), _full((n_exp, 1))],
        out_specs=[rows, pl.BlockSpec((r, d // 2), lambda i: (i, 0)), lanes, lanes, lanes,
                   pl.BlockSpec((1, hist, d), lambda i: (jnp.minimum(i // nt, b - 1), 0, 0)),
                   pl.BlockSpec((bb, hist, d), sample_blk), _full((n_exp, LANES))],
        out_shape=[jax.ShapeDtypeStruct((n, d), F32), jax.ShapeDtypeStruct((n, d // 2), jnp.uint32),
                   jax.ShapeDtypeStruct((TOP_K, n), jnp.int32), jax.ShapeDtypeStruct((TOP_K, n), F32),
                   jax.ShapeDtypeStruct((TOP_K, n), jnp.int32),
                   jax.ShapeDtypeStruct((b, hist, d), F32), jax.ShapeDtypeStruct((db, hist, d), F32),
                   jax.ShapeDtypeStruct((n_exp, LANES), F32)],
        scratch_shapes=[pltpu.VMEM((hist + r, d), F32),
                        pltpu.VMEM((len(POOL_WINDOWS) - 1, hist + r, d - d // len(POOL_WINDOWS)), F32),
                        pltpu.VMEM((bb, hist + t, d), F32), pltpu.VMEM((n_exp, LANES), F32), pltpu.VMEM((r, r), BF16)],
        compiler_params=_params(("arbitrary",)),
        name="pool_route",
    )(x_prompt.reshape(b * s, d), x_sample, hist_s, gmix, pw, ps, gffn, wr, br)


def _proj_kernel(ntp, op_ref, os_ref, x_ref, wo_ref, gffn_ref, wr_ref, br_ref,
                 xn_ref, tok_ref, idx_ref, gate_ref, rank_ref, cnt_out_ref, cnt_ref, tri_ref):
    i = pl.program_id(0)

    @pl.when(i == 0)
    def _():
        _route_init(cnt_ref, tri_ref)

    o = jnp.where(i < ntp, op_ref[...], os_ref[...])
    xn = x_ref[...] + _dot(o, wo_ref[...])
    xn_ref[...] = xn
    _route_tail(xn, gffn_ref, wr_ref, br_ref, cnt_ref, tri_ref, tok_ref, idx_ref, gate_ref, rank_ref)
    cnt_out_ref[...] = cnt_ref[...]


def _proj_layer(o_p, o_s, x, wo, gffn, wr, br):
    n, d = x.shape
    n_exp = wr.shape[0]
    r = ROW_TILE
    ntp = o_p.shape[0] // r
    rows = pl.BlockSpec((r, d), lambda i: (i, 0))
    lanes = pl.BlockSpec((TOP_K, r), lambda i: (0, i))
    return pl.pallas_call(
        functools.partial(_proj_kernel, ntp),
        grid=(n // r,),
        in_specs=[pl.BlockSpec((r, d), lambda i: (jnp.minimum(i, ntp - 1), 0)),
                  pl.BlockSpec((r, d), lambda i: (jnp.maximum(i - ntp, 0), 0)),
                  rows, _full((d, d)), _full((1, d)), _full((n_exp, d)), _full((n_exp, 1))],
        out_specs=[rows, pl.BlockSpec((r, d // 2), lambda i: (i, 0)), lanes, lanes, lanes, _full((n_exp, LANES))],
        out_shape=[jax.ShapeDtypeStruct((n, d), F32), jax.ShapeDtypeStruct((n, d // 2), jnp.uint32),
                   jax.ShapeDtypeStruct((TOP_K, n), jnp.int32), jax.ShapeDtypeStruct((TOP_K, n), F32),
                   jax.ShapeDtypeStruct((TOP_K, n), jnp.int32), jax.ShapeDtypeStruct((n_exp, LANES), F32)],
        scratch_shapes=[pltpu.VMEM((n_exp, LANES), F32), pltpu.VMEM((r, r), BF16)],
        compiler_params=_params(("arbitrary",)),
        name="proj_route",
    )(o_p, o_s, x, wo, gffn, wr, br)


def _sc_chunks(dest, w):
    n = dest.shape[1]
    return dest.reshape(TOP_K, n // w, w).transpose(1, 0, 2)


def _dispatch(tok, dest, n_buf_rows):
    n, d = tok.shape
    w = SC_CHUNK
    n_workers = SC_CORES * SC_SUBCORES
    per_worker = n // (w * n_workers)
    assert per_worker * w * n_workers == n
    mesh = plsc.VectorSubcoreMesh(core_axis_name="core", subcore_axis_name="subcore",
                                  num_cores=SC_CORES, num_subcores=SC_SUBCORES)

    @functools.partial(
        pl.kernel, mesh=mesh, out_type=jax.ShapeDtypeStruct((n_buf_rows, d), tok.dtype),
        scratch_types=[pltpu.VMEM((TOP_K, w), jnp.int32), pltpu.VMEM((w, d), tok.dtype), pltpu.SemaphoreType.DMA],
        name="moe_dispatch_sc")
    def scatter_rows(tok_hbm, dest_hbm, buf_hbm, idx_v, rows_v, sem):
        worker = lax.axis_index("subcore") * SC_CORES + lax.axis_index("core")

        @pl.loop(0, per_worker)
        def _(c):
            chunk = worker * per_worker + c
            pltpu.sync_copy(dest_hbm.at[chunk], idx_v)
            pltpu.sync_copy(tok_hbm.at[pl.ds(pl.multiple_of(chunk * w, w), w)], rows_v)
            copies = [pltpu.async_copy(rows_v, buf_hbm.at[idx_v.at[k]], sem) for k in range(TOP_K)]
            for cp in copies:
                cp.wait()

    return scatter_rows(tok, _sc_chunks(dest, w))


def _expert_kernel(be_ref, valid_ref, nused_ref, x_ref, w1_ref, b1_ref, w2_ref, b2_ref, o_ref, w1b_ref, w2b_ref):
    i = pl.program_id(0)
    used = valid_ref[i] > 0

    @pl.when(used & ((i == 0) | (be_ref[i] != be_ref[jnp.maximum(i - 1, 0)])))
    def _():
        w1b_ref[...] = w1_ref[0, 0].astype(BF16)
        w2b_ref[...] = w2_ref[0, 0].astype(BF16)

    @pl.when(used)
    def _():
        f = w2b_ref.shape[0]
        row = lax.broadcasted_iota(jnp.int32, (x_ref.shape[0], 1), 0)
        x = jnp.where(row < valid_ref[i], x_ref[...], jnp.uint32(0))
        xa, xb = _unpack_halves(x)
        half = x.shape[1]
        gu = (_dot(xa.astype(BF16), w1b_ref[:half, :]) + _dot(xb.astype(BF16), w1b_ref[half:, :])) + b1_ref[0]
        gate = jnp.minimum(gu[:, :f], SWIGLU_LIMIT)
        up = jnp.clip(gu[:, f:], -SWIGLU_LIMIT, SWIGLU_LIMIT)
        hid = (up + 1.0) * (gate * jax.nn.sigmoid(gate * SWIGLU_ALPHA))
        o_ref[...] = _pack_halves(_dot(hid.astype(BF16), w2b_ref[...]) + b2_ref[0])

    @pl.when(jnp.logical_not(used))
    def _():
        o_ref[...] = jnp.zeros(o_ref.shape, o_ref.dtype)


def _experts(buf, block_expert, valid, nused, layer, w1, b1, w2, b2, tm):
    rows, dw = buf.shape
    _, n_exp, d, f2 = w1.shape
    f = w2.shape[2]
    in_rows = lambda i, be, va, nu: (jnp.minimum(i, nu[0] - 1), 0)
    by_expert = lambda i, be, va, nu: (be[i], 0, 0)
    by_layer_expert = lambda i, be, va, nu: (layer, be[i], 0, 0)
    return pl.pallas_call(
        _expert_kernel,
        grid_spec=pltpu.PrefetchScalarGridSpec(
            num_scalar_prefetch=3,
            grid=(rows // tm,),
            in_specs=[pl.BlockSpec((tm, dw), in_rows),
                      pl.BlockSpec((1, 1, d, f2), by_layer_expert), pl.BlockSpec((1, 1, f2), by_expert),
                      pl.BlockSpec((1, 1, f, d), by_layer_expert), pl.BlockSpec((1, 1, d), by_expert)],
            out_specs=pl.BlockSpec((tm, dw), lambda i, be, va, nu: (i, 0)),
            scratch_shapes=[pltpu.VMEM((d, f2), BF16), pltpu.VMEM((f, d), BF16)]),
        out_shape=jax.ShapeDtypeStruct((rows, dw), jnp.uint32),
        compiler_params=_params(("arbitrary",)),
        name="moe_experts",
    )(block_expert, valid, nused, buf, w1, b1.reshape(n_exp, 1, f2), w2, b2.reshape(n_exp, 1, d))


def _gather_rows(ybuf, dest):
    _, d = ybuf.shape
    n = dest.shape[1]
    w = SC_GATHER_CHUNK
    n_workers = SC_CORES * SC_SUBCORES
    per_worker = n // (w * n_workers)
    assert per_worker * w * n_workers == n and TOP_K % 2 == 0
    mesh = plsc.VectorSubcoreMesh(core_axis_name="core", subcore_axis_name="subcore",
                                  num_cores=SC_CORES, num_subcores=SC_SUBCORES)

    @functools.partial(
        pl.kernel, mesh=mesh, out_type=jax.ShapeDtypeStruct((TOP_K, n, d), ybuf.dtype),
        scratch_types=[pltpu.VMEM((TOP_K, w), jnp.int32), pltpu.VMEM((2, w, d), ybuf.dtype),
                       pltpu.SemaphoreType.DMA((2,)), pltpu.SemaphoreType.DMA((2,))],
        name="moe_gather_sc")
    def gather_rows(y_hbm, dest_hbm, out_hbm, idx_v, rows_v, sem_in, sem_out):
        worker = lax.axis_index("subcore") * SC_CORES + lax.axis_index("core")

        @pl.loop(0, per_worker)
        def _(c):
            chunk = worker * per_worker + c
            tokens = pl.ds(pl.multiple_of(chunk * w, w), w)
            pltpu.sync_copy(dest_hbm.at[chunk], idx_v)
            fetch = lambda k: pltpu.async_copy(y_hbm.at[idx_v.at[k]], rows_v.at[k % 2], sem_in.at[k % 2])
            store = lambda k: pltpu.async_copy(rows_v.at[k % 2], out_hbm.at[k, tokens], sem_out.at[k % 2])
            fetches = [fetch(0), fetch(1)]
            stores = []
            for k in range(TOP_K):
                fetches[k].wait()
                stores.append(store(k))
                if k + 2 < TOP_K:
                    stores[k].wait()
                    fetches.append(fetch(k + 2))
            for k in range(TOP_K - 2, TOP_K):
                stores[k].wait()

    return gather_rows(ybuf, _sc_chunks(dest, w))


def _combine_kernel(ntp, rows_ref, gate_ref, x_ref, g_ref, *outs):
    i = pl.program_id(0)
    r = x_ref.shape[0]
    gates = jnp.concatenate([gate_ref[...], jnp.zeros((LANES - TOP_K, r), F32)], axis=0)
    gates_t = jnp.transpose(gates)
    halves = [_unpack_halves(rows_ref[k]) for k in range(TOP_K)]
    f = []
    for side in range(2):
        acc = halves[0][side] * gates_t[:, 0:1]
        for k in range(1, TOP_K):
            acc = acc + halves[k][side] * gates_t[:, k:k + 1]
        f.append(acc)
    xo = x_ref[...] + jnp.concatenate(f, axis=1)
    hn = _rms(xo, g_ref[...])
    if ntp is None:
        xo_ref, hn_ref = outs
        xo_ref[...] = xo
        hn_ref[...] = hn.astype(hn_ref.dtype)
    else:
        hp_ref, hs_ref = outs

        @pl.when(i < ntp)
        def _():
            hp_ref[...] = hn

        @pl.when(i >= ntp)
        def _():
            hs_ref[...] = hn


def _combine(rows4, gates, x, g, n_p):
    n, d = x.shape
    r = COMBINE_TILE
    rows = pl.BlockSpec((r, d), lambda i: (i, 0))
    if n_p is None:
        ntp = None
        out_specs = [rows, rows]
        out_shape = [jax.ShapeDtypeStruct((n, d), F32), jax.ShapeDtypeStruct((n, d), BF16)]
    else:
        ntp = n_p // r
        out_specs = [pl.BlockSpec((r, d), lambda i: (jnp.minimum(i, ntp - 1), 0)),
                     pl.BlockSpec((r, d), lambda i: (jnp.maximum(i - ntp, 0), 0))]
        out_shape = [jax.ShapeDtypeStruct((n_p, d), F32), jax.ShapeDtypeStruct((n - n_p, d), F32)]
    return pl.pallas_call(
        functools.partial(_combine_kernel, ntp),
        grid=(n // r,),
        in_specs=[pl.BlockSpec((TOP_K, r, d // 2), lambda i: (0, i, 0)), pl.BlockSpec((TOP_K, r), lambda i: (0, i)),
                  rows, _full((1, d))],
        out_specs=out_specs,
        out_shape=out_shape,
        compiler_params=_params(("arbitrary",)),
        name="moe_combine",
    )(rows4, gates, x, g)


def _moe(tok, idx, gates, rank, cnt, x, g_next, n_p, layer, w1, b1, w2, b2):
    n, d = tok.shape
    n_exp = w1.shape[1]
    pairs = n * TOP_K
    tm = 128
    for cand in (512, 256):
        if pairs >= 4 * cand * n_exp and pairs % cand == 0:
            tm = cand
            break
    n_blocks = -(-pairs // tm) + n_exp
    counts = cnt[:, 0].astype(jnp.int32)
    padded = (counts + tm - 1) // tm * tm
    pad_end = jnp.cumsum(padded)
    pad_start = pad_end - padded
    experts = jnp.arange(n_exp, dtype=jnp.int32)
    dest = rank + jnp.sum(jnp.where(idx[:, :, None] == experts, pad_start, 0), axis=-1)
    starts = jnp.arange(n_blocks, dtype=jnp.int32) * tm
    nused = (pad_end[-1:] // tm).astype(jnp.int32)
    block_expert = jnp.minimum(jnp.sum((pad_end[None, :] <= starts[:, None]).astype(jnp.int32), axis=1), n_exp - 1)
    tokens_end = jnp.sum(jnp.where(block_expert[:, None] == experts, pad_start + counts, 0), axis=-1)
    valid = jnp.clip(tokens_end - starts, 0, tm).astype(jnp.int32)
    buf = _dispatch(tok, dest, n_blocks * tm)
    ybuf = _experts(buf, block_expert, valid, nused, layer, w1, b1, w2, b2, tm)
    return _combine(_gather_rows(ybuf, dest), gates, x, g_next, n_p)


def _qkv_kernel(ntp, h_ref, wq_ref, wkt_ref, wvt_ref, q_ref, ktb_ref, vtb_ref, ktf_ref, vtf_ref,
                ksb_ref, vsb_ref, ksf_ref, vsf_ref):
    i = pl.program_id(0)
    r, d = h_ref.shape
    tk = ktb_ref.shape[-1]
    h = h_ref[...]
    q_ref[...] = (_dot(h, wq_ref[...]) * ((d // N_HEADS) ** -0.5)).astype(BF16)

    @pl.when(i < ntp)
    def _():
        for wt_ref, tb_ref, tf_ref in ((wkt_ref, ktb_ref, ktf_ref), (wvt_ref, vtb_ref, vtf_ref)):
            xt = _dot_nt(wt_ref[...], h)
            tf_ref[...] = xt
            for c in range(r // tk):
                tb_ref[0, c] = xt[:, c * tk:(c + 1) * tk].astype(BF16)

    @pl.when(i >= ntp)
    def _():
        for wt_ref, sb_ref, sf_ref in ((wkt_ref, ksb_ref, ksf_ref), (wvt_ref, vsb_ref, vsf_ref)):
            x = _dot_nt(h, wt_ref[...])
            sf_ref[...] = x
            sb_ref[...] = x.astype(BF16)


def _qkv(h, wq, wkt, wvt, n_streams, length):
    n, d = h.shape
    r = QKV_TILE
    tk = ATTN_TK
    n_p = n_streams * length
    nt = length // r
    ntp = n_p // r
    rows = pl.BlockSpec((r, d), lambda i: (i, 0))
    prompt_blk = lambda i: (jnp.minimum(i, ntp - 1) // nt, lax.rem(jnp.minimum(i, ntp - 1), nt))
    t_blocks = pl.BlockSpec((1, r // tk, d, tk), lambda i: prompt_blk(i) + (0, 0))
    t_full = pl.BlockSpec((d, r), prompt_blk)
    sample = pl.BlockSpec((r, d), lambda i: (jnp.maximum(i - ntp, 0), 0))
    w_spec = _full((d, d))
    return pl.pallas_call(
        functools.partial(_qkv_kernel, ntp),
        grid=(n // r,),
        in_specs=[rows, w_spec, w_spec, w_spec],
        out_specs=[rows, t_blocks, t_blocks, t_full, t_full, sample, sample, sample, sample],
        out_shape=[jax.ShapeDtypeStruct((n, d), BF16)]
        + [jax.ShapeDtypeStruct((n_streams, length // tk, d, tk), BF16)] * 2
        + [jax.ShapeDtypeStruct((n_streams * d, length), F32)] * 2
        + [jax.ShapeDtypeStruct((n - n_p, d), BF16)] * 2 + [jax.ShapeDtypeStruct((n - n_p, d), F32)] * 2,
        compiler_params=_params(("arbitrary",)),
        name="qkv",
    )(h, wq, wkt, wvt)


def _attn_setup(q_ref, qm_ref, carry_ref, acc_ref):
    tq, d = q_ref.shape
    lane = lax.broadcasted_iota(jnp.int32, (tq, LANES), 1)
    for p in range(d // LANES):
        qp = q_ref[:, p * LANES:(p + 1) * LANES]
        qm_ref[p, 0:tq, :] = jnp.where(lane < LANES // 2, qp, jnp.zeros_like(qp))
        qm_ref[p, tq:2 * tq, :] = jnp.where(lane >= LANES // 2, qp, jnp.zeros_like(qp))
    carry_ref[...] = jnp.zeros(carry_ref.shape, F32)
    acc_ref[...] = jnp.zeros(acc_ref.shape, F32)


def _suffix_sum_matrix(tk):
    src = lax.rem(lax.broadcasted_iota(jnp.int32, (2 * tk, 2 * tk), 0), tk)
    dst = lax.broadcasted_iota(jnp.int32, (2 * tk, 2 * tk), 1)
    return jnp.where((dst >= tk) | (src > dst), 1.0, 0.0).astype(BF16)


def _attn_block(qm_ref, carry_ref, acc_ref, keys, values, transposed, mask, sums):
    n_pairs, tq2, tk = carry_ref.shape
    tq = tq2 // 2
    lane = lax.broadcasted_iota(jnp.int32, (tq, LANES), 1)
    visible = (lambda x: x) if mask is None else (lambda x: jnp.where(mask, x, 0.0))
    scores = _dot if transposed else _dot_nt
    mix = _dot_nt if transposed else _dot
    zs = [scores(qm_ref[p], keys[p]) for p in range(n_pairs)]
    log_beta, parts = [], []
    for z in zs:
        sp = jnp.maximum(z, 0.0) + jnp.log(1.0 + jnp.exp(-jnp.abs(z)))
        log_keep = visible(-sp)
        hi = log_keep.astype(BF16)
        lo = (log_keep - hi.astype(F32)).astype(BF16)
        parts.append(jnp.concatenate([hi, lo], axis=1))
        log_beta.append(z - sp)
    sums_out = [_dot(part, sums) for part in parts]
    weights = []
    top = jnp.full((tq2, tk), -jnp.inf, F32)
    for p in range(n_pairs):
        carry = carry_ref[p]
        a = visible(jnp.exp(log_beta[p] + sums_out[p][:, :tk] + carry))
        weights.append(a.astype(BF16))
        carry = carry + sums_out[p][:, tk:]
        carry_ref[p] = carry
        top = jnp.maximum(top, carry)
    for p in range(n_pairs):
        out = mix(weights[p], values[p])
        acc_ref[:, p * LANES:(p + 1) * LANES] += jnp.where(lane < LANES // 2, out[:tq], out[tq:])
    return jnp.max(top)


def _keep_sweeping(state):
    j, top = state
    return (j >= 0) & (top > -ATTN_EXIT)


def _attn_prompt_kernel(q_ref, kt_ref, vt_ref, o_ref, qm_ref, carry_ref, acc_ref):
    qi = pl.program_id(1)
    tq, d = q_ref.shape
    tk = kt_ref.shape[-1]
    n_pairs = d // LANES
    _attn_setup(q_ref, qm_ref, carry_ref, acc_ref)
    row_pos = qi * tq + lax.rem(lax.broadcasted_iota(jnp.int32, (2 * tq, tk), 0), tq)
    col = lax.broadcasted_iota(jnp.int32, (2 * tq, tk), 1)
    sums = _suffix_sum_matrix(tk)

    def block(j, mask):
        keys = [kt_ref[0, j, p * LANES:(p + 1) * LANES, :] for p in range(n_pairs)]
        values = [vt_ref[0, j, p * LANES:(p + 1) * LANES, :] for p in range(n_pairs)]
        return _attn_block(qm_ref, carry_ref, acc_ref, keys, values, True, mask, sums)

    j0 = ((qi + 1) * tq - 2) // tk
    top = block(j0, (j0 * tk + col) < row_pos)
    lax.while_loop(_keep_sweeping, lambda state: (state[0] - 1, block(state[0], None)), (j0 - 1, top))
    o_ref[...] = acc_ref[...].astype(o_ref.dtype)


def _attention_prompt(q, kt, vt):
    n_streams, n_kblocks, d, tk = kt.shape
    length = n_kblocks * tk
    tq = min(length, ATTN_TQ)
    assert tk % tq == 0
    nq = length // tq
    kv = pl.BlockSpec((1, n_kblocks, d, tk), lambda b, i: (b, 0, 0, 0))
    rows = pl.BlockSpec((tq, d), lambda b, i: (b * nq + i, 0))
    return pl.pallas_call(
        _attn_prompt_kernel,
        grid=(n_streams, nq),
        in_specs=[rows, kv, kv],
        out_specs=rows,
        out_shape=jax.ShapeDtypeStruct((n_streams * length, d), BF16),
        scratch_shapes=[pltpu.VMEM((d // LANES, 2 * tq, LANES), BF16), pltpu.VMEM((d // LANES, 2 * tq, tk), F32),
                        pltpu.VMEM((tq, d), F32)],
        compiler_params=_params(("arbitrary", "arbitrary")),
        name="stick_breaking_prompt",
    )(q, kt, vt)


def _attn_sample_kernel(q_ref, kn_ref, vn_ref, ck_hbm, cv_hbm, o_ref, qm_ref, carry_ref, acc_ref, kbuf, vbuf, sem):
    b = pl.program_id(0)
    t, d = q_ref.shape
    tk = kbuf.shape[-1]
    n_pairs = d // LANES
    n_cache_blocks = ck_hbm.shape[1] // tk

    def fetch(j, slot):
        src = (pl.ds(pl.multiple_of(b * d, d), d), pl.ds(pl.multiple_of(j * tk, tk), tk))
        return (pltpu.make_async_copy(ck_hbm.at[src], kbuf.at[slot], sem.at[0, slot]),
                pltpu.make_async_copy(cv_hbm.at[src], vbuf.at[slot], sem.at[1, slot]))

    for cp in fetch(n_cache_blocks - 1, (n_cache_blocks - 1) % 2):
        cp.start()
    _attn_setup(q_ref, qm_ref, carry_ref, acc_ref)
    row = lax.rem(lax.broadcasted_iota(jnp.int32, (2 * t, tk), 0), t)
    col = lax.broadcasted_iota(jnp.int32, (2 * t, tk), 1)
    sums = _suffix_sum_matrix(tk)
    pad = jnp.zeros((tk - t, LANES), BF16)
    keys = [jnp.concatenate([kn_ref[:, p * LANES:(p + 1) * LANES], pad], axis=0) for p in range(n_pairs)]
    values = [jnp.concatenate([vn_ref[:, p * LANES:(p + 1) * LANES], pad], axis=0) for p in range(n_pairs)]
    top = _attn_block(qm_ref, carry_ref, acc_ref, keys, values, False, col < row, sums)

    def body(state):
        j, _ = state
        slot = lax.rem(j, 2)
        for cp in fetch(j, slot):
            cp.wait()

        @pl.when(j > 0)
        def _():
            for cp in fetch(j - 1, 1 - slot):
                cp.start()

        keys = [kbuf[slot, p * LANES:(p + 1) * LANES, :].astype(BF16) for p in range(n_pairs)]
        values = [vbuf[slot, p * LANES:(p + 1) * LANES, :].astype(BF16) for p in range(n_pairs)]
        return j - 1, _attn_block(qm_ref, carry_ref, acc_ref, keys, values, True, None, sums)

    j_end, _ = lax.while_loop(_keep_sweeping, body, (jnp.int32(n_cache_blocks - 1), top))

    @pl.when(j_end >= 0)
    def _():
        for cp in fetch(j_end, lax.rem(j_end, 2)):
            cp.wait()

    o_ref[...] = acc_ref[...].astype(o_ref.dtype)


def _attention_sample(q, k_new, v_new, cache_kt, cache_vt, t, q_row_off):
    d = q.shape[1]
    n_streams = k_new.shape[0] // t
    tk = ATTN_TK
    assert cache_kt.shape[1] % tk == 0 and t <= tk
    any_spec = pl.BlockSpec(memory_space=pl.ANY)
    new = pl.BlockSpec((t, d), lambda b: (b, 0))
    return pl.pallas_call(
        _attn_sample_kernel,
        grid=(n_streams,),
        in_specs=[pl.BlockSpec((t, d), lambda b: (q_row_off // t + b, 0)), new, new, any_spec, any_spec],
        out_specs=new,
        out_shape=jax.ShapeDtypeStruct((n_streams * t, d), BF16),
        scratch_shapes=[pltpu.VMEM((d // LANES, 2 * t, LANES), BF16), pltpu.VMEM((d // LANES, 2 * t, tk), F32),
                        pltpu.VMEM((t, d), F32), pltpu.VMEM((2, d, tk), F32), pltpu.VMEM((2, d, tk), F32),
                        pltpu.SemaphoreType.DMA((2, 2))],
        compiler_params=_params(("arbitrary",)),
        name="stick_breaking_sample",
    )(q, k_new, v_new, cache_kt, cache_vt)


def kernel(x_prompt, x_sample, state_pool, cache_k, cache_v, norm_mix, norm_ffn, pool_w, pool_scale, w_qkv, w_o, router_w, router_b, moe_w1, moe_b1, moe_w2, moe_b2, final_norm):
    b, s, d = x_prompt.shape
    db, t, _ = x_sample.shape
    past = cache_k.shape[2]
    n_exp = router_w.shape[2]
    hd = d // N_HEADS
    hist = POOL_HIST_ROWS
    n_p, n_s = b * s, db * t
    assert t >= hist and ROW_TILE % t == 0 and s % ROW_TILE == 0 and n_s % ROW_TILE == 0 and ROW_TILE % QKV_TILE == 0
    row = lambda a: a.reshape(1, -1)
    wr = [router_w[i].T for i in range(2)]
    br = [router_b[i].reshape(n_exp, 1) for i in range(2)]

    hist_s = jnp.concatenate([jnp.zeros((db, 1, d), F32), state_pool[0]], axis=1)
    xn, tok, idx, gates, rank, hlast_p, hlast_s, cnt = _pool_layer(
        x_prompt, x_sample, hist_s, past, row(norm_mix[0]), pool_w[0].astype(BF16), row(pool_scale[0]),
        row(norm_ffn[0]), wr[0], br[0])
    x1, h1 = _moe(tok, idx, gates, rank, cnt, xn, row(norm_mix[1]), None, 0, moe_w1, moe_b1[0], moe_w2, moe_b2[0])

    wqkv = w_qkv[0].astype(BF16)
    q, ktb, vtb, ktf, vtf, ksb, vsb, ksf, vsf = _qkv(h1, wqkv[:, :d], wqkv[:, d:2 * d].T, wqkv[:, 2 * d:].T, b, s)
    o_p = _attention_prompt(q, ktb, vtb)
    transposed = lambda cache: cache.transpose(0, 1, 3, 4, 2).reshape(db * d, past)
    o_s = _attention_sample(q, ksb, vsb, transposed(cache_k), transposed(cache_v), t, n_p)
    xn, tok, idx, gates, rank, cnt = _proj_layer(o_p, o_s, x1, w_o[0].astype(BF16), row(norm_ffn[1]), wr[1], br[1])
    y_p, y_s = _moe(tok, idx, gates, rank, cnt, xn, row(final_norm), n_p, 1, moe_w1, moe_b1[1], moe_w2, moe_b2[1])

    frames_major = lambda xt: xt.reshape(1, b, N_HEADS, hd, s).transpose(0, 1, 4, 2, 3)
    heads = lambda a: a.reshape(1, db, t, N_HEADS, hd)
    return (y_p.reshape(b, s, d), y_s.reshape(db, t, d), hlast_p[None, :, 1:, :],
            frames_major(ktf), frames_major(vtf), hlast_s[None, :, 1:, :], heads(ksf), heads(vsf))
```

```python
import functools

import jax
import jax.numpy as jnp
from jax import lax
from jax.experimental import pallas as pl
from jax.experimental.pallas import tpu as pltpu
from jax.experimental.pallas import tpu_sc as plsc

EPS = 1e-5
POOL_WINDOWS = (2, 4, 8, 16)
POOL_HIST_ROWS = 16
N_HEADS = 16
TOP_K = 4
SWIGLU_LIMIT = 7.0
SWIGLU_ALPHA = 1.702
LANES = 128
ROW_TILE = 512
QKV_TILE = 256
ATTN_TQ = 128
ATTN_TK = 128
ATTN_EXIT = 104.0
VMEM_LIMIT = 56 * 1024 * 1024
SC_CORES = 2
SC_SUBCORES = 16
SC_CHUNK = 96
SC_GATHER_CHUNK = 48
COMBINE_TILE = 512

F32 = jnp.float32
BF16 = jnp.bfloat16


def _rms(x, g):
    ms = jnp.mean(x * x, axis=-1, keepdims=True)
    return x * lax.rsqrt(ms + EPS) * g


def _dot(a, b):
    return jnp.dot(a, b, preferred_element_type=F32)


def _dot_nt(a, b, precision=None):
    return lax.dot_general(a, b, (((1,), (1,)), ((), ())), preferred_element_type=F32, precision=precision)


def _pack_halves(x):
    c = x.shape[1] // 2
    bits = lax.bitcast_convert_type(x.astype(BF16).astype(F32), jnp.uint32)
    return bits[:, :c] | (bits[:, c:] >> 16)


def _unpack_halves(w):
    return (lax.bitcast_convert_type(w & jnp.uint32(0xFFFF0000), F32), lax.bitcast_convert_type(w << 16, F32))


def _params(semantics):
    return pltpu.CompilerParams(dimension_semantics=semantics, vmem_limit_bytes=VMEM_LIMIT)


def _full(shape):
    return pl.BlockSpec(shape, lambda i, *_: (0,) * len(shape))


def _route_init(cnt_ref, tri_ref):
    r = tri_ref.shape[0]
    cnt_ref[...] = jnp.zeros(cnt_ref.shape, F32)
    tri_ref[...] = jnp.where(lax.broadcasted_iota(jnp.int32, (r, r), 0) < lax.broadcasted_iota(jnp.int32, (r, r), 1),
                             1.0, 0.0).astype(BF16)


def _route_tail(xn, gffn_ref, wr_ref, br_ref, cnt_ref, tri_ref, tok_ref, idx_ref, gate_ref, rank_ref):
    n_exp = wr_ref.shape[0]
    tok = _rms(xn, gffn_ref[...])
    tok_ref[...] = _pack_halves(tok)
    logits = _dot_nt(wr_ref[...], tok, precision=lax.Precision.HIGHEST) + br_ref[...]
    eidx = lax.broadcasted_iota(jnp.int32, logits.shape, 0).astype(F32)
    vals, idxs = [], []
    l = logits
    for _ in range(TOP_K):
        m = jnp.max(l, axis=0, keepdims=True)
        i = jnp.min(jnp.where(l == m, eidx, float(n_exp)), axis=0, keepdims=True)
        vals.append(m)
        idxs.append(i)
        l = jnp.where(eidx == i, -jnp.inf, l)
    es = [jnp.exp(v - vals[0]) for v in vals]
    den = es[0]
    for e in es[1:]:
        den = den + e
    gate_ref[...] = jnp.concatenate([e / den for e in es], axis=0)
    idx_ref[...] = jnp.concatenate(idxs, axis=0).astype(jnp.int32)
    member = jnp.zeros(logits.shape, F32)
    for i in idxs:
        member = member + jnp.where(eidx == i, 1.0, 0.0)
    before = _dot(member.astype(BF16), tri_ref[...]) + cnt_ref[:, :1]
    ranks = [jnp.sum(jnp.where(eidx == i, before, 0.0), axis=0, keepdims=True) for i in idxs]
    rank_ref[...] = jnp.concatenate(ranks, axis=0).astype(jnp.int32)
    cnt_ref[...] = cnt_ref[...] + jnp.sum(member, axis=1, keepdims=True)


def _pool_mix(h, ext_ref, pos, pw_ref, ps_ref):
    ts, d = h.shape[-2:]
    hist = POOL_HIST_ROWS
    group = d // len(POOL_WINDOWS)
    pre = (slice(None),) * (h.ndim - 2)
    ys = []
    for g, win in enumerate(POOL_WINDOWS):
        cols = slice(g * group, (g + 1) * group)
        hg = h[pre + (slice(None), cols)]
        acc = hg
        for j in range(1, win):
            acc = acc + ext_ref[pre + (slice(hist - j, hist - j + ts), cols)]
        cnt = jnp.minimum(pos + 1, win).astype(F32)
        dg = acc / cnt - hg
        ys.append(_dot(dg.reshape(-1, group).astype(BF16), pw_ref[g]))
    return jnp.concatenate(ys, axis=-1) * ps_ref[...]


def _pool_mix_tiled(h, ext_ref, lvl_ref, pos, pw_ref, ps_ref):
    ts, d = h.shape
    hist = POOL_HIST_ROWS
    group = d // len(POOL_WINDOWS)
    prev_ref, prev_col0 = ext_ref, 0
    ys = []
    for g, win in enumerate(POOL_WINDOWS):
        assert win == 2 * (POOL_WINDOWS[g - 1] if g else 1)
        col0 = g * group
        cols = slice(col0 - prev_col0, d - prev_col0)
        level = prev_ref[hist:hist + ts, cols] + prev_ref[hist - win // 2:hist - win // 2 + ts, cols]
        if g + 1 < len(POOL_WINDOWS):
            lvl_ref[g, hist:hist + ts, 0:d - col0 - group] = level[:, group:]
            prev_ref, prev_col0 = lvl_ref.at[g], col0 + group
        cnt = jnp.minimum(pos + 1, win).astype(F32)
        dg = level[:, :group] / cnt - h[:, col0:col0 + group]
        ys.append(_dot(dg.astype(BF16), pw_ref[g]))
    return jnp.concatenate(ys, axis=-1) * ps_ref[...]


def _pool_kernel(ntp, nt, pos0_s, xp_ref, xs_ref, hist_ref, gmix_ref, pw_ref, ps_ref, gffn_ref, wr_ref, br_ref,
                 xn_ref, tok_ref, idx_ref, gate_ref, rank_ref, hlast_p_ref, hlast_s_ref, cnt_out_ref,
                 ext_p, lvl_p, ext_s, cnt_ref, tri_ref):
    i = pl.program_id(0)
    hist = POOL_HIST_ROWS

    @pl.when(i == 0)
    def _():
        _route_init(cnt_ref, tri_ref)

    @pl.when(i < ntp)
    def _():
        ts, d = xp_ref.shape
        t = lax.rem(i, nt)

        @pl.when(t == 0)
        def _():
            ext_p[0:hist, :] = jnp.zeros((hist, d), F32)
            lvl_p[:, 0:hist, :] = jnp.zeros((lvl_p.shape[0], hist, lvl_p.shape[2]), F32)

        x = xp_ref[...]
        h = _rms(x, gmix_ref[...])
        ext_p[hist:hist + ts, :] = h

        @pl.when(t == nt - 1)
        def _():
            hlast_p_ref[0] = h[ts - hist:, :]

        pos = t * ts + lax.broadcasted_iota(jnp.int32, (ts, 1), 0)
        xn_ref[...] = x + _pool_mix_tiled(h, ext_p, lvl_p, pos, pw_ref, ps_ref)
        ext_p[0:hist, :] = ext_p[ts:ts + hist, :]
        lvl_p[:, 0:hist, :] = lvl_p[:, ts:ts + hist, :]

    @pl.when(i >= ntp)
    def _():
        bb, ts, d = xs_ref.shape
        ext_s[:, 0:hist, :] = hist_ref[...]
        x = xs_ref[...]
        h = _rms(x, gmix_ref[...])
        ext_s[:, hist:hist + ts, :] = h
        hlast_s_ref[...] = h[:, ts - hist:, :]
        pos = pos0_s + lax.broadcasted_iota(jnp.int32, (1, ts, 1), 1)
        xn_ref[...] = x.reshape(bb * ts, d) + _pool_mix(h, ext_s, pos, pw_ref, ps_ref)

    _route_tail(xn_ref[...], gffn_ref, wr_ref, br_ref, cnt_ref, tri_ref, tok_ref, idx_ref, gate_ref, rank_ref)
    cnt_out_ref[...] = cnt_ref[...]


def _pool_layer(x_prompt, x_sample, hist_s, pos0_s, gmix, pw, ps, gffn, wr, br):
    b, s, d = x_prompt.shape
    db, t, _ = x_sample.shape
    n_exp = wr.shape[0]
    r = ROW_TILE
    nt = s // r
    ntp = b * nt
    bb = r // t
    nts = db // bb
    n = b * s + db * t
    hist = POOL_HIST_ROWS
    rows = pl.BlockSpec((r, d), lambda i: (i, 0))
    lanes = pl.BlockSpec((TOP_K, r), lambda i: (0, i))
    sample_blk = lambda i: (jnp.maximum(i - ntp, 0), 0, 0)
    return pl.pallas_call(
        functools.partial(_pool_kernel, ntp, nt, pos0_s),
        grid=(ntp + nts,),
        in_specs=[pl.BlockSpec((r, d), lambda i: (jnp.minimum(i, ntp - 1), 0)),
                  pl.BlockSpec((bb, t, d), sample_blk), pl.BlockSpec((bb, hist, d), sample_blk),
                  _full((1, d)), _full(pw.shape), _full((1, d)), _full((1, d)), _full((n_exp, d)), _full((n_exp, 1))],
        out_specs=[rows, pl.BlockSpec((r, d // 2), lambda i: (i, 0)), lanes, lanes, lanes,
                   pl.BlockSpec((1, hist, d), lambda i: (jnp.minimum(i // nt, b - 1), 0, 0)),
                   pl.BlockSpec((bb, hist, d), sample_blk), _full((n_exp, LANES))],
        out_shape=[jax.ShapeDtypeStruct((n, d), F32), jax.ShapeDtypeStruct((n, d // 2), jnp.uint32),
                   jax.ShapeDtypeStruct((TOP_K, n), jnp.int32), jax.ShapeDtypeStruct((TOP_K, n), F32),
                   jax.ShapeDtypeStruct((TOP_K, n), jnp.int32),
                   jax.ShapeDtypeStruct((b, hist, d), F32), jax.ShapeDtypeStruct((db, hist, d), F32),
                   jax.ShapeDtypeStruct((n_exp, LANES), F32)],
        scratch_shapes=[pltpu.VMEM((hist + r, d), F32),
                        pltpu.VMEM((len(POOL_WINDOWS) - 1, hist + r, d - d // len(POOL_WINDOWS)), F32),
                        pltpu.VMEM((bb, hist + t, d), F32), pltpu.VMEM((n_exp, LANES), F32), pltpu.VMEM((r, r), BF16)],
        compiler_params=_params(("arbitrary",)),
        name="pool_route",
    )(x_prompt.reshape(b * s, d), x_sample, hist_s, gmix, pw, ps, gffn, wr, br)


def _proj_kernel(ntp, op_ref, os_ref, x_ref, wo_ref, gffn_ref, wr_ref, br_ref,
                 xn_ref, tok_ref, idx_ref, gate_ref, rank_ref, cnt_out_ref, cnt_ref, tri_ref):
    i = pl.program_id(0)

    @pl.when(i == 0)
    def _():
        _route_init(cnt_ref, tri_ref)

    o = jnp.where(i < ntp, op_ref[...], os_ref[...])
    xn = x_ref[...] + _dot(o, wo_ref[...])
    xn_ref[...] = xn
    _route_tail(xn, gffn_ref, wr_ref, br_ref, cnt_ref, tri_ref, tok_ref, idx_ref, gate_ref, rank_ref)
    cnt_out_ref[...] = cnt_ref[...]


def _proj_layer(o_p, o_s, x, wo, gffn, wr, br):
    n, d = x.shape
    n_exp = wr.shape[0]
    r = ROW_TILE
    ntp = o_p.shape[0] // r
    rows = pl.BlockSpec((r, d), lambda i: (i, 0))
    lanes = pl.BlockSpec((TOP_K, r), lambda i: (0, i))
    return pl.pallas_call(
        functools.partial(_proj_kernel, ntp),
        grid=(n // r,),
        in_specs=[pl.BlockSpec((r, d), lambda i: (jnp.minimum(i, ntp - 1), 0)),
                  pl.BlockSpec((r, d), lambda i: (jnp.maximum(i - ntp, 0), 0)),
                  rows, _full((d, d)), _full((1, d)), _full((n_exp, d)), _full((n_exp, 1))],
        out_specs=[rows, pl.BlockSpec((r, d // 2), lambda i: (i, 0)), lanes, lanes, lanes, _full((n_exp, LANES))],
        out_shape=[jax.ShapeDtypeStruct((n, d), F32), jax.ShapeDtypeStruct((n, d // 2), jnp.uint32),
                   jax.ShapeDtypeStruct((TOP_K, n), jnp.int32), jax.ShapeDtypeStruct((TOP_K, n), F32),
                   jax.ShapeDtypeStruct((TOP_K, n), jnp.int32), jax.ShapeDtypeStruct((n_exp, LANES), F32)],
        scratch_shapes=[pltpu.VMEM((n_exp, LANES), F32), pltpu.VMEM((r, r), BF16)],
        compiler_params=_params(("arbitrary",)),
        name="proj_route",
    )(o_p, o_s, x, wo, gffn, wr, br)


def _sc_chunks(dest, w):
    n = dest.shape[1]
    return dest.reshape(TOP_K, n // w, w).transpose(1, 0, 2)


def _dispatch(tok, dest, n_buf_rows):
    n, d = tok.shape
    w = SC_CHUNK
    n_workers = SC_CORES * SC_SUBCORES
    per_worker = n // (w * n_workers)
    assert per_worker * w * n_workers == n
    mesh = plsc.VectorSubcoreMesh(core_axis_name="core", subcore_axis_name="subcore",
                                  num_cores=SC_CORES, num_subcores=SC_SUBCORES)

    @functools.partial(
        pl.kernel, mesh=mesh, out_type=jax.ShapeDtypeStruct((n_buf_rows, d), tok.dtype),
        scratch_types=[pltpu.VMEM((TOP_K, w), jnp.int32), pltpu.VMEM((w, d), tok.dtype), pltpu.SemaphoreType.DMA],
        name="moe_dispatch_sc")
    def scatter_rows(tok_hbm, dest_hbm, buf_hbm, idx_v, rows_v, sem):
        worker = lax.axis_index("subcore") * SC_CORES + lax.axis_index("core")

        @pl.loop(0, per_worker)
        def _(c):
            chunk = worker * per_worker + c
            pltpu.sync_copy(dest_hbm.at[chunk], idx_v)
            pltpu.sync_copy(tok_hbm.at[pl.ds(pl.multiple_of(chunk * w, w), w)], rows_v)
            copies = [pltpu.async_copy(rows_v, buf_hbm.at[idx_v.at[k]], sem) for k in range(TOP_K)]
            for cp in copies:
                cp.wait()

    return scatter_rows(tok, _sc_chunks(dest, w))


def _expert_kernel(be_ref, valid_ref, nused_ref, x_ref, w1_ref, b1_ref, w2_ref, b2_ref, o_ref, w1b_ref, w2b_ref):
    i = pl.program_id(0)
    used = valid_ref[i] > 0

    @pl.when(used & ((i == 0) | (be_ref[i] != be_ref[jnp.maximum(i - 1, 0)])))
    def _():
        w1b_ref[...] = w1_ref[0, 0].astype(BF16)
        w2b_ref[...] = w2_ref[0, 0].astype(BF16)

    @pl.when(used)
    def _():
        f = w2b_ref.shape[0]
        row = lax.broadcasted_iota(jnp.int32, (x_ref.shape[0], 1), 0)
        x = jnp.where(row < valid_ref[i], x_ref[...], jnp.uint32(0))
        xa, xb = _unpack_halves(x)
        half = x.shape[1]
        gu = (_dot(xa.astype(BF16), w1b_ref[:half, :]) + _dot(xb.astype(BF16), w1b_ref[half:, :])) + b1_ref[0]
        gate = jnp.minimum(gu[:, :f], SWIGLU_LIMIT)
        up = jnp.clip(gu[:, f:], -SWIGLU_LIMIT, SWIGLU_LIMIT)
        hid = (up + 1.0) * (gate * jax.nn.sigmoid(gate * SWIGLU_ALPHA))
        o_ref[...] = _pack_halves(_dot(hid.astype(BF16), w2b_ref[...]) + b2_ref[0])

    @pl.when(jnp.logical_not(used))
    def _():
        o_ref[...] = jnp.zeros(o_ref.shape, o_ref.dtype)


def _experts(buf, block_expert, valid, nused, layer, w1, b1, w2, b2, tm):
    rows, dw = buf.shape
    _, n_exp, d, f2 = w1.shape
    f = w2.shape[2]
    in_rows = lambda i, be, va, nu: (jnp.minimum(i, nu[0] - 1), 0)
    by_expert = lambda i, be, va, nu: (be[i], 0, 0)
    by_layer_expert = lambda i, be, va, nu: (layer, be[i], 0, 0)
    return pl.pallas_call(
        _expert_kernel,
        grid_spec=pltpu.PrefetchScalarGridSpec(
            num_scalar_prefetch=3,
            grid=(rows // tm,),
            in_specs=[pl.BlockSpec((tm, dw), in_rows),
                      pl.BlockSpec((1, 1, d, f2), by_layer_expert), pl.BlockSpec((1, 1, f2), by_expert),
                      pl.BlockSpec((1, 1, f, d), by_layer_expert), pl.BlockSpec((1, 1, d), by_expert)],
            out_specs=pl.BlockSpec((tm, dw), lambda i, be, va, nu: (i, 0)),
            scratch_shapes=[pltpu.VMEM((d, f2), BF16), pltpu.VMEM((f, d), BF16)]),
        out_shape=jax.ShapeDtypeStruct((rows, dw), jnp.uint32),
        compiler_params=_params(("arbitrary",)),
        name="moe_experts",
    )(block_expert, valid, nused, buf, w1, b1.reshape(n_exp, 1, f2), w2, b2.reshape(n_exp, 1, d))


def _gather_rows(ybuf, dest):
    _, d = ybuf.shape
    n = dest.shape[1]
    w = SC_GATHER_CHUNK
    n_workers = SC_CORES * SC_SUBCORES
    per_worker = n // (w * n_workers)
    assert per_worker * w * n_workers == n and TOP_K % 2 == 0
    mesh = plsc.VectorSubcoreMesh(core_axis_name="core", subcore_axis_name="subcore",
                                  num_cores=SC_CORES, num_subcores=SC_SUBCORES)

    @functools.partial(
        pl.kernel, mesh=mesh, out_type=jax.ShapeDtypeStruct((TOP_K, n, d), ybuf.dtype),
        scratch_types=[pltpu.VMEM((TOP_K, w), jnp.int32), pltpu.VMEM((2, w, d), ybuf.dtype),
                       pltpu.SemaphoreType.DMA((2,)), pltpu.SemaphoreType.DMA((2,))],
        name="moe_gather_sc")
    def gather_rows(y_hbm, dest_hbm, out_hbm, idx_v, rows_v, sem_in, sem_out):
        worker = lax.axis_index("subcore") * SC_CORES + lax.axis_index("core")

        @pl.loop(0, per_worker)
        def _(c):
            chunk = worker * per_worker + c
            tokens = pl.ds(pl.multiple_of(chunk * w, w), w)
            pltpu.sync_copy(dest_hbm.at[chunk], idx_v)
            fetch = lambda k: pltpu.async_copy(y_hbm.at[idx_v.at[k]], rows_v.at[k % 2], sem_in.at[k % 2])
            store = lambda k: pltpu.async_copy(rows_v.at[k % 2], out_hbm.at[k, tokens], sem_out.at[k % 2])
            fetches = [fetch(0), fetch(1)]
            stores = []
            for k in range(TOP_K):
                fetches[k].wait()
                stores.append(store(k))
                if k + 2 < TOP_K:
                    stores[k].wait()
                    fetches.append(fetch(k + 2))
            for k in range(TOP_K - 2, TOP_K):
                stores[k].wait()

    return gather_rows(ybuf, _sc_chunks(dest, w))


def _combine_kernel(ntp, tile0, rows_ref, gate_ref, x_ref, g_ref, *refs):
    outs = refs[-2:]
    i = tile0 + pl.program_id(0)
    r = x_ref.shape[0]
    gates = jnp.concatenate([gate_ref[...], jnp.zeros((LANES - TOP_K, r), F32)], axis=0)
    gates_t = jnp.transpose(gates)
    halves = [_unpack_halves(rows_ref[k]) for k in range(TOP_K)]
    f = []
    for side in range(2):
        acc = halves[0][side] * gates_t[:, 0:1]
        for k in range(1, TOP_K):
            acc = acc + halves[k][side] * gates_t[:, k:k + 1]
        f.append(acc)
    xo = x_ref[...] + jnp.concatenate(f, axis=1)
    hn = _rms(xo, g_ref[...])
    if ntp is None:
        xo_ref, hn_ref = outs
        xo_ref[...] = xo
        hn_ref[...] = hn.astype(hn_ref.dtype)
    else:
        hp_ref, hs_ref = outs
        if tile0 < ntp:
            @pl.when(pl.program_id(0) == 0)
            def _():
                hs_ref[...] = jnp.zeros(hs_ref.shape, hs_ref.dtype)

        @pl.when(i < ntp)
        def _():
            hp_ref[...] = hn

        @pl.when(i >= ntp)
        def _():
            hs_ref[...] = hn


def _combine(rows4, gates, x, g, n_p, tile0, prev):
    n, d = x.shape
    r = COMBINE_TILE
    rows = pl.BlockSpec((r, d), lambda i: (tile0 + i, 0))
    if n_p is None:
        ntp = None
        out_specs = [rows, rows]
        out_shape = [jax.ShapeDtypeStruct((n, d), F32), jax.ShapeDtypeStruct((n, d), BF16)]
    else:
        ntp = n_p // r
        out_specs = [pl.BlockSpec((r, d), lambda i: (jnp.minimum(tile0 + i, ntp - 1), 0)),
                     pl.BlockSpec((r, d), lambda i: (jnp.maximum(tile0 + i - ntp, 0), 0))]
        out_shape = [jax.ShapeDtypeStruct((n_p, d), F32), jax.ShapeDtypeStruct((n - n_p, d), F32)]
    prev = [] if prev is None else list(prev)
    return pl.pallas_call(
        functools.partial(_combine_kernel, ntp, tile0),
        grid=(rows4.shape[1] // r,),
        in_specs=[pl.BlockSpec((TOP_K, r, d // 2), lambda i: (0, i, 0)),
                  pl.BlockSpec((TOP_K, r), lambda i: (0, tile0 + i)), rows, _full((1, d))]
        + [pl.BlockSpec(memory_space=pl.ANY)] * len(prev),
        out_specs=out_specs,
        out_shape=out_shape,
        input_output_aliases={4 + j: j for j in range(len(prev))},
        compiler_params=_params(("arbitrary",)),
        name="moe_combine",
    )(rows4, gates, x, g, *prev)


def _moe(tok, idx, gates, rank, cnt, x, g_next, n_p, layer, w1, b1, w2, b2):
    n, d = tok.shape
    n_exp = w1.shape[1]
    pairs = n * TOP_K
    tm = 128
    for cand in (512, 256):
        if pairs >= 4 * cand * n_exp and pairs % cand == 0:
            tm = cand
            break
    n_blocks = -(-pairs // tm) + n_exp
    counts = cnt[:, 0].astype(jnp.int32)
    padded = (counts + tm - 1) // tm * tm
    pad_end = jnp.cumsum(padded)
    pad_start = pad_end - padded
    experts = jnp.arange(n_exp, dtype=jnp.int32)
    dest = rank + jnp.sum(jnp.where(idx[:, :, None] == experts, pad_start, 0), axis=-1)
    starts = jnp.arange(n_blocks, dtype=jnp.int32) * tm
    nused = (pad_end[-1:] // tm).astype(jnp.int32)
    block_expert = jnp.minimum(jnp.sum((pad_end[None, :] <= starts[:, None]).astype(jnp.int32), axis=1), n_exp - 1)
    tokens_end = jnp.sum(jnp.where(block_expert[:, None] == experts, pad_start + counts, 0), axis=-1)
    valid = jnp.clip(tokens_end - starts, 0, tm).astype(jnp.int32)
    buf = _dispatch(tok, dest, n_blocks * tm)
    ybuf = _experts(buf, block_expert, valid, nused, layer, w1, b1, w2, b2, tm)
    r = COMBINE_TILE
    first = (n // r // 2) * r
    outs = None
    for start, stop in ((0, first), (first, n)):
        outs = _combine(_gather_rows(ybuf, dest[:, start:stop]), gates, x, g_next, n_p, start // r, outs)
    return outs


def _qkv_kernel(ntp, h_ref, wq_ref, wkt_ref, wvt_ref, q_ref, ktb_ref, vtb_ref, ktf_ref, vtf_ref,
                ksb_ref, vsb_ref, ksf_ref, vsf_ref):
    i = pl.program_id(0)
    r, d = h_ref.shape
    tk = ktb_ref.shape[-1]
    h = h_ref[...]
    q_ref[...] = (_dot(h, wq_ref[...]) * ((d // N_HEADS) ** -0.5)).astype(BF16)

    @pl.when(i < ntp)
    def _():
        for wt_ref, tb_ref, tf_ref in ((wkt_ref, ktb_ref, ktf_ref), (wvt_ref, vtb_ref, vtf_ref)):
            xt = _dot_nt(wt_ref[...], h)
            tf_ref[...] = xt
            for c in range(r // tk):
                tb_ref[0, c] = xt[:, c * tk:(c + 1) * tk].astype(BF16)

    @pl.when(i >= ntp)
    def _():
        for wt_ref, sb_ref, sf_ref in ((wkt_ref, ksb_ref, ksf_ref), (wvt_ref, vsb_ref, vsf_ref)):
            x = _dot_nt(h, wt_ref[...])
            sf_ref[...] = x
            sb_ref[...] = x.astype(BF16)


def _qkv(h, wq, wkt, wvt, n_streams, length):
    n, d = h.shape
    r = QKV_TILE
    tk = ATTN_TK
    n_p = n_streams * length
    nt = length // r
    ntp = n_p // r
    rows = pl.BlockSpec((r, d), lambda i: (i, 0))
    prompt_blk = lambda i: (jnp.minimum(i, ntp - 1) // nt, lax.rem(jnp.minimum(i, ntp - 1), nt))
    t_blocks = pl.BlockSpec((1, r // tk, d, tk), lambda i: prompt_blk(i) + (0, 0))
    t_full = pl.BlockSpec((d, r), prompt_blk)
    sample = pl.BlockSpec((r, d), lambda i: (jnp.maximum(i - ntp, 0), 0))
    w_spec = _full((d, d))
    return pl.pallas_call(
        functools.partial(_qkv_kernel, ntp),
        grid=(n // r,),
        in_specs=[rows, w_spec, w_spec, w_spec],
        out_specs=[rows, t_blocks, t_blocks, t_full, t_full, sample, sample, sample, sample],
        out_shape=[jax.ShapeDtypeStruct((n, d), BF16)]
        + [jax.ShapeDtypeStruct((n_streams, length // tk, d, tk), BF16)] * 2
        + [jax.ShapeDtypeStruct((n_streams * d, length), F32)] * 2
        + [jax.ShapeDtypeStruct((n - n_p, d), BF16)] * 2 + [jax.ShapeDtypeStruct((n - n_p, d), F32)] * 2,
        compiler_params=_params(("arbitrary",)),
        name="qkv",
    )(h, wq, wkt, wvt)


def _attn_setup(q_ref, qm_ref, carry_ref, acc_ref):
    tq, d = q_ref.shape
    lane = lax.broadcasted_iota(jnp.int32, (tq, LANES), 1)
    for p in range(d // LANES):
        qp = q_ref[:, p * LANES:(p + 1) * LANES]
        qm_ref[p, 0:tq, :] = jnp.where(lane < LANES // 2, qp, jnp.zeros_like(qp))
        qm_ref[p, tq:2 * tq, :] = jnp.where(lane >= LANES // 2, qp, jnp.zeros_like(qp))
    carry_ref[...] = jnp.zeros(carry_ref.shape, F32)
    acc_ref[...] = jnp.zeros(acc_ref.shape, F32)


def _suffix_sum_matrix(tk):
    src = lax.rem(lax.broadcasted_iota(jnp.int32, (2 * tk, 2 * tk), 0), tk)
    dst = lax.broadcasted_iota(jnp.int32, (2 * tk, 2 * tk), 1)
    return jnp.where((dst >= tk) | (src > dst), 1.0, 0.0).astype(BF16)


def _attn_block(qm_ref, carry_ref, acc_ref, keys, values, transposed, mask, sums):
    n_pairs, tq2, tk = carry_ref.shape
    tq = tq2 // 2
    lane = lax.broadcasted_iota(jnp.int32, (tq, LANES), 1)
    visible = (lambda x: x) if mask is None else (lambda x: jnp.where(mask, x, 0.0))
    scores = _dot if transposed else _dot_nt
    mix = _dot_nt if transposed else _dot
    zs = [scores(qm_ref[p], keys[p]) for p in range(n_pairs)]
    log_beta, parts = [], []
    for z in zs:
        sp = jnp.maximum(z, 0.0) + jnp.log(1.0 + jnp.exp(-jnp.abs(z)))
        log_keep = visible(-sp)
        hi = log_keep.astype(BF16)
        lo = (log_keep - hi.astype(F32)).astype(BF16)
        parts.append(jnp.concatenate([hi, lo], axis=1))
        log_beta.append(z - sp)
    sums_out = [_dot(part, sums) for part in parts]
    weights = []
    top = jnp.full((tq2, tk), -jnp.inf, F32)
    for p in range(n_pairs):
        carry = carry_ref[p]
        a = visible(jnp.exp(log_beta[p] + sums_out[p][:, :tk] + carry))
        weights.append(a.astype(BF16))
        carry = carry + sums_out[p][:, tk:]
        carry_ref[p] = carry
        top = jnp.maximum(top, carry)
    for p in range(n_pairs):
        out = mix(weights[p], values[p])
        acc_ref[:, p * LANES:(p + 1) * LANES] += jnp.where(lane < LANES // 2, out[:tq], out[tq:])
    return jnp.max(top)


def _keep_sweeping(state):
    j, top = state
    return (j >= 0) & (top > -ATTN_EXIT)


def _attn_prompt_kernel(q_ref, kt_ref, vt_ref, o_ref, qm_ref, carry_ref, acc_ref):
    qi = pl.program_id(1)
    tq, d = q_ref.shape
    tk = kt_ref.shape[-1]
    n_pairs = d // LANES
    _attn_setup(q_ref, qm_ref, carry_ref, acc_ref)
    row_pos = qi * tq + lax.rem(lax.broadcasted_iota(jnp.int32, (2 * tq, tk), 0), tq)
    col = lax.broadcasted_iota(jnp.int32, (2 * tq, tk), 1)
    sums = _suffix_sum_matrix(tk)

    def block(j, mask):
        keys = [kt_ref[0, j, p * LANES:(p + 1) * LANES, :] for p in range(n_pairs)]
        values = [vt_ref[0, j, p * LANES:(p + 1) * LANES, :] for p in range(n_pairs)]
        return _attn_block(qm_ref, carry_ref, acc_ref, keys, values, True, mask, sums)

    j0 = ((qi + 1) * tq - 2) // tk
    top = block(j0, (j0 * tk + col) < row_pos)
    lax.while_loop(_keep_sweeping, lambda state: (state[0] - 1, block(state[0], None)), (j0 - 1, top))
    o_ref[...] = acc_ref[...].astype(o_ref.dtype)


def _attention_prompt(q, kt, vt):
    n_streams, n_kblocks, d, tk = kt.shape
    length = n_kblocks * tk
    tq = min(length, ATTN_TQ)
    assert tk % tq == 0
    nq = length // tq
    kv = pl.BlockSpec((1, n_kblocks, d, tk), lambda b, i: (b, 0, 0, 0))
    rows = pl.BlockSpec((tq, d), lambda b, i: (b * nq + i, 0))
    return pl.pallas_call(
        _attn_prompt_kernel,
        grid=(n_streams, nq),
        in_specs=[rows, kv, kv],
        out_specs=rows,
        out_shape=jax.ShapeDtypeStruct((n_streams * length, d), BF16),
        scratch_shapes=[pltpu.VMEM((d // LANES, 2 * tq, LANES), BF16), pltpu.VMEM((d // LANES, 2 * tq, tk), F32),
                        pltpu.VMEM((tq, d), F32)],
        compiler_params=_params(("arbitrary", "arbitrary")),
        name="stick_breaking_prompt",
    )(q, kt, vt)


def _attn_sample_kernel(q_ref, kn_ref, vn_ref, ck_hbm, cv_hbm, o_ref, qm_ref, carry_ref, acc_ref, kbuf, vbuf, sem):
    b = pl.program_id(0)
    t, d = q_ref.shape
    tk = kbuf.shape[-1]
    n_pairs = d // LANES
    n_cache_blocks = ck_hbm.shape[1] // tk

    def fetch(j, slot):
        src = (pl.ds(pl.multiple_of(b * d, d), d), pl.ds(pl.multiple_of(j * tk, tk), tk))
        return (pltpu.make_async_copy(ck_hbm.at[src], kbuf.at[slot], sem.at[0, slot]),
                pltpu.make_async_copy(cv_hbm.at[src], vbuf.at[slot], sem.at[1, slot]))

    for cp in fetch(n_cache_blocks - 1, (n_cache_blocks - 1) % 2):
        cp.start()
    _attn_setup(q_ref, qm_ref, carry_ref, acc_ref)
    row = lax.rem(lax.broadcasted_iota(jnp.int32, (2 * t, tk), 0), t)
    col = lax.broadcasted_iota(jnp.int32, (2 * t, tk), 1)
    sums = _suffix_sum_matrix(tk)
    pad = jnp.zeros((tk - t, LANES), BF16)
    keys = [jnp.concatenate([kn_ref[:, p * LANES:(p + 1) * LANES], pad], axis=0) for p in range(n_pairs)]
    values = [jnp.concatenate([vn_ref[:, p * LANES:(p + 1) * LANES], pad], axis=0) for p in range(n_pairs)]
    top = _attn_block(qm_ref, carry_ref, acc_ref, keys, values, False, col < row, sums)

    def body(state):
        j, _ = state
        slot = lax.rem(j, 2)
        for cp in fetch(j, slot):
            cp.wait()

        @pl.when(j > 0)
        def _():
            for cp in fetch(j - 1, 1 - slot):
                cp.start()

        keys = [kbuf[slot, p * LANES:(p + 1) * LANES, :].astype(BF16) for p in range(n_pairs)]
        values = [vbuf[slot, p * LANES:(p + 1) * LANES, :].astype(BF16) for p in range(n_pairs)]
        return j - 1, _attn_block(qm_ref, carry_ref, acc_ref, keys, values, True, None, sums)

    j_end, _ = lax.while_loop(_keep_sweeping, body, (jnp.int32(n_cache_blocks - 1), top))

    @pl.when(j_end >= 0)
    def _():
        for cp in fetch(j_end, lax.rem(j_end, 2)):
            cp.wait()

    o_ref[...] = acc_ref[...].astype(o_ref.dtype)


def _attention_sample(q, k_new, v_new, cache_kt, cache_vt, t, q_row_off):
    d = q.shape[1]
    n_streams = k_new.shape[0] // t
    tk = ATTN_TK
    assert cache_kt.shape[1] % tk == 0 and t <= tk
    any_spec = pl.BlockSpec(memory_space=pl.ANY)
    new = pl.BlockSpec((t, d), lambda b: (b, 0))
    return pl.pallas_call(
        _attn_sample_kernel,
        grid=(n_streams,),
        in_specs=[pl.BlockSpec((t, d), lambda b: (q_row_off // t + b, 0)), new, new, any_spec, any_spec],
        out_specs=new,
        out_shape=jax.ShapeDtypeStruct((n_streams * t, d), BF16),
        scratch_shapes=[pltpu.VMEM((d // LANES, 2 * t, LANES), BF16), pltpu.VMEM((d // LANES, 2 * t, tk), F32),
                        pltpu.VMEM((t, d), F32), pltpu.VMEM((2, d, tk), F32), pltpu.VMEM((2, d, tk), F32),
                        pltpu.SemaphoreType.DMA((2, 2))],
        compiler_params=_params(("arbitrary",)),
        name="stick_breaking_sample",
    )(q, k_new, v_new, cache_kt, cache_vt)


def kernel(x_prompt, x_sample, state_pool, cache_k, cache_v, norm_mix, norm_ffn, pool_w, pool_scale, w_qkv, w_o, router_w, router_b, moe_w1, moe_b1, moe_w2, moe_b2, final_norm):
    b, s, d = x_prompt.shape
    db, t, _ = x_sample.shape
    past = cache_k.shape[2]
    n_exp = router_w.shape[2]
    hd = d // N_HEADS
    hist = POOL_HIST_ROWS
    n_p, n_s = b * s, db * t
    assert t >= hist and ROW_TILE % t == 0 and s % ROW_TILE == 0 and n_s % ROW_TILE == 0 and ROW_TILE % QKV_TILE == 0
    row = lambda a: a.reshape(1, -1)
    wr = [router_w[i].T for i in range(2)]
    br = [router_b[i].reshape(n_exp, 1) for i in range(2)]

    hist_s = jnp.concatenate([jnp.zeros((db, 1, d), F32), state_pool[0]], axis=1)
    xn, tok, idx, gates, rank, hlast_p, hlast_s, cnt = _pool_layer(
        x_prompt, x_sample, hist_s, past, row(norm_mix[0]), pool_w[0].astype(BF16), row(pool_scale[0]),
        row(norm_ffn[0]), wr[0], br[0])
    x1, h1 = _moe(tok, idx, gates, rank, cnt, xn, row(norm_mix[1]), None, 0, moe_w1, moe_b1[0], moe_w2, moe_b2[0])

    wqkv = w_qkv[0].astype(BF16)
    q, ktb, vtb, ktf, vtf, ksb, vsb, ksf, vsf = _qkv(h1, wqkv[:, :d], wqkv[:, d:2 * d].T, wqkv[:, 2 * d:].T, b, s)
    o_p = _attention_prompt(q, ktb, vtb)
    transposed = lambda cache: cache.transpose(0, 1, 3, 4, 2).reshape(db * d, past)
    o_s = _attention_sample(q, ksb, vsb, transposed(cache_k), transposed(cache_v), t, n_p)
    xn, tok, idx, gates, rank, cnt = _proj_layer(o_p, o_s, x1, w_o[0].astype(BF16), row(norm_ffn[1]), wr[1], br[1])
    y_p, y_s = _moe(tok, idx, gates, rank, cnt, xn, row(final_norm), n_p, 1, moe_w1, moe_b1[1], moe_w2, moe_b2[1])

    frames_major = lambda xt: xt.reshape(1, b, N_HEADS, hd, s).transpose(0, 1, 4, 2, 3)
    heads = lambda a: a.reshape(1, db, t, N_HEADS, hd)
    return (y_p.reshape(b, s, d), y_s.reshape(db, t, d), hlast_p[None, :, 1:, :],
            frames_major(ktf), frames_major(vtf), hlast_s[None, :, 1:, :], heads(ksf), heads(vsf))
```

```python
import functools

import jax
import jax.numpy as jnp
from jax import lax
from jax.experimental import pallas as pl
from jax.experimental.pallas import tpu as pltpu
from jax.experimental.pallas import tpu_sc as plsc

EPS = 1e-5
POOL_WINDOWS = (2, 4, 8, 16)
POOL_HIST_ROWS = 16
N_HEADS = 16
TOP_K = 4
SWIGLU_LIMIT = 7.0
SWIGLU_ALPHA = 1.702
LANES = 128
ROW_TILE = 512
QKV_TILE = 256
ATTN_TQ = 128
ATTN_TK = 128
ATTN_EXIT = 88.0
VMEM_LIMIT = 56 * 1024 * 1024
SC_CORES = 2
SC_SUBCORES = 16
SC_CHUNK = 96
SC_GATHER_CHUNK = 48
COMBINE_TILE = 512

F32 = jnp.float32
BF16 = jnp.bfloat16


def _rms(x, g):
    ms = jnp.mean(x * x, axis=-1, keepdims=True)
    return x * lax.rsqrt(ms + EPS) * g


def _dot(a, b):
    return jnp.dot(a, b, preferred_element_type=F32)


def _dot_nt(a, b, precision=None):
    return lax.dot_general(a, b, (((1,), (1,)), ((), ())), preferred_element_type=F32, precision=precision)


def _pack_halves(x):
    c = x.shape[1] // 2
    bits = lax.bitcast_convert_type(x.astype(BF16).astype(F32), jnp.uint32)
    return bits[:, :c] | (bits[:, c:] >> 16)


def _unpack_halves(w):
    return (lax.bitcast_convert_type(w & jnp.uint32(0xFFFF0000), F32), lax.bitcast_convert_type(w << 16, F32))


def _params(semantics):
    return pltpu.CompilerParams(dimension_semantics=semantics, vmem_limit_bytes=VMEM_LIMIT)


def _full(shape):
    return pl.BlockSpec(shape, lambda i, *_: (0,) * len(shape))


def _route_init(cnt_ref, tri_ref):
    r = tri_ref.shape[0]
    cnt_ref[...] = jnp.zeros(cnt_ref.shape, F32)
    tri_ref[...] = jnp.where(lax.broadcasted_iota(jnp.int32, (r, r), 0) < lax.broadcasted_iota(jnp.int32, (r, r), 1),
                             1.0, 0.0).astype(BF16)


def _route_tail(xn, gffn_ref, wr_ref, br_ref, cnt_ref, tri_ref, tok_ref, idx_ref, gate_ref, rank_ref):
    n_exp = wr_ref.shape[0]
    tok = _rms(xn, gffn_ref[...])
    tok_ref[...] = _pack_halves(tok)
    logits = _dot_nt(wr_ref[...], tok, precision=lax.Precision.HIGHEST) + br_ref[...]
    eidx = lax.broadcasted_iota(jnp.int32, logits.shape, 0).astype(F32)
    vals, idxs = [], []
    l = logits
    for _ in range(TOP_K):
        m = jnp.max(l, axis=0, keepdims=True)
        i = jnp.min(jnp.where(l == m, eidx, float(n_exp)), axis=0, keepdims=True)
        vals.append(m)
        idxs.append(i)
        l = jnp.where(eidx == i, -jnp.inf, l)
    es = [jnp.exp(v - vals[0]) for v in vals]
    den = es[0]
    for e in es[1:]:
        den = den + e
    gate_ref[...] = jnp.concatenate([e / den for e in es], axis=0)
    idx_ref[...] = jnp.concatenate(idxs, axis=0).astype(jnp.int32)
    member = jnp.zeros(logits.shape, F32)
    for i in idxs:
        member = member + jnp.where(eidx == i, 1.0, 0.0)
    before = _dot(member.astype(BF16), tri_ref[...]) + cnt_ref[:, :1]
    ranks = [jnp.sum(jnp.where(eidx == i, before, 0.0), axis=0, keepdims=True) for i in idxs]
    rank_ref[...] = jnp.concatenate(ranks, axis=0).astype(jnp.int32)
    cnt_ref[...] = cnt_ref[...] + jnp.sum(member, axis=1, keepdims=True)


def _pool_mix(h, ext_ref, pos, pw_ref, ps_ref):
    ts, d = h.shape[-2:]
    hist = POOL_HIST_ROWS
    group = d // len(POOL_WINDOWS)
    pre = (slice(None),) * (h.ndim - 2)
    ys = []
    for g, win in enumerate(POOL_WINDOWS):
        cols = slice(g * group, (g + 1) * group)
        hg = h[pre + (slice(None), cols)]
        acc = hg
        for j in range(1, win):
            acc = acc + ext_ref[pre + (slice(hist - j, hist - j + ts), cols)]
        cnt = jnp.minimum(pos + 1, win).astype(F32)
        dg = acc / cnt - hg
        ys.append(_dot(dg.reshape(-1, group).astype(BF16), pw_ref[g]))
    return jnp.concatenate(ys, axis=-1) * ps_ref[...]


def _pool_mix_tiled(h, ext_ref, lvl_ref, pos, pw_ref, ps_ref):
    ts, d = h.shape
    hist = POOL_HIST_ROWS
    group = d // len(POOL_WINDOWS)
    prev_ref, prev_col0 = ext_ref, 0
    ys = []
    for g, win in enumerate(POOL_WINDOWS):
        assert win == 2 * (POOL_WINDOWS[g - 1] if g else 1)
        col0 = g * group
        cols = slice(col0 - prev_col0, d - prev_col0)
        level = prev_ref[hist:hist + ts, cols] + prev_ref[hist - win // 2:hist - win // 2 + ts, cols]
        if g + 1 < len(POOL_WINDOWS):
            lvl_ref[g, hist:hist + ts, 0:d - col0 - group] = level[:, group:]
            prev_ref, prev_col0 = lvl_ref.at[g], col0 + group
        cnt = jnp.minimum(pos + 1, win).astype(F32)
        dg = level[:, :group] / cnt - h[:, col0:col0 + group]
        ys.append(_dot(dg.astype(BF16), pw_ref[g]))
    return jnp.concatenate(ys, axis=-1) * ps_ref[...]


def _pool_kernel(ntp, nt, pos0_s, xp_ref, xs_ref, hist_ref, gmix_ref, pw_ref, ps_ref, gffn_ref, wr_ref, br_ref,
                 xn_ref, tok_ref, idx_ref, gate_ref, rank_ref, hlast_p_ref, hlast_s_ref, cnt_out_ref,
                 ext_p, lvl_p, ext_s, cnt_ref, tri_ref):
    i = pl.program_id(0)
    hist = POOL_HIST_ROWS

    @pl.when(i == 0)
    def _():
        _route_init(cnt_ref, tri_ref)

    @pl.when(i < ntp)
    def _():
        ts, d = xp_ref.shape
        t = lax.rem(i, nt)

        @pl.when(t == 0)
        def _():
            ext_p[0:hist, :] = jnp.zeros((hist, d), F32)
            lvl_p[:, 0:hist, :] = jnp.zeros((lvl_p.shape[0], hist, lvl_p.shape[2]), F32)

        x = xp_ref[...]
        h = _rms(x, gmix_ref[...])
        ext_p[hist:hist + ts, :] = h

        @pl.when(t == nt - 1)
        def _():
            hlast_p_ref[0] = h[ts - hist:, :]

        pos = t * ts + lax.broadcasted_iota(jnp.int32, (ts, 1), 0)
        xn_ref[...] = x + _pool_mix_tiled(h, ext_p, lvl_p, pos, pw_ref, ps_ref)
        ext_p[0:hist, :] = ext_p[ts:ts + hist, :]
        lvl_p[:, 0:hist, :] = lvl_p[:, ts:ts + hist, :]

    @pl.when(i >= ntp)
    def _():
        bb, ts, d = xs_ref.shape
        ext_s[:, 0:hist, :] = hist_ref[...]
        x = xs_ref[...]
        h = _rms(x, gmix_ref[...])
        ext_s[:, hist:hist + ts, :] = h
        hlast_s_ref[...] = h[:, ts - hist:, :]
        pos = pos0_s + lax.broadcasted_iota(jnp.int32, (1, ts, 1), 1)
        xn_ref[...] = x.reshape(bb * ts, d) + _pool_mix(h, ext_s, pos, pw_ref, ps_ref)

    _route_tail(xn_ref[...], gffn_ref, wr_ref, br_ref, cnt_ref, tri_ref, tok_ref, idx_ref, gate_ref, rank_ref)
    cnt_out_ref[...] = cnt_ref[...]


def _pool_layer(x_prompt, x_sample, hist_s, pos0_s, gmix, pw, ps, gffn, wr, br):
    b, s, d = x_prompt.shape
    db, t, _ = x_sample.shape
    n_exp = wr.shape[0]
    r = ROW_TILE
    nt = s // r
    ntp = b * nt
    bb = r // t
    nts = db // bb
    n = b * s + db * t
    hist = POOL_HIST_ROWS
    rows = pl.BlockSpec((r, d), lambda i: (i, 0))
    lanes = pl.BlockSpec((TOP_K, r), lambda i: (0, i))
    sample_blk = lambda i: (jnp.maximum(i - ntp, 0), 0, 0)
    return pl.pallas_call(
        functools.partial(_pool_kernel, ntp, nt, pos0_s),
        grid=(ntp + nts,),
        in_specs=[pl.BlockSpec((r, d), lambda i: (jnp.minimum(i, ntp - 1), 0)),
                  pl.BlockSpec((bb, t, d), sample_blk), pl.BlockSpec((bb, hist, d), sample_blk),
                  _full((1, d)), _full(pw.shape), _full((1, d)), _full((1, d)), _full((n_exp, d)), _full((n_exp, 1))],
        out_specs=[rows, pl.BlockSpec((r, d // 2), lambda i: (i, 0)), lanes, lanes, lanes,
                   pl.BlockSpec((1, hist, d), lambda i: (jnp.minimum(i // nt, b - 1), 0, 0)),
                   pl.BlockSpec((bb, hist, d), sample_blk), _full((n_exp, LANES))],
        out_shape=[jax.ShapeDtypeStruct((n, d), F32), jax.ShapeDtypeStruct((n, d // 2), jnp.uint32),
                   jax.ShapeDtypeStruct((TOP_K, n), jnp.int32), jax.ShapeDtypeStruct((TOP_K, n), F32),
                   jax.ShapeDtypeStruct((TOP_K, n), jnp.int32),
                   jax.ShapeDtypeStruct((b, hist, d), F32), jax.ShapeDtypeStruct((db, hist, d), F32),
                   jax.ShapeDtypeStruct((n_exp, LANES), F32)],
        scratch_shapes=[pltpu.VMEM((hist + r, d), F32),
                        pltpu.VMEM((len(POOL_WINDOWS) - 1, hist + r, d - d // len(POOL_WINDOWS)), F32),
                        pltpu.VMEM((bb, hist + t, d), F32), pltpu.VMEM((n_exp, LANES), F32), pltpu.VMEM((r, r), BF16)],
        compiler_params=_params(("arbitrary",)),
        name="pool_route",
    )(x_prompt.reshape(b * s, d), x_sample, hist_s, gmix, pw, ps, gffn, wr, br)


def _proj_kernel(ntp, op_ref, os_ref, x_ref, wo_ref, gffn_ref, wr_ref, br_ref,
                 xn_ref, tok_ref, idx_ref, gate_ref, rank_ref, cnt_out_ref, cnt_ref, tri_ref):
    i = pl.program_id(0)

    @pl.when(i == 0)
    def _():
        _route_init(cnt_ref, tri_ref)

    o = jnp.where(i < ntp, op_ref[...], os_ref[...])
    xn = x_ref[...] + _dot(o, wo_ref[...])
    xn_ref[...] = xn
    _route_tail(xn, gffn_ref, wr_ref, br_ref, cnt_ref, tri_ref, tok_ref, idx_ref, gate_ref, rank_ref)
    cnt_out_ref[...] = cnt_ref[...]


def _proj_layer(o_p, o_s, x, wo, gffn, wr, br):
    n, d = x.shape
    n_exp = wr.shape[0]
    r = ROW_TILE
    ntp = o_p.shape[0] // r
    rows = pl.BlockSpec((r, d), lambda i: (i, 0))
    lanes = pl.BlockSpec((TOP_K, r), lambda i: (0, i))
    return pl.pallas_call(
        functools.partial(_proj_kernel, ntp),
        grid=(n // r,),
        in_specs=[pl.BlockSpec((r, d), lambda i: (jnp.minimum(i, ntp - 1), 0)),
                  pl.BlockSpec((r, d), lambda i: (jnp.maximum(i - ntp, 0), 0)),
                  rows, _full((d, d)), _full((1, d)), _full((n_exp, d)), _full((n_exp, 1))],
        out_specs=[rows, pl.BlockSpec((r, d // 2), lambda i: (i, 0)), lanes, lanes, lanes, _full((n_exp, LANES))],
        out_shape=[jax.ShapeDtypeStruct((n, d), F32), jax.ShapeDtypeStruct((n, d // 2), jnp.uint32),
                   jax.ShapeDtypeStruct((TOP_K, n), jnp.int32), jax.ShapeDtypeStruct((TOP_K, n), F32),
                   jax.ShapeDtypeStruct((TOP_K, n), jnp.int32), jax.ShapeDtypeStruct((n_exp, LANES), F32)],
        scratch_shapes=[pltpu.VMEM((n_exp, LANES), F32), pltpu.VMEM((r, r), BF16)],
        compiler_params=_params(("arbitrary",)),
        name="proj_route",
    )(o_p, o_s, x, wo, gffn, wr, br)


def _sc_chunks(dest, w):
    n = dest.shape[1]
    return dest.reshape(TOP_K, n // w, w).transpose(1, 0, 2)


def _dispatch(tok, dest, n_buf_rows):
    n, d = tok.shape
    w = SC_CHUNK
    n_workers = SC_CORES * SC_SUBCORES
    per_worker = n // (w * n_workers)
    assert per_worker * w * n_workers == n
    mesh = plsc.VectorSubcoreMesh(core_axis_name="core", subcore_axis_name="subcore",
                                  num_cores=SC_CORES, num_subcores=SC_SUBCORES)

    @functools.partial(
        pl.kernel, mesh=mesh, out_type=jax.ShapeDtypeStruct((n_buf_rows, d), tok.dtype),
        scratch_types=[pltpu.VMEM((TOP_K, w), jnp.int32), pltpu.VMEM((w, d), tok.dtype), pltpu.SemaphoreType.DMA],
        name="moe_dispatch_sc")
    def scatter_rows(tok_hbm, dest_hbm, buf_hbm, idx_v, rows_v, sem):
        worker = lax.axis_index("subcore") * SC_CORES + lax.axis_index("core")

        @pl.loop(0, per_worker)
        def _(c):
            chunk = worker * per_worker + c
            pltpu.sync_copy(dest_hbm.at[chunk], idx_v)
            pltpu.sync_copy(tok_hbm.at[pl.ds(pl.multiple_of(chunk * w, w), w)], rows_v)
            copies = [pltpu.async_copy(rows_v, buf_hbm.at[idx_v.at[k]], sem) for k in range(TOP_K)]
            for cp in copies:
                cp.wait()

    return scatter_rows(tok, _sc_chunks(dest, w))


def _expert_kernel(layer, be_ref, valid_ref, nused_ref, next_ref, slot_ref, x_ref, w1_hbm, b1_ref, w2_hbm, b2_ref,
                   o_ref, w1f_ref, w2f_ref, w1b_ref, w2b_ref, sem):
    i = pl.program_id(0)
    used = valid_ref[i] > 0

    def fetch(expert, slot):
        return (pltpu.make_async_copy(w1_hbm.at[layer, expert], w1f_ref.at[slot], sem.at[0, slot]),
                pltpu.make_async_copy(w2_hbm.at[layer, expert], w2f_ref.at[slot], sem.at[1, slot]))

    @pl.when(used & ((i == 0) | (be_ref[i] != be_ref[jnp.maximum(i - 1, 0)])))
    def _():
        slot = slot_ref[i]

        @pl.when(i == 0)
        def _():
            for cp in fetch(be_ref[i], slot):
                cp.start()

        for cp in fetch(be_ref[i], slot):
            cp.wait()

        @pl.when(next_ref[i] >= 0)
        def _():
            for cp in fetch(next_ref[i], 1 - slot):
                cp.start()

        w1b_ref[...] = w1f_ref[slot].astype(BF16)
        w2b_ref[...] = w2f_ref[slot].astype(BF16)

    @pl.when(used)
    def _():
        f = w2b_ref.shape[0]
        row = lax.broadcasted_iota(jnp.int32, (x_ref.shape[0], 1), 0)
        x = jnp.where(row < valid_ref[i], x_ref[...], jnp.uint32(0))
        xa, xb = _unpack_halves(x)
        half = x.shape[1]
        gu = (_dot(xa.astype(BF16), w1b_ref[:half, :]) + _dot(xb.astype(BF16), w1b_ref[half:, :])) + b1_ref[0]
        gate = jnp.minimum(gu[:, :f], SWIGLU_LIMIT)
        up = jnp.clip(gu[:, f:], -SWIGLU_LIMIT, SWIGLU_LIMIT)
        hid = (up + 1.0) * (gate * jax.nn.sigmoid(gate * SWIGLU_ALPHA))
        o_ref[...] = _pack_halves(_dot(hid.astype(BF16), w2b_ref[...]) + b2_ref[0])

    @pl.when(jnp.logical_not(used))
    def _():
        o_ref[...] = jnp.zeros(o_ref.shape, o_ref.dtype)


def _experts(buf, block_expert, valid, nused, next_expert, slot, layer, w1, b1, w2, b2, tm):
    rows, dw = buf.shape
    _, n_exp, d, f2 = w1.shape
    f = w2.shape[2]
    in_rows = lambda i, be, va, nu, nx, sl: (jnp.minimum(i, nu[0] - 1), 0)
    by_expert = lambda i, be, va, nu, nx, sl: (be[i], 0, 0)
    any_spec = pl.BlockSpec(memory_space=pl.ANY)
    return pl.pallas_call(
        functools.partial(_expert_kernel, layer),
        grid_spec=pltpu.PrefetchScalarGridSpec(
            num_scalar_prefetch=5,
            grid=(rows // tm,),
            in_specs=[pl.BlockSpec((tm, dw), in_rows), any_spec, pl.BlockSpec((1, 1, f2), by_expert),
                      any_spec, pl.BlockSpec((1, 1, d), by_expert)],
            out_specs=pl.BlockSpec((tm, dw), lambda i, be, va, nu, nx, sl: (i, 0)),
            scratch_shapes=[pltpu.VMEM((2, d, f2), F32), pltpu.VMEM((2, f, d), F32),
                            pltpu.VMEM((d, f2), BF16), pltpu.VMEM((f, d), BF16), pltpu.SemaphoreType.DMA((2, 2))]),
        out_shape=jax.ShapeDtypeStruct((rows, dw), jnp.uint32),
        compiler_params=_params(("arbitrary",)),
        name="moe_experts",
    )(block_expert, valid, nused, next_expert, slot, buf, w1, b1.reshape(n_exp, 1, f2), w2, b2.reshape(n_exp, 1, d))


def _gather_rows(ybuf, dest):
    _, d = ybuf.shape
    n = dest.shape[1]
    w = SC_GATHER_CHUNK
    n_workers = SC_CORES * SC_SUBCORES
    per_worker = n // (w * n_workers)
    assert per_worker * w * n_workers == n and TOP_K % 2 == 0
    mesh = plsc.VectorSubcoreMesh(core_axis_name="core", subcore_axis_name="subcore",
                                  num_cores=SC_CORES, num_subcores=SC_SUBCORES)

    @functools.partial(
        pl.kernel, mesh=mesh, out_type=jax.ShapeDtypeStruct((TOP_K, n, d), ybuf.dtype),
        scratch_types=[pltpu.VMEM((TOP_K, w), jnp.int32), pltpu.VMEM((2, w, d), ybuf.dtype),
                       pltpu.SemaphoreType.DMA((2,)), pltpu.SemaphoreType.DMA((2,))],
        name="moe_gather_sc")
    def gather_rows(y_hbm, dest_hbm, out_hbm, idx_v, rows_v, sem_in, sem_out):
        worker = lax.axis_index("subcore") * SC_CORES + lax.axis_index("core")

        @pl.loop(0, per_worker)
        def _(c):
            chunk = worker * per_worker + c
            tokens = pl.ds(pl.multiple_of(chunk * w, w), w)
            pltpu.sync_copy(dest_hbm.at[chunk], idx_v)
            fetch = lambda k: pltpu.async_copy(y_hbm.at[idx_v.at[k]], rows_v.at[k % 2], sem_in.at[k % 2])
            store = lambda k: pltpu.async_copy(rows_v.at[k % 2], out_hbm.at[k, tokens], sem_out.at[k % 2])
            fetches = [fetch(0), fetch(1)]
            stores = []
            for k in range(TOP_K):
                fetches[k].wait()
                stores.append(store(k))
                if k + 2 < TOP_K:
                    stores[k].wait()
                    fetches.append(fetch(k + 2))
            for k in range(TOP_K - 2, TOP_K):
                stores[k].wait()

    return gather_rows(ybuf, _sc_chunks(dest, w))


def _combine_kernel(ntp, rows_ref, gate_ref, x_ref, g_ref, *outs):
    i = pl.program_id(0)
    r = x_ref.shape[0]
    gates = jnp.concatenate([gate_ref[...], jnp.zeros((LANES - TOP_K, r), F32)], axis=0)
    gates_t = jnp.transpose(gates)
    halves = [_unpack_halves(rows_ref[k]) for k in range(TOP_K)]
    f = []
    for side in range(2):
        acc = halves[0][side] * gates_t[:, 0:1]
        for k in range(1, TOP_K):
            acc = acc + halves[k][side] * gates_t[:, k:k + 1]
        f.append(acc)
    xo = x_ref[...] + jnp.concatenate(f, axis=1)
    hn = _rms(xo, g_ref[...])
    if ntp is None:
        xo_ref, hn_ref = outs
        xo_ref[...] = xo
        hn_ref[...] = hn.astype(hn_ref.dtype)
    else:
        hp_ref, hs_ref = outs

        @pl.when(i < ntp)
        def _():
            hp_ref[...] = hn

        @pl.when(i >= ntp)
        def _():
            hs_ref[...] = hn


def _combine(rows4, gates, x, g, n_p):
    n, d = x.shape
    r = COMBINE_TILE
    rows = pl.BlockSpec((r, d), lambda i: (i, 0))
    if n_p is None:
        ntp = None
        out_specs = [rows, rows]
        out_shape = [jax.ShapeDtypeStruct((n, d), F32), jax.ShapeDtypeStruct((n, d), BF16)]
    else:
        ntp = n_p // r
        out_specs = [pl.BlockSpec((r, d), lambda i: (jnp.minimum(i, ntp - 1), 0)),
                     pl.BlockSpec((r, d), lambda i: (jnp.maximum(i - ntp, 0), 0))]
        out_shape = [jax.ShapeDtypeStruct((n_p, d), F32), jax.ShapeDtypeStruct((n - n_p, d), F32)]
    return pl.pallas_call(
        functools.partial(_combine_kernel, ntp),
        grid=(n // r,),
        in_specs=[pl.BlockSpec((TOP_K, r, d // 2), lambda i: (0, i, 0)), pl.BlockSpec((TOP_K, r), lambda i: (0, i)),
                  rows, _full((1, d))],
        out_specs=out_specs,
        out_shape=out_shape,
        compiler_params=_params(("arbitrary",)),
        name="moe_combine",
    )(rows4, gates, x, g)


def _moe(tok, idx, gates, rank, cnt, x, g_next, n_p, layer, w1, b1, w2, b2):
    n, d = tok.shape
    n_exp = w1.shape[1]
    pairs = n * TOP_K
    tm = 128
    for cand in (512, 256):
        if pairs >= 4 * cand * n_exp and pairs % cand == 0:
            tm = cand
            break
    n_blocks = -(-pairs // tm) + n_exp
    counts = cnt[:, 0].astype(jnp.int32)
    padded = (counts + tm - 1) // tm * tm
    pad_end = jnp.cumsum(padded)
    pad_start = pad_end - padded
    experts = jnp.arange(n_exp, dtype=jnp.int32)
    dest = rank + jnp.sum(jnp.where(idx[:, :, None] == experts, pad_start, 0), axis=-1)
    starts = jnp.arange(n_blocks, dtype=jnp.int32) * tm
    nused = (pad_end[-1:] // tm).astype(jnp.int32)
    block_expert = jnp.minimum(jnp.sum((pad_end[None, :] <= starts[:, None]).astype(jnp.int32), axis=1), n_exp - 1)
    tokens_end = jnp.sum(jnp.where(block_expert[:, None] == experts, pad_start + counts, 0), axis=-1)
    valid = jnp.clip(tokens_end - starts, 0, tm).astype(jnp.int32)
    has_tokens = counts > 0
    later = has_tokens[None, :] & (experts[None, :] > experts[:, None])
    successor = jnp.min(jnp.where(later, experts[None, :], n_exp), axis=1)
    successor = jnp.where(successor == n_exp, -1, successor)
    parity = (jnp.cumsum(has_tokens.astype(jnp.int32)) - 1) % 2
    of_block = lambda table: jnp.sum(jnp.where(block_expert[:, None] == experts, table, 0), axis=-1).astype(jnp.int32)
    buf = _dispatch(tok, dest, n_blocks * tm)
    ybuf = _experts(buf, block_expert, valid, nused, of_block(successor), of_block(parity), layer, w1, b1, w2, b2, tm)
    return _combine(_gather_rows(ybuf, dest), gates, x, g_next, n_p)


def _qkv_kernel(ntp, h_ref, wq_ref, wkt_ref, wvt_ref, q_ref, ktb_ref, vtb_ref, ktf_ref, vtf_ref,
                ksb_ref, vsb_ref, ksf_ref, vsf_ref):
    i = pl.program_id(0)
    r, d = h_ref.shape
    tk = ktb_ref.shape[-1]
    h = h_ref[...]
    q_ref[...] = (_dot(h, wq_ref[...]) * ((d // N_HEADS) ** -0.5)).astype(BF16)

    @pl.when(i < ntp)
    def _():
        for wt_ref, tb_ref, tf_ref in ((wkt_ref, ktb_ref, ktf_ref), (wvt_ref, vtb_ref, vtf_ref)):
            xt = _dot_nt(wt_ref[...], h)
            tf_ref[...] = xt
            for c in range(r // tk):
                tb_ref[0, c] = xt[:, c * tk:(c + 1) * tk].astype(BF16)

    @pl.when(i >= ntp)
    def _():
        for wt_ref, sb_ref, sf_ref in ((wkt_ref, ksb_ref, ksf_ref), (wvt_ref, vsb_ref, vsf_ref)):
            x = _dot_nt(h, wt_ref[...])
            sf_ref[...] = x
            sb_ref[...] = x.astype(BF16)


def _qkv(h, wq, wkt, wvt, n_streams, length):
    n, d = h.shape
    r = QKV_TILE
    tk = ATTN_TK
    n_p = n_streams * length
    nt = length // r
    ntp = n_p // r
    rows = pl.BlockSpec((r, d), lambda i: (i, 0))
    prompt_blk = lambda i: (jnp.minimum(i, ntp - 1) // nt, lax.rem(jnp.minimum(i, ntp - 1), nt))
    t_blocks = pl.BlockSpec((1, r // tk, d, tk), lambda i: prompt_blk(i) + (0, 0))
    t_full = pl.BlockSpec((d, r), prompt_blk)
    sample = pl.BlockSpec((r, d), lambda i: (jnp.maximum(i - ntp, 0), 0))
    w_spec = _full((d, d))
    return pl.pallas_call(
        functools.partial(_qkv_kernel, ntp),
        grid=(n // r,),
        in_specs=[rows, w_spec, w_spec, w_spec],
        out_specs=[rows, t_blocks, t_blocks, t_full, t_full, sample, sample, sample, sample],
        out_shape=[jax.ShapeDtypeStruct((n, d), BF16)]
        + [jax.ShapeDtypeStruct((n_streams, length // tk, d, tk), BF16)] * 2
        + [jax.ShapeDtypeStruct((n_streams * d, length), F32)] * 2
        + [jax.ShapeDtypeStruct((n - n_p, d), BF16)] * 2 + [jax.ShapeDtypeStruct((n - n_p, d), F32)] * 2,
        compiler_params=_params(("arbitrary",)),
        name="qkv",
    )(h, wq, wkt, wvt)


def _attn_setup(q_ref, qm_ref, carry_ref, acc_ref):
    tq, d = q_ref.shape
    lane = lax.broadcasted_iota(jnp.int32, (tq, LANES), 1)
    for p in range(d // LANES):
        qp = q_ref[:, p * LANES:(p + 1) * LANES]
        qm_ref[p, 0:tq, :] = jnp.where(lane < LANES // 2, qp, jnp.zeros_like(qp))
        qm_ref[p, tq:2 * tq, :] = jnp.where(lane >= LANES // 2, qp, jnp.zeros_like(qp))
    carry_ref[...] = jnp.zeros(carry_ref.shape, F32)
    acc_ref[...] = jnp.zeros(acc_ref.shape, F32)


def _suffix_sum_matrix(tk):
    src = lax.rem(lax.broadcasted_iota(jnp.int32, (2 * tk, 2 * tk), 0), tk)
    dst = lax.broadcasted_iota(jnp.int32, (2 * tk, 2 * tk), 1)
    return jnp.where((dst >= tk) | (src > dst), 1.0, 0.0).astype(BF16)


def _attn_block(qm_ref, carry_ref, acc_ref, keys, values, transposed, mask, sums):
    n_pairs, tq2, tk = carry_ref.shape
    tq = tq2 // 2
    lane = lax.broadcasted_iota(jnp.int32, (tq, LANES), 1)
    visible = (lambda x: x) if mask is None else (lambda x: jnp.where(mask, x, 0.0))
    scores = _dot if transposed else _dot_nt
    mix = _dot_nt if transposed else _dot
    zs = [scores(qm_ref[p], keys[p]) for p in range(n_pairs)]
    log_beta, parts = [], []
    for z in zs:
        sp = jnp.maximum(z, 0.0) + jnp.log(1.0 + jnp.exp(-jnp.abs(z)))
        log_keep = visible(-sp)
        hi = log_keep.astype(BF16)
        lo = (log_keep - hi.astype(F32)).astype(BF16)
        parts.append(jnp.concatenate([hi, lo], axis=1))
        log_beta.append(z - sp)
    sums_out = [_dot(part, sums) for part in parts]
    weights = []
    top = jnp.full((tq2, tk), -jnp.inf, F32)
    for p in range(n_pairs):
        carry = carry_ref[p]
        a = visible(jnp.exp(log_beta[p] + sums_out[p][:, :tk] + carry))
        weights.append(a.astype(BF16))
        carry = carry + sums_out[p][:, tk:]
        carry_ref[p] = carry
        top = jnp.maximum(top, carry)
    for p in range(n_pairs):
        out = mix(weights[p], values[p])
        acc_ref[:, p * LANES:(p + 1) * LANES] += jnp.where(lane < LANES // 2, out[:tq], out[tq:])
    return jnp.max(top)


def _keep_sweeping(state):
    j, top = state
    return (j >= 0) & (top > -ATTN_EXIT)


def _attn_prompt_kernel(q_ref, kt_ref, vt_ref, o_ref, qm_ref, carry_ref, acc_ref):
    qi = pl.program_id(1)
    tq, d = q_ref.shape
    tk = kt_ref.shape[-1]
    n_pairs = d // LANES
    _attn_setup(q_ref, qm_ref, carry_ref, acc_ref)
    row_pos = qi * tq + lax.rem(lax.broadcasted_iota(jnp.int32, (2 * tq, tk), 0), tq)
    col = lax.broadcasted_iota(jnp.int32, (2 * tq, tk), 1)
    sums = _suffix_sum_matrix(tk)

    def block(j, mask):
        keys = [kt_ref[0, j, p * LANES:(p + 1) * LANES, :] for p in range(n_pairs)]
        values = [vt_ref[0, j, p * LANES:(p + 1) * LANES, :] for p in range(n_pairs)]
        return _attn_block(qm_ref, carry_ref, acc_ref, keys, values, True, mask, sums)

    j0 = ((qi + 1) * tq - 2) // tk
    top = block(j0, (j0 * tk + col) < row_pos)
    lax.while_loop(_keep_sweeping, lambda state: (state[0] - 1, block(state[0], None)), (j0 - 1, top))
    o_ref[...] = acc_ref[...].astype(o_ref.dtype)


def _attention_prompt(q, kt, vt):
    n_streams, n_kblocks, d, tk = kt.shape
    length = n_kblocks * tk
    tq = min(length, ATTN_TQ)
    assert tk % tq == 0
    nq = length // tq
    kv = pl.BlockSpec((1, n_kblocks, d, tk), lambda b, i: (b, 0, 0, 0))
    rows = pl.BlockSpec((tq, d), lambda b, i: (b * nq + i, 0))
    return pl.pallas_call(
        _attn_prompt_kernel,
        grid=(n_streams, nq),
        in_specs=[rows, kv, kv],
        out_specs=rows,
        out_shape=jax.ShapeDtypeStruct((n_streams * length, d), BF16),
        scratch_shapes=[pltpu.VMEM((d // LANES, 2 * tq, LANES), BF16), pltpu.VMEM((d // LANES, 2 * tq, tk), F32),
                        pltpu.VMEM((tq, d), F32)],
        compiler_params=_params(("arbitrary", "arbitrary")),
        name="stick_breaking_prompt",
    )(q, kt, vt)


def _attn_sample_kernel(q_ref, kn_ref, vn_ref, ck_hbm, cv_hbm, o_ref, qm_ref, carry_ref, acc_ref, kbuf, vbuf, sem):
    b = pl.program_id(0)
    t, d = q_ref.shape
    tk = kbuf.shape[-1]
    n_pairs = d // LANES
    n_cache_blocks = ck_hbm.shape[1] // tk

    def fetch(j, slot):
        src = (pl.ds(pl.multiple_of(b * d, d), d), pl.ds(pl.multiple_of(j * tk, tk), tk))
        return (pltpu.make_async_copy(ck_hbm.at[src], kbuf.at[slot], sem.at[0, slot]),
                pltpu.make_async_copy(cv_hbm.at[src], vbuf.at[slot], sem.at[1, slot]))

    for cp in fetch(n_cache_blocks - 1, (n_cache_blocks - 1) % 2):
        cp.start()
    _attn_setup(q_ref, qm_ref, carry_ref, acc_ref)
    row = lax.rem(lax.broadcasted_iota(jnp.int32, (2 * t, tk), 0), t)
    col = lax.broadcasted_iota(jnp.int32, (2 * t, tk), 1)
    sums = _suffix_sum_matrix(tk)
    pad = jnp.zeros((tk - t, LANES), BF16)
    keys = [jnp.concatenate([kn_ref[:, p * LANES:(p + 1) * LANES], pad], axis=0) for p in range(n_pairs)]
    values = [jnp.concatenate([vn_ref[:, p * LANES:(p + 1) * LANES], pad], axis=0) for p in range(n_pairs)]
    top = _attn_block(qm_ref, carry_ref, acc_ref, keys, values, False, col < row, sums)

    def body(state):
        j, _ = state
        slot = lax.rem(j, 2)
        for cp in fetch(j, slot):
            cp.wait()

        @pl.when(j > 0)
        def _():
            for cp in fetch(j - 1, 1 - slot):
                cp.start()

        keys = [kbuf[slot, p * LANES:(p + 1) * LANES, :].astype(BF16) for p in range(n_pairs)]
        values = [vbuf[slot, p * LANES:(p + 1) * LANES, :].astype(BF16) for p in range(n_pairs)]
        return j - 1, _attn_block(qm_ref, carry_ref, acc_ref, keys, values, True, None, sums)

    j_end, _ = lax.while_loop(_keep_sweeping, body, (jnp.int32(n_cache_blocks - 1), top))

    @pl.when(j_end >= 0)
    def _():
        for cp in fetch(j_end, lax.rem(j_end, 2)):
            cp.wait()

    o_ref[...] = acc_ref[...].astype(o_ref.dtype)


def _attention_sample(q, k_new, v_new, cache_kt, cache_vt, t, q_row_off):
    d = q.shape[1]
    n_streams = k_new.shape[0] // t
    tk = ATTN_TK
    assert cache_kt.shape[1] % tk == 0 and t <= tk
    any_spec = pl.BlockSpec(memory_space=pl.ANY)
    new = pl.BlockSpec((t, d), lambda b: (b, 0))
    return pl.pallas_call(
        _attn_sample_kernel,
        grid=(n_streams,),
        in_specs=[pl.BlockSpec((t, d), lambda b: (q_row_off // t + b, 0)), new, new, any_spec, any_spec],
        out_specs=new,
        out_shape=jax.ShapeDtypeStruct((n_streams * t, d), BF16),
        scratch_shapes=[pltpu.VMEM((d // LANES, 2 * t, LANES), BF16), pltpu.VMEM((d // LANES, 2 * t, tk), F32),
                        pltpu.VMEM((t, d), F32), pltpu.VMEM((2, d, tk), F32), pltpu.VMEM((2, d, tk), F32),
                        pltpu.SemaphoreType.DMA((2, 2))],
        compiler_params=_params(("arbitrary",)),
        name="stick_breaking_sample",
    )(q, k_new, v_new, cache_kt, cache_vt)


def kernel(x_prompt, x_sample, state_pool, cache_k, cache_v, norm_mix, norm_ffn, pool_w, pool_scale, w_qkv, w_o, router_w, router_b, moe_w1, moe_b1, moe_w2, moe_b2, final_norm):
    b, s, d = x_prompt.shape
    db, t, _ = x_sample.shape
    past = cache_k.shape[2]
    n_exp = router_w.shape[2]
    hd = d // N_HEADS
    hist = POOL_HIST_ROWS
    n_p, n_s = b * s, db * t
    assert t >= hist and ROW_TILE % t == 0 and s % ROW_TILE == 0 and n_s % ROW_TILE == 0 and ROW_TILE % QKV_TILE == 0
    row = lambda a: a.reshape(1, -1)
    wr = [router_w[i].T for i in range(2)]
    br = [router_b[i].reshape(n_exp, 1) for i in range(2)]

    hist_s = jnp.concatenate([jnp.zeros((db, 1, d), F32), state_pool[0]], axis=1)
    xn, tok, idx, gates, rank, hlast_p, hlast_s, cnt = _pool_layer(
        x_prompt, x_sample, hist_s, past, row(norm_mix[0]), pool_w[0].astype(BF16), row(pool_scale[0]),
        row(norm_ffn[0]), wr[0], br[0])
    x1, h1 = _moe(tok, idx, gates, rank, cnt, xn, row(norm_mix[1]), None, 0, moe_w1, moe_b1[0], moe_w2, moe_b2[0])

    wqkv = w_qkv[0].astype(BF16)
    q, ktb, vtb, ktf, vtf, ksb, vsb, ksf, vsf = _qkv(h1, wqkv[:, :d], wqkv[:, d:2 * d].T, wqkv[:, 2 * d:].T, b, s)
    o_p = _attention_prompt(q, ktb, vtb)
    transposed = lambda cache: cache.transpose(0, 1, 3, 4, 2).reshape(db * d, past)
    o_s = _attention_sample(q, ksb, vsb, transposed(cache_k), transposed(cache_v), t, n_p)
    xn, tok, idx, gates, rank, cnt = _proj_layer(o_p, o_s, x1, w_o[0].astype(BF16), row(norm_ffn[1]), wr[1], br[1])
    y_p, y_s = _moe(tok, idx, gates, rank, cnt, xn, row(final_norm), n_p, 1, moe_w1, moe_b1[1], moe_w2, moe_b2[1])

    frames_major = lambda xt: xt.reshape(1, b, N_HEADS, hd, s).transpose(0, 1, 4, 2, 3)
    heads = lambda a: a.reshape(1, db, t, N_HEADS, hd)
    return (y_p.reshape(b, s, d), y_s.reshape(db, t, d), hlast_p[None, :, 1:, :],
            frames_major(ktf), frames_major(vtf), hlast_s[None, :, 1:, :], heads(ksf), heads(vsf))
```

```python
import functools

import jax
import jax.numpy as jnp
from jax import lax
from jax.experimental import pallas as pl
from jax.experimental.pallas import tpu as pltpu
from jax.experimental.pallas import tpu_sc as plsc

EPS = 1e-5
POOL_WINDOWS = (2, 4, 8, 16)
POOL_HIST_ROWS = 16
N_HEADS = 16
TOP_K = 4
SWIGLU_LIMIT = 7.0
SWIGLU_ALPHA = 1.702
LANES = 128
ROW_TILE = 512
QKV_TILE = 512
ATTN_TQ = 128
ATTN_TK = 128
ATTN_EXIT = 88.0
VMEM_LIMIT = 56 * 1024 * 1024
SC_CORES = 2
SC_SUBCORES = 16
SC_CHUNK = 96
SC_GATHER_CHUNK = 48
COMBINE_TILE = 512

F32 = jnp.float32
BF16 = jnp.bfloat16


def _rms(x, g):
    ms = jnp.mean(x * x, axis=-1, keepdims=True)
    return x * lax.rsqrt(ms + EPS) * g


def _dot(a, b):
    return jnp.dot(a, b, preferred_element_type=F32)


def _dot_nt(a, b, precision=None):
    return lax.dot_general(a, b, (((1,), (1,)), ((), ())), preferred_element_type=F32, precision=precision)


def _pack_halves(x):
    c = x.shape[1] // 2
    bits = lax.bitcast_convert_type(x.astype(BF16).astype(F32), jnp.uint32)
    return bits[:, :c] | (bits[:, c:] >> 16)


def _unpack_halves(w):
    return (lax.bitcast_convert_type(w & jnp.uint32(0xFFFF0000), F32), lax.bitcast_convert_type(w << 16, F32))


def _params(semantics):
    return pltpu.CompilerParams(dimension_semantics=semantics, vmem_limit_bytes=VMEM_LIMIT)


def _full(shape):
    return pl.BlockSpec(shape, lambda i, *_: (0,) * len(shape))


def _route_init(cnt_ref, tri_ref):
    r = tri_ref.shape[0]
    cnt_ref[...] = jnp.zeros(cnt_ref.shape, F32)
    tri_ref[...] = jnp.where(lax.broadcasted_iota(jnp.int32, (r, r), 0) < lax.broadcasted_iota(jnp.int32, (r, r), 1),
                             1.0, 0.0).astype(BF16)


def _route_tail(xn, gffn_ref, wr_ref, br_ref, cnt_ref, tri_ref, tok_ref, idx_ref, gate_ref, rank_ref):
    n_exp = wr_ref.shape[0]
    tok = _rms(xn, gffn_ref[...])
    tok_ref[...] = _pack_halves(tok)
    logits = _dot_nt(wr_ref[...], tok, precision=lax.Precision.HIGHEST) + br_ref[...]
    eidx = lax.broadcasted_iota(jnp.int32, logits.shape, 0).astype(F32)
    vals, idxs = [], []
    l = logits
    for _ in range(TOP_K):
        m = jnp.max(l, axis=0, keepdims=True)
        i = jnp.min(jnp.where(l == m, eidx, float(n_exp)), axis=0, keepdims=True)
        vals.append(m)
        idxs.append(i)
        l = jnp.where(eidx == i, -jnp.inf, l)
    es = [jnp.exp(v - vals[0]) for v in vals]
    den = es[0]
    for e in es[1:]:
        den = den + e
    gate_ref[...] = jnp.concatenate([e / den for e in es], axis=0)
    idx_ref[...] = jnp.concatenate(idxs, axis=0).astype(jnp.int32)
    member = jnp.zeros(logits.shape, F32)
    for i in idxs:
        member = member + jnp.where(eidx == i, 1.0, 0.0)
    before = _dot(member.astype(BF16), tri_ref[...]) + cnt_ref[:, :1]
    ranks = [jnp.sum(jnp.where(eidx == i, before, 0.0), axis=0, keepdims=True) for i in idxs]
    rank_ref[...] = jnp.concatenate(ranks, axis=0).astype(jnp.int32)
    cnt_ref[...] = cnt_ref[...] + jnp.sum(member, axis=1, keepdims=True)


def _pool_mix(h, ext_ref, pos, pw_ref, ps_ref):
    ts, d = h.shape[-2:]
    hist = POOL_HIST_ROWS
    group = d // len(POOL_WINDOWS)
    pre = (slice(None),) * (h.ndim - 2)
    ys = []
    for g, win in enumerate(POOL_WINDOWS):
        cols = slice(g * group, (g + 1) * group)
        hg = h[pre + (slice(None), cols)]
        acc = hg
        for j in range(1, win):
            acc = acc + ext_ref[pre + (slice(hist - j, hist - j + ts), cols)]
        cnt = jnp.minimum(pos + 1, win).astype(F32)
        dg = acc / cnt - hg
        ys.append(_dot(dg.reshape(-1, group).astype(BF16), pw_ref[g]))
    return jnp.concatenate(ys, axis=-1) * ps_ref[...]


def _pool_mix_tiled(h, ext_ref, lvl_ref, pos, pw_ref, ps_ref):
    ts, d = h.shape
    hist = POOL_HIST_ROWS
    group = d // len(POOL_WINDOWS)
    prev_ref, prev_col0 = ext_ref, 0
    ys = []
    for g, win in enumerate(POOL_WINDOWS):
        assert win == 2 * (POOL_WINDOWS[g - 1] if g else 1)
        col0 = g * group
        cols = slice(col0 - prev_col0, d - prev_col0)
        level = prev_ref[hist:hist + ts, cols] + prev_ref[hist - win // 2:hist - win // 2 + ts, cols]
        if g + 1 < len(POOL_WINDOWS):
            lvl_ref[g, hist:hist + ts, 0:d - col0 - group] = level[:, group:]
            prev_ref, prev_col0 = lvl_ref.at[g], col0 + group
        cnt = jnp.minimum(pos + 1, win).astype(F32)
        dg = level[:, :group] / cnt - h[:, col0:col0 + group]
        ys.append(_dot(dg.astype(BF16), pw_ref[g]))
    return jnp.concatenate(ys, axis=-1) * ps_ref[...]


def _pool_kernel(ntp, nt, pos0_s, xp_ref, xs_ref, hist_ref, gmix_ref, pw_ref, ps_ref, gffn_ref, wr_ref, br_ref,
                 xn_ref, tok_ref, idx_ref, gate_ref, rank_ref, hlast_p_ref, hlast_s_ref, cnt_out_ref,
                 ext_p, lvl_p, ext_s, cnt_ref, tri_ref):
    i = pl.program_id(0)
    hist = POOL_HIST_ROWS

    @pl.when(i == 0)
    def _():
        _route_init(cnt_ref, tri_ref)

    @pl.when(i < ntp)
    def _():
        ts, d = xp_ref.shape
        t = lax.rem(i, nt)

        @pl.when(t == 0)
        def _():
            ext_p[0:hist, :] = jnp.zeros((hist, d), F32)
            lvl_p[:, 0:hist, :] = jnp.zeros((lvl_p.shape[0], hist, lvl_p.shape[2]), F32)

        x = xp_ref[...]
        h = _rms(x, gmix_ref[...])
        ext_p[hist:hist + ts, :] = h

        @pl.when(t == nt - 1)
        def _():
            hlast_p_ref[0] = h[ts - hist:, :]

        pos = t * ts + lax.broadcasted_iota(jnp.int32, (ts, 1), 0)
        xn_ref[...] = x + _pool_mix_tiled(h, ext_p, lvl_p, pos, pw_ref, ps_ref)
        ext_p[0:hist, :] = ext_p[ts:ts + hist, :]
        lvl_p[:, 0:hist, :] = lvl_p[:, ts:ts + hist, :]

    @pl.when(i >= ntp)
    def _():
        bb, ts, d = xs_ref.shape
        ext_s[:, 0:hist, :] = hist_ref[...]
        x = xs_ref[...]
        h = _rms(x, gmix_ref[...])
        ext_s[:, hist:hist + ts, :] = h
        hlast_s_ref[...] = h[:, ts - hist:, :]
        pos = pos0_s + lax.broadcasted_iota(jnp.int32, (1, ts, 1), 1)
        xn_ref[...] = x.reshape(bb * ts, d) + _pool_mix(h, ext_s, pos, pw_ref, ps_ref)

    _route_tail(xn_ref[...], gffn_ref, wr_ref, br_ref, cnt_ref, tri_ref, tok_ref, idx_ref, gate_ref, rank_ref)
    cnt_out_ref[...] = cnt_ref[...]


def _pool_layer(x_prompt, x_sample, hist_s, pos0_s, gmix, pw, ps, gffn, wr, br):
    b, s, d = x_prompt.shape
    db, t, _ = x_sample.shape
    n_exp = wr.shape[0]
    r = ROW_TILE
    nt = s // r
    ntp = b * nt
    bb = r // t
    nts = db // bb
    n = b * s + db * t
    hist = POOL_HIST_ROWS
    rows = pl.BlockSpec((r, d), lambda i: (i, 0))
    lanes = pl.BlockSpec((TOP_K, r), lambda i: (0, i))
    sample_blk = lambda i: (jnp.maximum(i - ntp, 0), 0, 0)
    return pl.pallas_call(
        functools.partial(_pool_kernel, ntp, nt, pos0_s),
        grid=(ntp + nts,),
        in_specs=[pl.BlockSpec((r, d), lambda i: (jnp.minimum(i, ntp - 1), 0)),
                  pl.BlockSpec((bb, t, d), sample_blk), pl.BlockSpec((bb, hist, d), sample_blk),
                  _full((1, d)), _full(pw.shape), _full((1, d)), _full((1, d)), _full((n_exp, d)), _full((n_exp, 1))],
        out_specs=[rows, pl.BlockSpec((r, d // 2), lambda i: (i, 0)), lanes, lanes, lanes,
                   pl.BlockSpec((1, hist, d), lambda i: (jnp.minimum(i // nt, b - 1), 0, 0)),
                   pl.BlockSpec((bb, hist, d), sample_blk), _full((n_exp, LANES))],
        out_shape=[jax.ShapeDtypeStruct((n, d), F32), jax.ShapeDtypeStruct((n, d // 2), jnp.uint32),
                   jax.ShapeDtypeStruct((TOP_K, n), jnp.int32), jax.ShapeDtypeStruct((TOP_K, n), F32),
                   jax.ShapeDtypeStruct((TOP_K, n), jnp.int32),
                   jax.ShapeDtypeStruct((b, hist, d), F32), jax.ShapeDtypeStruct((db, hist, d), F32),
                   jax.ShapeDtypeStruct((n_exp, LANES), F32)],
        scratch_shapes=[pltpu.VMEM((hist + r, d), F32),
                        pltpu.VMEM((len(POOL_WINDOWS) - 1, hist + r, d - d // len(POOL_WINDOWS)), F32),
                        pltpu.VMEM((bb, hist + t, d), F32), pltpu.VMEM((n_exp, LANES), F32), pltpu.VMEM((r, r), BF16)],
        compiler_params=_params(("arbitrary",)),
        name="pool_route",
    )(x_prompt.reshape(b * s, d), x_sample, hist_s, gmix, pw, ps, gffn, wr, br)


def _proj_kernel(ntp, op_ref, os_ref, x_ref, wo_ref, gffn_ref, wr_ref, br_ref,
                 xn_ref, tok_ref, idx_ref, gate_ref, rank_ref, cnt_out_ref, cnt_ref, tri_ref):
    i = pl.program_id(0)

    @pl.when(i == 0)
    def _():
        _route_init(cnt_ref, tri_ref)

    o = jnp.where(i < ntp, op_ref[...], os_ref[...])
    xn = x_ref[...] + _dot(o, wo_ref[...])
    xn_ref[...] = xn
    _route_tail(xn, gffn_ref, wr_ref, br_ref, cnt_ref, tri_ref, tok_ref, idx_ref, gate_ref, rank_ref)
    cnt_out_ref[...] = cnt_ref[...]


def _proj_layer(o_p, o_s, x, wo, gffn, wr, br):
    n, d = x.shape
    n_exp = wr.shape[0]
    r = ROW_TILE
    ntp = o_p.shape[0] // r
    rows = pl.BlockSpec((r, d), lambda i: (i, 0))
    lanes = pl.BlockSpec((TOP_K, r), lambda i: (0, i))
    return pl.pallas_call(
        functools.partial(_proj_kernel, ntp),
        grid=(n // r,),
        in_specs=[pl.BlockSpec((r, d), lambda i: (jnp.minimum(i, ntp - 1), 0)),
                  pl.BlockSpec((r, d), lambda i: (jnp.maximum(i - ntp, 0), 0)),
                  rows, _full((d, d)), _full((1, d)), _full((n_exp, d)), _full((n_exp, 1))],
        out_specs=[rows, pl.BlockSpec((r, d // 2), lambda i: (i, 0)), lanes, lanes, lanes, _full((n_exp, LANES))],
        out_shape=[jax.ShapeDtypeStruct((n, d), F32), jax.ShapeDtypeStruct((n, d // 2), jnp.uint32),
                   jax.ShapeDtypeStruct((TOP_K, n), jnp.int32), jax.ShapeDtypeStruct((TOP_K, n), F32),
                   jax.ShapeDtypeStruct((TOP_K, n), jnp.int32), jax.ShapeDtypeStruct((n_exp, LANES), F32)],
        scratch_shapes=[pltpu.VMEM((n_exp, LANES), F32), pltpu.VMEM((r, r), BF16)],
        compiler_params=_params(("arbitrary",)),
        name="proj_route",
    )(o_p, o_s, x, wo, gffn, wr, br)


def _sc_chunks(dest, w):
    n = dest.shape[1]
    return dest.reshape(TOP_K, n // w, w).transpose(1, 0, 2)


def _dispatch(tok, dest, n_buf_rows):
    n, d = tok.shape
    w = SC_CHUNK
    n_workers = SC_CORES * SC_SUBCORES
    per_worker = n // (w * n_workers)
    assert per_worker * w * n_workers == n
    mesh = plsc.VectorSubcoreMesh(core_axis_name="core", subcore_axis_name="subcore",
                                  num_cores=SC_CORES, num_subcores=SC_SUBCORES)

    @functools.partial(
        pl.kernel, mesh=mesh, out_type=jax.ShapeDtypeStruct((n_buf_rows, d), tok.dtype),
        scratch_types=[pltpu.VMEM((TOP_K, w), jnp.int32), pltpu.VMEM((w, d), tok.dtype), pltpu.SemaphoreType.DMA],
        name="moe_dispatch_sc")
    def scatter_rows(tok_hbm, dest_hbm, buf_hbm, idx_v, rows_v, sem):
        worker = lax.axis_index("subcore") * SC_CORES + lax.axis_index("core")

        @pl.loop(0, per_worker)
        def _(c):
            chunk = worker * per_worker + c
            pltpu.sync_copy(dest_hbm.at[chunk], idx_v)
            pltpu.sync_copy(tok_hbm.at[pl.ds(pl.multiple_of(chunk * w, w), w)], rows_v)
            copies = [pltpu.async_copy(rows_v, buf_hbm.at[idx_v.at[k]], sem) for k in range(TOP_K)]
            for cp in copies:
                cp.wait()

    return scatter_rows(tok, _sc_chunks(dest, w))


def _expert_kernel(layer, be_ref, valid_ref, nused_ref, next_ref, slot_ref, x_ref, w1_hbm, b1_ref, w2_hbm, b2_ref,
                   o_ref, w1f_ref, w2f_ref, w1b_ref, w2b_ref, sem):
    i = pl.program_id(0)
    used = valid_ref[i] > 0

    def fetch(expert, slot):
        return (pltpu.make_async_copy(w1_hbm.at[layer, expert], w1f_ref.at[slot], sem.at[0, slot]),
                pltpu.make_async_copy(w2_hbm.at[layer, expert], w2f_ref.at[slot], sem.at[1, slot]))

    @pl.when(used & ((i == 0) | (be_ref[i] != be_ref[jnp.maximum(i - 1, 0)])))
    def _():
        slot = slot_ref[i]

        @pl.when(i == 0)
        def _():
            for cp in fetch(be_ref[i], slot):
                cp.start()

        for cp in fetch(be_ref[i], slot):
            cp.wait()

        @pl.when(next_ref[i] >= 0)
        def _():
            for cp in fetch(next_ref[i], 1 - slot):
                cp.start()

        w1b_ref[...] = w1f_ref[slot].astype(BF16)
        w2b_ref[...] = w2f_ref[slot].astype(BF16)

    @pl.when(used)
    def _():
        f = w2b_ref.shape[0]
        row = lax.broadcasted_iota(jnp.int32, (x_ref.shape[0], 1), 0)
        x = jnp.where(row < valid_ref[i], x_ref[...], jnp.uint32(0))
        xa, xb = _unpack_halves(x)
        half = x.shape[1]
        gu = (_dot(xa.astype(BF16), w1b_ref[:half, :]) + _dot(xb.astype(BF16), w1b_ref[half:, :])) + b1_ref[0]
        gate = jnp.minimum(gu[:, :f], SWIGLU_LIMIT)
        up = jnp.clip(gu[:, f:], -SWIGLU_LIMIT, SWIGLU_LIMIT)
        hid = (up + 1.0) * (gate * jax.nn.sigmoid(gate * SWIGLU_ALPHA))
        o_ref[...] = _pack_halves(_dot(hid.astype(BF16), w2b_ref[...]) + b2_ref[0])

    @pl.when(jnp.logical_not(used))
    def _():
        o_ref[...] = jnp.zeros(o_ref.shape, o_ref.dtype)


def _experts(buf, block_expert, valid, nused, next_expert, slot, layer, w1, b1, w2, b2, tm):
    rows, dw = buf.shape
    _, n_exp, d, f2 = w1.shape
    f = w2.shape[2]
    in_rows = lambda i, be, va, nu, nx, sl: (jnp.minimum(i, nu[0] - 1), 0)
    by_expert = lambda i, be, va, nu, nx, sl: (be[i], 0, 0)
    any_spec = pl.BlockSpec(memory_space=pl.ANY)
    return pl.pallas_call(
        functools.partial(_expert_kernel, layer),
        grid_spec=pltpu.PrefetchScalarGridSpec(
            num_scalar_prefetch=5,
            grid=(rows // tm,),
            in_specs=[pl.BlockSpec((tm, dw), in_rows), any_spec, pl.BlockSpec((1, 1, f2), by_expert),
                      any_spec, pl.BlockSpec((1, 1, d), by_expert)],
            out_specs=pl.BlockSpec((tm, dw), lambda i, be, va, nu, nx, sl: (i, 0)),
            scratch_shapes=[pltpu.VMEM((2, d, f2), F32), pltpu.VMEM((2, f, d), F32),
                            pltpu.VMEM((d, f2), BF16), pltpu.VMEM((f, d), BF16), pltpu.SemaphoreType.DMA((2, 2))]),
        out_shape=jax.ShapeDtypeStruct((rows, dw), jnp.uint32),
        compiler_params=_params(("arbitrary",)),
        name="moe_experts",
    )(block_expert, valid, nused, next_expert, slot, buf, w1, b1.reshape(n_exp, 1, f2), w2, b2.reshape(n_exp, 1, d))


def _gather_rows(ybuf, dest):
    _, d = ybuf.shape
    n = dest.shape[1]
    w = SC_GATHER_CHUNK
    n_workers = SC_CORES * SC_SUBCORES
    per_worker = n // (w * n_workers)
    assert per_worker * w * n_workers == n and TOP_K % 2 == 0
    mesh = plsc.VectorSubcoreMesh(core_axis_name="core", subcore_axis_name="subcore",
                                  num_cores=SC_CORES, num_subcores=SC_SUBCORES)

    @functools.partial(
        pl.kernel, mesh=mesh, out_type=jax.ShapeDtypeStruct((TOP_K, n, d), ybuf.dtype),
        scratch_types=[pltpu.VMEM((TOP_K, w), jnp.int32), pltpu.VMEM((2, w, d), ybuf.dtype),
                       pltpu.SemaphoreType.DMA((2,)), pltpu.SemaphoreType.DMA((2,))],
        name="moe_gather_sc")
    def gather_rows(y_hbm, dest_hbm, out_hbm, idx_v, rows_v, sem_in, sem_out):
        worker = lax.axis_index("subcore") * SC_CORES + lax.axis_index("core")

        @pl.loop(0, per_worker)
        def _(c):
            chunk = worker * per_worker + c
            tokens = pl.ds(pl.multiple_of(chunk * w, w), w)
            pltpu.sync_copy(dest_hbm.at[chunk], idx_v)
            fetch = lambda k: pltpu.async_copy(y_hbm.at[idx_v.at[k]], rows_v.at[k % 2], sem_in.at[k % 2])
            store = lambda k: pltpu.async_copy(rows_v.at[k % 2], out_hbm.at[k, tokens], sem_out.at[k % 2])
            fetches = [fetch(0), fetch(1)]
            stores = []
            for k in range(TOP_K):
                fetches[k].wait()
                stores.append(store(k))
                if k + 2 < TOP_K:
                    stores[k].wait()
                    fetches.append(fetch(k + 2))
            for k in range(TOP_K - 2, TOP_K):
                stores[k].wait()

    return gather_rows(ybuf, _sc_chunks(dest, w))


def _combine_kernel(ntp, rows_ref, gate_ref, x_ref, g_ref, *outs):
    i = pl.program_id(0)
    r = x_ref.shape[0]
    gates = jnp.concatenate([gate_ref[...], jnp.zeros((LANES - TOP_K, r), F32)], axis=0)
    gates_t = jnp.transpose(gates)
    halves = [_unpack_halves(rows_ref[k]) for k in range(TOP_K)]
    f = []
    for side in range(2):
        acc = halves[0][side] * gates_t[:, 0:1]
        for k in range(1, TOP_K):
            acc = acc + halves[k][side] * gates_t[:, k:k + 1]
        f.append(acc)
    xo = x_ref[...] + jnp.concatenate(f, axis=1)
    hn = _rms(xo, g_ref[...])
    if ntp is None:
        xo_ref, hn_ref = outs
        xo_ref[...] = xo
        hn_ref[...] = hn.astype(hn_ref.dtype)
    else:
        hp_ref, hs_ref = outs

        @pl.when(i < ntp)
        def _():
            hp_ref[...] = hn

        @pl.when(i >= ntp)
        def _():
            hs_ref[...] = hn


def _combine(rows4, gates, x, g, n_p):
    n, d = x.shape
    r = COMBINE_TILE
    rows = pl.BlockSpec((r, d), lambda i: (i, 0))
    if n_p is None:
        ntp = None
        out_specs = [rows, rows]
        out_shape = [jax.ShapeDtypeStruct((n, d), F32), jax.ShapeDtypeStruct((n, d), BF16)]
    else:
        ntp = n_p // r
        out_specs = [pl.BlockSpec((r, d), lambda i: (jnp.minimum(i, ntp - 1), 0)),
                     pl.BlockSpec((r, d), lambda i: (jnp.maximum(i - ntp, 0), 0))]
        out_shape = [jax.ShapeDtypeStruct((n_p, d), F32), jax.ShapeDtypeStruct((n - n_p, d), F32)]
    return pl.pallas_call(
        functools.partial(_combine_kernel, ntp),
        grid=(n // r,),
        in_specs=[pl.BlockSpec((TOP_K, r, d // 2), lambda i: (0, i, 0)), pl.BlockSpec((TOP_K, r), lambda i: (0, i)),
                  rows, _full((1, d))],
        out_specs=out_specs,
        out_shape=out_shape,
        compiler_params=_params(("arbitrary",)),
        name="moe_combine",
    )(rows4, gates, x, g)


def _moe(tok, idx, gates, rank, cnt, x, g_next, n_p, layer, w1, b1, w2, b2):
    n, d = tok.shape
    n_exp = w1.shape[1]
    pairs = n * TOP_K
    tm = 128
    for cand in (1024, 512, 256):
        if pairs >= 4 * cand * n_exp and pairs % cand == 0:
            tm = cand
            break
    n_blocks = -(-pairs // tm) + n_exp
    counts = cnt[:, 0].astype(jnp.int32)
    padded = (counts + tm - 1) // tm * tm
    pad_end = jnp.cumsum(padded)
    pad_start = pad_end - padded
    experts = jnp.arange(n_exp, dtype=jnp.int32)
    dest = rank + jnp.sum(jnp.where(idx[:, :, None] == experts, pad_start, 0), axis=-1)
    starts = jnp.arange(n_blocks, dtype=jnp.int32) * tm
    nused = (pad_end[-1:] // tm).astype(jnp.int32)
    block_expert = jnp.minimum(jnp.sum((pad_end[None, :] <= starts[:, None]).astype(jnp.int32), axis=1), n_exp - 1)
    tokens_end = jnp.sum(jnp.where(block_expert[:, None] == experts, pad_start + counts, 0), axis=-1)
    valid = jnp.clip(tokens_end - starts, 0, tm).astype(jnp.int32)
    has_tokens = counts > 0
    later = has_tokens[None, :] & (experts[None, :] > experts[:, None])
    successor = jnp.min(jnp.where(later, experts[None, :], n_exp), axis=1)
    successor = jnp.where(successor == n_exp, -1, successor)
    parity = (jnp.cumsum(has_tokens.astype(jnp.int32)) - 1) % 2
    of_block = lambda table: jnp.sum(jnp.where(block_expert[:, None] == experts, table, 0), axis=-1).astype(jnp.int32)
    buf = _dispatch(tok, dest, n_blocks * tm)
    ybuf = _experts(buf, block_expert, valid, nused, of_block(successor), of_block(parity), layer, w1, b1, w2, b2, tm)
    return _combine(_gather_rows(ybuf, dest), gates, x, g_next, n_p)


def _qkv_kernel(ntp, h_ref, wq_ref, wkt_ref, wvt_ref, q_ref, ktb_ref, vtb_ref, ktf_ref, vtf_ref,
                ksb_ref, vsb_ref, ksf_ref, vsf_ref):
    i = pl.program_id(0)
    r, d = h_ref.shape
    tk = ktb_ref.shape[-1]
    h = h_ref[...]
    q_ref[...] = (_dot(h, wq_ref[...]) * ((d // N_HEADS) ** -0.5)).astype(BF16)

    @pl.when(i < ntp)
    def _():
        for wt_ref, tb_ref, tf_ref in ((wkt_ref, ktb_ref, ktf_ref), (wvt_ref, vtb_ref, vtf_ref)):
            xt = _dot_nt(wt_ref[...], h)
            tf_ref[...] = xt
            for c in range(r // tk):
                tb_ref[0, c] = xt[:, c * tk:(c + 1) * tk].astype(BF16)

    @pl.when(i >= ntp)
    def _():
        for wt_ref, sb_ref, sf_ref in ((wkt_ref, ksb_ref, ksf_ref), (wvt_ref, vsb_ref, vsf_ref)):
            x = _dot_nt(h, wt_ref[...])
            sf_ref[...] = x
            sb_ref[...] = x.astype(BF16)


def _qkv(h, wq, wkt, wvt, n_streams, length):
    n, d = h.shape
    r = QKV_TILE
    tk = ATTN_TK
    n_p = n_streams * length
    nt = length // r
    ntp = n_p // r
    rows = pl.BlockSpec((r, d), lambda i: (i, 0))
    prompt_blk = lambda i: (jnp.minimum(i, ntp - 1) // nt, lax.rem(jnp.minimum(i, ntp - 1), nt))
    t_blocks = pl.BlockSpec((1, r // tk, d, tk), lambda i: prompt_blk(i) + (0, 0))
    t_full = pl.BlockSpec((d, r), prompt_blk)
    sample = pl.BlockSpec((r, d), lambda i: (jnp.maximum(i - ntp, 0), 0))
    w_spec = _full((d, d))
    return pl.pallas_call(
        functools.partial(_qkv_kernel, ntp),
        grid=(n // r,),
        in_specs=[rows, w_spec, w_spec, w_spec],
        out_specs=[rows, t_blocks, t_blocks, t_full, t_full, sample, sample, sample, sample],
        out_shape=[jax.ShapeDtypeStruct((n, d), BF16)]
        + [jax.ShapeDtypeStruct((n_streams, length // tk, d, tk), BF16)] * 2
        + [jax.ShapeDtypeStruct((n_streams * d, length), F32)] * 2
        + [jax.ShapeDtypeStruct((n - n_p, d), BF16)] * 2 + [jax.ShapeDtypeStruct((n - n_p, d), F32)] * 2,
        compiler_params=_params(("arbitrary",)),
        name="qkv",
    )(h, wq, wkt, wvt)


def _attn_setup(q_ref, qm_ref, carry_ref, acc_ref):
    tq, d = q_ref.shape
    lane = lax.broadcasted_iota(jnp.int32, (tq, LANES), 1)
    for p in range(d // LANES):
        qp = q_ref[:, p * LANES:(p + 1) * LANES]
        qm_ref[p, 0:tq, :] = jnp.where(lane < LANES // 2, qp, jnp.zeros_like(qp))
        qm_ref[p, tq:2 * tq, :] = jnp.where(lane >= LANES // 2, qp, jnp.zeros_like(qp))
    carry_ref[...] = jnp.zeros(carry_ref.shape, F32)
    acc_ref[...] = jnp.zeros(acc_ref.shape, F32)


def _suffix_sum_matrix(tk):
    src = lax.rem(lax.broadcasted_iota(jnp.int32, (2 * tk, 2 * tk), 0), tk)
    dst = lax.broadcasted_iota(jnp.int32, (2 * tk, 2 * tk), 1)
    return jnp.where((dst >= tk) | (src > dst), 1.0, 0.0).astype(BF16)


def _attn_block(qm_ref, carry_ref, acc_ref, keys, values, transposed, mask, sums):
    n_pairs, tq2, tk = carry_ref.shape
    tq = tq2 // 2
    lane = lax.broadcasted_iota(jnp.int32, (tq, LANES), 1)
    visible = (lambda x: x) if mask is None else (lambda x: jnp.where(mask, x, 0.0))
    scores = _dot if transposed else _dot_nt
    mix = _dot_nt if transposed else _dot
    zs = [scores(qm_ref[p], keys[p]) for p in range(n_pairs)]
    log_beta, parts = [], []
    for z in zs:
        sp = jnp.maximum(z, 0.0) + jnp.log(1.0 + jnp.exp(-jnp.abs(z)))
        log_keep = visible(-sp)
        hi = log_keep.astype(BF16)
        lo = (log_keep - hi.astype(F32)).astype(BF16)
        parts.append(jnp.concatenate([hi, lo], axis=1))
        log_beta.append(z - sp)
    sums_out = [_dot(part, sums) for part in parts]
    weights = []
    top = jnp.full((tq2, tk), -jnp.inf, F32)
    for p in range(n_pairs):
        carry = carry_ref[p]
        a = visible(jnp.exp(log_beta[p] + sums_out[p][:, :tk] + carry))
        weights.append(a.astype(BF16))
        carry = carry + sums_out[p][:, tk:]
        carry_ref[p] = carry
        top = jnp.maximum(top, carry)
    for p in range(n_pairs):
        out = mix(weights[p], values[p])
        acc_ref[:, p * LANES:(p + 1) * LANES] += jnp.where(lane < LANES // 2, out[:tq], out[tq:])
    return jnp.max(top)


def _keep_sweeping(state):
    j, top = state
    return (j >= 0) & (top > -ATTN_EXIT)


def _attn_prompt_kernel(q_ref, kt_ref, vt_ref, o_ref, qm_ref, carry_ref, acc_ref):
    qi = pl.program_id(1)
    tq, d = q_ref.shape
    tk = kt_ref.shape[-1]
    n_pairs = d // LANES
    _attn_setup(q_ref, qm_ref, carry_ref, acc_ref)
    row_pos = qi * tq + lax.rem(lax.broadcasted_iota(jnp.int32, (2 * tq, tk), 0), tq)
    col = lax.broadcasted_iota(jnp.int32, (2 * tq, tk), 1)
    sums = _suffix_sum_matrix(tk)

    def block(j, mask):
        keys = [kt_ref[0, j, p * LANES:(p + 1) * LANES, :] for p in range(n_pairs)]
        values = [vt_ref[0, j, p * LANES:(p + 1) * LANES, :] for p in range(n_pairs)]
        return _attn_block(qm_ref, carry_ref, acc_ref, keys, values, True, mask, sums)

    j0 = ((qi + 1) * tq - 2) // tk
    top = block(j0, (j0 * tk + col) < row_pos)
    lax.while_loop(_keep_sweeping, lambda state: (state[0] - 1, block(state[0], None)), (j0 - 1, top))
    o_ref[...] = acc_ref[...].astype(o_ref.dtype)


def _attention_prompt(q, kt, vt):
    n_streams, n_kblocks, d, tk = kt.shape
    length = n_kblocks * tk
    tq = min(length, ATTN_TQ)
    assert tk % tq == 0
    nq = length // tq
    kv = pl.BlockSpec((1, n_kblocks, d, tk), lambda b, i: (b, 0, 0, 0))
    rows = pl.BlockSpec((tq, d), lambda b, i: (b * nq + i, 0))
    return pl.pallas_call(
        _attn_prompt_kernel,
        grid=(n_streams, nq),
        in_specs=[rows, kv, kv],
        out_specs=rows,
        out_shape=jax.ShapeDtypeStruct((n_streams * length, d), BF16),
        scratch_shapes=[pltpu.VMEM((d // LANES, 2 * tq, LANES), BF16), pltpu.VMEM((d // LANES, 2 * tq, tk), F32),
                        pltpu.VMEM((tq, d), F32)],
        compiler_params=_params(("arbitrary", "arbitrary")),
        name="stick_breaking_prompt",
    )(q, kt, vt)


def _attn_sample_kernel(q_ref, kn_ref, vn_ref, ck_hbm, cv_hbm, o_ref, qm_ref, carry_ref, acc_ref, kbuf, vbuf, sem):
    b = pl.program_id(0)
    t, d = q_ref.shape
    tk = kbuf.shape[-1]
    n_pairs = d // LANES
    n_cache_blocks = ck_hbm.shape[1] // tk

    def fetch(j, slot):
        src = (pl.ds(pl.multiple_of(b * d, d), d), pl.ds(pl.multiple_of(j * tk, tk), tk))
        return (pltpu.make_async_copy(ck_hbm.at[src], kbuf.at[slot], sem.at[0, slot]),
                pltpu.make_async_copy(cv_hbm.at[src], vbuf.at[slot], sem.at[1, slot]))

    for cp in fetch(n_cache_blocks - 1, (n_cache_blocks - 1) % 2):
        cp.start()
    _attn_setup(q_ref, qm_ref, carry_ref, acc_ref)
    row = lax.rem(lax.broadcasted_iota(jnp.int32, (2 * t, tk), 0), t)
    col = lax.broadcasted_iota(jnp.int32, (2 * t, tk), 1)
    sums = _suffix_sum_matrix(tk)
    pad = jnp.zeros((tk - t, LANES), BF16)
    keys = [jnp.concatenate([kn_ref[:, p * LANES:(p + 1) * LANES], pad], axis=0) for p in range(n_pairs)]
    values = [jnp.concatenate([vn_ref[:, p * LANES:(p + 1) * LANES], pad], axis=0) for p in range(n_pairs)]
    top = _attn_block(qm_ref, carry_ref, acc_ref, keys, values, False, col < row, sums)

    def body(state):
        j, _ = state
        slot = lax.rem(j, 2)
        for cp in fetch(j, slot):
            cp.wait()

        @pl.when(j > 0)
        def _():
            for cp in fetch(j - 1, 1 - slot):
                cp.start()

        keys = [kbuf[slot, p * LANES:(p + 1) * LANES, :].astype(BF16) for p in range(n_pairs)]
        values = [vbuf[slot, p * LANES:(p + 1) * LANES, :].astype(BF16) for p in range(n_pairs)]
        return j - 1, _attn_block(qm_ref, carry_ref, acc_ref, keys, values, True, None, sums)

    j_end, _ = lax.while_loop(_keep_sweeping, body, (jnp.int32(n_cache_blocks - 1), top))

    @pl.when(j_end >= 0)
    def _():
        for cp in fetch(j_end, lax.rem(j_end, 2)):
            cp.wait()

    o_ref[...] = acc_ref[...].astype(o_ref.dtype)


def _attention_sample(q, k_new, v_new, cache_kt, cache_vt, t, q_row_off):
    d = q.shape[1]
    n_streams = k_new.shape[0] // t
    tk = ATTN_TK
    assert cache_kt.shape[1] % tk == 0 and t <= tk
    any_spec = pl.BlockSpec(memory_space=pl.ANY)
    new = pl.BlockSpec((t, d), lambda b: (b, 0))
    return pl.pallas_call(
        _attn_sample_kernel,
        grid=(n_streams,),
        in_specs=[pl.BlockSpec((t, d), lambda b: (q_row_off // t + b, 0)), new, new, any_spec, any_spec],
        out_specs=new,
        out_shape=jax.ShapeDtypeStruct((n_streams * t, d), BF16),
        scratch_shapes=[pltpu.VMEM((d // LANES, 2 * t, LANES), BF16), pltpu.VMEM((d // LANES, 2 * t, tk), F32),
                        pltpu.VMEM((t, d), F32), pltpu.VMEM((2, d, tk), F32), pltpu.VMEM((2, d, tk), F32),
                        pltpu.SemaphoreType.DMA((2, 2))],
        compiler_params=_params(("arbitrary",)),
        name="stick_breaking_sample",
    )(q, k_new, v_new, cache_kt, cache_vt)


def kernel(x_prompt, x_sample, state_pool, cache_k, cache_v, norm_mix, norm_ffn, pool_w, pool_scale, w_qkv, w_o, router_w, router_b, moe_w1, moe_b1, moe_w2, moe_b2, final_norm):
    b, s, d = x_prompt.shape
    db, t, _ = x_sample.shape
    past = cache_k.shape[2]
    n_exp = router_w.shape[2]
    hd = d // N_HEADS
    hist = POOL_HIST_ROWS
    n_p, n_s = b * s, db * t
    assert t >= hist and ROW_TILE % t == 0 and s % ROW_TILE == 0 and n_s % ROW_TILE == 0 and ROW_TILE % QKV_TILE == 0
    row = lambda a: a.reshape(1, -1)
    wr = [router_w[i].T for i in range(2)]
    br = [router_b[i].reshape(n_exp, 1) for i in range(2)]

    hist_s = jnp.concatenate([jnp.zeros((db, 1, d), F32), state_pool[0]], axis=1)
    xn, tok, idx, gates, rank, hlast_p, hlast_s, cnt = _pool_layer(
        x_prompt, x_sample, hist_s, past, row(norm_mix[0]), pool_w[0].astype(BF16), row(pool_scale[0]),
        row(norm_ffn[0]), wr[0], br[0])
    x1, h1 = _moe(tok, idx, gates, rank, cnt, xn, row(norm_mix[1]), None, 0, moe_w1, moe_b1[0], moe_w2, moe_b2[0])

    wqkv = w_qkv[0].astype(BF16)
    q, ktb, vtb, ktf, vtf, ksb, vsb, ksf, vsf = _qkv(h1, wqkv[:, :d], wqkv[:, d:2 * d].T, wqkv[:, 2 * d:].T, b, s)
    o_p = _attention_prompt(q, ktb, vtb)
    transposed = lambda cache: cache.transpose(0, 1, 3, 4, 2).reshape(db * d, past)
    o_s = _attention_sample(q, ksb, vsb, transposed(cache_k), transposed(cache_v), t, n_p)
    xn, tok, idx, gates, rank, cnt = _proj_layer(o_p, o_s, x1, w_o[0].astype(BF16), row(norm_ffn[1]), wr[1], br[1])
    y_p, y_s = _moe(tok, idx, gates, rank, cnt, xn, row(final_norm), n_p, 1, moe_w1, moe_b1[1], moe_w2, moe_b2[1])

    frames_major = lambda xt: xt.reshape(1, b, N_HEADS, hd, s).transpose(0, 1, 4, 2, 3)
    heads = lambda a: a.reshape(1, db, t, N_HEADS, hd)
    return (y_p.reshape(b, s, d), y_s.reshape(db, t, d), hlast_p[None, :, 1:, :],
            frames_major(ktf), frames_major(vtf), hlast_s[None, :, 1:, :], heads(ksf), heads(vsf))
```

```python
import functools

import jax
import jax.numpy as jnp
from jax import lax
from jax.experimental import pallas as pl
from jax.experimental.pallas import tpu as pltpu
from jax.experimental.pallas import tpu_sc as plsc

EPS = 1e-5
POOL_WINDOWS = (2, 4, 8, 16)
POOL_HIST_ROWS = 16
N_HEADS = 16
TOP_K = 4
SWIGLU_LIMIT = 7.0
SWIGLU_ALPHA = 1.702
LANES = 128
ROW_TILE = 512
QKV_TILE = 512
ATTN_TQ = 128
ATTN_TK = 128
ATTN_EXIT = 88.0
VMEM_LIMIT = 56 * 1024 * 1024
SC_CORES = 2
SC_SUBCORES = 16
SC_CHUNK = 96
SC_GATHER_CHUNK = 48
COMBINE_TILE = 512

F32 = jnp.float32
BF16 = jnp.bfloat16


def _rms(x, g):
    ms = jnp.mean(x * x, axis=-1, keepdims=True)
    return x * lax.rsqrt(ms + EPS) * g


def _dot(a, b):
    return jnp.dot(a, b, preferred_element_type=F32)


def _dot_nt(a, b, precision=None):
    return lax.dot_general(a, b, (((1,), (1,)), ((), ())), preferred_element_type=F32, precision=precision)


def _pack_halves(x):
    c = x.shape[1] // 2
    bits = lax.bitcast_convert_type(x.astype(BF16).astype(F32), jnp.uint32)
    return bits[:, :c] | (bits[:, c:] >> 16)


def _unpack_halves(w):
    return (lax.bitcast_convert_type(w & jnp.uint32(0xFFFF0000), F32), lax.bitcast_convert_type(w << 16, F32))


def _params(semantics):
    return pltpu.CompilerParams(dimension_semantics=semantics, vmem_limit_bytes=VMEM_LIMIT)


def _full(shape):
    return pl.BlockSpec(shape, lambda i, *_: (0,) * len(shape))


def _route_init(cnt_ref, tri_ref):
    r = tri_ref.shape[0]
    cnt_ref[...] = jnp.zeros(cnt_ref.shape, F32)
    tri_ref[...] = jnp.where(lax.broadcasted_iota(jnp.int32, (r, r), 0) < lax.broadcasted_iota(jnp.int32, (r, r), 1),
                             1.0, 0.0).astype(BF16)


def _route_tail(xn, gffn_ref, wr_ref, br_ref, cnt_ref, tri_ref, tok_ref, idx_ref, gate_ref, rank_ref):
    n_exp = wr_ref.shape[0]
    tok = _rms(xn, gffn_ref[...])
    tok_ref[...] = _pack_halves(tok)
    logits = _dot_nt(wr_ref[...], tok, precision=lax.Precision.HIGHEST) + br_ref[...]
    eidx = lax.broadcasted_iota(jnp.int32, logits.shape, 0).astype(F32)
    vals, idxs = [], []
    l = logits
    for _ in range(TOP_K):
        m = jnp.max(l, axis=0, keepdims=True)
        i = jnp.min(jnp.where(l == m, eidx, float(n_exp)), axis=0, keepdims=True)
        vals.append(m)
        idxs.append(i)
        l = jnp.where(eidx == i, -jnp.inf, l)
    es = [jnp.exp(v - vals[0]) for v in vals]
    den = es[0]
    for e in es[1:]:
        den = den + e
    gate_ref[...] = jnp.concatenate([e / den for e in es], axis=0)
    idx_ref[...] = jnp.concatenate(idxs, axis=0).astype(jnp.int32)
    member = jnp.zeros(logits.shape, F32)
    for i in idxs:
        member = member + jnp.where(eidx == i, 1.0, 0.0)
    before = _dot(member.astype(BF16), tri_ref[...]) + cnt_ref[:, :1]
    ranks = [jnp.sum(jnp.where(eidx == i, before, 0.0), axis=0, keepdims=True) for i in idxs]
    rank_ref[...] = jnp.concatenate(ranks, axis=0).astype(jnp.int32)
    cnt_ref[...] = cnt_ref[...] + jnp.sum(member, axis=1, keepdims=True)


def _pool_mix(h, ext_ref, pos, pw_ref, ps_ref):
    ts, d = h.shape[-2:]
    hist = POOL_HIST_ROWS
    group = d // len(POOL_WINDOWS)
    pre = (slice(None),) * (h.ndim - 2)
    ys = []
    for g, win in enumerate(POOL_WINDOWS):
        cols = slice(g * group, (g + 1) * group)
        hg = h[pre + (slice(None), cols)]
        acc = hg
        for j in range(1, win):
            acc = acc + ext_ref[pre + (slice(hist - j, hist - j + ts), cols)]
        cnt = jnp.minimum(pos + 1, win).astype(F32)
        dg = acc / cnt - hg
        ys.append(_dot(dg.reshape(-1, group).astype(BF16), pw_ref[g]))
    return jnp.concatenate(ys, axis=-1) * ps_ref[...]


def _pool_mix_tiled(h, ext_ref, lvl_ref, pos, pw_ref, ps_ref):
    ts, d = h.shape
    hist = POOL_HIST_ROWS
    group = d // len(POOL_WINDOWS)
    prev_ref, prev_col0 = ext_ref, 0
    ys = []
    for g, win in enumerate(POOL_WINDOWS):
        assert win == 2 * (POOL_WINDOWS[g - 1] if g else 1)
        col0 = g * group
        cols = slice(col0 - prev_col0, d - prev_col0)
        level = prev_ref[hist:hist + ts, cols] + prev_ref[hist - win // 2:hist - win // 2 + ts, cols]
        if g + 1 < len(POOL_WINDOWS):
            lvl_ref[g, hist:hist + ts, 0:d - col0 - group] = level[:, group:]
            prev_ref, prev_col0 = lvl_ref.at[g], col0 + group
        cnt = jnp.minimum(pos + 1, win).astype(F32)
        dg = level[:, :group] / cnt - h[:, col0:col0 + group]
        ys.append(_dot(dg.astype(BF16), pw_ref[g]))
    return jnp.concatenate(ys, axis=-1) * ps_ref[...]


def _pool_kernel(ntp, nt, pos0_s, xp_ref, xs_ref, hist_ref, gmix_ref, pw_ref, ps_ref, gffn_ref, wr_ref, br_ref,
                 xn_ref, tok_ref, idx_ref, gate_ref, rank_ref, hlast_p_ref, hlast_s_ref, cnt_out_ref,
                 ext_p, lvl_p, ext_s, cnt_ref, tri_ref):
    i = pl.program_id(0)
    hist = POOL_HIST_ROWS

    @pl.when(i == 0)
    def _():
        _route_init(cnt_ref, tri_ref)

    @pl.when(i < ntp)
    def _():
        ts, d = xp_ref.shape
        t = lax.rem(i, nt)

        @pl.when(t == 0)
        def _():
            ext_p[0:hist, :] = jnp.zeros((hist, d), F32)
            lvl_p[:, 0:hist, :] = jnp.zeros((lvl_p.shape[0], hist, lvl_p.shape[2]), F32)

        x = xp_ref[...]
        h = _rms(x, gmix_ref[...])
        ext_p[hist:hist + ts, :] = h

        @pl.when(t == nt - 1)
        def _():
            hlast_p_ref[0] = h[ts - hist:, :]

        pos = t * ts + lax.broadcasted_iota(jnp.int32, (ts, 1), 0)
        xn_ref[...] = x + _pool_mix_tiled(h, ext_p, lvl_p, pos, pw_ref, ps_ref)
        ext_p[0:hist, :] = ext_p[ts:ts + hist, :]
        lvl_p[:, 0:hist, :] = lvl_p[:, ts:ts + hist, :]

    @pl.when(i >= ntp)
    def _():
        bb, ts, d = xs_ref.shape
        ext_s[:, 0:hist, :] = hist_ref[...]
        x = xs_ref[...]
        h = _rms(x, gmix_ref[...])
        ext_s[:, hist:hist + ts, :] = h
        hlast_s_ref[...] = h[:, ts - hist:, :]
        pos = pos0_s + lax.broadcasted_iota(jnp.int32, (1, ts, 1), 1)
        xn_ref[...] = x.reshape(bb * ts, d) + _pool_mix(h, ext_s, pos, pw_ref, ps_ref)

    _route_tail(xn_ref[...], gffn_ref, wr_ref, br_ref, cnt_ref, tri_ref, tok_ref, idx_ref, gate_ref, rank_ref)
    cnt_out_ref[...] = cnt_ref[...]


def _pool_layer(x_prompt, x_sample, hist_s, pos0_s, gmix, pw, ps, gffn, wr, br):
    b, s, d = x_prompt.shape
    db, t, _ = x_sample.shape
    n_exp = wr.shape[0]
    r = ROW_TILE
    nt = s // r
    ntp = b * nt
    bb = r // t
    nts = db // bb
    n = b * s + db * t
    hist = POOL_HIST_ROWS
    rows = pl.BlockSpec((r, d), lambda i: (i, 0))
    lanes = pl.BlockSpec((TOP_K, r), lambda i: (0, i))
    sample_blk = lambda i: (jnp.maximum(i - ntp, 0), 0, 0)
    return pl.pallas_call(
        functools.partial(_pool_kernel, ntp, nt, pos0_s),
        grid=(ntp + nts,),
        in_specs=[pl.BlockSpec((r, d), lambda i: (jnp.minimum(i, ntp - 1), 0)),
                  pl.BlockSpec((bb, t, d), sample_blk), pl.BlockSpec((bb, hist, d), sample_blk),
                  _full((1, d)), _full(pw.shape), _full((1, d)), _full((1, d)), _full((n_exp, d)), _full((n_exp, 1))],
        out_specs=[rows, pl.BlockSpec((r, d // 2), lambda i: (i, 0)), lanes, lanes, lanes,
                   pl.BlockSpec((1, hist, d), lambda i: (jnp.minimum(i // nt, b - 1), 0, 0)),
                   pl.BlockSpec((bb, hist, d), sample_blk), _full((n_exp, LANES))],
        out_shape=[jax.ShapeDtypeStruct((n, d), F32), jax.ShapeDtypeStruct((n, d // 2), jnp.uint32),
                   jax.ShapeDtypeStruct((TOP_K, n), jnp.int32), jax.ShapeDtypeStruct((TOP_K, n), F32),
                   jax.ShapeDtypeStruct((TOP_K, n), jnp.int32),
                   jax.ShapeDtypeStruct((b, hist, d), F32), jax.ShapeDtypeStruct((db, hist, d), F32),
                   jax.ShapeDtypeStruct((n_exp, LANES), F32)],
        scratch_shapes=[pltpu.VMEM((hist + r, d), F32),
                        pltpu.VMEM((len(POOL_WINDOWS) - 1, hist + r, d - d // len(POOL_WINDOWS)), F32),
                        pltpu.VMEM((bb, hist + t, d), F32), pltpu.VMEM((n_exp, LANES), F32), pltpu.VMEM((r, r), BF16)],
        compiler_params=_params(("arbitrary",)),
        name="pool_route",
    )(x_prompt.reshape(b * s, d), x_sample, hist_s, gmix, pw, ps, gffn, wr, br)


def _proj_kernel(ntp, op_ref, os_ref, x_ref, wo_ref, gffn_ref, wr_ref, br_ref,
                 xn_ref, tok_ref, idx_ref, gate_ref, rank_ref, cnt_out_ref, cnt_ref, tri_ref):
    i = pl.program_id(0)

    @pl.when(i == 0)
    def _():
        _route_init(cnt_ref, tri_ref)

    o = jnp.where(i < ntp, op_ref[...], os_ref[...])
    xn = x_ref[...] + _dot(o, wo_ref[...])
    xn_ref[...] = xn
    _route_tail(xn, gffn_ref, wr_ref, br_ref, cnt_ref, tri_ref, tok_ref, idx_ref, gate_ref, rank_ref)
    cnt_out_ref[...] = cnt_ref[...]


def _proj_layer(o_p, o_s, x, wo, gffn, wr, br):
    n, d = x.shape
    n_exp = wr.shape[0]
    r = ROW_TILE
    ntp = o_p.shape[0] // r
    rows = pl.BlockSpec((r, d), lambda i: (i, 0))
    lanes = pl.BlockSpec((TOP_K, r), lambda i: (0, i))
    return pl.pallas_call(
        functools.partial(_proj_kernel, ntp),
        grid=(n // r,),
        in_specs=[pl.BlockSpec((r, d), lambda i: (jnp.minimum(i, ntp - 1), 0)),
                  pl.BlockSpec((r, d), lambda i: (jnp.maximum(i - ntp, 0), 0)),
                  rows, _full((d, d)), _full((1, d)), _full((n_exp, d)), _full((n_exp, 1))],
        out_specs=[rows, pl.BlockSpec((r, d // 2), lambda i: (i, 0)), lanes, lanes, lanes, _full((n_exp, LANES))],
        out_shape=[jax.ShapeDtypeStruct((n, d), F32), jax.ShapeDtypeStruct((n, d // 2), jnp.uint32),
                   jax.ShapeDtypeStruct((TOP_K, n), jnp.int32), jax.ShapeDtypeStruct((TOP_K, n), F32),
                   jax.ShapeDtypeStruct((TOP_K, n), jnp.int32), jax.ShapeDtypeStruct((n_exp, LANES), F32)],
        scratch_shapes=[pltpu.VMEM((n_exp, LANES), F32), pltpu.VMEM((r, r), BF16)],
        compiler_params=_params(("arbitrary",)),
        name="proj_route",
    )(o_p, o_s, x, wo, gffn, wr, br)


def _sc_chunks(dest, w):
    n = dest.shape[1]
    return dest.reshape(TOP_K, n // w, w).transpose(1, 0, 2)


def _dispatch(tok, dest, n_buf_rows):
    n, d = tok.shape
    w = SC_CHUNK
    n_workers = SC_CORES * SC_SUBCORES
    per_worker = n // (w * n_workers)
    assert per_worker * w * n_workers == n
    mesh = plsc.VectorSubcoreMesh(core_axis_name="core", subcore_axis_name="subcore",
                                  num_cores=SC_CORES, num_subcores=SC_SUBCORES)

    @functools.partial(
        pl.kernel, mesh=mesh, out_type=jax.ShapeDtypeStruct((n_buf_rows, d), tok.dtype),
        scratch_types=[pltpu.VMEM((TOP_K, w), jnp.int32), pltpu.VMEM((w, d), tok.dtype), pltpu.SemaphoreType.DMA],
        name="moe_dispatch_sc")
    def scatter_rows(tok_hbm, dest_hbm, buf_hbm, idx_v, rows_v, sem):
        worker = lax.axis_index("subcore") * SC_CORES + lax.axis_index("core")

        @pl.loop(0, per_worker)
        def _(c):
            chunk = worker * per_worker + c
            pltpu.sync_copy(dest_hbm.at[chunk], idx_v)
            pltpu.sync_copy(tok_hbm.at[pl.ds(pl.multiple_of(chunk * w, w), w)], rows_v)
            copies = [pltpu.async_copy(rows_v, buf_hbm.at[idx_v.at[k]], sem) for k in range(TOP_K)]
            for cp in copies:
                cp.wait()

    return scatter_rows(tok, _sc_chunks(dest, w))


def _expert_kernel(layer, be_ref, valid_ref, nused_ref, next_ref, slot_ref, x_ref, w1_hbm, b1_ref, w2_hbm, b2_ref,
                   o_ref, w1f_ref, w2f_ref, w1b_ref, w2b_ref, sem):
    i = pl.program_id(0)
    used = valid_ref[i] > 0

    def fetch(expert, slot):
        return (pltpu.make_async_copy(w1_hbm.at[layer, expert], w1f_ref.at[slot], sem.at[0, slot]),
                pltpu.make_async_copy(w2_hbm.at[layer, expert], w2f_ref.at[slot], sem.at[1, slot]))

    @pl.when(used & ((i == 0) | (be_ref[i] != be_ref[jnp.maximum(i - 1, 0)])))
    def _():
        slot = slot_ref[i]

        @pl.when(i == 0)
        def _():
            for cp in fetch(be_ref[i], slot):
                cp.start()

        for cp in fetch(be_ref[i], slot):
            cp.wait()

        @pl.when(next_ref[i] >= 0)
        def _():
            for cp in fetch(next_ref[i], 1 - slot):
                cp.start()

        w1b_ref[...] = w1f_ref[slot].astype(BF16)
        w2b_ref[...] = w2f_ref[slot].astype(BF16)

    @pl.when(used)
    def _():
        f = w2b_ref.shape[0]
        row = lax.broadcasted_iota(jnp.int32, (x_ref.shape[0], 1), 0)
        x = jnp.where(row < valid_ref[i], x_ref[...], jnp.uint32(0))
        xa, xb = _unpack_halves(x)
        gu = _dot(jnp.concatenate([xa.astype(BF16), xb.astype(BF16)], axis=1), w1b_ref[...]) + b1_ref[0]
        gate = jnp.minimum(gu[:, :f], SWIGLU_LIMIT)
        up = jnp.clip(gu[:, f:], -SWIGLU_LIMIT, SWIGLU_LIMIT)
        hid = (up + 1.0) * (gate * jax.nn.sigmoid(gate * SWIGLU_ALPHA))
        o_ref[...] = _pack_halves(_dot(hid.astype(BF16), w2b_ref[...]) + b2_ref[0])

    @pl.when(jnp.logical_not(used))
    def _():
        o_ref[...] = jnp.zeros(o_ref.shape, o_ref.dtype)


def _experts(buf, block_expert, valid, nused, next_expert, slot, layer, w1, b1, w2, b2, tm):
    rows, dw = buf.shape
    _, n_exp, d, f2 = w1.shape
    f = w2.shape[2]
    in_rows = lambda i, be, va, nu, nx, sl: (jnp.minimum(i, nu[0] - 1), 0)
    by_expert = lambda i, be, va, nu, nx, sl: (be[i], 0, 0)
    any_spec = pl.BlockSpec(memory_space=pl.ANY)
    return pl.pallas_call(
        functools.partial(_expert_kernel, layer),
        grid_spec=pltpu.PrefetchScalarGridSpec(
            num_scalar_prefetch=5,
            grid=(rows // tm,),
            in_specs=[pl.BlockSpec((tm, dw), in_rows), any_spec, pl.BlockSpec((1, 1, f2), by_expert),
                      any_spec, pl.BlockSpec((1, 1, d), by_expert)],
            out_specs=pl.BlockSpec((tm, dw), lambda i, be, va, nu, nx, sl: (i, 0)),
            scratch_shapes=[pltpu.VMEM((2, d, f2), F32), pltpu.VMEM((2, f, d), F32),
                            pltpu.VMEM((d, f2), BF16), pltpu.VMEM((f, d), BF16), pltpu.SemaphoreType.DMA((2, 2))]),
        out_shape=jax.ShapeDtypeStruct((rows, dw), jnp.uint32),
        compiler_params=_params(("arbitrary",)),
        name="moe_experts",
    )(block_expert, valid, nused, next_expert, slot, buf, w1, b1.reshape(n_exp, 1, f2), w2, b2.reshape(n_exp, 1, d))


def _gather_rows(ybuf, dest):
    _, d = ybuf.shape
    n = dest.shape[1]
    w = SC_GATHER_CHUNK
    n_workers = SC_CORES * SC_SUBCORES
    per_worker = n // (w * n_workers)
    assert per_worker * w * n_workers == n and TOP_K % 2 == 0
    mesh = plsc.VectorSubcoreMesh(core_axis_name="core", subcore_axis_name="subcore",
                                  num_cores=SC_CORES, num_subcores=SC_SUBCORES)

    @functools.partial(
        pl.kernel, mesh=mesh, out_type=jax.ShapeDtypeStruct((TOP_K, n, d), ybuf.dtype),
        scratch_types=[pltpu.VMEM((TOP_K, w), jnp.int32), pltpu.VMEM((2, w, d), ybuf.dtype),
                       pltpu.SemaphoreType.DMA((2,)), pltpu.SemaphoreType.DMA((2,))],
        name="moe_gather_sc")
    def gather_rows(y_hbm, dest_hbm, out_hbm, idx_v, rows_v, sem_in, sem_out):
        worker = lax.axis_index("subcore") * SC_CORES + lax.axis_index("core")

        @pl.loop(0, per_worker)
        def _(c):
            chunk = worker * per_worker + c
            tokens = pl.ds(pl.multiple_of(chunk * w, w), w)
            pltpu.sync_copy(dest_hbm.at[chunk], idx_v)
            fetch = lambda k: pltpu.async_copy(y_hbm.at[idx_v.at[k]], rows_v.at[k % 2], sem_in.at[k % 2])
            store = lambda k: pltpu.async_copy(rows_v.at[k % 2], out_hbm.at[k, tokens], sem_out.at[k % 2])
            fetches = [fetch(0), fetch(1)]
            stores = []
            for k in range(TOP_K):
                fetches[k].wait()
                stores.append(store(k))
                if k + 2 < TOP_K:
                    stores[k].wait()
                    fetches.append(fetch(k + 2))
            for k in range(TOP_K - 2, TOP_K):
                stores[k].wait()

    return gather_rows(ybuf, _sc_chunks(dest, w))


def _combine_kernel(ntp, rows_ref, gate_ref, x_ref, g_ref, *outs):
    i = pl.program_id(0)
    r = x_ref.shape[0]
    gates = jnp.concatenate([gate_ref[...], jnp.zeros((LANES - TOP_K, r), F32)], axis=0)
    gates_t = jnp.transpose(gates)
    halves = [_unpack_halves(rows_ref[k]) for k in range(TOP_K)]
    f = []
    for side in range(2):
        acc = halves[0][side] * gates_t[:, 0:1]
        for k in range(1, TOP_K):
            acc = acc + halves[k][side] * gates_t[:, k:k + 1]
        f.append(acc)
    xo = x_ref[...] + jnp.concatenate(f, axis=1)
    hn = _rms(xo, g_ref[...])
    if ntp is None:
        xo_ref, hn_ref = outs
        xo_ref[...] = xo
        hn_ref[...] = hn.astype(hn_ref.dtype)
    else:
        hp_ref, hs_ref = outs

        @pl.when(i < ntp)
        def _():
            hp_ref[...] = hn

        @pl.when(i >= ntp)
        def _():
            hs_ref[...] = hn


def _combine(rows4, gates, x, g, n_p):
    n, d = x.shape
    r = COMBINE_TILE
    rows = pl.BlockSpec((r, d), lambda i: (i, 0))
    if n_p is None:
        ntp = None
        out_specs = [rows, rows]
        out_shape = [jax.ShapeDtypeStruct((n, d), F32), jax.ShapeDtypeStruct((n, d), BF16)]
    else:
        ntp = n_p // r
        out_specs = [pl.BlockSpec((r, d), lambda i: (jnp.minimum(i, ntp - 1), 0)),
                     pl.BlockSpec((r, d), lambda i: (jnp.maximum(i - ntp, 0), 0))]
        out_shape = [jax.ShapeDtypeStruct((n_p, d), F32), jax.ShapeDtypeStruct((n - n_p, d), F32)]
    return pl.pallas_call(
        functools.partial(_combine_kernel, ntp),
        grid=(n // r,),
        in_specs=[pl.BlockSpec((TOP_K, r, d // 2), lambda i: (0, i, 0)), pl.BlockSpec((TOP_K, r), lambda i: (0, i)),
                  rows, _full((1, d))],
        out_specs=out_specs,
        out_shape=out_shape,
        compiler_params=_params(("arbitrary",)),
        name="moe_combine",
    )(rows4, gates, x, g)


def _moe(tok, idx, gates, rank, cnt, x, g_next, n_p, layer, w1, b1, w2, b2):
    n, d = tok.shape
    n_exp = w1.shape[1]
    pairs = n * TOP_K
    tm = 128
    for cand in (512, 256):
        if pairs >= 4 * cand * n_exp and pairs % cand == 0:
            tm = cand
            break
    n_blocks = -(-pairs // tm) + n_exp
    counts = cnt[:, 0].astype(jnp.int32)
    padded = (counts + tm - 1) // tm * tm
    pad_end = jnp.cumsum(padded)
    pad_start = pad_end - padded
    experts = jnp.arange(n_exp, dtype=jnp.int32)
    dest = rank + jnp.sum(jnp.where(idx[:, :, None] == experts, pad_start, 0), axis=-1)
    starts = jnp.arange(n_blocks, dtype=jnp.int32) * tm
    nused = (pad_end[-1:] // tm).astype(jnp.int32)
    block_expert = jnp.minimum(jnp.sum((pad_end[None, :] <= starts[:, None]).astype(jnp.int32), axis=1), n_exp - 1)
    tokens_end = jnp.sum(jnp.where(block_expert[:, None] == experts, pad_start + counts, 0), axis=-1)
    valid = jnp.clip(tokens_end - starts, 0, tm).astype(jnp.int32)
    has_tokens = counts > 0
    later = has_tokens[None, :] & (experts[None, :] > experts[:, None])
    successor = jnp.min(jnp.where(later, experts[None, :], n_exp), axis=1)
    successor = jnp.where(successor == n_exp, -1, successor)
    parity = (jnp.cumsum(has_tokens.astype(jnp.int32)) - 1) % 2
    of_block = lambda table: jnp.sum(jnp.where(block_expert[:, None] == experts, table, 0), axis=-1).astype(jnp.int32)
    buf = _dispatch(tok, dest, n_blocks * tm)
    ybuf = _experts(buf, block_expert, valid, nused, of_block(successor), of_block(parity), layer, w1, b1, w2, b2, tm)
    return _combine(_gather_rows(ybuf, dest), gates, x, g_next, n_p)


def _qkv_kernel(ntp, h_ref, wq_ref, wkt_ref, wvt_ref, q_ref, ktb_ref, vtb_ref, ktf_ref, vtf_ref,
                ksb_ref, vsb_ref, ksf_ref, vsf_ref):
    i = pl.program_id(0)
    r, d = h_ref.shape
    tk = ktb_ref.shape[-1]
    h = h_ref[...]
    q_ref[...] = (_dot(h, wq_ref[...]) * ((d // N_HEADS) ** -0.5)).astype(BF16)

    @pl.when(i < ntp)
    def _():
        for wt_ref, tb_ref, tf_ref in ((wkt_ref, ktb_ref, ktf_ref), (wvt_ref, vtb_ref, vtf_ref)):
            xt = _dot_nt(wt_ref[...], h)
            tf_ref[...] = xt
            for c in range(r // tk):
                tb_ref[0, c] = xt[:, c * tk:(c + 1) * tk].astype(BF16)

    @pl.when(i >= ntp)
    def _():
        for wt_ref, sb_ref, sf_ref in ((wkt_ref, ksb_ref, ksf_ref), (wvt_ref, vsb_ref, vsf_ref)):
            x = _dot_nt(h, wt_ref[...])
            sf_ref[...] = x
            sb_ref[...] = x.astype(BF16)


def _qkv(h, wq, wkt, wvt, n_streams, length):
    n, d = h.shape
    r = QKV_TILE
    tk = ATTN_TK
    n_p = n_streams * length
    nt = length // r
    ntp = n_p // r
    rows = pl.BlockSpec((r, d), lambda i: (i, 0))
    prompt_blk = lambda i: (jnp.minimum(i, ntp - 1) // nt, lax.rem(jnp.minimum(i, ntp - 1), nt))
    t_blocks = pl.BlockSpec((1, r // tk, d, tk), lambda i: prompt_blk(i) + (0, 0))
    t_full = pl.BlockSpec((d, r), prompt_blk)
    sample = pl.BlockSpec((r, d), lambda i: (jnp.maximum(i - ntp, 0), 0))
    w_spec = _full((d, d))
    return pl.pallas_call(
        functools.partial(_qkv_kernel, ntp),
        grid=(n // r,),
        in_specs=[rows, w_spec, w_spec, w_spec],
        out_specs=[rows, t_blocks, t_blocks, t_full, t_full, sample, sample, sample, sample],
        out_shape=[jax.ShapeDtypeStruct((n, d), BF16)]
        + [jax.ShapeDtypeStruct((n_streams, length // tk, d, tk), BF16)] * 2
        + [jax.ShapeDtypeStruct((n_streams * d, length), F32)] * 2
        + [jax.ShapeDtypeStruct((n - n_p, d), BF16)] * 2 + [jax.ShapeDtypeStruct((n - n_p, d), F32)] * 2,
        compiler_params=_params(("arbitrary",)),
        name="qkv",
    )(h, wq, wkt, wvt)


def _attn_setup(q_ref, qm_ref, carry_ref, acc_ref):
    tq, d = q_ref.shape
    lane = lax.broadcasted_iota(jnp.int32, (tq, LANES), 1)
    for p in range(d // LANES):
        qp = q_ref[:, p * LANES:(p + 1) * LANES]
        qm_ref[p, 0:tq, :] = jnp.where(lane < LANES // 2, qp, jnp.zeros_like(qp))
        qm_ref[p, tq:2 * tq, :] = jnp.where(lane >= LANES // 2, qp, jnp.zeros_like(qp))
    carry_ref[...] = jnp.zeros(carry_ref.shape, F32)
    acc_ref[...] = jnp.zeros(acc_ref.shape, F32)


def _suffix_sum_matrix(tk):
    src = lax.rem(lax.broadcasted_iota(jnp.int32, (2 * tk, 2 * tk), 0), tk)
    dst = lax.broadcasted_iota(jnp.int32, (2 * tk, 2 * tk), 1)
    return jnp.where((dst >= tk) | (src > dst), 1.0, 0.0).astype(BF16)


def _attn_block(qm_ref, carry_ref, acc_ref, keys, values, transposed, mask, sums):
    n_pairs, tq2, tk = carry_ref.shape
    tq = tq2 // 2
    lane = lax.broadcasted_iota(jnp.int32, (tq, LANES), 1)
    visible = (lambda x: x) if mask is None else (lambda x: jnp.where(mask, x, 0.0))
    scores = _dot if transposed else _dot_nt
    mix = _dot_nt if transposed else _dot
    zs = [scores(qm_ref[p], keys[p]) for p in range(n_pairs)]
    log_beta, parts = [], []
    for z in zs:
        sp = jnp.maximum(z, 0.0) + jnp.log(1.0 + jnp.exp(-jnp.abs(z)))
        log_keep = visible(-sp)
        hi = log_keep.astype(BF16)
        lo = (log_keep - hi.astype(F32)).astype(BF16)
        parts.append(jnp.concatenate([hi, lo], axis=1))
        log_beta.append(z - sp)
    sums_out = [_dot(part, sums) for part in parts]
    weights = []
    top = jnp.full((tq2, tk), -jnp.inf, F32)
    for p in range(n_pairs):
        carry = carry_ref[p]
        a = visible(jnp.exp(log_beta[p] + sums_out[p][:, :tk] + carry))
        weights.append(a.astype(BF16))
        carry = carry + sums_out[p][:, tk:]
        carry_ref[p] = carry
        top = jnp.maximum(top, carry)
    for p in range(n_pairs):
        out = mix(weights[p], values[p])
        acc_ref[:, p * LANES:(p + 1) * LANES] += jnp.where(lane < LANES // 2, out[:tq], out[tq:])
    return jnp.max(top)


def _keep_sweeping(state):
    j, top = state
    return (j >= 0) & (top > -ATTN_EXIT)


def _attn_prompt_kernel(q_ref, kt_ref, vt_ref, o_ref, qm_ref, carry_ref, acc_ref):
    qi = pl.program_id(1)
    tq, d = q_ref.shape
    tk = kt_ref.shape[-1]
    n_pairs = d // LANES
    _attn_setup(q_ref, qm_ref, carry_ref, acc_ref)
    row_pos = qi * tq + lax.rem(lax.broadcasted_iota(jnp.int32, (2 * tq, tk), 0), tq)
    col = lax.broadcasted_iota(jnp.int32, (2 * tq, tk), 1)
    sums = _suffix_sum_matrix(tk)

    def block(j, mask):
        keys = [kt_ref[0, j, p * LANES:(p + 1) * LANES, :] for p in range(n_pairs)]
        values = [vt_ref[0, j, p * LANES:(p + 1) * LANES, :] for p in range(n_pairs)]
        return _attn_block(qm_ref, carry_ref, acc_ref, keys, values, True, mask, sums)

    j0 = ((qi + 1) * tq - 2) // tk
    top = block(j0, (j0 * tk + col) < row_pos)
    lax.while_loop(_keep_sweeping, lambda state: (state[0] - 1, block(state[0], None)), (j0 - 1, top))
    o_ref[...] = acc_ref[...].astype(o_ref.dtype)


def _attention_prompt(q, kt, vt):
    n_streams, n_kblocks, d, tk = kt.shape
    length = n_kblocks * tk
    tq = min(length, ATTN_TQ)
    assert tk % tq == 0
    nq = length // tq
    kv = pl.BlockSpec((1, n_kblocks, d, tk), lambda b, i: (b, 0, 0, 0))
    rows = pl.BlockSpec((tq, d), lambda b, i: (b * nq + i, 0))
    return pl.pallas_call(
        _attn_prompt_kernel,
        grid=(n_streams, nq),
        in_specs=[rows, kv, kv],
        out_specs=rows,
        out_shape=jax.ShapeDtypeStruct((n_streams * length, d), BF16),
        scratch_shapes=[pltpu.VMEM((d // LANES, 2 * tq, LANES), BF16), pltpu.VMEM((d // LANES, 2 * tq, tk), F32),
                        pltpu.VMEM((tq, d), F32)],
        compiler_params=_params(("arbitrary", "arbitrary")),
        name="stick_breaking_prompt",
    )(q, kt, vt)


def _attn_sample_kernel(q_ref, kn_ref, vn_ref, ck_hbm, cv_hbm, o_ref, qm_ref, carry_ref, acc_ref, kbuf, vbuf, sem):
    b = pl.program_id(0)
    t, d = q_ref.shape
    tk = kbuf.shape[-1]
    n_pairs = d // LANES
    n_cache_blocks = ck_hbm.shape[1] // tk

    def fetch(j, slot):
        src = (pl.ds(pl.multiple_of(b * d, d), d), pl.ds(pl.multiple_of(j * tk, tk), tk))
        return (pltpu.make_async_copy(ck_hbm.at[src], kbuf.at[slot], sem.at[0, slot]),
                pltpu.make_async_copy(cv_hbm.at[src], vbuf.at[slot], sem.at[1, slot]))

    for cp in fetch(n_cache_blocks - 1, (n_cache_blocks - 1) % 2):
        cp.start()
    _attn_setup(q_ref, qm_ref, carry_ref, acc_ref)
    row = lax.rem(lax.broadcasted_iota(jnp.int32, (2 * t, tk), 0), t)
    col = lax.broadcasted_iota(jnp.int32, (2 * t, tk), 1)
    sums = _suffix_sum_matrix(tk)
    pad = jnp.zeros((tk - t, LANES), BF16)
    keys = [jnp.concatenate([kn_ref[:, p * LANES:(p + 1) * LANES], pad], axis=0) for p in range(n_pairs)]
    values = [jnp.concatenate([vn_ref[:, p * LANES:(p + 1) * LANES], pad], axis=0) for p in range(n_pairs)]
    top = _attn_block(qm_ref, carry_ref, acc_ref, keys, values, False, col < row, sums)

    def body(state):
        j, _ = state
        slot = lax.rem(j, 2)
        for cp in fetch(j, slot):
            cp.wait()

        @pl.when(j > 0)
        def _():
            for cp in fetch(j - 1, 1 - slot):
                cp.start()

        keys = [kbuf[slot, p * LANES:(p + 1) * LANES, :].astype(BF16) for p in range(n_pairs)]
        values = [vbuf[slot, p * LANES:(p + 1) * LANES, :].astype(BF16) for p in range(n_pairs)]
        return j - 1, _attn_block(qm_ref, carry_ref, acc_ref, keys, values, True, None, sums)

    j_end, _ = lax.while_loop(_keep_sweeping, body, (jnp.int32(n_cache_blocks - 1), top))

    @pl.when(j_end >= 0)
    def _():
        for cp in fetch(j_end, lax.rem(j_end, 2)):
            cp.wait()

    o_ref[...] = acc_ref[...].astype(o_ref.dtype)


def _attention_sample(q, k_new, v_new, cache_kt, cache_vt, t, q_row_off):
    d = q.shape[1]
    n_streams = k_new.shape[0] // t
    tk = ATTN_TK
    assert cache_kt.shape[1] % tk == 0 and t <= tk
    any_spec = pl.BlockSpec(memory_space=pl.ANY)
    new = pl.BlockSpec((t, d), lambda b: (b, 0))
    return pl.pallas_call(
        _attn_sample_kernel,
        grid=(n_streams,),
        in_specs=[pl.BlockSpec((t, d), lambda b: (q_row_off // t + b, 0)), new, new, any_spec, any_spec],
        out_specs=new,
        out_shape=jax.ShapeDtypeStruct((n_streams * t, d), BF16),
        scratch_shapes=[pltpu.VMEM((d // LANES, 2 * t, LANES), BF16), pltpu.VMEM((d // LANES, 2 * t, tk), F32),
                        pltpu.VMEM((t, d), F32), pltpu.VMEM((2, d, tk), F32), pltpu.VMEM((2, d, tk), F32),
                        pltpu.SemaphoreType.DMA((2, 2))],
        compiler_params=_params(("arbitrary",)),
        name="stick_breaking_sample",
    )(q, k_new, v_new, cache_kt, cache_vt)


def kernel(x_prompt, x_sample, state_pool, cache_k, cache_v, norm_mix, norm_ffn, pool_w, pool_scale, w_qkv, w_o, router_w, router_b, moe_w1, moe_b1, moe_w2, moe_b2, final_norm):
    b, s, d = x_prompt.shape
    db, t, _ = x_sample.shape
    past = cache_k.shape[2]
    n_exp = router_w.shape[2]
    hd = d // N_HEADS
    hist = POOL_HIST_ROWS
    n_p, n_s = b * s, db * t
    assert t >= hist and ROW_TILE % t == 0 and s % ROW_TILE == 0 and n_s % ROW_TILE == 0 and ROW_TILE % QKV_TILE == 0
    row = lambda a: a.reshape(1, -1)
    wr = [router_w[i].T for i in range(2)]
    br = [router_b[i].reshape(n_exp, 1) for i in range(2)]

    hist_s = jnp.concatenate([jnp.zeros((db, 1, d), F32), state_pool[0]], axis=1)
    xn, tok, idx, gates, rank, hlast_p, hlast_s, cnt = _pool_layer(
        x_prompt, x_sample, hist_s, past, row(norm_mix[0]), pool_w[0].astype(BF16), row(pool_scale[0]),
        row(norm_ffn[0]), wr[0], br[0])
    x1, h1 = _moe(tok, idx, gates, rank, cnt, xn, row(norm_mix[1]), None, 0, moe_w1, moe_b1[0], moe_w2, moe_b2[0])

    wqkv = w_qkv[0].astype(BF16)
    q, ktb, vtb, ktf, vtf, ksb, vsb, ksf, vsf = _qkv(h1, wqkv[:, :d], wqkv[:, d:2 * d].T, wqkv[:, 2 * d:].T, b, s)
    o_p = _attention_prompt(q, ktb, vtb)
    transposed = lambda cache: cache.transpose(0, 1, 3, 4, 2).reshape(db * d, past)
    o_s = _attention_sample(q, ksb, vsb, transposed(cache_k), transposed(cache_v), t, n_p)
    xn, tok, idx, gates, rank, cnt = _proj_layer(o_p, o_s, x1, w_o[0].astype(BF16), row(norm_ffn[1]), wr[1], br[1])
    y_p, y_s = _moe(tok, idx, gates, rank, cnt, xn, row(final_norm), n_p, 1, moe_w1, moe_b1[1], moe_w2, moe_b2[1])

    frames_major = lambda xt: xt.reshape(1, b, N_HEADS, hd, s).transpose(0, 1, 4, 2, 3)
    heads = lambda a: a.reshape(1, db, t, N_HEADS, hd)
    return (y_p.reshape(b, s, d), y_s.reshape(db, t, d), hlast_p[None, :, 1:, :],
            frames_major(ktf), frames_major(vtf), hlast_s[None, :, 1:, :], heads(ksf), heads(vsf))
```

```python
import functools

import jax
import jax.numpy as jnp
from jax import lax
from jax.experimental import pallas as pl
from jax.experimental.pallas import tpu as pltpu
from jax.experimental.pallas import tpu_sc as plsc

EPS = 1e-5
POOL_WINDOWS = (2, 4, 8, 16)
POOL_HIST_ROWS = 16
N_HEADS = 16
TOP_K = 4
SWIGLU_LIMIT = 7.0
SWIGLU_ALPHA = 1.702
LANES = 128
ROW_TILE = 512
QKV_TILE = 512
ATTN_TQ = 128
ATTN_TK = 128
ATTN_EXIT = 88.0
VMEM_LIMIT = 56 * 1024 * 1024
SC_CORES = 2
SC_SUBCORES = 16
SC_CHUNK = 96
SC_GATHER_CHUNK = 48
COMBINE_TILE = 512

F32 = jnp.float32
BF16 = jnp.bfloat16


def _rms(x, g):
    ms = jnp.mean(x * x, axis=-1, keepdims=True)
    return x * lax.rsqrt(ms + EPS) * g


def _dot(a, b):
    return jnp.dot(a, b, preferred_element_type=F32)


def _dot_nt(a, b, precision=None):
    return lax.dot_general(a, b, (((1,), (1,)), ((), ())), preferred_element_type=F32, precision=precision)


def _pack_halves(x):
    c = x.shape[1] // 2
    bits = lax.bitcast_convert_type(x.astype(BF16).astype(F32), jnp.uint32)
    return bits[:, :c] | (bits[:, c:] >> 16)


def _unpack_halves(w):
    return (lax.bitcast_convert_type(w & jnp.uint32(0xFFFF0000), F32), lax.bitcast_convert_type(w << 16, F32))


def _params(semantics):
    return pltpu.CompilerParams(dimension_semantics=semantics, vmem_limit_bytes=VMEM_LIMIT)


def _full(shape):
    return pl.BlockSpec(shape, lambda i, *_: (0,) * len(shape))


def _route_init(cnt_ref, tri_ref):
    r = tri_ref.shape[0]
    cnt_ref[...] = jnp.zeros(cnt_ref.shape, F32)
    tri_ref[...] = jnp.where(lax.broadcasted_iota(jnp.int32, (r, r), 0) < lax.broadcasted_iota(jnp.int32, (r, r), 1),
                             1.0, 0.0).astype(BF16)


def _route_tail(xn, gffn_ref, wr_ref, br_ref, cnt_ref, tri_ref, tok_ref, idx_ref, gate_ref, rank_ref):
    n_exp = wr_ref.shape[0]
    tok = _rms(xn, gffn_ref[...])
    tok_ref[...] = _pack_halves(tok)
    logits = _dot_nt(wr_ref[...], tok, precision=lax.Precision.HIGHEST) + br_ref[...]
    eidx = lax.broadcasted_iota(jnp.int32, logits.shape, 0).astype(F32)
    vals, idxs = [], []
    l = logits
    for _ in range(TOP_K):
        m = jnp.max(l, axis=0, keepdims=True)
        i = jnp.min(jnp.where(l == m, eidx, float(n_exp)), axis=0, keepdims=True)
        vals.append(m)
        idxs.append(i)
        l = jnp.where(eidx == i, -jnp.inf, l)
    es = [jnp.exp(v - vals[0]) for v in vals]
    den = es[0]
    for e in es[1:]:
        den = den + e
    gate_ref[...] = jnp.concatenate([e / den for e in es], axis=0)
    idx_ref[...] = jnp.concatenate(idxs, axis=0).astype(jnp.int32)
    member = jnp.zeros(logits.shape, F32)
    for i in idxs:
        member = member + jnp.where(eidx == i, 1.0, 0.0)
    before = _dot(member.astype(BF16), tri_ref[...]) + cnt_ref[:, :1]
    ranks = [jnp.sum(jnp.where(eidx == i, before, 0.0), axis=0, keepdims=True) for i in idxs]
    rank_ref[...] = jnp.concatenate(ranks, axis=0).astype(jnp.int32)
    cnt_ref[...] = cnt_ref[...] + jnp.sum(member, axis=1, keepdims=True)


def _pool_mix(h, ext_ref, pos, pw_ref, ps_ref):
    ts, d = h.shape[-2:]
    hist = POOL_HIST_ROWS
    group = d // len(POOL_WINDOWS)
    pre = (slice(None),) * (h.ndim - 2)
    ys = []
    for g, win in enumerate(POOL_WINDOWS):
        cols = slice(g * group, (g + 1) * group)
        hg = h[pre + (slice(None), cols)]
        acc = hg
        for j in range(1, win):
            acc = acc + ext_ref[pre + (slice(hist - j, hist - j + ts), cols)]
        cnt = jnp.minimum(pos + 1, win).astype(F32)
        dg = acc / cnt - hg
        ys.append(_dot(dg.reshape(-1, group).astype(BF16), pw_ref[g]))
    return jnp.concatenate(ys, axis=-1) * ps_ref[...]


def _pool_mix_tiled(h, ext_ref, lvl_ref, pos, pw_ref, ps_ref):
    ts, d = h.shape
    hist = POOL_HIST_ROWS
    group = d // len(POOL_WINDOWS)
    prev_ref, prev_col0 = ext_ref, 0
    ys = []
    for g, win in enumerate(POOL_WINDOWS):
        assert win == 2 * (POOL_WINDOWS[g - 1] if g else 1)
        col0 = g * group
        cols = slice(col0 - prev_col0, d - prev_col0)
        level = prev_ref[hist:hist + ts, cols] + prev_ref[hist - win // 2:hist - win // 2 + ts, cols]
        if g + 1 < len(POOL_WINDOWS):
            lvl_ref[g, hist:hist + ts, 0:d - col0 - group] = level[:, group:]
            prev_ref, prev_col0 = lvl_ref.at[g], col0 + group
        cnt = jnp.minimum(pos + 1, win).astype(F32)
        dg = level[:, :group] / cnt - h[:, col0:col0 + group]
        ys.append(_dot(dg.astype(BF16), pw_ref[g]))
    return jnp.concatenate(ys, axis=-1) * ps_ref[...]


def _pool_kernel(ntp, nt, pos0_s, xp_ref, xs_ref, hist_ref, gmix_ref, pw_ref, ps_ref, gffn_ref, wr_ref, br_ref,
                 xn_ref, tok_ref, idx_ref, gate_ref, rank_ref, hlast_p_ref, hlast_s_ref, cnt_out_ref,
                 ext_p, lvl_p, ext_s, cnt_ref, tri_ref):
    i = pl.program_id(0)
    hist = POOL_HIST_ROWS

    @pl.when(i == 0)
    def _():
        _route_init(cnt_ref, tri_ref)

    @pl.when(i < ntp)
    def _():
        ts, d = xp_ref.shape
        t = lax.rem(i, nt)

        @pl.when(t == 0)
        def _():
            ext_p[0:hist, :] = jnp.zeros((hist, d), F32)
            lvl_p[:, 0:hist, :] = jnp.zeros((lvl_p.shape[0], hist, lvl_p.shape[2]), F32)

        x = xp_ref[...]
        h = _rms(x, gmix_ref[...])
        ext_p[hist:hist + ts, :] = h

        @pl.when(t == nt - 1)
        def _():
            hlast_p_ref[0] = h[ts - hist:, :]

        pos = t * ts + lax.broadcasted_iota(jnp.int32, (ts, 1), 0)
        xn_ref[...] = x + _pool_mix_tiled(h, ext_p, lvl_p, pos, pw_ref, ps_ref)
        ext_p[0:hist, :] = ext_p[ts:ts + hist, :]
        lvl_p[:, 0:hist, :] = lvl_p[:, ts:ts + hist, :]

    @pl.when(i >= ntp)
    def _():
        bb, ts, d = xs_ref.shape
        ext_s[:, 0:hist, :] = hist_ref[...]
        x = xs_ref[...]
        h = _rms(x, gmix_ref[...])
        ext_s[:, hist:hist + ts, :] = h
        hlast_s_ref[...] = h[:, ts - hist:, :]
        pos = pos0_s + lax.broadcasted_iota(jnp.int32, (1, ts, 1), 1)
        xn_ref[...] = x.reshape(bb * ts, d) + _pool_mix(h, ext_s, pos, pw_ref, ps_ref)

    _route_tail(xn_ref[...], gffn_ref, wr_ref, br_ref, cnt_ref, tri_ref, tok_ref, idx_ref, gate_ref, rank_ref)
    cnt_out_ref[...] = cnt_ref[...]


def _pool_layer(x_prompt, x_sample, hist_s, pos0_s, gmix, pw, ps, gffn, wr, br):
    b, s, d = x_prompt.shape
    db, t, _ = x_sample.shape
    n_exp = wr.shape[0]
    r = ROW_TILE
    nt = s // r
    ntp = b * nt
    bb = r // t
    nts = db // bb
    n = b * s + db * t
    hist = POOL_HIST_ROWS
    rows = pl.BlockSpec((r, d), lambda i: (i, 0))
    lanes = pl.BlockSpec((TOP_K, r), lambda i: (0, i))
    sample_blk = lambda i: (jnp.maximum(i - ntp, 0), 0, 0)
    return pl.pallas_call(
        functools.partial(_pool_kernel, ntp, nt, pos0_s),
        grid=(ntp + nts,),
        in_specs=[pl.BlockSpec((r, d), lambda i: (jnp.minimum(i, ntp - 1), 0)),
                  pl.BlockSpec((bb, t, d), sample_blk), pl.BlockSpec((bb, hist, d), sample_blk),
                  _full((1, d)), _full(pw.shape), _full((1, d)), _full((1, d)), _full((n_exp, d)), _full((n_exp, 1))],
        out_specs=[rows, pl.BlockSpec((r, d // 2), lambda i: (i, 0)), lanes, lanes, lanes,
                   pl.BlockSpec((1, hist, d), lambda i: (jnp.minimum(i // nt, b - 1), 0, 0)),
                   pl.BlockSpec((bb, hist, d), sample_blk), _full((n_exp, LANES))],
        out_shape=[jax.ShapeDtypeStruct((n, d), F32), jax.ShapeDtypeStruct((n, d // 2), jnp.uint32),
                   jax.ShapeDtypeStruct((TOP_K, n), jnp.int32), jax.ShapeDtypeStruct((TOP_K, n), F32),
                   jax.ShapeDtypeStruct((TOP_K, n), jnp.int32),
                   jax.ShapeDtypeStruct((b, hist, d), F32), jax.ShapeDtypeStruct((db, hist, d), F32),
                   jax.ShapeDtypeStruct((n_exp, LANES), F32)],
        scratch_shapes=[pltpu.VMEM((hist + r, d), F32),
                        pltpu.VMEM((len(POOL_WINDOWS) - 1, hist + r, d - d // len(POOL_WINDOWS)), F32),
                        pltpu.VMEM((bb, hist + t, d), F32), pltpu.VMEM((n_exp, LANES), F32), pltpu.VMEM((r, r), BF16)],
        compiler_params=_params(("arbitrary",)),
        name="pool_route",
    )(x_prompt.reshape(b * s, d), x_sample, hist_s, gmix, pw, ps, gffn, wr, br)


def _proj_kernel(ntp, op_ref, os_ref, x_ref, wo_ref, gffn_ref, wr_ref, br_ref,
                 xn_ref, tok_ref, idx_ref, gate_ref, rank_ref, cnt_out_ref, cnt_ref, tri_ref):
    i = pl.program_id(0)

    @pl.when(i == 0)
    def _():
        _route_init(cnt_ref, tri_ref)

    o = jnp.where(i < ntp, op_ref[...], os_ref[...])
    xn = x_ref[...] + _dot(o, wo_ref[...])
    xn_ref[...] = xn
    _route_tail(xn, gffn_ref, wr_ref, br_ref, cnt_ref, tri_ref, tok_ref, idx_ref, gate_ref, rank_ref)
    cnt_out_ref[...] = cnt_ref[...]


def _proj_layer(o_p, o_s, x, wo, gffn, wr, br):
    n, d = x.shape
    n_exp = wr.shape[0]
    r = ROW_TILE
    ntp = o_p.shape[0] // r
    rows = pl.BlockSpec((r, d), lambda i: (i, 0))
    lanes = pl.BlockSpec((TOP_K, r), lambda i: (0, i))
    return pl.pallas_call(
        functools.partial(_proj_kernel, ntp),
        grid=(n // r,),
        in_specs=[pl.BlockSpec((r, d), lambda i: (jnp.minimum(i, ntp - 1), 0)),
                  pl.BlockSpec((r, d), lambda i: (jnp.maximum(i - ntp, 0), 0)),
                  rows, _full((d, d)), _full((1, d)), _full((n_exp, d)), _full((n_exp, 1))],
        out_specs=[rows, pl.BlockSpec((r, d // 2), lambda i: (i, 0)), lanes, lanes, lanes, _full((n_exp, LANES))],
        out_shape=[jax.ShapeDtypeStruct((n, d), F32), jax.ShapeDtypeStruct((n, d // 2), jnp.uint32),
                   jax.ShapeDtypeStruct((TOP_K, n), jnp.int32), jax.ShapeDtypeStruct((TOP_K, n), F32),
                   jax.ShapeDtypeStruct((TOP_K, n), jnp.int32), jax.ShapeDtypeStruct((n_exp, LANES), F32)],
        scratch_shapes=[pltpu.VMEM((n_exp, LANES), F32), pltpu.VMEM((r, r), BF16)],
        compiler_params=_params(("arbitrary",)),
        name="proj_route",
    )(o_p, o_s, x, wo, gffn, wr, br)


def _sc_chunks(dest, w):
    n = dest.shape[1]
    return dest.reshape(TOP_K, n // w, w).transpose(1, 0, 2)


def _dispatch(tok, dest, n_buf_rows):
    n, d = tok.shape
    w = SC_CHUNK
    n_workers = SC_CORES * SC_SUBCORES
    per_worker = n // (w * n_workers)
    assert per_worker * w * n_workers == n
    mesh = plsc.VectorSubcoreMesh(core_axis_name="core", subcore_axis_name="subcore",
                                  num_cores=SC_CORES, num_subcores=SC_SUBCORES)

    @functools.partial(
        pl.kernel, mesh=mesh, out_type=jax.ShapeDtypeStruct((n_buf_rows, d), tok.dtype),
        scratch_types=[pltpu.VMEM((TOP_K, w), jnp.int32), pltpu.VMEM((w, d), tok.dtype), pltpu.SemaphoreType.DMA],
        name="moe_dispatch_sc")
    def scatter_rows(tok_hbm, dest_hbm, buf_hbm, idx_v, rows_v, sem):
        worker = lax.axis_index("subcore") * SC_CORES + lax.axis_index("core")

        @pl.loop(0, per_worker)
        def _(c):
            chunk = worker * per_worker + c
            pltpu.sync_copy(dest_hbm.at[chunk], idx_v)
            pltpu.sync_copy(tok_hbm.at[pl.ds(pl.multiple_of(chunk * w, w), w)], rows_v)
            copies = [pltpu.async_copy(rows_v, buf_hbm.at[idx_v.at[k]], sem) for k in range(TOP_K)]
            for cp in copies:
                cp.wait()

    return scatter_rows(tok, _sc_chunks(dest, w))


def _expert_kernel(layer, be_ref, valid_ref, nused_ref, next_ref, slot_ref, x_ref, w1_hbm, b1_ref, w2_hbm, b2_ref,
                   o_ref, w1f_ref, w2f_ref, w1b_ref, w2b_ref, sem):
    i = pl.program_id(0)
    used = valid_ref[i] > 0

    def fetch(expert, slot):
        return (pltpu.make_async_copy(w1_hbm.at[layer, expert], w1f_ref.at[slot], sem.at[0, slot]),
                pltpu.make_async_copy(w2_hbm.at[layer, expert], w2f_ref.at[slot], sem.at[1, slot]))

    @pl.when(used & ((i == 0) | (be_ref[i] != be_ref[jnp.maximum(i - 1, 0)])))
    def _():
        slot = slot_ref[i]

        @pl.when(i == 0)
        def _():
            for cp in fetch(be_ref[i], slot):
                cp.start()

        for cp in fetch(be_ref[i], slot):
            cp.wait()

        @pl.when(next_ref[i] >= 0)
        def _():
            for cp in fetch(next_ref[i], 1 - slot):
                cp.start()

        w1b_ref[...] = w1f_ref[slot].astype(BF16)
        w2b_ref[...] = w2f_ref[slot].astype(BF16)

    @pl.when(used)
    def _():
        f = w2b_ref.shape[0]
        row = lax.broadcasted_iota(jnp.int32, (x_ref.shape[0], 1), 0)
        x = jnp.where(row < valid_ref[i], x_ref[...], jnp.uint32(0))
        xa, xb = _unpack_halves(x)
        gu = _dot(jnp.concatenate([xa.astype(BF16), xb.astype(BF16)], axis=1), w1b_ref[...]) + b1_ref[0]
        gate = jnp.minimum(gu[:, :f], SWIGLU_LIMIT)
        up = jnp.clip(gu[:, f:], -SWIGLU_LIMIT, SWIGLU_LIMIT)
        hid = (up + 1.0) * (gate * jax.nn.sigmoid(gate * SWIGLU_ALPHA))
        o_ref[...] = _pack_halves(_dot(hid.astype(BF16), w2b_ref[...]) + b2_ref[0])

    @pl.when(jnp.logical_not(used))
    def _():
        o_ref[...] = jnp.zeros(o_ref.shape, o_ref.dtype)


def _experts(buf, block_expert, valid, nused, next_expert, slot, layer, w1, b1, w2, b2, tm):
    rows, dw = buf.shape
    _, n_exp, d, f2 = w1.shape
    f = w2.shape[2]
    in_rows = lambda i, be, va, nu, nx, sl: (jnp.minimum(i, nu[0] - 1), 0)
    by_expert = lambda i, be, va, nu, nx, sl: (be[i], 0, 0)
    any_spec = pl.BlockSpec(memory_space=pl.ANY)
    return pl.pallas_call(
        functools.partial(_expert_kernel, layer),
        grid_spec=pltpu.PrefetchScalarGridSpec(
            num_scalar_prefetch=5,
            grid=(rows // tm,),
            in_specs=[pl.BlockSpec((tm, dw), in_rows), any_spec, pl.BlockSpec((1, 1, f2), by_expert),
                      any_spec, pl.BlockSpec((1, 1, d), by_expert)],
            out_specs=pl.BlockSpec((tm, dw), lambda i, be, va, nu, nx, sl: (i, 0)),
            scratch_shapes=[pltpu.VMEM((2, d, f2), F32), pltpu.VMEM((2, f, d), F32),
                            pltpu.VMEM((d, f2), BF16), pltpu.VMEM((f, d), BF16), pltpu.SemaphoreType.DMA((2, 2))]),
        out_shape=jax.ShapeDtypeStruct((rows, dw), jnp.uint32),
        compiler_params=_params(("arbitrary",)),
        name="moe_experts",
    )(block_expert, valid, nused, next_expert, slot, buf, w1, b1.reshape(n_exp, 1, f2), w2, b2.reshape(n_exp, 1, d))


def _gather_rows(ybuf, dest):
    _, d = ybuf.shape
    n = dest.shape[1]
    w = SC_GATHER_CHUNK
    n_workers = SC_CORES * SC_SUBCORES
    per_worker = n // (w * n_workers)
    assert per_worker * w * n_workers == n and TOP_K % 2 == 0
    mesh = plsc.VectorSubcoreMesh(core_axis_name="core", subcore_axis_name="subcore",
                                  num_cores=SC_CORES, num_subcores=SC_SUBCORES)

    @functools.partial(
        pl.kernel, mesh=mesh, out_type=jax.ShapeDtypeStruct((TOP_K, n, d), ybuf.dtype),
        scratch_types=[pltpu.VMEM((TOP_K, w), jnp.int32), pltpu.VMEM((2, w, d), ybuf.dtype),
                       pltpu.SemaphoreType.DMA((2,)), pltpu.SemaphoreType.DMA((2,))],
        name="moe_gather_sc")
    def gather_rows(y_hbm, dest_hbm, out_hbm, idx_v, rows_v, sem_in, sem_out):
        worker = lax.axis_index("subcore") * SC_CORES + lax.axis_index("core")

        @pl.loop(0, per_worker)
        def _(c):
            chunk = worker * per_worker + c
            tokens = pl.ds(pl.multiple_of(chunk * w, w), w)
            pltpu.sync_copy(dest_hbm.at[chunk], idx_v)
            fetch = lambda k: pltpu.async_copy(y_hbm.at[idx_v.at[k]], rows_v.at[k % 2], sem_in.at[k % 2])
            store = lambda k: pltpu.async_copy(rows_v.at[k % 2], out_hbm.at[k, tokens], sem_out.at[k % 2])
            fetches = [fetch(0), fetch(1)]
            stores = []
            for k in range(TOP_K):
                fetches[k].wait()
                stores.append(store(k))
                if k + 2 < TOP_K:
                    stores[k].wait()
                    fetches.append(fetch(k + 2))
            for k in range(TOP_K - 2, TOP_K):
                stores[k].wait()

    return gather_rows(ybuf, _sc_chunks(dest, w))


def _combine_kernel(ntp, rows_ref, gate_ref, x_ref, g_ref, *outs):
    i = pl.program_id(0)
    r = x_ref.shape[0]
    gates = jnp.concatenate([gate_ref[...], jnp.zeros((LANES - TOP_K, r), F32)], axis=0)
    gates_t = jnp.transpose(gates)
    halves = [_unpack_halves(rows_ref[k]) for k in range(TOP_K)]
    f = []
    for side in range(2):
        acc = halves[0][side] * gates_t[:, 0:1]
        for k in range(1, TOP_K):
            acc = acc + halves[k][side] * gates_t[:, k:k + 1]
        f.append(acc)
    xo = x_ref[...] + jnp.concatenate(f, axis=1)
    hn = _rms(xo, g_ref[...])
    if ntp is None:
        xo_ref, hn_ref = outs
        xo_ref[...] = xo
        hn_ref[...] = hn.astype(hn_ref.dtype)
    else:
        hp_ref, hs_ref = outs

        @pl.when(i < ntp)
        def _():
            hp_ref[...] = hn

        @pl.when(i >= ntp)
        def _():
            hs_ref[...] = hn


def _combine(rows4, gates, x, g, n_p):
    n, d = x.shape
    r = COMBINE_TILE
    rows = pl.BlockSpec((r, d), lambda i: (i, 0))
    if n_p is None:
        ntp = None
        out_specs = [rows, rows]
        out_shape = [jax.ShapeDtypeStruct((n, d), F32), jax.ShapeDtypeStruct((n, d), BF16)]
    else:
        ntp = n_p // r
        out_specs = [pl.BlockSpec((r, d), lambda i: (jnp.minimum(i, ntp - 1), 0)),
                     pl.BlockSpec((r, d), lambda i: (jnp.maximum(i - ntp, 0), 0))]
        out_shape = [jax.ShapeDtypeStruct((n_p, d), F32), jax.ShapeDtypeStruct((n - n_p, d), F32)]
    return pl.pallas_call(
        functools.partial(_combine_kernel, ntp),
        grid=(n // r,),
        in_specs=[pl.BlockSpec((TOP_K, r, d // 2), lambda i: (0, i, 0)), pl.BlockSpec((TOP_K, r), lambda i: (0, i)),
                  rows, _full((1, d))],
        out_specs=out_specs,
        out_shape=out_shape,
        compiler_params=_params(("arbitrary",)),
        name="moe_combine",
    )(rows4, gates, x, g)


def _moe(tok, idx, gates, rank, cnt, x, g_next, n_p, layer, w1, b1, w2, b2):
    n, d = tok.shape
    n_exp = w1.shape[1]
    pairs = n * TOP_K
    tm = 128
    for cand in (512, 256):
        if pairs >= 4 * cand * n_exp and pairs % cand == 0:
            tm = cand
            break
    n_blocks = -(-pairs // tm) + n_exp
    counts = cnt[:, 0].astype(jnp.int32)
    padded = (counts + tm - 1) // tm * tm
    pad_end = jnp.cumsum(padded)
    pad_start = pad_end - padded
    experts = jnp.arange(n_exp, dtype=jnp.int32)
    dest = rank + jnp.sum(jnp.where(idx[:, :, None] == experts, pad_start, 0), axis=-1)
    starts = jnp.arange(n_blocks, dtype=jnp.int32) * tm
    nused = (pad_end[-1:] // tm).astype(jnp.int32)
    block_expert = jnp.minimum(jnp.sum((pad_end[None, :] <= starts[:, None]).astype(jnp.int32), axis=1), n_exp - 1)
    tokens_end = jnp.sum(jnp.where(block_expert[:, None] == experts, pad_start + counts, 0), axis=-1)
    valid = jnp.clip(tokens_end - starts, 0, tm).astype(jnp.int32)
    has_tokens = counts > 0
    later = has_tokens[None, :] & (experts[None, :] > experts[:, None])
    successor = jnp.min(jnp.where(later, experts[None, :], n_exp), axis=1)
    successor = jnp.where(successor == n_exp, -1, successor)
    parity = (jnp.cumsum(has_tokens.astype(jnp.int32)) - 1) % 2
    of_block = lambda table: jnp.sum(jnp.where(block_expert[:, None] == experts, table, 0), axis=-1).astype(jnp.int32)
    buf = _dispatch(tok, dest, n_blocks * tm)
    ybuf = _experts(buf, block_expert, valid, nused, of_block(successor), of_block(parity), layer, w1, b1, w2, b2, tm)
    return _combine(_gather_rows(ybuf, dest), gates, x, g_next, n_p)


def _qkv_kernel(ntp, h_ref, wq_ref, wkt_ref, wvt_ref, q_ref, ktb_ref, vtb_ref, ktf_ref, vtf_ref,
                ksb_ref, vsb_ref, ksf_ref, vsf_ref):
    i = pl.program_id(0)
    r, d = h_ref.shape
    tk = ktb_ref.shape[-1]
    h = h_ref[...]
    q_ref[...] = (_dot(h, wq_ref[...]) * ((d // N_HEADS) ** -0.5)).astype(BF16)

    @pl.when(i < ntp)
    def _():
        for wt_ref, tb_ref, tf_ref in ((wkt_ref, ktb_ref, ktf_ref), (wvt_ref, vtb_ref, vtf_ref)):
            xt = _dot_nt(wt_ref[...], h)
            tf_ref[...] = xt
            for c in range(r // tk):
                tb_ref[0, c] = xt[:, c * tk:(c + 1) * tk].astype(BF16)

    @pl.when(i >= ntp)
    def _():
        for wt_ref, sb_ref, sf_ref in ((wkt_ref, ksb_ref, ksf_ref), (wvt_ref, vsb_ref, vsf_ref)):
            x = _dot_nt(h, wt_ref[...])
            sf_ref[...] = x
            sb_ref[...] = x.astype(BF16)


def _qkv(h, wq, wkt, wvt, n_streams, length):
    n, d = h.shape
    r = QKV_TILE
    tk = ATTN_TK
    n_p = n_streams * length
    nt = length // r
    ntp = n_p // r
    rows = pl.BlockSpec((r, d), lambda i: (i, 0))
    prompt_blk = lambda i: (jnp.minimum(i, ntp - 1) // nt, lax.rem(jnp.minimum(i, ntp - 1), nt))
    t_blocks = pl.BlockSpec((1, r // tk, d, tk), lambda i: prompt_blk(i) + (0, 0))
    t_full = pl.BlockSpec((d, r), prompt_blk)
    sample = pl.BlockSpec((r, d), lambda i: (jnp.maximum(i - ntp, 0), 0))
    w_spec = _full((d, d))
    return pl.pallas_call(
        functools.partial(_qkv_kernel, ntp),
        grid=(n // r,),
        in_specs=[rows, w_spec, w_spec, w_spec],
        out_specs=[rows, t_blocks, t_blocks, t_full, t_full, sample, sample, sample, sample],
        out_shape=[jax.ShapeDtypeStruct((n, d), BF16)]
        + [jax.ShapeDtypeStruct((n_streams, length // tk, d, tk), BF16)] * 2
        + [jax.ShapeDtypeStruct((n_streams * d, length), F32)] * 2
        + [jax.ShapeDtypeStruct((n - n_p, d), BF16)] * 2 + [jax.ShapeDtypeStruct((n - n_p, d), F32)] * 2,
        compiler_params=_params(("arbitrary",)),
        name="qkv",
    )(h, wq, wkt, wvt)


def _attn_setup(q_ref, qm_ref):
    tq, d = q_ref.shape
    lane = lax.broadcasted_iota(jnp.int32, (tq, LANES), 1)
    for p in range(d // LANES):
        qp = q_ref[:, p * LANES:(p + 1) * LANES]
        qm_ref[p, 0:tq, :] = jnp.where(lane < LANES // 2, qp, jnp.zeros_like(qp))
        qm_ref[p, tq:2 * tq, :] = jnp.where(lane >= LANES // 2, qp, jnp.zeros_like(qp))


def _suffix_sum_matrix(tk):
    src = lax.rem(lax.broadcasted_iota(jnp.int32, (2 * tk, 2 * tk), 0), tk)
    dst = lax.broadcasted_iota(jnp.int32, (2 * tk, 2 * tk), 1)
    return jnp.where((dst >= tk) | (src > dst), 1.0, 0.0).astype(BF16)


def _attn_block(qm_ref, carry_ref, acc_ref, keys, values, transposed, mask, sums):
    n_pairs, tq2, tk = carry_ref.shape
    tq = tq2 // 2
    lane = lax.broadcasted_iota(jnp.int32, (tq, LANES), 1)
    first = mask is not None
    visible = (lambda x: jnp.where(mask, x, 0.0)) if first else (lambda x: x)
    scores = _dot if transposed else _dot_nt
    mix = _dot_nt if transposed else _dot
    zs = [scores(qm_ref[p], keys[p]) for p in range(n_pairs)]
    log_beta, parts = [], []
    for z in zs:
        sp = jnp.maximum(z, 0.0) + jnp.log(1.0 + jnp.exp(-jnp.abs(z)))
        log_keep = visible(-sp)
        hi = log_keep.astype(BF16)
        lo = (log_keep - hi.astype(F32)).astype(BF16)
        parts.append(jnp.concatenate([hi, lo], axis=1))
        log_beta.append(z - sp)
    sums_out = [_dot(part, sums) for part in parts]
    weights = []
    top = jnp.full((tq2, tk), -jnp.inf, F32)
    for p in range(n_pairs):
        after, carry = sums_out[p][:, :tk], sums_out[p][:, tk:]
        if not first:
            after = after + carry_ref[p]
            carry = carry + carry_ref[p]
        weights.append(visible(jnp.exp(log_beta[p] + after)).astype(BF16))
        carry_ref[p] = carry
        top = jnp.maximum(top, carry)
    for p in range(n_pairs):
        out = mix(weights[p], values[p])
        out = jnp.where(lane < LANES // 2, out[:tq], out[tq:])
        cols = slice(p * LANES, (p + 1) * LANES)
        acc_ref[:, cols] = out if first else acc_ref[:, cols] + out
    return jnp.max(top)


def _keep_sweeping(state):
    j, top = state
    return (j >= 0) & (top > -ATTN_EXIT)


def _attn_prompt_kernel(q_ref, kt_ref, vt_ref, o_ref, qm_ref, carry_ref, acc_ref):
    qi = pl.program_id(1)
    tq, d = q_ref.shape
    tk = kt_ref.shape[-1]
    n_pairs = d // LANES
    _attn_setup(q_ref, qm_ref)
    row_pos = qi * tq + lax.rem(lax.broadcasted_iota(jnp.int32, (2 * tq, tk), 0), tq)
    col = lax.broadcasted_iota(jnp.int32, (2 * tq, tk), 1)
    sums = _suffix_sum_matrix(tk)

    def block(j, mask):
        keys = [kt_ref[0, j, p * LANES:(p + 1) * LANES, :] for p in range(n_pairs)]
        values = [vt_ref[0, j, p * LANES:(p + 1) * LANES, :] for p in range(n_pairs)]
        return _attn_block(qm_ref, carry_ref, acc_ref, keys, values, True, mask, sums)

    j0 = ((qi + 1) * tq - 2) // tk
    top = block(j0, (j0 * tk + col) < row_pos)
    lax.while_loop(_keep_sweeping, lambda state: (state[0] - 1, block(state[0], None)), (j0 - 1, top))
    o_ref[...] = acc_ref[...].astype(o_ref.dtype)


def _attention_prompt(q, kt, vt):
    n_streams, n_kblocks, d, tk = kt.shape
    length = n_kblocks * tk
    tq = min(length, ATTN_TQ)
    assert tk % tq == 0
    nq = length // tq
    kv = pl.BlockSpec((1, n_kblocks, d, tk), lambda b, i: (b, 0, 0, 0))
    rows = pl.BlockSpec((tq, d), lambda b, i: (b * nq + i, 0))
    return pl.pallas_call(
        _attn_prompt_kernel,
        grid=(n_streams, nq),
        in_specs=[rows, kv, kv],
        out_specs=rows,
        out_shape=jax.ShapeDtypeStruct((n_streams * length, d), BF16),
        scratch_shapes=[pltpu.VMEM((d // LANES, 2 * tq, LANES), BF16), pltpu.VMEM((d // LANES, 2 * tq, tk), F32),
                        pltpu.VMEM((tq, d), F32)],
        compiler_params=_params(("arbitrary", "arbitrary")),
        name="stick_breaking_prompt",
    )(q, kt, vt)


def _attn_sample_kernel(q_ref, kn_ref, vn_ref, ck_hbm, cv_hbm, o_ref, qm_ref, carry_ref, acc_ref, kbuf, vbuf, sem):
    b = pl.program_id(0)
    t, d = q_ref.shape
    tk = kbuf.shape[-1]
    n_pairs = d // LANES
    n_cache_blocks = ck_hbm.shape[1] // tk

    def fetch(j, slot):
        src = (pl.ds(pl.multiple_of(b * d, d), d), pl.ds(pl.multiple_of(j * tk, tk), tk))
        return (pltpu.make_async_copy(ck_hbm.at[src], kbuf.at[slot], sem.at[0, slot]),
                pltpu.make_async_copy(cv_hbm.at[src], vbuf.at[slot], sem.at[1, slot]))

    for cp in fetch(n_cache_blocks - 1, (n_cache_blocks - 1) % 2):
        cp.start()
    _attn_setup(q_ref, qm_ref)
    row = lax.rem(lax.broadcasted_iota(jnp.int32, (2 * t, tk), 0), t)
    col = lax.broadcasted_iota(jnp.int32, (2 * t, tk), 1)
    sums = _suffix_sum_matrix(tk)
    pad = jnp.zeros((tk - t, LANES), BF16)
    keys = [jnp.concatenate([kn_ref[:, p * LANES:(p + 1) * LANES], pad], axis=0) for p in range(n_pairs)]
    values = [jnp.concatenate([vn_ref[:, p * LANES:(p + 1) * LANES], pad], axis=0) for p in range(n_pairs)]
    top = _attn_block(qm_ref, carry_ref, acc_ref, keys, values, False, col < row, sums)

    def body(state):
        j, _ = state
        slot = lax.rem(j, 2)
        for cp in fetch(j, slot):
            cp.wait()

        @pl.when(j > 0)
        def _():
            for cp in fetch(j - 1, 1 - slot):
                cp.start()

        keys = [kbuf[slot, p * LANES:(p + 1) * LANES, :].astype(BF16) for p in range(n_pairs)]
        values = [vbuf[slot, p * LANES:(p + 1) * LANES, :].astype(BF16) for p in range(n_pairs)]
        return j - 1, _attn_block(qm_ref, carry_ref, acc_ref, keys, values, True, None, sums)

    j_end, _ = lax.while_loop(_keep_sweeping, body, (jnp.int32(n_cache_blocks - 1), top))

    @pl.when(j_end >= 0)
    def _():
        for cp in fetch(j_end, lax.rem(j_end, 2)):
            cp.wait()

    o_ref[...] = acc_ref[...].astype(o_ref.dtype)


def _attention_sample(q, k_new, v_new, cache_kt, cache_vt, t, q_row_off):
    d = q.shape[1]
    n_streams = k_new.shape[0] // t
    tk = ATTN_TK
    assert cache_kt.shape[1] % tk == 0 and t <= tk
    any_spec = pl.BlockSpec(memory_space=pl.ANY)
    new = pl.BlockSpec((t, d), lambda b: (b, 0))
    return pl.pallas_call(
        _attn_sample_kernel,
        grid=(n_streams,),
        in_specs=[pl.BlockSpec((t, d), lambda b: (q_row_off // t + b, 0)), new, new, any_spec, any_spec],
        out_specs=new,
        out_shape=jax.ShapeDtypeStruct((n_streams * t, d), BF16),
        scratch_shapes=[pltpu.VMEM((d // LANES, 2 * t, LANES), BF16), pltpu.VMEM((d // LANES, 2 * t, tk), F32),
                        pltpu.VMEM((t, d), F32), pltpu.VMEM((2, d, tk), F32), pltpu.VMEM((2, d, tk), F32),
                        pltpu.SemaphoreType.DMA((2, 2))],
        compiler_params=_params(("arbitrary",)),
        name="stick_breaking_sample",
    )(q, k_new, v_new, cache_kt, cache_vt)


def kernel(x_prompt, x_sample, state_pool, cache_k, cache_v, norm_mix, norm_ffn, pool_w, pool_scale, w_qkv, w_o, router_w, router_b, moe_w1, moe_b1, moe_w2, moe_b2, final_norm):
    b, s, d = x_prompt.shape
    db, t, _ = x_sample.shape
    past = cache_k.shape[2]
    n_exp = router_w.shape[2]
    hd = d // N_HEADS
    hist = POOL_HIST_ROWS
    n_p, n_s = b * s, db * t
    assert t >= hist and ROW_TILE % t == 0 and s % ROW_TILE == 0 and n_s % ROW_TILE == 0 and ROW_TILE % QKV_TILE == 0
    row = lambda a: a.reshape(1, -1)
    wr = [router_w[i].T for i in range(2)]
    br = [router_b[i].reshape(n_exp, 1) for i in range(2)]

    hist_s = jnp.concatenate([jnp.zeros((db, 1, d), F32), state_pool[0]], axis=1)
    xn, tok, idx, gates, rank, hlast_p, hlast_s, cnt = _pool_layer(
        x_prompt, x_sample, hist_s, past, row(norm_mix[0]), pool_w[0].astype(BF16), row(pool_scale[0]),
        row(norm_ffn[0]), wr[0], br[0])
    x1, h1 = _moe(tok, idx, gates, rank, cnt, xn, row(norm_mix[1]), None, 0, moe_w1, moe_b1[0], moe_w2, moe_b2[0])

    wqkv = w_qkv[0].astype(BF16)
    q, ktb, vtb, ktf, vtf, ksb, vsb, ksf, vsf = _qkv(h1, wqkv[:, :d], wqkv[:, d:2 * d].T, wqkv[:, 2 * d:].T, b, s)
    o_p = _attention_prompt(q, ktb, vtb)
    transposed = lambda cache: cache.transpose(0, 1, 3, 4, 2).reshape(db * d, past)
    o_s = _attention_sample(q, ksb, vsb, transposed(cache_k), transposed(cache_v), t, n_p)
    xn, tok, idx, gates, rank, cnt = _proj_layer(o_p, o_s, x1, w_o[0].astype(BF16), row(norm_ffn[1]), wr[1], br[1])
    y_p, y_s = _moe(tok, idx, gates, rank, cnt, xn, row(final_norm), n_p, 1, moe_w1, moe_b1[1], moe_w2, moe_b2[1])

    frames_major = lambda xt: xt.reshape(1, b, N_HEADS, hd, s).transpose(0, 1, 4, 2, 3)
    heads = lambda a: a.reshape(1, db, t, N_HEADS, hd)
    return (y_p.reshape(b, s, d), y_s.reshape(db, t, d), hlast_p[None, :, 1:, :],
            frames_major(ktf), frames_major(vtf), hlast_s[None, :, 1:, :], heads(ksf), heads(vsf))
```

```python
import functools

import jax
import jax.numpy as jnp
from jax import lax
from jax.experimental import pallas as pl
from jax.experimental.pallas import tpu as pltpu
from jax.experimental.pallas import tpu_sc as plsc

EPS = 1e-5
POOL_WINDOWS = (2, 4, 8, 16)
POOL_HIST_ROWS = 16
N_HEADS = 16
TOP_K = 4
SWIGLU_LIMIT = 7.0
SWIGLU_ALPHA = 1.702
LANES = 128
ROW_TILE = 512
QKV_TILE = 512
ATTN_TQ = 128
ATTN_TK = 128
ATTN_EXIT = 88.0
VMEM_LIMIT = 56 * 1024 * 1024
SC_CORES = 2
SC_SUBCORES = 16
SC_CHUNK = 96
SC_GATHER_CHUNK = 48
COMBINE_TILE = 512

F32 = jnp.float32
BF16 = jnp.bfloat16


def _rms(x, g):
    ms = jnp.mean(x * x, axis=-1, keepdims=True)
    return x * lax.rsqrt(ms + EPS) * g


def _dot(a, b):
    return jnp.dot(a, b, preferred_element_type=F32)


def _dot_nt(a, b, precision=None):
    return lax.dot_general(a, b, (((1,), (1,)), ((), ())), preferred_element_type=F32, precision=precision)


def _pack_halves(x):
    c = x.shape[1] // 2
    bits = lax.bitcast_convert_type(x.astype(BF16).astype(F32), jnp.uint32)
    return bits[:, :c] | (bits[:, c:] >> 16)


def _unpack_halves(w):
    return (lax.bitcast_convert_type(w & jnp.uint32(0xFFFF0000), F32), lax.bitcast_convert_type(w << 16, F32))


def _params(semantics):
    return pltpu.CompilerParams(dimension_semantics=semantics, vmem_limit_bytes=VMEM_LIMIT)


def _full(shape):
    return pl.BlockSpec(shape, lambda i, *_: (0,) * len(shape))


def _route_init(cnt_ref, tri_ref):
    r = tri_ref.shape[0]
    cnt_ref[...] = jnp.zeros(cnt_ref.shape, F32)
    tri_ref[...] = jnp.where(lax.broadcasted_iota(jnp.int32, (r, r), 0) < lax.broadcasted_iota(jnp.int32, (r, r), 1),
                             1.0, 0.0).astype(BF16)


def _route_tail(xn, gffn_ref, wr_ref, br_ref, cnt_ref, tri_ref, tok_ref, idx_ref, gate_ref, rank_ref):
    n_exp = br_ref.shape[0]
    tok = _rms(xn, gffn_ref[...])
    tok_ref[...] = _pack_halves(tok)
    tok_hi = tok.astype(BF16)
    tok_lo = (tok - tok_hi.astype(F32)).astype(BF16)
    by_hi = _dot_nt(wr_ref[...], tok_hi)
    logits = by_hi[:n_exp] + by_hi[n_exp:] + _dot_nt(wr_ref[:n_exp, :], tok_lo) + br_ref[...]
    eidx = lax.broadcasted_iota(jnp.int32, logits.shape, 0).astype(F32)
    vals, idxs = [], []
    l = logits
    for _ in range(TOP_K):
        m = jnp.max(l, axis=0, keepdims=True)
        i = jnp.min(jnp.where(l == m, eidx, float(n_exp)), axis=0, keepdims=True)
        vals.append(m)
        idxs.append(i)
        l = jnp.where(eidx == i, -jnp.inf, l)
    es = [jnp.exp(v - vals[0]) for v in vals]
    den = es[0]
    for e in es[1:]:
        den = den + e
    gate_ref[...] = jnp.concatenate([e / den for e in es], axis=0)
    idx_ref[...] = jnp.concatenate(idxs, axis=0).astype(jnp.int32)
    member = jnp.zeros(logits.shape, F32)
    for i in idxs:
        member = member + jnp.where(eidx == i, 1.0, 0.0)
    before = _dot(member.astype(BF16), tri_ref[...]) + cnt_ref[:, :1]
    ranks = [jnp.sum(jnp.where(eidx == i, before, 0.0), axis=0, keepdims=True) for i in idxs]
    rank_ref[...] = jnp.concatenate(ranks, axis=0).astype(jnp.int32)
    cnt_ref[...] = cnt_ref[...] + jnp.sum(member, axis=1, keepdims=True)


def _pool_mix(h, ext_ref, pos, pw_ref, ps_ref):
    ts, d = h.shape[-2:]
    hist = POOL_HIST_ROWS
    group = d // len(POOL_WINDOWS)
    pre = (slice(None),) * (h.ndim - 2)
    ys = []
    for g, win in enumerate(POOL_WINDOWS):
        cols = slice(g * group, (g + 1) * group)
        hg = h[pre + (slice(None), cols)]
        acc = hg
        for j in range(1, win):
            acc = acc + ext_ref[pre + (slice(hist - j, hist - j + ts), cols)]
        cnt = jnp.minimum(pos + 1, win).astype(F32)
        dg = acc / cnt - hg
        ys.append(_dot(dg.reshape(-1, group).astype(BF16), pw_ref[g]))
    return jnp.concatenate(ys, axis=-1) * ps_ref[...]


def _pool_mix_tiled(h, ext_ref, lvl_ref, pos, pw_ref, ps_ref):
    ts, d = h.shape
    hist = POOL_HIST_ROWS
    group = d // len(POOL_WINDOWS)
    prev_ref, prev_col0 = ext_ref, 0
    ys = []
    for g, win in enumerate(POOL_WINDOWS):
        assert win == 2 * (POOL_WINDOWS[g - 1] if g else 1)
        col0 = g * group
        cols = slice(col0 - prev_col0, d - prev_col0)
        level = prev_ref[hist:hist + ts, cols] + prev_ref[hist - win // 2:hist - win // 2 + ts, cols]
        if g + 1 < len(POOL_WINDOWS):
            lvl_ref[g, hist:hist + ts, 0:d - col0 - group] = level[:, group:]
            prev_ref, prev_col0 = lvl_ref.at[g], col0 + group
        cnt = jnp.minimum(pos + 1, win).astype(F32)
        dg = level[:, :group] / cnt - h[:, col0:col0 + group]
        ys.append(_dot(dg.astype(BF16), pw_ref[g]))
    return jnp.concatenate(ys, axis=-1) * ps_ref[...]


def _pool_kernel(ntp, nt, pos0_s, xp_ref, xs_ref, hist_ref, gmix_ref, pw_ref, ps_ref, gffn_ref, wr_ref, br_ref,
                 xn_ref, tok_ref, idx_ref, gate_ref, rank_ref, hlast_p_ref, hlast_s_ref, cnt_out_ref,
                 ext_p, lvl_p, ext_s, cnt_ref, tri_ref):
    i = pl.program_id(0)
    hist = POOL_HIST_ROWS

    @pl.when(i == 0)
    def _():
        _route_init(cnt_ref, tri_ref)

    @pl.when(i < ntp)
    def _():
        ts, d = xp_ref.shape
        t = lax.rem(i, nt)

        @pl.when(t == 0)
        def _():
            ext_p[0:hist, :] = jnp.zeros((hist, d), F32)
            lvl_p[:, 0:hist, :] = jnp.zeros((lvl_p.shape[0], hist, lvl_p.shape[2]), F32)

        x = xp_ref[...]
        h = _rms(x, gmix_ref[...])
        ext_p[hist:hist + ts, :] = h

        @pl.when(t == nt - 1)
        def _():
            hlast_p_ref[0] = h[ts - hist:, :]

        pos = t * ts + lax.broadcasted_iota(jnp.int32, (ts, 1), 0)
        xn_ref[...] = x + _pool_mix_tiled(h, ext_p, lvl_p, pos, pw_ref, ps_ref)
        ext_p[0:hist, :] = ext_p[ts:ts + hist, :]
        lvl_p[:, 0:hist, :] = lvl_p[:, ts:ts + hist, :]

    @pl.when(i >= ntp)
    def _():
        bb, ts, d = xs_ref.shape
        ext_s[:, 0:hist, :] = hist_ref[...]
        x = xs_ref[...]
        h = _rms(x, gmix_ref[...])
        ext_s[:, hist:hist + ts, :] = h
        hlast_s_ref[...] = h[:, ts - hist:, :]
        pos = pos0_s + lax.broadcasted_iota(jnp.int32, (1, ts, 1), 1)
        xn_ref[...] = x.reshape(bb * ts, d) + _pool_mix(h, ext_s, pos, pw_ref, ps_ref)

    _route_tail(xn_ref[...], gffn_ref, wr_ref, br_ref, cnt_ref, tri_ref, tok_ref, idx_ref, gate_ref, rank_ref)
    cnt_out_ref[...] = cnt_ref[...]


def _pool_layer(x_prompt, x_sample, hist_s, pos0_s, gmix, pw, ps, gffn, wr, br):
    b, s, d = x_prompt.shape
    db, t, _ = x_sample.shape
    n_exp = br.shape[0]
    r = ROW_TILE
    nt = s // r
    ntp = b * nt
    bb = r // t
    nts = db // bb
    n = b * s + db * t
    hist = POOL_HIST_ROWS
    rows = pl.BlockSpec((r, d), lambda i: (i, 0))
    lanes = pl.BlockSpec((TOP_K, r), lambda i: (0, i))
    sample_blk = lambda i: (jnp.maximum(i - ntp, 0), 0, 0)
    return pl.pallas_call(
        functools.partial(_pool_kernel, ntp, nt, pos0_s),
        grid=(ntp + nts,),
        in_specs=[pl.BlockSpec((r, d), lambda i: (jnp.minimum(i, ntp - 1), 0)),
                  pl.BlockSpec((bb, t, d), sample_blk), pl.BlockSpec((bb, hist, d), sample_blk),
                  _full((1, d)), _full(pw.shape), _full((1, d)), _full((1, d)), _full((2 * n_exp, d)), _full((n_exp, 1))],
        out_specs=[rows, pl.BlockSpec((r, d // 2), lambda i: (i, 0)), lanes, lanes, lanes,
                   pl.BlockSpec((1, hist, d), lambda i: (jnp.minimum(i // nt, b - 1), 0, 0)),
                   pl.BlockSpec((bb, hist, d), sample_blk), _full((n_exp, LANES))],
        out_shape=[jax.ShapeDtypeStruct((n, d), F32), jax.ShapeDtypeStruct((n, d // 2), jnp.uint32),
                   jax.ShapeDtypeStruct((TOP_K, n), jnp.int32), jax.ShapeDtypeStruct((TOP_K, n), F32),
                   jax.ShapeDtypeStruct((TOP_K, n), jnp.int32),
                   jax.ShapeDtypeStruct((b, hist, d), F32), jax.ShapeDtypeStruct((db, hist, d), F32),
                   jax.ShapeDtypeStruct((n_exp, LANES), F32)],
        scratch_shapes=[pltpu.VMEM((hist + r, d), F32),
                        pltpu.VMEM((len(POOL_WINDOWS) - 1, hist + r, d - d // len(POOL_WINDOWS)), F32),
                        pltpu.VMEM((bb, hist + t, d), F32), pltpu.VMEM((n_exp, LANES), F32), pltpu.VMEM((r, r), BF16)],
        compiler_params=_params(("arbitrary",)),
        name="pool_route",
    )(x_prompt.reshape(b * s, d), x_sample, hist_s, gmix, pw, ps, gffn, wr, br)


def _proj_kernel(ntp, op_ref, os_ref, x_ref, wo_ref, gffn_ref, wr_ref, br_ref,
                 xn_ref, tok_ref, idx_ref, gate_ref, rank_ref, cnt_out_ref, cnt_ref, tri_ref):
    i = pl.program_id(0)

    @pl.when(i == 0)
    def _():
        _route_init(cnt_ref, tri_ref)

    o = jnp.where(i < ntp, op_ref[...], os_ref[...])
    xn = x_ref[...] + _dot(o, wo_ref[...])
    xn_ref[...] = xn
    _route_tail(xn, gffn_ref, wr_ref, br_ref, cnt_ref, tri_ref, tok_ref, idx_ref, gate_ref, rank_ref)
    cnt_out_ref[...] = cnt_ref[...]


def _proj_layer(o_p, o_s, x, wo, gffn, wr, br):
    n, d = x.shape
    n_exp = br.shape[0]
    r = ROW_TILE
    ntp = o_p.shape[0] // r
    rows = pl.BlockSpec((r, d), lambda i: (i, 0))
    lanes = pl.BlockSpec((TOP_K, r), lambda i: (0, i))
    return pl.pallas_call(
        functools.partial(_proj_kernel, ntp),
        grid=(n // r,),
        in_specs=[pl.BlockSpec((r, d), lambda i: (jnp.minimum(i, ntp - 1), 0)),
                  pl.BlockSpec((r, d), lambda i: (jnp.maximum(i - ntp, 0), 0)),
                  rows, _full((d, d)), _full((1, d)), _full((2 * n_exp, d)), _full((n_exp, 1))],
        out_specs=[rows, pl.BlockSpec((r, d // 2), lambda i: (i, 0)), lanes, lanes, lanes, _full((n_exp, LANES))],
        out_shape=[jax.ShapeDtypeStruct((n, d), F32), jax.ShapeDtypeStruct((n, d // 2), jnp.uint32),
                   jax.ShapeDtypeStruct((TOP_K, n), jnp.int32), jax.ShapeDtypeStruct((TOP_K, n), F32),
                   jax.ShapeDtypeStruct((TOP_K, n), jnp.int32), jax.ShapeDtypeStruct((n_exp, LANES), F32)],
        scratch_shapes=[pltpu.VMEM((n_exp, LANES), F32), pltpu.VMEM((r, r), BF16)],
        compiler_params=_params(("arbitrary",)),
        name="proj_route",
    )(o_p, o_s, x, wo, gffn, wr, br)


def _sc_chunks(dest, w):
    n = dest.shape[1]
    return dest.reshape(TOP_K, n // w, w).transpose(1, 0, 2)


def _dispatch(tok, dest, n_buf_rows):
    n, d = tok.shape
    w = SC_CHUNK
    n_workers = SC_CORES * SC_SUBCORES
    per_worker = n // (w * n_workers)
    assert per_worker * w * n_workers == n
    mesh = plsc.VectorSubcoreMesh(core_axis_name="core", subcore_axis_name="subcore",
                                  num_cores=SC_CORES, num_subcores=SC_SUBCORES)

    @functools.partial(
        pl.kernel, mesh=mesh, out_type=jax.ShapeDtypeStruct((n_buf_rows, d), tok.dtype),
        scratch_types=[pltpu.VMEM((TOP_K, w), jnp.int32), pltpu.VMEM((w, d), tok.dtype), pltpu.SemaphoreType.DMA],
        name="moe_dispatch_sc")
    def scatter_rows(tok_hbm, dest_hbm, buf_hbm, idx_v, rows_v, sem):
        worker = lax.axis_index("subcore") * SC_CORES + lax.axis_index("core")

        @pl.loop(0, per_worker)
        def _(c):
            chunk = worker * per_worker + c
            pltpu.sync_copy(dest_hbm.at[chunk], idx_v)
            pltpu.sync_copy(tok_hbm.at[pl.ds(pl.multiple_of(chunk * w, w), w)], rows_v)
            copies = [pltpu.async_copy(rows_v, buf_hbm.at[idx_v.at[k]], sem) for k in range(TOP_K)]
            for cp in copies:
                cp.wait()

    return scatter_rows(tok, _sc_chunks(dest, w))


def _expert_kernel(layer, be_ref, valid_ref, nused_ref, next_ref, slot_ref, x_ref, w1_hbm, b1_ref, w2_hbm, b2_ref,
                   o_ref, w1f_ref, w2f_ref, w1b_ref, w2b_ref, sem):
    i = pl.program_id(0)
    used = valid_ref[i] > 0

    def fetch(expert, slot):
        return (pltpu.make_async_copy(w1_hbm.at[layer, expert], w1f_ref.at[slot], sem.at[0, slot]),
                pltpu.make_async_copy(w2_hbm.at[layer, expert], w2f_ref.at[slot], sem.at[1, slot]))

    @pl.when(used & ((i == 0) | (be_ref[i] != be_ref[jnp.maximum(i - 1, 0)])))
    def _():
        slot = slot_ref[i]

        @pl.when(i == 0)
        def _():
            for cp in fetch(be_ref[i], slot):
                cp.start()

        for cp in fetch(be_ref[i], slot):
            cp.wait()

        @pl.when(next_ref[i] >= 0)
        def _():
            for cp in fetch(next_ref[i], 1 - slot):
                cp.start()

        w1b_ref[...] = w1f_ref[slot].astype(BF16)
        w2b_ref[...] = w2f_ref[slot].astype(BF16)

    @pl.when(used)
    def _():
        f = w2b_ref.shape[0]
        row = lax.broadcasted_iota(jnp.int32, (x_ref.shape[0], 1), 0)
        x = jnp.where(row < valid_ref[i], x_ref[...], jnp.uint32(0))
        xa, xb = _unpack_halves(x)
        gu = _dot(jnp.concatenate([xa.astype(BF16), xb.astype(BF16)], axis=1), w1b_ref[...]) + b1_ref[0]
        gate = jnp.minimum(gu[:, :f], SWIGLU_LIMIT)
        up = jnp.clip(gu[:, f:], -SWIGLU_LIMIT, SWIGLU_LIMIT)
        hid = (up + 1.0) * (gate * jax.nn.sigmoid(gate * SWIGLU_ALPHA))
        o_ref[...] = _pack_halves(_dot(hid.astype(BF16), w2b_ref[...]) + b2_ref[0])

    @pl.when(jnp.logical_not(used))
    def _():
        o_ref[...] = jnp.zeros(o_ref.shape, o_ref.dtype)


def _experts(buf, block_expert, valid, nused, next_expert, slot, layer, w1, b1, w2, b2, tm):
    rows, dw = buf.shape
    _, n_exp, d, f2 = w1.shape
    f = w2.shape[2]
    in_rows = lambda i, be, va, nu, nx, sl: (jnp.minimum(i, nu[0] - 1), 0)
    by_expert = lambda i, be, va, nu, nx, sl: (be[i], 0, 0)
    any_spec = pl.BlockSpec(memory_space=pl.ANY)
    return pl.pallas_call(
        functools.partial(_expert_kernel, layer),
        grid_spec=pltpu.PrefetchScalarGridSpec(
            num_scalar_prefetch=5,
            grid=(rows // tm,),
            in_specs=[pl.BlockSpec((tm, dw), in_rows), any_spec, pl.BlockSpec((1, 1, f2), by_expert),
                      any_spec, pl.BlockSpec((1, 1, d), by_expert)],
            out_specs=pl.BlockSpec((tm, dw), lambda i, be, va, nu, nx, sl: (i, 0)),
            scratch_shapes=[pltpu.VMEM((2, d, f2), F32), pltpu.VMEM((2, f, d), F32),
                            pltpu.VMEM((d, f2), BF16), pltpu.VMEM((f, d), BF16), pltpu.SemaphoreType.DMA((2, 2))]),
        out_shape=jax.ShapeDtypeStruct((rows, dw), jnp.uint32),
        compiler_params=_params(("arbitrary",)),
        name="moe_experts",
    )(block_expert, valid, nused, next_expert, slot, buf, w1, b1.reshape(n_exp, 1, f2), w2, b2.reshape(n_exp, 1, d))


def _gather_rows(ybuf, dest):
    _, d = ybuf.shape
    n = dest.shape[1]
    w = SC_GATHER_CHUNK
    n_workers = SC_CORES * SC_SUBCORES
    per_worker = n // (w * n_workers)
    assert per_worker * w * n_workers == n and TOP_K % 2 == 0
    mesh = plsc.VectorSubcoreMesh(core_axis_name="core", subcore_axis_name="subcore",
                                  num_cores=SC_CORES, num_subcores=SC_SUBCORES)

    @functools.partial(
        pl.kernel, mesh=mesh, out_type=jax.ShapeDtypeStruct((TOP_K, n, d), ybuf.dtype),
        scratch_types=[pltpu.VMEM((TOP_K, w), jnp.int32), pltpu.VMEM((2, w, d), ybuf.dtype),
                       pltpu.SemaphoreType.DMA((2,)), pltpu.SemaphoreType.DMA((2,))],
        name="moe_gather_sc")
    def gather_rows(y_hbm, dest_hbm, out_hbm, idx_v, rows_v, sem_in, sem_out):
        worker = lax.axis_index("subcore") * SC_CORES + lax.axis_index("core")

        @pl.loop(0, per_worker)
        def _(c):
            chunk = worker * per_worker + c
            tokens = pl.ds(pl.multiple_of(chunk * w, w), w)
            pltpu.sync_copy(dest_hbm.at[chunk], idx_v)
            fetch = lambda k: pltpu.async_copy(y_hbm.at[idx_v.at[k]], rows_v.at[k % 2], sem_in.at[k % 2])
            store = lambda k: pltpu.async_copy(rows_v.at[k % 2], out_hbm.at[k, tokens], sem_out.at[k % 2])
            fetches = [fetch(0), fetch(1)]
            stores = []
            for k in range(TOP_K):
                fetches[k].wait()
                stores.append(store(k))
                if k + 2 < TOP_K:
                    stores[k].wait()
                    fetches.append(fetch(k + 2))
            for k in range(TOP_K - 2, TOP_K):
                stores[k].wait()

    return gather_rows(ybuf, _sc_chunks(dest, w))


def _combine_kernel(ntp, rows_ref, gate_ref, x_ref, g_ref, *outs):
    i = pl.program_id(0)
    r = x_ref.shape[0]
    gates = jnp.concatenate([gate_ref[...], jnp.zeros((LANES - TOP_K, r), F32)], axis=0)
    gates_t = jnp.transpose(gates)
    halves = [_unpack_halves(rows_ref[k]) for k in range(TOP_K)]
    f = []
    for side in range(2):
        acc = halves[0][side] * gates_t[:, 0:1]
        for k in range(1, TOP_K):
            acc = acc + halves[k][side] * gates_t[:, k:k + 1]
        f.append(acc)
    xo = x_ref[...] + jnp.concatenate(f, axis=1)
    hn = _rms(xo, g_ref[...])
    if ntp is None:
        xo_ref, hn_ref = outs
        xo_ref[...] = xo
        hn_ref[...] = hn.astype(hn_ref.dtype)
    else:
        hp_ref, hs_ref = outs

        @pl.when(i < ntp)
        def _():
            hp_ref[...] = hn

        @pl.when(i >= ntp)
        def _():
            hs_ref[...] = hn


def _combine(rows4, gates, x, g, n_p):
    n, d = x.shape
    r = COMBINE_TILE
    rows = pl.BlockSpec((r, d), lambda i: (i, 0))
    if n_p is None:
        ntp = None
        out_specs = [rows, rows]
        out_shape = [jax.ShapeDtypeStruct((n, d), F32), jax.ShapeDtypeStruct((n, d), BF16)]
    else:
        ntp = n_p // r
        out_specs = [pl.BlockSpec((r, d), lambda i: (jnp.minimum(i, ntp - 1), 0)),
                     pl.BlockSpec((r, d), lambda i: (jnp.maximum(i - ntp, 0), 0))]
        out_shape = [jax.ShapeDtypeStruct((n_p, d), F32), jax.ShapeDtypeStruct((n - n_p, d), F32)]
    return pl.pallas_call(
        functools.partial(_combine_kernel, ntp),
        grid=(n // r,),
        in_specs=[pl.BlockSpec((TOP_K, r, d // 2), lambda i: (0, i, 0)), pl.BlockSpec((TOP_K, r), lambda i: (0, i)),
                  rows, _full((1, d))],
        out_specs=out_specs,
        out_shape=out_shape,
        compiler_params=_params(("arbitrary",)),
        name="moe_combine",
    )(rows4, gates, x, g)


def _moe(tok, idx, gates, rank, cnt, x, g_next, n_p, layer, w1, b1, w2, b2):
    n, d = tok.shape
    n_exp = w1.shape[1]
    pairs = n * TOP_K
    tm = 128
    for cand in (512, 256):
        if pairs >= 4 * cand * n_exp and pairs % cand == 0:
            tm = cand
            break
    n_blocks = -(-pairs // tm) + n_exp
    counts = cnt[:, 0].astype(jnp.int32)
    padded = (counts + tm - 1) // tm * tm
    pad_end = jnp.cumsum(padded)
    pad_start = pad_end - padded
    experts = jnp.arange(n_exp, dtype=jnp.int32)
    dest = rank + jnp.sum(jnp.where(idx[:, :, None] == experts, pad_start, 0), axis=-1)
    starts = jnp.arange(n_blocks, dtype=jnp.int32) * tm
    nused = (pad_end[-1:] // tm).astype(jnp.int32)
    block_expert = jnp.minimum(jnp.sum((pad_end[None, :] <= starts[:, None]).astype(jnp.int32), axis=1), n_exp - 1)
    tokens_end = jnp.sum(jnp.where(block_expert[:, None] == experts, pad_start + counts, 0), axis=-1)
    valid = jnp.clip(tokens_end - starts, 0, tm).astype(jnp.int32)
    has_tokens = counts > 0
    later = has_tokens[None, :] & (experts[None, :] > experts[:, None])
    successor = jnp.min(jnp.where(later, experts[None, :], n_exp), axis=1)
    successor = jnp.where(successor == n_exp, -1, successor)
    parity = (jnp.cumsum(has_tokens.astype(jnp.int32)) - 1) % 2
    of_block = lambda table: jnp.sum(jnp.where(block_expert[:, None] == experts, table, 0), axis=-1).astype(jnp.int32)
    buf = _dispatch(tok, dest, n_blocks * tm)
    ybuf = _experts(buf, block_expert, valid, nused, of_block(successor), of_block(parity), layer, w1, b1, w2, b2, tm)
    return _combine(_gather_rows(ybuf, dest), gates, x, g_next, n_p)


def _qkv_kernel(ntp, h_ref, wq_ref, wkt_ref, wvt_ref, q_ref, ktb_ref, vtb_ref, ktf_ref, vtf_ref,
                ksb_ref, vsb_ref, ksf_ref, vsf_ref):
    i = pl.program_id(0)
    r, d = h_ref.shape
    tk = ktb_ref.shape[-1]
    h = h_ref[...]
    q_ref[...] = (_dot(h, wq_ref[...]) * ((d // N_HEADS) ** -0.5)).astype(BF16)

    @pl.when(i < ntp)
    def _():
        for wt_ref, tb_ref, tf_ref in ((wkt_ref, ktb_ref, ktf_ref), (wvt_ref, vtb_ref, vtf_ref)):
            xt = _dot_nt(wt_ref[...], h)
            tf_ref[...] = xt
            for c in range(r // tk):
                tb_ref[0, c] = xt[:, c * tk:(c + 1) * tk].astype(BF16)

    @pl.when(i >= ntp)
    def _():
        for wt_ref, sb_ref, sf_ref in ((wkt_ref, ksb_ref, ksf_ref), (wvt_ref, vsb_ref, vsf_ref)):
            x = _dot_nt(h, wt_ref[...])
            sf_ref[...] = x
            sb_ref[...] = x.astype(BF16)


def _qkv(h, wq, wkt, wvt, n_streams, length):
    n, d = h.shape
    r = QKV_TILE
    tk = ATTN_TK
    n_p = n_streams * length
    nt = length // r
    ntp = n_p // r
    rows = pl.BlockSpec((r, d), lambda i: (i, 0))
    prompt_blk = lambda i: (jnp.minimum(i, ntp - 1) // nt, lax.rem(jnp.minimum(i, ntp - 1), nt))
    t_blocks = pl.BlockSpec((1, r // tk, d, tk), lambda i: prompt_blk(i) + (0, 0))
    t_full = pl.BlockSpec((d, r), prompt_blk)
    sample = pl.BlockSpec((r, d), lambda i: (jnp.maximum(i - ntp, 0), 0))
    w_spec = _full((d, d))
    return pl.pallas_call(
        functools.partial(_qkv_kernel, ntp),
        grid=(n // r,),
        in_specs=[rows, w_spec, w_spec, w_spec],
        out_specs=[rows, t_blocks, t_blocks, t_full, t_full, sample, sample, sample, sample],
        out_shape=[jax.ShapeDtypeStruct((n, d), BF16)]
        + [jax.ShapeDtypeStruct((n_streams, length // tk, d, tk), BF16)] * 2
        + [jax.ShapeDtypeStruct((n_streams * d, length), F32)] * 2
        + [jax.ShapeDtypeStruct((n - n_p, d), BF16)] * 2 + [jax.ShapeDtypeStruct((n - n_p, d), F32)] * 2,
        compiler_params=_params(("arbitrary",)),
        name="qkv",
    )(h, wq, wkt, wvt)


def _attn_setup(q_ref, qm_ref):
    tq, d = q_ref.shape
    lane = lax.broadcasted_iota(jnp.int32, (tq, LANES), 1)
    for p in range(d // LANES):
        qp = q_ref[:, p * LANES:(p + 1) * LANES]
        qm_ref[p, 0:tq, :] = jnp.where(lane < LANES // 2, qp, jnp.zeros_like(qp))
        qm_ref[p, tq:2 * tq, :] = jnp.where(lane >= LANES // 2, qp, jnp.zeros_like(qp))


def _suffix_sum_matrix(tk):
    src = lax.rem(lax.broadcasted_iota(jnp.int32, (2 * tk, 2 * tk), 0), tk)
    dst = lax.broadcasted_iota(jnp.int32, (2 * tk, 2 * tk), 1)
    return jnp.where((dst >= tk) | (src > dst), 1.0, 0.0).astype(BF16)


def _attn_block(qm_ref, carry_ref, acc_ref, keys, values, transposed, mask, sums):
    n_pairs, tq2, tk = carry_ref.shape
    tq = tq2 // 2
    lane = lax.broadcasted_iota(jnp.int32, (tq, LANES), 1)
    first = mask is not None
    visible = (lambda x: jnp.where(mask, x, 0.0)) if first else (lambda x: x)
    scores = _dot if transposed else _dot_nt
    mix = _dot_nt if transposed else _dot
    zs = [scores(qm_ref[p], keys[p]) for p in range(n_pairs)]
    log_beta, parts = [], []
    for z in zs:
        sp = jnp.maximum(z, 0.0) + jnp.log(1.0 + jnp.exp(-jnp.abs(z)))
        log_keep = visible(-sp)
        hi = log_keep.astype(BF16)
        lo = (log_keep - hi.astype(F32)).astype(BF16)
        parts.append(jnp.concatenate([hi, lo], axis=1))
        log_beta.append(z - sp)
    sums_out = [_dot(part, sums) for part in parts]
    weights = []
    top = jnp.full((tq2, tk), -jnp.inf, F32)
    for p in range(n_pairs):
        after, carry = sums_out[p][:, :tk], sums_out[p][:, tk:]
        if not first:
            after = after + carry_ref[p]
            carry = carry + carry_ref[p]
        weights.append(visible(jnp.exp(log_beta[p] + after)).astype(BF16))
        carry_ref[p] = carry
        top = jnp.maximum(top, carry)
    for p in range(n_pairs):
        out = mix(weights[p], values[p])
        out = jnp.where(lane < LANES // 2, out[:tq], out[tq:])
        cols = slice(p * LANES, (p + 1) * LANES)
        acc_ref[:, cols] = out if first else acc_ref[:, cols] + out
    return jnp.max(top)


def _keep_sweeping(state):
    j, top = state
    return (j >= 0) & (top > -ATTN_EXIT)


def _attn_prompt_kernel(q_ref, kt_ref, vt_ref, o_ref, qm_ref, carry_ref, acc_ref):
    qi = pl.program_id(1)
    tq, d = q_ref.shape
    tk = kt_ref.shape[-1]
    n_pairs = d // LANES
    _attn_setup(q_ref, qm_ref)
    row_pos = qi * tq + lax.rem(lax.broadcasted_iota(jnp.int32, (2 * tq, tk), 0), tq)
    col = lax.broadcasted_iota(jnp.int32, (2 * tq, tk), 1)
    sums = _suffix_sum_matrix(tk)

    def block(j, mask):
        keys = [kt_ref[0, j, p * LANES:(p + 1) * LANES, :] for p in range(n_pairs)]
        values = [vt_ref[0, j, p * LANES:(p + 1) * LANES, :] for p in range(n_pairs)]
        return _attn_block(qm_ref, carry_ref, acc_ref, keys, values, True, mask, sums)

    j0 = ((qi + 1) * tq - 2) // tk
    top = block(j0, (j0 * tk + col) < row_pos)
    lax.while_loop(_keep_sweeping, lambda state: (state[0] - 1, block(state[0], None)), (j0 - 1, top))
    o_ref[...] = acc_ref[...].astype(o_ref.dtype)


def _attention_prompt(q, kt, vt):
    n_streams, n_kblocks, d, tk = kt.shape
    length = n_kblocks * tk
    tq = min(length, ATTN_TQ)
    assert tk % tq == 0
    nq = length // tq
    kv = pl.BlockSpec((1, n_kblocks, d, tk), lambda b, i: (b, 0, 0, 0))
    rows = pl.BlockSpec((tq, d), lambda b, i: (b * nq + i, 0))
    return pl.pallas_call(
        _attn_prompt_kernel,
        grid=(n_streams, nq),
        in_specs=[rows, kv, kv],
        out_specs=rows,
        out_shape=jax.ShapeDtypeStruct((n_streams * length, d), BF16),
        scratch_shapes=[pltpu.VMEM((d // LANES, 2 * tq, LANES), BF16), pltpu.VMEM((d // LANES, 2 * tq, tk), F32),
                        pltpu.VMEM((tq, d), F32)],
        compiler_params=_params(("arbitrary", "arbitrary")),
        name="stick_breaking_prompt",
    )(q, kt, vt)


def _attn_sample_kernel(q_ref, kn_ref, vn_ref, ck_hbm, cv_hbm, o_ref, qm_ref, carry_ref, acc_ref, kbuf, vbuf, sem):
    b = pl.program_id(0)
    t, d = q_ref.shape
    tk = kbuf.shape[-1]
    n_pairs = d // LANES
    n_cache_blocks = ck_hbm.shape[1] // tk

    def fetch(j, slot):
        src = (pl.ds(pl.multiple_of(b * d, d), d), pl.ds(pl.multiple_of(j * tk, tk), tk))
        return (pltpu.make_async_copy(ck_hbm.at[src], kbuf.at[slot], sem.at[0, slot]),
                pltpu.make_async_copy(cv_hbm.at[src], vbuf.at[slot], sem.at[1, slot]))

    for cp in fetch(n_cache_blocks - 1, (n_cache_blocks - 1) % 2):
        cp.start()
    _attn_setup(q_ref, qm_ref)
    row = lax.rem(lax.broadcasted_iota(jnp.int32, (2 * t, tk), 0), t)
    col = lax.broadcasted_iota(jnp.int32, (2 * t, tk), 1)
    sums = _suffix_sum_matrix(tk)
    pad = jnp.zeros((tk - t, LANES), BF16)
    keys = [jnp.concatenate([kn_ref[:, p * LANES:(p + 1) * LANES], pad], axis=0) for p in range(n_pairs)]
    values = [jnp.concatenate([vn_ref[:, p * LANES:(p + 1) * LANES], pad], axis=0) for p in range(n_pairs)]
    top = _attn_block(qm_ref, carry_ref, acc_ref, keys, values, False, col < row, sums)

    def body(state):
        j, _ = state
        slot = lax.rem(j, 2)
        for cp in fetch(j, slot):
            cp.wait()

        @pl.when(j > 0)
        def _():
            for cp in fetch(j - 1, 1 - slot):
                cp.start()

        keys = [kbuf[slot, p * LANES:(p + 1) * LANES, :].astype(BF16) for p in range(n_pairs)]
        values = [vbuf[slot, p * LANES:(p + 1) * LANES, :].astype(BF16) for p in range(n_pairs)]
        return j - 1, _attn_block(qm_ref, carry_ref, acc_ref, keys, values, True, None, sums)

    j_end, _ = lax.while_loop(_keep_sweeping, body, (jnp.int32(n_cache_blocks - 1), top))

    @pl.when(j_end >= 0)
    def _():
        for cp in fetch(j_end, lax.rem(j_end, 2)):
            cp.wait()

    o_ref[...] = acc_ref[...].astype(o_ref.dtype)


def _attention_sample(q, k_new, v_new, cache_kt, cache_vt, t, q_row_off):
    d = q.shape[1]
    n_streams = k_new.shape[0] // t
    tk = ATTN_TK
    assert cache_kt.shape[1] % tk == 0 and t <= tk
    any_spec = pl.BlockSpec(memory_space=pl.ANY)
    new = pl.BlockSpec((t, d), lambda b: (b, 0))
    return pl.pallas_call(
        _attn_sample_kernel,
        grid=(n_streams,),
        in_specs=[pl.BlockSpec((t, d), lambda b: (q_row_off // t + b, 0)), new, new, any_spec, any_spec],
        out_specs=new,
        out_shape=jax.ShapeDtypeStruct((n_streams * t, d), BF16),
        scratch_shapes=[pltpu.VMEM((d // LANES, 2 * t, LANES), BF16), pltpu.VMEM((d // LANES, 2 * t, tk), F32),
                        pltpu.VMEM((t, d), F32), pltpu.VMEM((2, d, tk), F32), pltpu.VMEM((2, d, tk), F32),
                        pltpu.SemaphoreType.DMA((2, 2))],
        compiler_params=_params(("arbitrary",)),
        name="stick_breaking_sample",
    )(q, k_new, v_new, cache_kt, cache_vt)


def kernel(x_prompt, x_sample, state_pool, cache_k, cache_v, norm_mix, norm_ffn, pool_w, pool_scale, w_qkv, w_o, router_w, router_b, moe_w1, moe_b1, moe_w2, moe_b2, final_norm):
    b, s, d = x_prompt.shape
    db, t, _ = x_sample.shape
    past = cache_k.shape[2]
    n_exp = router_w.shape[2]
    hd = d // N_HEADS
    hist = POOL_HIST_ROWS
    n_p, n_s = b * s, db * t
    assert t >= hist and ROW_TILE % t == 0 and s % ROW_TILE == 0 and n_s % ROW_TILE == 0 and ROW_TILE % QKV_TILE == 0
    row = lambda a: a.reshape(1, -1)
    def split_router(w):
        w_t = w.T
        hi = w_t.astype(BF16)
        return jnp.concatenate([hi, (w_t - hi.astype(F32)).astype(BF16)], axis=0)

    wr = [split_router(router_w[i]) for i in range(2)]
    br = [router_b[i].reshape(n_exp, 1) for i in range(2)]

    hist_s = jnp.concatenate([jnp.zeros((db, 1, d), F32), state_pool[0]], axis=1)
    xn, tok, idx, gates, rank, hlast_p, hlast_s, cnt = _pool_layer(
        x_prompt, x_sample, hist_s, past, row(norm_mix[0]), pool_w[0].astype(BF16), row(pool_scale[0]),
        row(norm_ffn[0]), wr[0], br[0])
    x1, h1 = _moe(tok, idx, gates, rank, cnt, xn, row(norm_mix[1]), None, 0, moe_w1, moe_b1[0], moe_w2, moe_b2[0])

    wqkv = w_qkv[0].astype(BF16)
    q, ktb, vtb, ktf, vtf, ksb, vsb, ksf, vsf = _qkv(h1, wqkv[:, :d], wqkv[:, d:2 * d].T, wqkv[:, 2 * d:].T, b, s)
    o_p = _attention_prompt(q, ktb, vtb)
    transposed = lambda cache: cache.transpose(0, 1, 3, 4, 2).reshape(db * d, past)
    o_s = _attention_sample(q, ksb, vsb, transposed(cache_k), transposed(cache_v), t, n_p)
    xn, tok, idx, gates, rank, cnt = _proj_layer(o_p, o_s, x1, w_o[0].astype(BF16), row(norm_ffn[1]), wr[1], br[1])
    y_p, y_s = _moe(tok, idx, gates, rank, cnt, xn, row(final_norm), n_p, 1, moe_w1, moe_b1[1], moe_w2, moe_b2[1])

    frames_major = lambda xt: xt.reshape(1, b, N_HEADS, hd, s).transpose(0, 1, 4, 2, 3)
    heads = lambda a: a.reshape(1, db, t, N_HEADS, hd)
    return (y_p.reshape(b, s, d), y_s.reshape(db, t, d), hlast_p[None, :, 1:, :],
            frames_major(ktf), frames_major(vtf), hlast_s[None, :, 1:, :], heads(ksf), heads(vsf))
```

```python
import functools

import jax
import jax.numpy as jnp
from jax import lax
from jax.experimental import pallas as pl
from jax.experimental.pallas import tpu as pltpu
from jax.experimental.pallas import tpu_sc as plsc

EPS = 1e-5
POOL_WINDOWS = (2, 4, 8, 16)
POOL_HIST_ROWS = 16
N_HEADS = 16
TOP_K = 4
SWIGLU_LIMIT = 7.0
SWIGLU_ALPHA = 1.702
LANES = 128
ROW_TILE = 512
QKV_TILE = 512
ATTN_TQ = 128
ATTN_TK = 128
ATTN_EXIT = 88.0
VMEM_LIMIT = 56 * 1024 * 1024
SC_CORES = 2
SC_SUBCORES = 16
SC_CHUNK = 96
SC_GATHER_CHUNK = 48
COMBINE_TILE = 512

F32 = jnp.float32
BF16 = jnp.bfloat16


def _rms(x, g):
    ms = jnp.mean(x * x, axis=-1, keepdims=True)
    return x * lax.rsqrt(ms + EPS) * g


def _dot(a, b):
    return jnp.dot(a, b, preferred_element_type=F32)


def _dot_nt(a, b, precision=None):
    return lax.dot_general(a, b, (((1,), (1,)), ((), ())), preferred_element_type=F32, precision=precision)


def _pack_halves(x):
    c = x.shape[1] // 2
    bits = lax.bitcast_convert_type(x.astype(BF16).astype(F32), jnp.uint32)
    return bits[:, :c] | (bits[:, c:] >> 16)


def _unpack_halves(w):
    return (lax.bitcast_convert_type(w & jnp.uint32(0xFFFF0000), F32), lax.bitcast_convert_type(w << 16, F32))


def _params(semantics):
    return pltpu.CompilerParams(dimension_semantics=semantics, vmem_limit_bytes=VMEM_LIMIT)


def _full(shape):
    return pl.BlockSpec(shape, lambda i, *_: (0,) * len(shape))


def _route_init(cnt_ref, tri_ref):
    r = tri_ref.shape[0]
    cnt_ref[...] = jnp.zeros(cnt_ref.shape, F32)
    tri_ref[...] = jnp.where(lax.broadcasted_iota(jnp.int32, (r, r), 0) < lax.broadcasted_iota(jnp.int32, (r, r), 1),
                             1.0, 0.0).astype(BF16)


def _route_tail(xn, gffn_ref, wr_ref, br_ref, cnt_ref, tri_ref, tok_ref, idx_ref, gate_ref, rank_ref):
    n_exp = br_ref.shape[0]
    tok = _rms(xn, gffn_ref[...])
    tok_ref[...] = _pack_halves(tok)
    tok_hi = tok.astype(BF16)
    tok_lo = (tok - tok_hi.astype(F32)).astype(BF16)
    by_hi = _dot_nt(wr_ref[...], tok_hi)
    logits = by_hi[:n_exp] + by_hi[n_exp:] + _dot_nt(wr_ref[:n_exp, :], tok_lo) + br_ref[...]
    eidx = lax.broadcasted_iota(jnp.int32, logits.shape, 0).astype(F32)
    vals, idxs = [], []
    l = logits
    for _ in range(TOP_K):
        m = jnp.max(l, axis=0, keepdims=True)
        i = jnp.min(jnp.where(l == m, eidx, float(n_exp)), axis=0, keepdims=True)
        vals.append(m)
        idxs.append(i)
        l = jnp.where(eidx == i, -jnp.inf, l)
    es = [jnp.exp(v - vals[0]) for v in vals]
    den = es[0]
    for e in es[1:]:
        den = den + e
    gate_ref[...] = jnp.concatenate([e / den for e in es], axis=0)
    idx_ref[...] = jnp.concatenate(idxs, axis=0).astype(jnp.int32)
    member = jnp.zeros(logits.shape, F32)
    for i in idxs:
        member = member + jnp.where(eidx == i, 1.0, 0.0)
    before = _dot(member.astype(BF16), tri_ref[...]) + cnt_ref[:, :1]
    ranks = [jnp.sum(jnp.where(eidx == i, before, 0.0), axis=0, keepdims=True) for i in idxs]
    rank_ref[...] = jnp.concatenate(ranks, axis=0).astype(jnp.int32)
    cnt_ref[...] = cnt_ref[...] + jnp.sum(member, axis=1, keepdims=True)


def _pool_mix(h, ext_ref, pos, pw_ref, ps_ref):
    ts, d = h.shape[-2:]
    hist = POOL_HIST_ROWS
    group = d // len(POOL_WINDOWS)
    pre = (slice(None),) * (h.ndim - 2)
    ys = []
    for g, win in enumerate(POOL_WINDOWS):
        cols = slice(g * group, (g + 1) * group)
        hg = h[pre + (slice(None), cols)]
        acc = hg
        for j in range(1, win):
            acc = acc + ext_ref[pre + (slice(hist - j, hist - j + ts), cols)]
        cnt = jnp.minimum(pos + 1, win).astype(F32)
        dg = acc / cnt - hg
        ys.append(_dot(dg.reshape(-1, group).astype(BF16), pw_ref[g]))
    return jnp.concatenate(ys, axis=-1) * ps_ref[...]


def _pool_mix_tiled(h, ext_ref, lvl_ref, pos, pw_ref, ps_ref):
    ts, d = h.shape
    hist = POOL_HIST_ROWS
    group = d // len(POOL_WINDOWS)
    prev_ref, prev_col0 = ext_ref, 0
    ys = []
    for g, win in enumerate(POOL_WINDOWS):
        assert win == 2 * (POOL_WINDOWS[g - 1] if g else 1)
        col0 = g * group
        cols = slice(col0 - prev_col0, d - prev_col0)
        level = prev_ref[hist:hist + ts, cols] + prev_ref[hist - win // 2:hist - win // 2 + ts, cols]
        if g + 1 < len(POOL_WINDOWS):
            lvl_ref[g, hist:hist + ts, 0:d - col0 - group] = level[:, group:]
            prev_ref, prev_col0 = lvl_ref.at[g], col0 + group
        cnt = jnp.minimum(pos + 1, win).astype(F32)
        dg = level[:, :group] / cnt - h[:, col0:col0 + group]
        ys.append(_dot(dg.astype(BF16), pw_ref[g]))
    return jnp.concatenate(ys, axis=-1) * ps_ref[...]


def _pool_kernel(ntp, nt, pos0_s, xp_ref, xs_ref, hist_ref, gmix_ref, pw_ref, ps_ref, gffn_ref, wr_ref, br_ref,
                 xn_ref, tok_ref, idx_ref, gate_ref, rank_ref, hlast_p_ref, hlast_s_ref, cnt_out_ref,
                 ext_p, lvl_p, ext_s, cnt_ref, tri_ref):
    i = pl.program_id(0)
    hist = POOL_HIST_ROWS

    @pl.when(i == 0)
    def _():
        _route_init(cnt_ref, tri_ref)

    @pl.when(i < ntp)
    def _():
        ts, d = xp_ref.shape
        t = lax.rem(i, nt)

        @pl.when(t == 0)
        def _():
            ext_p[0:hist, :] = jnp.zeros((hist, d), F32)
            lvl_p[:, 0:hist, :] = jnp.zeros((lvl_p.shape[0], hist, lvl_p.shape[2]), F32)

        x = xp_ref[...]
        h = _rms(x, gmix_ref[...])
        ext_p[hist:hist + ts, :] = h

        @pl.when(t == nt - 1)
        def _():
            hlast_p_ref[0] = h[ts - hist:, :]

        pos = t * ts + lax.broadcasted_iota(jnp.int32, (ts, 1), 0)
        xn_ref[...] = x + _pool_mix_tiled(h, ext_p, lvl_p, pos, pw_ref, ps_ref)
        ext_p[0:hist, :] = ext_p[ts:ts + hist, :]
        lvl_p[:, 0:hist, :] = lvl_p[:, ts:ts + hist, :]

    @pl.when(i >= ntp)
    def _():
        bb, ts, d = xs_ref.shape
        ext_s[:, 0:hist, :] = hist_ref[...]
        x = xs_ref[...]
        h = _rms(x, gmix_ref[...])
        ext_s[:, hist:hist + ts, :] = h
        hlast_s_ref[...] = h[:, ts - hist:, :]
        pos = pos0_s + lax.broadcasted_iota(jnp.int32, (1, ts, 1), 1)
        xn_ref[...] = x.reshape(bb * ts, d) + _pool_mix(h, ext_s, pos, pw_ref, ps_ref)

    _route_tail(xn_ref[...], gffn_ref, wr_ref, br_ref, cnt_ref, tri_ref, tok_ref, idx_ref, gate_ref, rank_ref)
    cnt_out_ref[...] = cnt_ref[...]


def _pool_layer(x_prompt, x_sample, hist_s, pos0_s, gmix, pw, ps, gffn, wr, br):
    b, s, d = x_prompt.shape
    db, t, _ = x_sample.shape
    n_exp = br.shape[0]
    r = ROW_TILE
    nt = s // r
    ntp = b * nt
    bb = r // t
    nts = db // bb
    n = b * s + db * t
    hist = POOL_HIST_ROWS
    rows = pl.BlockSpec((r, d), lambda i: (i, 0))
    lanes = pl.BlockSpec((TOP_K, r), lambda i: (0, i))
    sample_blk = lambda i: (jnp.maximum(i - ntp, 0), 0, 0)
    return pl.pallas_call(
        functools.partial(_pool_kernel, ntp, nt, pos0_s),
        grid=(ntp + nts,),
        in_specs=[pl.BlockSpec((r, d), lambda i: (jnp.minimum(i, ntp - 1), 0)),
                  pl.BlockSpec((bb, t, d), sample_blk), pl.BlockSpec((bb, hist, d), sample_blk),
                  _full((1, d)), _full(pw.shape), _full((1, d)), _full((1, d)), _full((2 * n_exp, d)), _full((n_exp, 1))],
        out_specs=[rows, pl.BlockSpec((r, d // 2), lambda i: (i, 0)), lanes, lanes, lanes,
                   pl.BlockSpec((1, hist, d), lambda i: (jnp.minimum(i // nt, b - 1), 0, 0)),
                   pl.BlockSpec((bb, hist, d), sample_blk), _full((n_exp, LANES))],
        out_shape=[jax.ShapeDtypeStruct((n, d), F32), jax.ShapeDtypeStruct((n, d // 2), jnp.uint32),
                   jax.ShapeDtypeStruct((TOP_K, n), jnp.int32), jax.ShapeDtypeStruct((TOP_K, n), F32),
                   jax.ShapeDtypeStruct((TOP_K, n), jnp.int32),
                   jax.ShapeDtypeStruct((b, hist, d), F32), jax.ShapeDtypeStruct((db, hist, d), F32),
                   jax.ShapeDtypeStruct((n_exp, LANES), F32)],
        scratch_shapes=[pltpu.VMEM((hist + r, d), F32),
                        pltpu.VMEM((len(POOL_WINDOWS) - 1, hist + r, d - d // len(POOL_WINDOWS)), F32),
                        pltpu.VMEM((bb, hist + t, d), F32), pltpu.VMEM((n_exp, LANES), F32), pltpu.VMEM((r, r), BF16)],
        compiler_params=_params(("arbitrary",)),
        name="pool_route",
    )(x_prompt.reshape(b * s, d), x_sample, hist_s, gmix, pw, ps, gffn, wr, br)


def _proj_kernel(ntp, op_ref, os_ref, x_ref, wo_ref, gffn_ref, wr_ref, br_ref,
                 xn_ref, tok_ref, idx_ref, gate_ref, rank_ref, cnt_out_ref, cnt_ref, tri_ref):
    i = pl.program_id(0)

    @pl.when(i == 0)
    def _():
        _route_init(cnt_ref, tri_ref)

    o = jnp.where(i < ntp, op_ref[...], os_ref[...])
    xn = x_ref[...] + _dot(o, wo_ref[...])
    xn_ref[...] = xn
    _route_tail(xn, gffn_ref, wr_ref, br_ref, cnt_ref, tri_ref, tok_ref, idx_ref, gate_ref, rank_ref)
    cnt_out_ref[...] = cnt_ref[...]


def _proj_layer(o_p, o_s, x, wo, gffn, wr, br):
    n, d = x.shape
    n_exp = br.shape[0]
    r = ROW_TILE
    ntp = o_p.shape[0] // r
    rows = pl.BlockSpec((r, d), lambda i: (i, 0))
    lanes = pl.BlockSpec((TOP_K, r), lambda i: (0, i))
    return pl.pallas_call(
        functools.partial(_proj_kernel, ntp),
        grid=(n // r,),
        in_specs=[pl.BlockSpec((r, d), lambda i: (jnp.minimum(i, ntp - 1), 0)),
                  pl.BlockSpec((r, d), lambda i: (jnp.maximum(i - ntp, 0), 0)),
                  rows, _full((d, d)), _full((1, d)), _full((2 * n_exp, d)), _full((n_exp, 1))],
        out_specs=[rows, pl.BlockSpec((r, d // 2), lambda i: (i, 0)), lanes, lanes, lanes, _full((n_exp, LANES))],
        out_shape=[jax.ShapeDtypeStruct((n, d), F32), jax.ShapeDtypeStruct((n, d // 2), jnp.uint32),
                   jax.ShapeDtypeStruct((TOP_K, n), jnp.int32), jax.ShapeDtypeStruct((TOP_K, n), F32),
                   jax.ShapeDtypeStruct((TOP_K, n), jnp.int32), jax.ShapeDtypeStruct((n_exp, LANES), F32)],
        scratch_shapes=[pltpu.VMEM((n_exp, LANES), F32), pltpu.VMEM((r, r), BF16)],
        compiler_params=_params(("arbitrary",)),
        name="proj_route",
    )(o_p, o_s, x, wo, gffn, wr, br)


def _sc_chunks(dest, w):
    n = dest.shape[1]
    return dest.reshape(TOP_K, n // w, w).transpose(1, 0, 2)


def _dispatch(tok, dest, n_buf_rows):
    n, d = tok.shape
    w = SC_CHUNK
    n_workers = SC_CORES * SC_SUBCORES
    per_worker = n // (w * n_workers)
    assert per_worker * w * n_workers == n
    mesh = plsc.VectorSubcoreMesh(core_axis_name="core", subcore_axis_name="subcore",
                                  num_cores=SC_CORES, num_subcores=SC_SUBCORES)

    @functools.partial(
        pl.kernel, mesh=mesh, out_type=jax.ShapeDtypeStruct((n_buf_rows, d), tok.dtype),
        scratch_types=[pltpu.VMEM((TOP_K, w), jnp.int32), pltpu.VMEM((w, d), tok.dtype), pltpu.SemaphoreType.DMA],
        name="moe_dispatch_sc")
    def scatter_rows(tok_hbm, dest_hbm, buf_hbm, idx_v, rows_v, sem):
        worker = lax.axis_index("subcore") * SC_CORES + lax.axis_index("core")

        @pl.loop(0, per_worker)
        def _(c):
            chunk = worker * per_worker + c
            pltpu.sync_copy(dest_hbm.at[chunk], idx_v)
            pltpu.sync_copy(tok_hbm.at[pl.ds(pl.multiple_of(chunk * w, w), w)], rows_v)
            copies = [pltpu.async_copy(rows_v, buf_hbm.at[idx_v.at[k]], sem) for k in range(TOP_K)]
            for cp in copies:
                cp.wait()

    return scatter_rows(tok, _sc_chunks(dest, w))


def _expert_kernel(layer, be_ref, valid_ref, nused_ref, next_ref, slot_ref, x_ref, w1_hbm, b1_ref, w2_hbm, b2_ref,
                   o_ref, w1f_ref, w2f_ref, w1b_ref, w2b_ref, sem):
    i = pl.program_id(0)
    used = valid_ref[i] > 0

    def fetch(expert, slot):
        return (pltpu.make_async_copy(w1_hbm.at[layer, expert], w1f_ref.at[slot], sem.at[0, slot]),
                pltpu.make_async_copy(w2_hbm.at[layer, expert], w2f_ref.at[slot], sem.at[1, slot]))

    @pl.when(used & ((i == 0) | (be_ref[i] != be_ref[jnp.maximum(i - 1, 0)])))
    def _():
        slot = slot_ref[i]

        @pl.when(i == 0)
        def _():
            for cp in fetch(be_ref[i], slot):
                cp.start()

        for cp in fetch(be_ref[i], slot):
            cp.wait()

        @pl.when(next_ref[i] >= 0)
        def _():
            for cp in fetch(next_ref[i], 1 - slot):
                cp.start()

        w1b_ref[...] = w1f_ref[slot].astype(BF16)
        w2b_ref[...] = w2f_ref[slot].astype(BF16)

    @pl.when(used)
    def _():
        f = w2b_ref.shape[0]
        row = lax.broadcasted_iota(jnp.int32, (x_ref.shape[0], 1), 0)
        x = jnp.where(row < valid_ref[i], x_ref[...], jnp.uint32(0))
        xa, xb = _unpack_halves(x)
        gu = _dot(jnp.concatenate([xa.astype(BF16), xb.astype(BF16)], axis=1), w1b_ref[...]) + b1_ref[0]
        gate = jnp.minimum(gu[:, :f], SWIGLU_LIMIT)
        up = jnp.clip(gu[:, f:], -SWIGLU_LIMIT, SWIGLU_LIMIT)
        hid = (up + 1.0) * (gate * jax.nn.sigmoid(gate * SWIGLU_ALPHA))
        o_ref[...] = _pack_halves(_dot(hid.astype(BF16), w2b_ref[...]) + b2_ref[0])

    @pl.when(jnp.logical_not(used))
    def _():
        o_ref[...] = jnp.zeros(o_ref.shape, o_ref.dtype)


def _experts(buf, block_expert, valid, nused, next_expert, slot, layer, w1, b1, w2, b2, tm):
    rows, dw = buf.shape
    _, n_exp, d, f2 = w1.shape
    f = w2.shape[2]
    in_rows = lambda i, be, va, nu, nx, sl: (jnp.minimum(i, nu[0] - 1), 0)
    by_expert = lambda i, be, va, nu, nx, sl: (be[i], 0, 0)
    any_spec = pl.BlockSpec(memory_space=pl.ANY)
    return pl.pallas_call(
        functools.partial(_expert_kernel, layer),
        grid_spec=pltpu.PrefetchScalarGridSpec(
            num_scalar_prefetch=5,
            grid=(rows // tm,),
            in_specs=[pl.BlockSpec((tm, dw), in_rows), any_spec, pl.BlockSpec((1, 1, f2), by_expert),
                      any_spec, pl.BlockSpec((1, 1, d), by_expert)],
            out_specs=pl.BlockSpec((tm, dw), lambda i, be, va, nu, nx, sl: (i, 0)),
            scratch_shapes=[pltpu.VMEM((2, d, f2), F32), pltpu.VMEM((2, f, d), F32),
                            pltpu.VMEM((d, f2), BF16), pltpu.VMEM((f, d), BF16), pltpu.SemaphoreType.DMA((2, 2))]),
        out_shape=jax.ShapeDtypeStruct((rows, dw), jnp.uint32),
        compiler_params=_params(("arbitrary",)),
        name="moe_experts",
    )(block_expert, valid, nused, next_expert, slot, buf, w1, b1.reshape(n_exp, 1, f2), w2, b2.reshape(n_exp, 1, d))


def _gather_rows(ybuf, dest):
    _, d = ybuf.shape
    n = dest.shape[1]
    w = SC_GATHER_CHUNK
    n_workers = SC_CORES * SC_SUBCORES
    per_worker = n // (w * n_workers)
    assert per_worker * w * n_workers == n and TOP_K % 2 == 0
    mesh = plsc.VectorSubcoreMesh(core_axis_name="core", subcore_axis_name="subcore",
                                  num_cores=SC_CORES, num_subcores=SC_SUBCORES)

    @functools.partial(
        pl.kernel, mesh=mesh, out_type=jax.ShapeDtypeStruct((TOP_K, n, d), ybuf.dtype),
        scratch_types=[pltpu.VMEM((TOP_K, w), jnp.int32), pltpu.VMEM((2, w, d), ybuf.dtype),
                       pltpu.SemaphoreType.DMA((2,)), pltpu.SemaphoreType.DMA((2,))],
        name="moe_gather_sc")
    def gather_rows(y_hbm, dest_hbm, out_hbm, idx_v, rows_v, sem_in, sem_out):
        worker = lax.axis_index("subcore") * SC_CORES + lax.axis_index("core")

        @pl.loop(0, per_worker)
        def _(c):
            chunk = worker * per_worker + c
            tokens = pl.ds(pl.multiple_of(chunk * w, w), w)
            pltpu.sync_copy(dest_hbm.at[chunk], idx_v)
            fetch = lambda k: pltpu.async_copy(y_hbm.at[idx_v.at[k]], rows_v.at[k % 2], sem_in.at[k % 2])
            store = lambda k: pltpu.async_copy(rows_v.at[k % 2], out_hbm.at[k, tokens], sem_out.at[k % 2])
            fetches = [fetch(0), fetch(1)]
            stores = []
            for k in range(TOP_K):
                fetches[k].wait()
                stores.append(store(k))
                if k + 2 < TOP_K:
                    stores[k].wait()
                    fetches.append(fetch(k + 2))
            for k in range(TOP_K - 2, TOP_K):
                stores[k].wait()

    return gather_rows(ybuf, _sc_chunks(dest, w))


def _combine_kernel(ntp, rows_ref, gate_ref, x_ref, g_ref, *outs):
    i = pl.program_id(0)
    r = x_ref.shape[0]
    gates = jnp.concatenate([gate_ref[...], jnp.zeros((LANES - TOP_K, r), F32)], axis=0)
    gates_t = jnp.transpose(gates)
    halves = [_unpack_halves(rows_ref[k]) for k in range(TOP_K)]
    f = []
    for side in range(2):
        acc = halves[0][side] * gates_t[:, 0:1]
        for k in range(1, TOP_K):
            acc = acc + halves[k][side] * gates_t[:, k:k + 1]
        f.append(acc)
    xo = x_ref[...] + jnp.concatenate(f, axis=1)
    hn = _rms(xo, g_ref[...])
    if ntp is None:
        xo_ref, hn_ref = outs
        xo_ref[...] = xo
        hn_ref[...] = hn.astype(hn_ref.dtype)
    else:
        hp_ref, hs_ref = outs

        @pl.when(i < ntp)
        def _():
            hp_ref[...] = hn

        @pl.when(i >= ntp)
        def _():
            hs_ref[...] = hn


def _combine(rows4, gates, x, g, n_p):
    n, d = x.shape
    r = COMBINE_TILE
    rows = pl.BlockSpec((r, d), lambda i: (i, 0))
    if n_p is None:
        ntp = None
        out_specs = [rows, rows]
        out_shape = [jax.ShapeDtypeStruct((n, d), F32), jax.ShapeDtypeStruct((n, d), BF16)]
    else:
        ntp = n_p // r
        out_specs = [pl.BlockSpec((r, d), lambda i: (jnp.minimum(i, ntp - 1), 0)),
                     pl.BlockSpec((r, d), lambda i: (jnp.maximum(i - ntp, 0), 0))]
        out_shape = [jax.ShapeDtypeStruct((n_p, d), F32), jax.ShapeDtypeStruct((n - n_p, d), F32)]
    return pl.pallas_call(
        functools.partial(_combine_kernel, ntp),
        grid=(n // r,),
        in_specs=[pl.BlockSpec((TOP_K, r, d // 2), lambda i: (0, i, 0)), pl.BlockSpec((TOP_K, r), lambda i: (0, i)),
                  rows, _full((1, d))],
        out_specs=out_specs,
        out_shape=out_shape,
        compiler_params=_params(("arbitrary",)),
        name="moe_combine",
    )(rows4, gates, x, g)


def _moe(tok, idx, gates, rank, cnt, x, g_next, n_p, layer, w1, b1, w2, b2):
    n = tok.shape[0]
    n_exp = w1.shape[1]
    pairs = n * TOP_K
    tm = 128
    for cand in (512, 256):
        if pairs >= 4 * cand * n_exp and pairs % cand == 0:
            tm = cand
            break
    n_blocks = -(-pairs // tm) + n_exp
    counts = cnt[:, 0].astype(jnp.int32)
    padded = (counts + tm - 1) // tm * tm
    pad_end = jnp.cumsum(padded)
    pad_start = pad_end - padded
    experts = jnp.arange(n_exp, dtype=jnp.int32)
    dest = rank + jnp.sum(jnp.where(idx[:, :, None] == experts, pad_start, 0), axis=-1)
    starts = jnp.arange(n_blocks, dtype=jnp.int32) * tm
    nused = (pad_end[-1:] // tm).astype(jnp.int32)
    block_expert = jnp.minimum(jnp.sum((pad_end[None, :] <= starts[:, None]).astype(jnp.int32), axis=1), n_exp - 1)
    tokens_end = jnp.sum(jnp.where(block_expert[:, None] == experts, pad_start + counts, 0), axis=-1)
    valid = jnp.clip(tokens_end - starts, 0, tm).astype(jnp.int32)
    has_tokens = counts > 0
    later = has_tokens[None, :] & (experts[None, :] > experts[:, None])
    successor = jnp.min(jnp.where(later, experts[None, :], n_exp), axis=1)
    successor = jnp.where(successor == n_exp, -1, successor)
    parity = (jnp.cumsum(has_tokens.astype(jnp.int32)) - 1) % 2
    of_block = lambda table: jnp.sum(jnp.where(block_expert[:, None] == experts, table, 0), axis=-1).astype(jnp.int32)
    buf = _dispatch(tok, dest, n_blocks * tm)
    ybuf = _experts(buf, block_expert, valid, nused, of_block(successor), of_block(parity), layer, w1, b1, w2, b2, tm)
    return _combine(_gather_rows(ybuf, dest), gates, x, g_next, n_p)


def _qkv_kernel(ntp, h_ref, wq_ref, wkt_ref, wvt_ref, q_ref, ktb_ref, vtb_ref, ktf_ref, vtf_ref,
                ksb_ref, vsb_ref, ksf_ref, vsf_ref):
    i = pl.program_id(0)
    r, d = h_ref.shape
    tk = ktb_ref.shape[-1]
    h = h_ref[...]
    q_ref[...] = (_dot(h, wq_ref[...]) * ((d // N_HEADS) ** -0.5)).astype(BF16)

    @pl.when(i < ntp)
    def _():
        for wt_ref, tb_ref, tf_ref in ((wkt_ref, ktb_ref, ktf_ref), (wvt_ref, vtb_ref, vtf_ref)):
            xt = _dot_nt(wt_ref[...], h)
            tf_ref[...] = xt
            for c in range(r // tk):
                tb_ref[0, c] = xt[:, c * tk:(c + 1) * tk].astype(BF16)

    @pl.when(i >= ntp)
    def _():
        for wt_ref, sb_ref, sf_ref in ((wkt_ref, ksb_ref, ksf_ref), (wvt_ref, vsb_ref, vsf_ref)):
            x = _dot_nt(h, wt_ref[...])
            sf_ref[...] = x
            sb_ref[...] = x.astype(BF16)


def _qkv(h, wq, wkt, wvt, n_streams, length):
    n, d = h.shape
    r = QKV_TILE
    tk = ATTN_TK
    n_p = n_streams * length
    nt = length // r
    ntp = n_p // r
    rows = pl.BlockSpec((r, d), lambda i: (i, 0))
    prompt_blk = lambda i: (jnp.minimum(i, ntp - 1) // nt, lax.rem(jnp.minimum(i, ntp - 1), nt))
    t_blocks = pl.BlockSpec((1, r // tk, d, tk), lambda i: prompt_blk(i) + (0, 0))
    t_full = pl.BlockSpec((d, r), prompt_blk)
    sample = pl.BlockSpec((r, d), lambda i: (jnp.maximum(i - ntp, 0), 0))
    w_spec = _full((d, d))
    return pl.pallas_call(
        functools.partial(_qkv_kernel, ntp),
        grid=(n // r,),
        in_specs=[rows, w_spec, w_spec, w_spec],
        out_specs=[rows, t_blocks, t_blocks, t_full, t_full, sample, sample, sample, sample],
        out_shape=[jax.ShapeDtypeStruct((n, d), BF16)]
        + [jax.ShapeDtypeStruct((n_streams, length // tk, d, tk), BF16)] * 2
        + [jax.ShapeDtypeStruct((n_streams * d, length), F32)] * 2
        + [jax.ShapeDtypeStruct((n - n_p, d), BF16)] * 2 + [jax.ShapeDtypeStruct((n - n_p, d), F32)] * 2,
        compiler_params=_params(("arbitrary",)),
        name="qkv",
    )(h, wq, wkt, wvt)


def _attn_setup(q_ref, qm_ref):
    tq, d = q_ref.shape
    lane = lax.broadcasted_iota(jnp.int32, (tq, LANES), 1)
    for p in range(d // LANES):
        qp = q_ref[:, p * LANES:(p + 1) * LANES]
        qm_ref[p, 0:tq, :] = jnp.where(lane < LANES // 2, qp, jnp.zeros_like(qp))
        qm_ref[p, tq:2 * tq, :] = jnp.where(lane >= LANES // 2, qp, jnp.zeros_like(qp))


def _suffix_sum_matrix(tk):
    src = lax.rem(lax.broadcasted_iota(jnp.int32, (2 * tk, 2 * tk), 0), tk)
    dst = lax.broadcasted_iota(jnp.int32, (2 * tk, 2 * tk), 1)
    return jnp.where((dst >= tk) | (src > dst), 1.0, 0.0).astype(BF16)


def _attn_block(qm_ref, carry_ref, acc_ref, keys, values, transposed, mask, sums):
    n_pairs, tq2, tk = carry_ref.shape
    tq = tq2 // 2
    lane = lax.broadcasted_iota(jnp.int32, (tq, LANES), 1)
    first = mask is not None
    visible = (lambda x: jnp.where(mask, x, 0.0)) if first else (lambda x: x)
    scores = _dot if transposed else _dot_nt
    mix = _dot_nt if transposed else _dot
    zs = [scores(qm_ref[p], keys[p]) for p in range(n_pairs)]
    log_beta, parts = [], []
    for z in zs:
        sp = jnp.maximum(z, 0.0) + jnp.log(1.0 + jnp.exp(-jnp.abs(z)))
        log_keep = visible(-sp)
        hi = log_keep.astype(BF16)
        lo = (log_keep - hi.astype(F32)).astype(BF16)
        parts.append(jnp.concatenate([hi, lo], axis=1))
        log_beta.append(z - sp)
    sums_out = [_dot(part, sums) for part in parts]
    weights = []
    top = jnp.full((tq2, tk), -jnp.inf, F32)
    for p in range(n_pairs):
        after, carry = sums_out[p][:, :tk], sums_out[p][:, tk:]
        if not first:
            after = after + carry_ref[p]
            carry = carry + carry_ref[p]
        weights.append(visible(jnp.exp(log_beta[p] + after)).astype(BF16))
        carry_ref[p] = carry
        if not first:
            top = jnp.maximum(top, carry)
    for p in range(n_pairs):
        out = mix(weights[p], values[p])
        out = jnp.where(lane < LANES // 2, out[:tq], out[tq:])
        cols = slice(p * LANES, (p + 1) * LANES)
        acc_ref[:, cols] = out if first else acc_ref[:, cols] + out
    return jnp.float32(0.0) if first else jnp.max(top)


def _keep_sweeping(state):
    j, top = state
    return (j >= 0) & (top > -ATTN_EXIT)


def _attn_prompt_kernel(q_ref, kt_ref, vt_ref, o_ref, qm_ref, carry_ref, acc_ref):
    qi = pl.program_id(1)
    tq, d = q_ref.shape
    tk = kt_ref.shape[-1]
    n_pairs = d // LANES
    _attn_setup(q_ref, qm_ref)
    row_pos = qi * tq + lax.rem(lax.broadcasted_iota(jnp.int32, (2 * tq, tk), 0), tq)
    col = lax.broadcasted_iota(jnp.int32, (2 * tq, tk), 1)
    sums = _suffix_sum_matrix(tk)

    def block(j, mask):
        keys = [kt_ref[0, j, p * LANES:(p + 1) * LANES, :] for p in range(n_pairs)]
        values = [vt_ref[0, j, p * LANES:(p + 1) * LANES, :] for p in range(n_pairs)]
        return _attn_block(qm_ref, carry_ref, acc_ref, keys, values, True, mask, sums)

    j0 = ((qi + 1) * tq - 2) // tk
    top = block(j0, (j0 * tk + col) < row_pos)
    lax.while_loop(_keep_sweeping, lambda state: (state[0] - 1, block(state[0], None)), (j0 - 1, top))
    o_ref[...] = acc_ref[...].astype(o_ref.dtype)


def _attention_prompt(q, kt, vt):
    n_streams, n_kblocks, d, tk = kt.shape
    length = n_kblocks * tk
    tq = min(length, ATTN_TQ)
    assert tk % tq == 0
    nq = length // tq
    kv = pl.BlockSpec((1, n_kblocks, d, tk), lambda b, i: (b, 0, 0, 0))
    rows = pl.BlockSpec((tq, d), lambda b, i: (b * nq + i, 0))
    return pl.pallas_call(
        _attn_prompt_kernel,
        grid=(n_streams, nq),
        in_specs=[rows, kv, kv],
        out_specs=rows,
        out_shape=jax.ShapeDtypeStruct((n_streams * length, d), BF16),
        scratch_shapes=[pltpu.VMEM((d // LANES, 2 * tq, LANES), BF16), pltpu.VMEM((d // LANES, 2 * tq, tk), F32),
                        pltpu.VMEM((tq, d), F32)],
        compiler_params=_params(("arbitrary", "arbitrary")),
        name="stick_breaking_prompt",
    )(q, kt, vt)


def _attn_sample_kernel(q_ref, kn_ref, vn_ref, ck_hbm, cv_hbm, o_ref, qm_ref, carry_ref, acc_ref, kbuf, vbuf, sem):
    b = pl.program_id(0)
    t, d = q_ref.shape
    tk = kbuf.shape[-1]
    n_pairs = d // LANES
    n_cache_blocks = ck_hbm.shape[1] // tk

    def fetch(j, slot):
        src = (pl.ds(pl.multiple_of(b * d, d), d), pl.ds(pl.multiple_of(j * tk, tk), tk))
        return (pltpu.make_async_copy(ck_hbm.at[src], kbuf.at[slot], sem.at[0, slot]),
                pltpu.make_async_copy(cv_hbm.at[src], vbuf.at[slot], sem.at[1, slot]))

    for cp in fetch(n_cache_blocks - 1, (n_cache_blocks - 1) % 2):
        cp.start()
    _attn_setup(q_ref, qm_ref)
    row = lax.rem(lax.broadcasted_iota(jnp.int32, (2 * t, tk), 0), t)
    col = lax.broadcasted_iota(jnp.int32, (2 * t, tk), 1)
    sums = _suffix_sum_matrix(tk)
    pad = jnp.zeros((tk - t, LANES), BF16)
    keys = [jnp.concatenate([kn_ref[:, p * LANES:(p + 1) * LANES], pad], axis=0) for p in range(n_pairs)]
    values = [jnp.concatenate([vn_ref[:, p * LANES:(p + 1) * LANES], pad], axis=0) for p in range(n_pairs)]
    top = _attn_block(qm_ref, carry_ref, acc_ref, keys, values, False, col < row, sums)

    def body(state):
        j, _ = state
        slot = lax.rem(j, 2)
        for cp in fetch(j, slot):
            cp.wait()

        @pl.when(j > 0)
        def _():
            for cp in fetch(j - 1, 1 - slot):
                cp.start()

        keys = [kbuf[slot, p * LANES:(p + 1) * LANES, :].astype(BF16) for p in range(n_pairs)]
        values = [vbuf[slot, p * LANES:(p + 1) * LANES, :].astype(BF16) for p in range(n_pairs)]
        return j - 1, _attn_block(qm_ref, carry_ref, acc_ref, keys, values, True, None, sums)

    j_end, _ = lax.while_loop(_keep_sweeping, body, (jnp.int32(n_cache_blocks - 1), top))

    @pl.when(j_end >= 0)
    def _():
        for cp in fetch(j_end, lax.rem(j_end, 2)):
            cp.wait()

    o_ref[...] = acc_ref[...].astype(o_ref.dtype)


def _attention_sample(q, k_new, v_new, cache_kt, cache_vt, t, q_row_off):
    d = q.shape[1]
    n_streams = k_new.shape[0] // t
    tk = ATTN_TK
    assert cache_kt.shape[1] % tk == 0 and t <= tk
    any_spec = pl.BlockSpec(memory_space=pl.ANY)
    new = pl.BlockSpec((t, d), lambda b: (b, 0))
    return pl.pallas_call(
        _attn_sample_kernel,
        grid=(n_streams,),
        in_specs=[pl.BlockSpec((t, d), lambda b: (q_row_off // t + b, 0)), new, new, any_spec, any_spec],
        out_specs=new,
        out_shape=jax.ShapeDtypeStruct((n_streams * t, d), BF16),
        scratch_shapes=[pltpu.VMEM((d // LANES, 2 * t, LANES), BF16), pltpu.VMEM((d // LANES, 2 * t, tk), F32),
                        pltpu.VMEM((t, d), F32), pltpu.VMEM((2, d, tk), F32), pltpu.VMEM((2, d, tk), F32),
                        pltpu.SemaphoreType.DMA((2, 2))],
        compiler_params=_params(("arbitrary",)),
        name="stick_breaking_sample",
    )(q, k_new, v_new, cache_kt, cache_vt)


def kernel(x_prompt, x_sample, state_pool, cache_k, cache_v, norm_mix, norm_ffn, pool_w, pool_scale, w_qkv, w_o, router_w, router_b, moe_w1, moe_b1, moe_w2, moe_b2, final_norm):
    b, s, d = x_prompt.shape
    db, t, _ = x_sample.shape
    past = cache_k.shape[2]
    n_exp = router_w.shape[2]
    hd = d // N_HEADS
    hist = POOL_HIST_ROWS
    n_p, n_s = b * s, db * t
    assert t >= hist and ROW_TILE % t == 0 and s % ROW_TILE == 0 and n_s % ROW_TILE == 0 and ROW_TILE % QKV_TILE == 0
    row = lambda a: a.reshape(1, -1)
    def split_router(w):
        w_t = w.T
        hi = w_t.astype(BF16)
        return jnp.concatenate([hi, (w_t - hi.astype(F32)).astype(BF16)], axis=0)

    wr = [split_router(router_w[i]) for i in range(2)]
    br = [router_b[i].reshape(n_exp, 1) for i in range(2)]

    hist_s = jnp.concatenate([jnp.zeros((db, 1, d), F32), state_pool[0]], axis=1)
    xn, tok, idx, gates, rank, hlast_p, hlast_s, cnt = _pool_layer(
        x_prompt, x_sample, hist_s, past, row(norm_mix[0]), pool_w[0].astype(BF16), row(pool_scale[0]),
        row(norm_ffn[0]), wr[0], br[0])
    x1, h1 = _moe(tok, idx, gates, rank, cnt, xn, row(norm_mix[1]), None, 0, moe_w1, moe_b1[0], moe_w2, moe_b2[0])

    wqkv = w_qkv[0].astype(BF16)
    q, ktb, vtb, ktf, vtf, ksb, vsb, ksf, vsf = _qkv(h1, wqkv[:, :d], wqkv[:, d:2 * d].T, wqkv[:, 2 * d:].T, b, s)
    o_p = _attention_prompt(q, ktb, vtb)
    transposed = lambda cache: cache.transpose(0, 1, 3, 4, 2).reshape(db * d, past)
    o_s = _attention_sample(q, ksb, vsb, transposed(cache_k), transposed(cache_v), t, n_p)
    xn, tok, idx, gates, rank, cnt = _proj_layer(o_p, o_s, x1, w_o[0].astype(BF16), row(norm_ffn[1]), wr[1], br[1])
    y_p, y_s = _moe(tok, idx, gates, rank, cnt, xn, row(final_norm), n_p, 1, moe_w1, moe_b1[1], moe_w2, moe_b2[1])

    frames_major = lambda xt: xt.reshape(1, b, N_HEADS, hd, s).transpose(0, 1, 4, 2, 3)
    heads = lambda a: a.reshape(1, db, t, N_HEADS, hd)
    return (y_p.reshape(b, s, d), y_s.reshape(db, t, d), hlast_p[None, :, 1:, :],
            frames_major(ktf), frames_major(vtf), hlast_s[None, :, 1:, :], heads(ksf), heads(vsf))
```

```python
import functools

import jax
import jax.numpy as jnp
from jax import lax
from jax.experimental import pallas as pl
from jax.experimental.pallas import tpu as pltpu
from jax.experimental.pallas import tpu_sc as plsc

EPS = 1e-5
POOL_WINDOWS = (2, 4, 8, 16)
POOL_HIST_ROWS = 16
N_HEADS = 16
TOP_K = 4
SWIGLU_LIMIT = 7.0
SWIGLU_ALPHA = 1.702
LANES = 128
ROW_TILE = 512
QKV_TILE = 512
ATTN_TQ = 128
ATTN_TK = 128
ATTN_EXIT = 88.0
VMEM_LIMIT = 56 * 1024 * 1024
SC_CORES = 2
SC_SUBCORES = 16
SC_CHUNK = 96
SC_GATHER_CHUNK = 48
COMBINE_TILE = 512

F32 = jnp.float32
BF16 = jnp.bfloat16


def _rms(x, g):
    ms = jnp.mean(x * x, axis=-1, keepdims=True)
    return x * lax.rsqrt(ms + EPS) * g


def _dot(a, b):
    return jnp.dot(a, b, preferred_element_type=F32)


def _dot_nt(a, b, precision=None):
    return lax.dot_general(a, b, (((1,), (1,)), ((), ())), preferred_element_type=F32, precision=precision)


def _pack_halves(x):
    c = x.shape[1] // 2
    bits = lax.bitcast_convert_type(x.astype(BF16).astype(F32), jnp.uint32)
    return bits[:, :c] | (bits[:, c:] >> 16)


def _unpack_halves(w):
    return (lax.bitcast_convert_type(w & jnp.uint32(0xFFFF0000), F32), lax.bitcast_convert_type(w << 16, F32))


def _params(semantics):
    return pltpu.CompilerParams(dimension_semantics=semantics, vmem_limit_bytes=VMEM_LIMIT)


def _full(shape):
    return pl.BlockSpec(shape, lambda i, *_: (0,) * len(shape))


def _route_init(cnt_ref, tri_ref):
    r = tri_ref.shape[0]
    cnt_ref[...] = jnp.zeros(cnt_ref.shape, F32)
    tri_ref[...] = jnp.where(lax.broadcasted_iota(jnp.int32, (r, r), 0) < lax.broadcasted_iota(jnp.int32, (r, r), 1),
                             1.0, 0.0).astype(BF16)


def _route_tail(xn, gffn_ref, wr_ref, br_ref, cnt_ref, tri_ref, tok_ref, idx_ref, gate_ref, rank_ref):
    n_exp = br_ref.shape[0]
    tok = _rms(xn, gffn_ref[...])
    tok_ref[...] = _pack_halves(tok)
    tok_hi = tok.astype(BF16)
    tok_lo = (tok - tok_hi.astype(F32)).astype(BF16)
    by_hi = _dot_nt(wr_ref[...], tok_hi)
    logits = by_hi[:n_exp] + by_hi[n_exp:] + _dot_nt(wr_ref[:n_exp, :], tok_lo) + br_ref[...]
    eidx = lax.broadcasted_iota(jnp.int32, logits.shape, 0).astype(F32)
    vals, idxs = [], []
    l = logits
    for _ in range(TOP_K):
        m = jnp.max(l, axis=0, keepdims=True)
        i = jnp.min(jnp.where(l == m, eidx, float(n_exp)), axis=0, keepdims=True)
        vals.append(m)
        idxs.append(i)
        l = jnp.where(eidx == i, -jnp.inf, l)
    es = [jnp.exp(v - vals[0]) for v in vals]
    den = es[0]
    for e in es[1:]:
        den = den + e
    gate_ref[...] = jnp.concatenate([e / den for e in es], axis=0)
    idx_ref[...] = jnp.concatenate(idxs, axis=0).astype(jnp.int32)
    member = jnp.zeros(logits.shape, F32)
    for i in idxs:
        member = member + jnp.where(eidx == i, 1.0, 0.0)
    before = _dot(member.astype(BF16), tri_ref[...]) + cnt_ref[:, :1]
    ranks = [jnp.sum(jnp.where(eidx == i, before, 0.0), axis=0, keepdims=True) for i in idxs]
    rank_ref[...] = jnp.concatenate(ranks, axis=0).astype(jnp.int32)
    cnt_ref[...] = cnt_ref[...] + jnp.sum(member, axis=1, keepdims=True)


def _pool_mix(h, ext_ref, pos, pw_ref, ps_ref):
    ts, d = h.shape[-2:]
    hist = POOL_HIST_ROWS
    group = d // len(POOL_WINDOWS)
    pre = (slice(None),) * (h.ndim - 2)
    ys = []
    for g, win in enumerate(POOL_WINDOWS):
        cols = slice(g * group, (g + 1) * group)
        hg = h[pre + (slice(None), cols)]
        acc = hg
        for j in range(1, win):
            acc = acc + ext_ref[pre + (slice(hist - j, hist - j + ts), cols)]
        cnt = jnp.minimum(pos + 1, win).astype(F32)
        dg = acc / cnt - hg
        ys.append(_dot(dg.reshape(-1, group).astype(BF16), pw_ref[g]))
    return jnp.concatenate(ys, axis=-1) * ps_ref[...]


def _pool_mix_tiled(h, ext_ref, lvl_ref, pos, pw_ref, ps_ref):
    ts, d = h.shape
    hist = POOL_HIST_ROWS
    group = d // len(POOL_WINDOWS)
    prev_ref, prev_col0 = ext_ref, 0
    ys = []
    for g, win in enumerate(POOL_WINDOWS):
        assert win == 2 * (POOL_WINDOWS[g - 1] if g else 1)
        col0 = g * group
        cols = slice(col0 - prev_col0, d - prev_col0)
        level = prev_ref[hist:hist + ts, cols] + prev_ref[hist - win // 2:hist - win // 2 + ts, cols]
        if g + 1 < len(POOL_WINDOWS):
            lvl_ref[g, hist:hist + ts, 0:d - col0 - group] = level[:, group:]
            prev_ref, prev_col0 = lvl_ref.at[g], col0 + group
        cnt = jnp.minimum(pos + 1, win).astype(F32)
        dg = level[:, :group] / cnt - h[:, col0:col0 + group]
        ys.append(_dot(dg.astype(BF16), pw_ref[g]))
    return jnp.concatenate(ys, axis=-1) * ps_ref[...]


def _pool_kernel(ntp, nt, pos0_s, xp_ref, xs_ref, hist_ref, gmix_ref, pw_ref, ps_ref, gffn_ref, wr_ref, br_ref,
                 xn_ref, tok_ref, idx_ref, gate_ref, rank_ref, hlast_p_ref, hlast_s_ref, cnt_out_ref,
                 ext_p, lvl_p, ext_s, cnt_ref, tri_ref):
    i = pl.program_id(0)
    hist = POOL_HIST_ROWS

    @pl.when(i == 0)
    def _():
        _route_init(cnt_ref, tri_ref)

    @pl.when(i < ntp)
    def _():
        ts, d = xp_ref.shape
        t = lax.rem(i, nt)

        @pl.when(t == 0)
        def _():
            ext_p[0:hist, :] = jnp.zeros((hist, d), F32)
            lvl_p[:, 0:hist, :] = jnp.zeros((lvl_p.shape[0], hist, lvl_p.shape[2]), F32)

        x = xp_ref[...]
        h = _rms(x, gmix_ref[...])
        ext_p[hist:hist + ts, :] = h

        @pl.when(t == nt - 1)
        def _():
            hlast_p_ref[0] = h[ts - hist:, :]

        pos = t * ts + lax.broadcasted_iota(jnp.int32, (ts, 1), 0)
        xn_ref[...] = x + _pool_mix_tiled(h, ext_p, lvl_p, pos, pw_ref, ps_ref)
        ext_p[0:hist, :] = ext_p[ts:ts + hist, :]
        lvl_p[:, 0:hist, :] = lvl_p[:, ts:ts + hist, :]

    @pl.when(i >= ntp)
    def _():
        bb, ts, d = xs_ref.shape
        ext_s[:, 0:hist, :] = hist_ref[...]
        x = xs_ref[...]
        h = _rms(x, gmix_ref[...])
        ext_s[:, hist:hist + ts, :] = h
        hlast_s_ref[...] = h[:, ts - hist:, :]
        pos = pos0_s + lax.broadcasted_iota(jnp.int32, (1, ts, 1), 1)
        xn_ref[...] = x.reshape(bb * ts, d) + _pool_mix(h, ext_s, pos, pw_ref, ps_ref)

    _route_tail(xn_ref[...], gffn_ref, wr_ref, br_ref, cnt_ref, tri_ref, tok_ref, idx_ref, gate_ref, rank_ref)
    cnt_out_ref[...] = cnt_ref[...]


def _pool_layer(x_prompt, x_sample, hist_s, pos0_s, gmix, pw, ps, gffn, wr, br):
    b, s, d = x_prompt.shape
    db, t, _ = x_sample.shape
    n_exp = br.shape[0]
    r = ROW_TILE
    nt = s // r
    ntp = b * nt
    bb = r // t
    nts = db // bb
    n = b * s + db * t
    hist = POOL_HIST_ROWS
    rows = pl.BlockSpec((r, d), lambda i: (i, 0))
    lanes = pl.BlockSpec((TOP_K, r), lambda i: (0, i))
    sample_blk = lambda i: (jnp.maximum(i - ntp, 0), 0, 0)
    return pl.pallas_call(
        functools.partial(_pool_kernel, ntp, nt, pos0_s),
        grid=(ntp + nts,),
        in_specs=[pl.BlockSpec((r, d), lambda i: (jnp.minimum(i, ntp - 1), 0)),
                  pl.BlockSpec((bb, t, d), sample_blk), pl.BlockSpec((bb, hist, d), sample_blk),
                  _full((1, d)), _full(pw.shape), _full((1, d)), _full((1, d)), _full((2 * n_exp, d)), _full((n_exp, 1))],
        out_specs=[rows, pl.BlockSpec((r, d // 2), lambda i: (i, 0)), lanes, lanes, lanes,
                   pl.BlockSpec((1, hist, d), lambda i: (jnp.minimum(i // nt, b - 1), 0, 0)),
                   pl.BlockSpec((bb, hist, d), sample_blk), _full((n_exp, LANES))],
        out_shape=[jax.ShapeDtypeStruct((n, d), F32), jax.ShapeDtypeStruct((n, d // 2), jnp.uint32),
                   jax.ShapeDtypeStruct((TOP_K, n), jnp.int32), jax.ShapeDtypeStruct((TOP_K, n), F32),
                   jax.ShapeDtypeStruct((TOP_K, n), jnp.int32),
                   jax.ShapeDtypeStruct((b, hist, d), F32), jax.ShapeDtypeStruct((db, hist, d), F32),
                   jax.ShapeDtypeStruct((n_exp, LANES), F32)],
        scratch_shapes=[pltpu.VMEM((hist + r, d), F32),
                        pltpu.VMEM((len(POOL_WINDOWS) - 1, hist + r, d - d // len(POOL_WINDOWS)), F32),
                        pltpu.VMEM((bb, hist + t, d), F32), pltpu.VMEM((n_exp, LANES), F32), pltpu.VMEM((r, r), BF16)],
        compiler_params=_params(("arbitrary",)),
        name="pool_route",
    )(x_prompt.reshape(b * s, d), x_sample, hist_s, gmix, pw, ps, gffn, wr, br)


def _proj_kernel(ntp, op_ref, os_ref, x_ref, wo_ref, gffn_ref, wr_ref, br_ref,
                 xn_ref, tok_ref, idx_ref, gate_ref, rank_ref, cnt_out_ref, cnt_ref, tri_ref):
    i = pl.program_id(0)

    @pl.when(i == 0)
    def _():
        _route_init(cnt_ref, tri_ref)

    o = jnp.where(i < ntp, op_ref[...], os_ref[...])
    xn = x_ref[...] + _dot(o, wo_ref[...])
    xn_ref[...] = xn
    _route_tail(xn, gffn_ref, wr_ref, br_ref, cnt_ref, tri_ref, tok_ref, idx_ref, gate_ref, rank_ref)
    cnt_out_ref[...] = cnt_ref[...]


def _proj_layer(o_p, o_s, x, wo, gffn, wr, br):
    n, d = x.shape
    n_exp = br.shape[0]
    r = ROW_TILE
    ntp = o_p.shape[0] // r
    rows = pl.BlockSpec((r, d), lambda i: (i, 0))
    lanes = pl.BlockSpec((TOP_K, r), lambda i: (0, i))
    return pl.pallas_call(
        functools.partial(_proj_kernel, ntp),
        grid=(n // r,),
        in_specs=[pl.BlockSpec((r, d), lambda i: (jnp.minimum(i, ntp - 1), 0)),
                  pl.BlockSpec((r, d), lambda i: (jnp.maximum(i - ntp, 0), 0)),
                  rows, _full((d, d)), _full((1, d)), _full((2 * n_exp, d)), _full((n_exp, 1))],
        out_specs=[rows, pl.BlockSpec((r, d // 2), lambda i: (i, 0)), lanes, lanes, lanes, _full((n_exp, LANES))],
        out_shape=[jax.ShapeDtypeStruct((n, d), F32), jax.ShapeDtypeStruct((n, d // 2), jnp.uint32),
                   jax.ShapeDtypeStruct((TOP_K, n), jnp.int32), jax.ShapeDtypeStruct((TOP_K, n), F32),
                   jax.ShapeDtypeStruct((TOP_K, n), jnp.int32), jax.ShapeDtypeStruct((n_exp, LANES), F32)],
        scratch_shapes=[pltpu.VMEM((n_exp, LANES), F32), pltpu.VMEM((r, r), BF16)],
        compiler_params=_params(("arbitrary",)),
        name="proj_route",
    )(o_p, o_s, x, wo, gffn, wr, br)


def _sc_chunks(dest, w):
    n = dest.shape[1]
    return dest.reshape(TOP_K, n // w, w).transpose(1, 0, 2)


def _dispatch(tok, dest, n_buf_rows):
    n, d = tok.shape
    w = SC_CHUNK
    n_workers = SC_CORES * SC_SUBCORES
    per_worker = n // (w * n_workers)
    assert per_worker * w * n_workers == n
    mesh = plsc.VectorSubcoreMesh(core_axis_name="core", subcore_axis_name="subcore",
                                  num_cores=SC_CORES, num_subcores=SC_SUBCORES)

    @functools.partial(
        pl.kernel, mesh=mesh, out_type=jax.ShapeDtypeStruct((n_buf_rows, d), tok.dtype),
        scratch_types=[pltpu.VMEM((TOP_K, w), jnp.int32), pltpu.VMEM((w, d), tok.dtype), pltpu.SemaphoreType.DMA],
        name="moe_dispatch_sc")
    def scatter_rows(tok_hbm, dest_hbm, buf_hbm, idx_v, rows_v, sem):
        worker = lax.axis_index("subcore") * SC_CORES + lax.axis_index("core")

        @pl.loop(0, per_worker)
        def _(c):
            chunk = worker * per_worker + c
            pltpu.sync_copy(dest_hbm.at[chunk], idx_v)
            pltpu.sync_copy(tok_hbm.at[pl.ds(pl.multiple_of(chunk * w, w), w)], rows_v)
            copies = [pltpu.async_copy(rows_v, buf_hbm.at[idx_v.at[k]], sem) for k in range(TOP_K)]
            for cp in copies:
                cp.wait()

    return scatter_rows(tok, _sc_chunks(dest, w))


def _expert_kernel(layer, be_ref, valid_ref, nused_ref, next_ref, slot_ref, x_ref, w1_hbm, b1_ref, w2_hbm, b2_ref,
                   o_ref, w1f_ref, w2f_ref, w1b_ref, w2b_ref, sem):
    i = pl.program_id(0)
    used = valid_ref[i] > 0

    def fetch(expert, slot):
        return (pltpu.make_async_copy(w1_hbm.at[layer, expert], w1f_ref.at[slot], sem.at[0, slot]),
                pltpu.make_async_copy(w2_hbm.at[layer, expert], w2f_ref.at[slot], sem.at[1, slot]))

    @pl.when(used & ((i == 0) | (be_ref[i] != be_ref[jnp.maximum(i - 1, 0)])))
    def _():
        slot = slot_ref[i]

        @pl.when(i == 0)
        def _():
            for cp in fetch(be_ref[i], slot):
                cp.start()

        for cp in fetch(be_ref[i], slot):
            cp.wait()

        @pl.when(next_ref[i] >= 0)
        def _():
            for cp in fetch(next_ref[i], 1 - slot):
                cp.start()

        w1b_ref[...] = w1f_ref[slot].astype(BF16)
        w2b_ref[...] = w2f_ref[slot].astype(BF16)

    @pl.when(used)
    def _():
        f = w2b_ref.shape[0]
        row = lax.broadcasted_iota(jnp.int32, (x_ref.shape[0], 1), 0)
        x = jnp.where(row < valid_ref[i], x_ref[...], jnp.uint32(0))
        xa, xb = _unpack_halves(x)
        gu = _dot(jnp.concatenate([xa.astype(BF16), xb.astype(BF16)], axis=1), w1b_ref[...]) + b1_ref[0]
        gate = jnp.minimum(gu[:, :f], SWIGLU_LIMIT)
        up = jnp.clip(gu[:, f:], -SWIGLU_LIMIT, SWIGLU_LIMIT)
        hid = (up + 1.0) * (gate * jax.nn.sigmoid(gate * SWIGLU_ALPHA))
        o_ref[...] = _pack_halves(_dot(hid.astype(BF16), w2b_ref[...]) + b2_ref[0])

    @pl.when(jnp.logical_not(used))
    def _():
        o_ref[...] = jnp.zeros(o_ref.shape, o_ref.dtype)


def _experts(buf, block_expert, valid, nused, next_expert, slot, layer, w1, b1, w2, b2, tm):
    rows, dw = buf.shape
    _, n_exp, d, f2 = w1.shape
    f = w2.shape[2]
    in_rows = lambda i, be, va, nu, nx, sl: (jnp.minimum(i, nu[0] - 1), 0)
    by_expert = lambda i, be, va, nu, nx, sl: (be[i], 0, 0)
    any_spec = pl.BlockSpec(memory_space=pl.ANY)
    return pl.pallas_call(
        functools.partial(_expert_kernel, layer),
        grid_spec=pltpu.PrefetchScalarGridSpec(
            num_scalar_prefetch=5,
            grid=(rows // tm,),
            in_specs=[pl.BlockSpec((tm, dw), in_rows), any_spec, pl.BlockSpec((1, 1, f2), by_expert),
                      any_spec, pl.BlockSpec((1, 1, d), by_expert)],
            out_specs=pl.BlockSpec((tm, dw), lambda i, be, va, nu, nx, sl: (i, 0)),
            scratch_shapes=[pltpu.VMEM((2, d, f2), F32), pltpu.VMEM((2, f, d), F32),
                            pltpu.VMEM((d, f2), BF16), pltpu.VMEM((f, d), BF16), pltpu.SemaphoreType.DMA((2, 2))]),
        out_shape=jax.ShapeDtypeStruct((rows, dw), jnp.uint32),
        compiler_params=_params(("arbitrary",)),
        name="moe_experts",
    )(block_expert, valid, nused, next_expert, slot, buf, w1, b1.reshape(n_exp, 1, f2), w2, b2.reshape(n_exp, 1, d))


def _gather_rows(ybuf, dest):
    _, d = ybuf.shape
    n = dest.shape[1]
    w = SC_GATHER_CHUNK
    n_workers = SC_CORES * SC_SUBCORES
    per_worker = n // (w * n_workers)
    assert per_worker * w * n_workers == n and TOP_K % 2 == 0
    mesh = plsc.VectorSubcoreMesh(core_axis_name="core", subcore_axis_name="subcore",
                                  num_cores=SC_CORES, num_subcores=SC_SUBCORES)

    @functools.partial(
        pl.kernel, mesh=mesh, out_type=jax.ShapeDtypeStruct((TOP_K, n, d), ybuf.dtype),
        scratch_types=[pltpu.VMEM((TOP_K, w), jnp.int32), pltpu.VMEM((2, w, d), ybuf.dtype),
                       pltpu.SemaphoreType.DMA((2,)), pltpu.SemaphoreType.DMA((2,))],
        name="moe_gather_sc")
    def gather_rows(y_hbm, dest_hbm, out_hbm, idx_v, rows_v, sem_in, sem_out):
        worker = lax.axis_index("subcore") * SC_CORES + lax.axis_index("core")

        @pl.loop(0, per_worker)
        def _(c):
            chunk = worker * per_worker + c
            tokens = pl.ds(pl.multiple_of(chunk * w, w), w)
            pltpu.sync_copy(dest_hbm.at[chunk], idx_v)
            fetch = lambda k: pltpu.async_copy(y_hbm.at[idx_v.at[k]], rows_v.at[k % 2], sem_in.at[k % 2])
            store = lambda k: pltpu.async_copy(rows_v.at[k % 2], out_hbm.at[k, tokens], sem_out.at[k % 2])
            fetches = [fetch(0), fetch(1)]
            stores = []
            for k in range(TOP_K):
                fetches[k].wait()
                stores.append(store(k))
                if k + 2 < TOP_K:
                    stores[k].wait()
                    fetches.append(fetch(k + 2))
            for k in range(TOP_K - 2, TOP_K):
                stores[k].wait()

    return gather_rows(ybuf, _sc_chunks(dest, w))


def _combine_kernel(ntp, rows_ref, gate_ref, x_ref, g_ref, *outs):
    i = pl.program_id(0)
    r = x_ref.shape[0]
    gates = jnp.concatenate([gate_ref[...], jnp.zeros((LANES - TOP_K, r), F32)], axis=0)
    gates_t = jnp.transpose(gates)
    halves = [_unpack_halves(rows_ref[k]) for k in range(TOP_K)]
    f = []
    for side in range(2):
        acc = halves[0][side] * gates_t[:, 0:1]
        for k in range(1, TOP_K):
            acc = acc + halves[k][side] * gates_t[:, k:k + 1]
        f.append(acc)
    xo = x_ref[...] + jnp.concatenate(f, axis=1)
    hn = _rms(xo, g_ref[...])
    if ntp is None:
        xo_ref, hn_ref = outs
        xo_ref[...] = xo
        hn_ref[...] = hn.astype(hn_ref.dtype)
    else:
        hp_ref, hs_ref = outs

        @pl.when(i < ntp)
        def _():
            hp_ref[...] = hn

        @pl.when(i >= ntp)
        def _():
            hs_ref[...] = hn


def _combine(rows4, gates, x, g, n_p):
    n, d = x.shape
    r = COMBINE_TILE
    rows = pl.BlockSpec((r, d), lambda i: (i, 0))
    if n_p is None:
        ntp = None
        out_specs = [rows, rows]
        out_shape = [jax.ShapeDtypeStruct((n, d), F32), jax.ShapeDtypeStruct((n, d), BF16)]
    else:
        ntp = n_p // r
        out_specs = [pl.BlockSpec((r, d), lambda i: (jnp.minimum(i, ntp - 1), 0)),
                     pl.BlockSpec((r, d), lambda i: (jnp.maximum(i - ntp, 0), 0))]
        out_shape = [jax.ShapeDtypeStruct((n_p, d), F32), jax.ShapeDtypeStruct((n - n_p, d), F32)]
    return pl.pallas_call(
        functools.partial(_combine_kernel, ntp),
        grid=(n // r,),
        in_specs=[pl.BlockSpec((TOP_K, r, d // 2), lambda i: (0, i, 0)), pl.BlockSpec((TOP_K, r), lambda i: (0, i)),
                  rows, _full((1, d))],
        out_specs=out_specs,
        out_shape=out_shape,
        compiler_params=_params(("arbitrary",)),
        name="moe_combine",
    )(rows4, gates, x, g)


def _moe(tok, idx, gates, rank, cnt, x, g_next, n_p, layer, w1, b1, w2, b2):
    n = tok.shape[0]
    n_exp = w1.shape[1]
    pairs = n * TOP_K
    tm = 128
    for cand in (512, 256):
        if pairs >= 4 * cand * n_exp and pairs % cand == 0:
            tm = cand
            break
    n_blocks = -(-pairs // tm) + n_exp
    counts = cnt[:, 0].astype(jnp.int32)
    padded = (counts + tm - 1) // tm * tm
    pad_end = jnp.cumsum(padded)
    pad_start = pad_end - padded
    experts = jnp.arange(n_exp, dtype=jnp.int32)
    dest = rank + jnp.sum(jnp.where(idx[:, :, None] == experts, pad_start, 0), axis=-1)
    starts = jnp.arange(n_blocks, dtype=jnp.int32) * tm
    nused = (pad_end[-1:] // tm).astype(jnp.int32)
    block_expert = jnp.minimum(jnp.sum((pad_end[None, :] <= starts[:, None]).astype(jnp.int32), axis=1), n_exp - 1)
    tokens_end = jnp.sum(jnp.where(block_expert[:, None] == experts, pad_start + counts, 0), axis=-1)
    valid = jnp.clip(tokens_end - starts, 0, tm).astype(jnp.int32)
    has_tokens = counts > 0
    later = has_tokens[None, :] & (experts[None, :] > experts[:, None])
    successor = jnp.min(jnp.where(later, experts[None, :], n_exp), axis=1)
    successor = jnp.where(successor == n_exp, -1, successor)
    parity = (jnp.cumsum(has_tokens.astype(jnp.int32)) - 1) % 2
    of_block = lambda table: jnp.sum(jnp.where(block_expert[:, None] == experts, table, 0), axis=-1).astype(jnp.int32)
    buf = _dispatch(tok, dest, n_blocks * tm)
    ybuf = _experts(buf, block_expert, valid, nused, of_block(successor), of_block(parity), layer, w1, b1, w2, b2, tm)
    return _combine(_gather_rows(ybuf, dest), gates, x, g_next, n_p)


def _qkv_kernel(ntp, h_ref, wq_ref, wkt_ref, wvt_ref, q_ref, ktb_ref, vtb_ref, ktf_ref, vtf_ref,
                ksb_ref, vsb_ref, ksf_ref, vsf_ref):
    i = pl.program_id(0)
    r, d = h_ref.shape
    tk = ktb_ref.shape[-1]
    h = h_ref[...]
    q_ref[...] = (_dot(h, wq_ref[...]) * ((d // N_HEADS) ** -0.5)).astype(BF16)

    @pl.when(i < ntp)
    def _():
        for wt_ref, tb_ref, tf_ref in ((wkt_ref, ktb_ref, ktf_ref), (wvt_ref, vtb_ref, vtf_ref)):
            xt = _dot_nt(wt_ref[...], h)
            tf_ref[...] = xt
            for c in range(r // tk):
                tb_ref[0, c] = xt[:, c * tk:(c + 1) * tk].astype(BF16)

    @pl.when(i >= ntp)
    def _():
        for wt_ref, sb_ref, sf_ref in ((wkt_ref, ksb_ref, ksf_ref), (wvt_ref, vsb_ref, vsf_ref)):
            x = _dot_nt(h, wt_ref[...])
            sf_ref[...] = x
            sb_ref[...] = x.astype(BF16)


def _qkv(h, wq, wkt, wvt, n_streams, length):
    n, d = h.shape
    r = QKV_TILE
    tk = ATTN_TK
    n_p = n_streams * length
    nt = length // r
    ntp = n_p // r
    rows = pl.BlockSpec((r, d), lambda i: (i, 0))
    prompt_blk = lambda i: (jnp.minimum(i, ntp - 1) // nt, lax.rem(jnp.minimum(i, ntp - 1), nt))
    t_blocks = pl.BlockSpec((1, r // tk, d, tk), lambda i: prompt_blk(i) + (0, 0))
    t_full = pl.BlockSpec((d, r), prompt_blk)
    sample = pl.BlockSpec((r, d), lambda i: (jnp.maximum(i - ntp, 0), 0))
    w_spec = _full((d, d))
    return pl.pallas_call(
        functools.partial(_qkv_kernel, ntp),
        grid=(n // r,),
        in_specs=[rows, w_spec, w_spec, w_spec],
        out_specs=[rows, t_blocks, t_blocks, t_full, t_full, sample, sample, sample, sample],
        out_shape=[jax.ShapeDtypeStruct((n, d), BF16)]
        + [jax.ShapeDtypeStruct((n_streams, length // tk, d, tk), BF16)] * 2
        + [jax.ShapeDtypeStruct((n_streams * d, length), F32)] * 2
        + [jax.ShapeDtypeStruct((n - n_p, d), BF16)] * 2 + [jax.ShapeDtypeStruct((n - n_p, d), F32)] * 2,
        compiler_params=_params(("arbitrary",)),
        name="qkv",
    )(h, wq, wkt, wvt)


def _attn_setup(q_ref, qm_ref):
    tq, d = q_ref.shape
    lane = lax.broadcasted_iota(jnp.int32, (tq, LANES), 1)
    for p in range(d // LANES):
        qp = q_ref[:, p * LANES:(p + 1) * LANES]
        qm_ref[p, 0:tq, :] = jnp.where(lane < LANES // 2, qp, jnp.zeros_like(qp))
        qm_ref[p, tq:2 * tq, :] = jnp.where(lane >= LANES // 2, qp, jnp.zeros_like(qp))


def _suffix_sum_matrix(tk):
    src = lax.rem(lax.broadcasted_iota(jnp.int32, (2 * tk, 2 * tk), 0), tk)
    dst = lax.broadcasted_iota(jnp.int32, (2 * tk, 2 * tk), 1)
    return jnp.where((dst >= tk) | (src > dst), 1.0, 0.0).astype(BF16)


def _attn_block(qm_ref, carry_ref, acc_ref, keys, values, transposed, mask, sums):
    n_pairs, tq2, tk = carry_ref.shape
    tq = tq2 // 2
    lane = lax.broadcasted_iota(jnp.int32, (tq, LANES), 1)
    first = mask is not None
    visible = (lambda x: jnp.where(mask, x, 0.0)) if first else (lambda x: x)
    scores = _dot if transposed else _dot_nt
    mix = _dot_nt if transposed else _dot
    zs = [scores(qm_ref[p], keys[p]) for p in range(n_pairs)]
    log_beta, parts = [], []
    for z in zs:
        neg_z = -z
        minus_softplus = jnp.minimum(neg_z, 0.0) - jnp.log(1.0 + jnp.exp(jnp.minimum(z, neg_z)))
        log_keep = visible(minus_softplus)
        hi = log_keep.astype(BF16)
        lo = (log_keep - hi.astype(F32)).astype(BF16)
        parts.append(jnp.concatenate([hi, lo], axis=1))
        log_beta.append(z + minus_softplus)
    sums_out = [_dot(part, sums) for part in parts]
    weights = []
    top = jnp.full((tq2, tk), -jnp.inf, F32)
    for p in range(n_pairs):
        after, carry = sums_out[p][:, :tk], sums_out[p][:, tk:]
        if not first:
            after = after + carry_ref[p]
            carry = carry + carry_ref[p]
        weights.append(visible(jnp.exp(log_beta[p] + after)).astype(BF16))
        carry_ref[p] = carry
        if not first:
            top = jnp.maximum(top, carry)
    for p in range(n_pairs):
        out = mix(weights[p], values[p])
        out = jnp.where(lane < LANES // 2, out[:tq], out[tq:])
        cols = slice(p * LANES, (p + 1) * LANES)
        acc_ref[:, cols] = out if first else acc_ref[:, cols] + out
    return jnp.float32(0.0) if first else jnp.max(top)


def _keep_sweeping(state):
    j, top = state
    return (j >= 0) & (top > -ATTN_EXIT)


def _attn_prompt_kernel(q_ref, kt_ref, vt_ref, o_ref, qm_ref, carry_ref, acc_ref):
    qi = pl.program_id(1)
    tq, d = q_ref.shape
    tk = kt_ref.shape[-1]
    n_pairs = d // LANES
    _attn_setup(q_ref, qm_ref)
    row_pos = qi * tq + lax.rem(lax.broadcasted_iota(jnp.int32, (2 * tq, tk), 0), tq)
    col = lax.broadcasted_iota(jnp.int32, (2 * tq, tk), 1)
    sums = _suffix_sum_matrix(tk)

    def block(j, mask):
        keys = [kt_ref[0, j, p * LANES:(p + 1) * LANES, :] for p in range(n_pairs)]
        values = [vt_ref[0, j, p * LANES:(p + 1) * LANES, :] for p in range(n_pairs)]
        return _attn_block(qm_ref, carry_ref, acc_ref, keys, values, True, mask, sums)

    j0 = ((qi + 1) * tq - 2) // tk
    top = block(j0, (j0 * tk + col) < row_pos)
    lax.while_loop(_keep_sweeping, lambda state: (state[0] - 1, block(state[0], None)), (j0 - 1, top))
    o_ref[...] = acc_ref[...].astype(o_ref.dtype)


def _attention_prompt(q, kt, vt):
    n_streams, n_kblocks, d, tk = kt.shape
    length = n_kblocks * tk
    tq = min(length, ATTN_TQ)
    assert tk % tq == 0
    nq = length // tq
    kv = pl.BlockSpec((1, n_kblocks, d, tk), lambda b, i: (b, 0, 0, 0))
    rows = pl.BlockSpec((tq, d), lambda b, i: (b * nq + i, 0))
    return pl.pallas_call(
        _attn_prompt_kernel,
        grid=(n_streams, nq),
        in_specs=[rows, kv, kv],
        out_specs=rows,
        out_shape=jax.ShapeDtypeStruct((n_streams * length, d), BF16),
        scratch_shapes=[pltpu.VMEM((d // LANES, 2 * tq, LANES), BF16), pltpu.VMEM((d // LANES, 2 * tq, tk), F32),
                        pltpu.VMEM((tq, d), F32)],
        compiler_params=_params(("arbitrary", "arbitrary")),
        name="stick_breaking_prompt",
    )(q, kt, vt)


def _attn_sample_kernel(q_ref, kn_ref, vn_ref, ck_hbm, cv_hbm, o_ref, qm_ref, carry_ref, acc_ref, kbuf, vbuf, sem):
    b = pl.program_id(0)
    t, d = q_ref.shape
    tk = kbuf.shape[-1]
    n_pairs = d // LANES
    n_cache_blocks = ck_hbm.shape[1] // tk

    def fetch(j, slot):
        src = (pl.ds(pl.multiple_of(b * d, d), d), pl.ds(pl.multiple_of(j * tk, tk), tk))
        return (pltpu.make_async_copy(ck_hbm.at[src], kbuf.at[slot], sem.at[0, slot]),
                pltpu.make_async_copy(cv_hbm.at[src], vbuf.at[slot], sem.at[1, slot]))

    for cp in fetch(n_cache_blocks - 1, (n_cache_blocks - 1) % 2):
        cp.start()
    _attn_setup(q_ref, qm_ref)
    row = lax.rem(lax.broadcasted_iota(jnp.int32, (2 * t, tk), 0), t)
    col = lax.broadcasted_iota(jnp.int32, (2 * t, tk), 1)
    sums = _suffix_sum_matrix(tk)
    pad = jnp.zeros((tk - t, LANES), BF16)
    keys = [jnp.concatenate([kn_ref[:, p * LANES:(p + 1) * LANES], pad], axis=0) for p in range(n_pairs)]
    values = [jnp.concatenate([vn_ref[:, p * LANES:(p + 1) * LANES], pad], axis=0) for p in range(n_pairs)]
    top = _attn_block(qm_ref, carry_ref, acc_ref, keys, values, False, col < row, sums)

    def body(state):
        j, _ = state
        slot = lax.rem(j, 2)
        for cp in fetch(j, slot):
            cp.wait()

        @pl.when(j > 0)
        def _():
            for cp in fetch(j - 1, 1 - slot):
                cp.start()

        keys = [kbuf[slot, p * LANES:(p + 1) * LANES, :].astype(BF16) for p in range(n_pairs)]
        values = [vbuf[slot, p * LANES:(p + 1) * LANES, :].astype(BF16) for p in range(n_pairs)]
        return j - 1, _attn_block(qm_ref, carry_ref, acc_ref, keys, values, True, None, sums)

    j_end, _ = lax.while_loop(_keep_sweeping, body, (jnp.int32(n_cache_blocks - 1), top))

    @pl.when(j_end >= 0)
    def _():
        for cp in fetch(j_end, lax.rem(j_end, 2)):
            cp.wait()

    o_ref[...] = acc_ref[...].astype(o_ref.dtype)


def _attention_sample(q, k_new, v_new, cache_kt, cache_vt, t, q_row_off):
    d = q.shape[1]
    n_streams = k_new.shape[0] // t
    tk = ATTN_TK
    assert cache_kt.shape[1] % tk == 0 and t <= tk
    any_spec = pl.BlockSpec(memory_space=pl.ANY)
    new = pl.BlockSpec((t, d), lambda b: (b, 0))
    return pl.pallas_call(
        _attn_sample_kernel,
        grid=(n_streams,),
        in_specs=[pl.BlockSpec((t, d), lambda b: (q_row_off // t + b, 0)), new, new, any_spec, any_spec],
        out_specs=new,
        out_shape=jax.ShapeDtypeStruct((n_streams * t, d), BF16),
        scratch_shapes=[pltpu.VMEM((d // LANES, 2 * t, LANES), BF16), pltpu.VMEM((d // LANES, 2 * t, tk), F32),
                        pltpu.VMEM((t, d), F32), pltpu.VMEM((2, d, tk), F32), pltpu.VMEM((2, d, tk), F32),
                        pltpu.SemaphoreType.DMA((2, 2))],
        compiler_params=_params(("arbitrary",)),
        name="stick_breaking_sample",
    )(q, k_new, v_new, cache_kt, cache_vt)


def kernel(x_prompt, x_sample, state_pool, cache_k, cache_v, norm_mix, norm_ffn, pool_w, pool_scale, w_qkv, w_o, router_w, router_b, moe_w1, moe_b1, moe_w2, moe_b2, final_norm):
    b, s, d = x_prompt.shape
    db, t, _ = x_sample.shape
    past = cache_k.shape[2]
    n_exp = router_w.shape[2]
    hd = d // N_HEADS
    hist = POOL_HIST_ROWS
    n_p, n_s = b * s, db * t
    assert t >= hist and ROW_TILE % t == 0 and s % ROW_TILE == 0 and n_s % ROW_TILE == 0 and ROW_TILE % QKV_TILE == 0
    row = lambda a: a.reshape(1, -1)
    def split_router(w):
        w_t = w.T
        hi = w_t.astype(BF16)
        return jnp.concatenate([hi, (w_t - hi.astype(F32)).astype(BF16)], axis=0)

    wr = [split_router(router_w[i]) for i in range(2)]
    br = [router_b[i].reshape(n_exp, 1) for i in range(2)]

    hist_s = jnp.concatenate([jnp.zeros((db, 1, d), F32), state_pool[0]], axis=1)
    xn, tok, idx, gates, rank, hlast_p, hlast_s, cnt = _pool_layer(
        x_prompt, x_sample, hist_s, past, row(norm_mix[0]), pool_w[0].astype(BF16), row(pool_scale[0]),
        row(norm_ffn[0]), wr[0], br[0])
    x1, h1 = _moe(tok, idx, gates, rank, cnt, xn, row(norm_mix[1]), None, 0, moe_w1, moe_b1[0], moe_w2, moe_b2[0])

    wqkv = w_qkv[0].astype(BF16)
    q, ktb, vtb, ktf, vtf, ksb, vsb, ksf, vsf = _qkv(h1, wqkv[:, :d], wqkv[:, d:2 * d].T, wqkv[:, 2 * d:].T, b, s)
    o_p = _attention_prompt(q, ktb, vtb)
    transposed = lambda cache: cache.transpose(0, 1, 3, 4, 2).reshape(db * d, past)
    o_s = _attention_sample(q, ksb, vsb, transposed(cache_k), transposed(cache_v), t, n_p)
    xn, tok, idx, gates, rank, cnt = _proj_layer(o_p, o_s, x1, w_o[0].astype(BF16), row(norm_ffn[1]), wr[1], br[1])
    y_p, y_s = _moe(tok, idx, gates, rank, cnt, xn, row(final_norm), n_p, 1, moe_w1, moe_b1[1], moe_w2, moe_b2[1])

    frames_major = lambda xt: xt.reshape(1, b, N_HEADS, hd, s).transpose(0, 1, 4, 2, 3)
    heads = lambda a: a.reshape(1, db, t, N_HEADS, hd)
    return (y_p.reshape(b, s, d), y_s.reshape(db, t, d), hlast_p[None, :, 1:, :],
            frames_major(ktf), frames_major(vtf), hlast_s[None, :, 1:, :], heads(ksf), heads(vsf))
```

```python
import functools

import jax
import jax.numpy as jnp
from jax import lax
from jax.experimental import pallas as pl
from jax.experimental.pallas import tpu as pltpu
from jax.experimental.pallas import tpu_sc as plsc

EPS = 1e-5
POOL_WINDOWS = (2, 4, 8, 16)
POOL_HIST_ROWS = 16
N_HEADS = 16
TOP_K = 4
SWIGLU_LIMIT = 7.0
SWIGLU_ALPHA = 1.702
LANES = 128
ROW_TILE = 512
QKV_TILE = 512
ATTN_TQ = 128
ATTN_TK = 128
ATTN_EXIT = 88.0
VMEM_LIMIT = 56 * 1024 * 1024
SC_CORES = 2
SC_SUBCORES = 16
SC_CHUNK = 96
SC_GATHER_CHUNK = 96
COMBINE_TILE = 512

F32 = jnp.float32
BF16 = jnp.bfloat16


def _rms(x, g):
    ms = jnp.mean(x * x, axis=-1, keepdims=True)
    return x * lax.rsqrt(ms + EPS) * g


def _dot(a, b):
    return jnp.dot(a, b, preferred_element_type=F32)


def _dot_nt(a, b, precision=None):
    return lax.dot_general(a, b, (((1,), (1,)), ((), ())), preferred_element_type=F32, precision=precision)


def _pack_halves(x):
    c = x.shape[1] // 2
    bits = lax.bitcast_convert_type(x.astype(BF16).astype(F32), jnp.uint32)
    return bits[:, :c] | (bits[:, c:] >> 16)


def _unpack_halves(w):
    return (lax.bitcast_convert_type(w & jnp.uint32(0xFFFF0000), F32), lax.bitcast_convert_type(w << 16, F32))


def _params(semantics):
    return pltpu.CompilerParams(dimension_semantics=semantics, vmem_limit_bytes=VMEM_LIMIT)


def _full(shape):
    return pl.BlockSpec(shape, lambda i, *_: (0,) * len(shape))


def _route_init(cnt_ref, tri_ref):
    r = tri_ref.shape[0]
    cnt_ref[...] = jnp.zeros(cnt_ref.shape, F32)
    tri_ref[...] = jnp.where(lax.broadcasted_iota(jnp.int32, (r, r), 0) < lax.broadcasted_iota(jnp.int32, (r, r), 1),
                             1.0, 0.0).astype(BF16)


def _route_tail(xn, gffn_ref, wr_ref, br_ref, cnt_ref, tri_ref, tok_ref, idx_ref, gate_ref, rank_ref):
    n_exp = br_ref.shape[0]
    tok = _rms(xn, gffn_ref[...])
    tok_ref[...] = _pack_halves(tok)
    tok_hi = tok.astype(BF16)
    tok_lo = (tok - tok_hi.astype(F32)).astype(BF16)
    by_hi = _dot_nt(wr_ref[...], tok_hi)
    logits = by_hi[:n_exp] + by_hi[n_exp:] + _dot_nt(wr_ref[:n_exp, :], tok_lo) + br_ref[...]
    eidx = lax.broadcasted_iota(jnp.int32, logits.shape, 0).astype(F32)
    vals, idxs = [], []
    l = logits
    for _ in range(TOP_K):
        m = jnp.max(l, axis=0, keepdims=True)
        i = jnp.min(jnp.where(l == m, eidx, float(n_exp)), axis=0, keepdims=True)
        vals.append(m)
        idxs.append(i)
        l = jnp.where(eidx == i, -jnp.inf, l)
    es = [jnp.exp(v - vals[0]) for v in vals]
    den = es[0]
    for e in es[1:]:
        den = den + e
    gate_ref[...] = jnp.concatenate([e / den for e in es], axis=0)
    idx_ref[...] = jnp.concatenate(idxs, axis=0).astype(jnp.int32)
    member = jnp.zeros(logits.shape, F32)
    for i in idxs:
        member = member + jnp.where(eidx == i, 1.0, 0.0)
    before = _dot(member.astype(BF16), tri_ref[...]) + cnt_ref[:, :1]
    ranks = [jnp.sum(jnp.where(eidx == i, before, 0.0), axis=0, keepdims=True) for i in idxs]
    rank_ref[...] = jnp.concatenate(ranks, axis=0).astype(jnp.int32)
    cnt_ref[...] = cnt_ref[...] + jnp.sum(member, axis=1, keepdims=True)


def _pool_mix(h, ext_ref, pos, pw_ref, ps_ref):
    ts, d = h.shape[-2:]
    hist = POOL_HIST_ROWS
    group = d // len(POOL_WINDOWS)
    pre = (slice(None),) * (h.ndim - 2)
    ys = []
    for g, win in enumerate(POOL_WINDOWS):
        cols = slice(g * group, (g + 1) * group)
        hg = h[pre + (slice(None), cols)]
        acc = hg
        for j in range(1, win):
            acc = acc + ext_ref[pre + (slice(hist - j, hist - j + ts), cols)]
        cnt = jnp.minimum(pos + 1, win).astype(F32)
        dg = acc / cnt - hg
        ys.append(_dot(dg.reshape(-1, group).astype(BF16), pw_ref[g]))
    return jnp.concatenate(ys, axis=-1) * ps_ref[...]


def _pool_mix_tiled(h, ext_ref, lvl_ref, pos, pw_ref, ps_ref):
    ts, d = h.shape
    hist = POOL_HIST_ROWS
    group = d // len(POOL_WINDOWS)
    prev_ref, prev_col0 = ext_ref, 0
    ys = []
    for g, win in enumerate(POOL_WINDOWS):
        assert win == 2 * (POOL_WINDOWS[g - 1] if g else 1)
        col0 = g * group
        cols = slice(col0 - prev_col0, d - prev_col0)
        level = prev_ref[hist:hist + ts, cols] + prev_ref[hist - win // 2:hist - win // 2 + ts, cols]
        if g + 1 < len(POOL_WINDOWS):
            lvl_ref[g, hist:hist + ts, 0:d - col0 - group] = level[:, group:]
            prev_ref, prev_col0 = lvl_ref.at[g], col0 + group
        cnt = jnp.minimum(pos + 1, win).astype(F32)
        dg = level[:, :group] / cnt - h[:, col0:col0 + group]
        ys.append(_dot(dg.astype(BF16), pw_ref[g]))
    return jnp.concatenate(ys, axis=-1) * ps_ref[...]


def _pool_kernel(ntp, nt, pos0_s, xp_ref, xs_ref, hist_ref, gmix_ref, pw_ref, ps_ref, gffn_ref, wr_ref, br_ref,
                 xn_ref, tok_ref, idx_ref, gate_ref, rank_ref, hlast_p_ref, hlast_s_ref, cnt_out_ref,
                 ext_p, lvl_p, ext_s, cnt_ref, tri_ref):
    i = pl.program_id(0)
    hist = POOL_HIST_ROWS

    @pl.when(i == 0)
    def _():
        _route_init(cnt_ref, tri_ref)

    @pl.when(i < ntp)
    def _():
        ts, d = xp_ref.shape
        t = lax.rem(i, nt)

        @pl.when(t == 0)
        def _():
            ext_p[0:hist, :] = jnp.zeros((hist, d), F32)
            lvl_p[:, 0:hist, :] = jnp.zeros((lvl_p.shape[0], hist, lvl_p.shape[2]), F32)

        x = xp_ref[...]
        h = _rms(x, gmix_ref[...])
        ext_p[hist:hist + ts, :] = h

        @pl.when(t == nt - 1)
        def _():
            hlast_p_ref[0] = h[ts - hist:, :]

        pos = t * ts + lax.broadcasted_iota(jnp.int32, (ts, 1), 0)
        xn_ref[...] = x + _pool_mix_tiled(h, ext_p, lvl_p, pos, pw_ref, ps_ref)
        ext_p[0:hist, :] = ext_p[ts:ts + hist, :]
        lvl_p[:, 0:hist, :] = lvl_p[:, ts:ts + hist, :]

    @pl.when(i >= ntp)
    def _():
        bb, ts, d = xs_ref.shape
        ext_s[:, 0:hist, :] = hist_ref[...]
        x = xs_ref[...]
        h = _rms(x, gmix_ref[...])
        ext_s[:, hist:hist + ts, :] = h
        hlast_s_ref[...] = h[:, ts - hist:, :]
        pos = pos0_s + lax.broadcasted_iota(jnp.int32, (1, ts, 1), 1)
        xn_ref[...] = x.reshape(bb * ts, d) + _pool_mix(h, ext_s, pos, pw_ref, ps_ref)

    _route_tail(xn_ref[...], gffn_ref, wr_ref, br_ref, cnt_ref, tri_ref, tok_ref, idx_ref, gate_ref, rank_ref)
    cnt_out_ref[...] = cnt_ref[...]


def _pool_layer(x_prompt, x_sample, hist_s, pos0_s, gmix, pw, ps, gffn, wr, br):
    b, s, d = x_prompt.shape
    db, t, _ = x_sample.shape
    n_exp = br.shape[0]
    r = ROW_TILE
    nt = s // r
    ntp = b * nt
    bb = r // t
    nts = db // bb
    n = b * s + db * t
    hist = POOL_HIST_ROWS
    rows = pl.BlockSpec((r, d), lambda i: (i, 0))
    lanes = pl.BlockSpec((TOP_K, r), lambda i: (0, i))
    sample_blk = lambda i: (jnp.maximum(i - ntp, 0), 0, 0)
    return pl.pallas_call(
        functools.partial(_pool_kernel, ntp, nt, pos0_s),
        grid=(ntp + nts,),
        in_specs=[pl.BlockSpec((r, d), lambda i: (jnp.minimum(i, ntp - 1), 0)),
                  pl.BlockSpec((bb, t, d), sample_blk), pl.BlockSpec((bb, hist, d), sample_blk),
                  _full((1, d)), _full(pw.shape), _full((1, d)), _full((1, d)), _full((2 * n_exp, d)), _full((n_exp, 1))],
        out_specs=[rows, pl.BlockSpec((r, d // 2), lambda i: (i, 0)), lanes, lanes, lanes,
                   pl.BlockSpec((1, hist, d), lambda i: (jnp.minimum(i // nt, b - 1), 0, 0)),
                   pl.BlockSpec((bb, hist, d), sample_blk), _full((n_exp, LANES))],
        out_shape=[jax.ShapeDtypeStruct((n, d), F32), jax.ShapeDtypeStruct((n, d // 2), jnp.uint32),
                   jax.ShapeDtypeStruct((TOP_K, n), jnp.int32), jax.ShapeDtypeStruct((TOP_K, n), F32),
                   jax.ShapeDtypeStruct((TOP_K, n), jnp.int32),
                   jax.ShapeDtypeStruct((b, hist, d), F32), jax.ShapeDtypeStruct((db, hist, d), F32),
                   jax.ShapeDtypeStruct((n_exp, LANES), F32)],
        scratch_shapes=[pltpu.VMEM((hist + r, d), F32),
                        pltpu.VMEM((len(POOL_WINDOWS) - 1, hist + r, d - d // len(POOL_WINDOWS)), F32),
                        pltpu.VMEM((bb, hist + t, d), F32), pltpu.VMEM((n_exp, LANES), F32), pltpu.VMEM((r, r), BF16)],
        compiler_params=_params(("arbitrary",)),
        name="pool_route",
    )(x_prompt.reshape(b * s, d), x_sample, hist_s, gmix, pw, ps, gffn, wr, br)


def _proj_kernel(ntp, op_ref, os_ref, x_ref, wo_ref, gffn_ref, wr_ref, br_ref,
                 xn_ref, tok_ref, idx_ref, gate_ref, rank_ref, cnt_out_ref, cnt_ref, tri_ref):
    i = pl.program_id(0)

    @pl.when(i == 0)
    def _():
        _route_init(cnt_ref, tri_ref)

    o = jnp.where(i < ntp, op_ref[...], os_ref[...])
    xn = x_ref[...] + _dot(o, wo_ref[...])
    xn_ref[...] = xn
    _route_tail(xn, gffn_ref, wr_ref, br_ref, cnt_ref, tri_ref, tok_ref, idx_ref, gate_ref, rank_ref)
    cnt_out_ref[...] = cnt_ref[...]


def _proj_layer(o_p, o_s, x, wo, gffn, wr, br):
    n, d = x.shape
    n_exp = br.shape[0]
    r = ROW_TILE
    ntp = o_p.shape[0] // r
    rows = pl.BlockSpec((r, d), lambda i: (i, 0))
    lanes = pl.BlockSpec((TOP_K, r), lambda i: (0, i))
    return pl.pallas_call(
        functools.partial(_proj_kernel, ntp),
        grid=(n // r,),
        in_specs=[pl.BlockSpec((r, d), lambda i: (jnp.minimum(i, ntp - 1), 0)),
                  pl.BlockSpec((r, d), lambda i: (jnp.maximum(i - ntp, 0), 0)),
                  rows, _full((d, d)), _full((1, d)), _full((2 * n_exp, d)), _full((n_exp, 1))],
        out_specs=[rows, pl.BlockSpec((r, d // 2), lambda i: (i, 0)), lanes, lanes, lanes, _full((n_exp, LANES))],
        out_shape=[jax.ShapeDtypeStruct((n, d), F32), jax.ShapeDtypeStruct((n, d // 2), jnp.uint32),
                   jax.ShapeDtypeStruct((TOP_K, n), jnp.int32), jax.ShapeDtypeStruct((TOP_K, n), F32),
                   jax.ShapeDtypeStruct((TOP_K, n), jnp.int32), jax.ShapeDtypeStruct((n_exp, LANES), F32)],
        scratch_shapes=[pltpu.VMEM((n_exp, LANES), F32), pltpu.VMEM((r, r), BF16)],
        compiler_params=_params(("arbitrary",)),
        name="proj_route",
    )(o_p, o_s, x, wo, gffn, wr, br)


def _sc_chunks(dest, w):
    n = dest.shape[1]
    return dest.reshape(TOP_K, n // w, w).transpose(1, 0, 2)


def _dispatch(tok, dest, n_buf_rows):
    n, d = tok.shape
    w = SC_CHUNK
    n_workers = SC_CORES * SC_SUBCORES
    per_worker = n // (w * n_workers)
    assert per_worker * w * n_workers == n
    mesh = plsc.VectorSubcoreMesh(core_axis_name="core", subcore_axis_name="subcore",
                                  num_cores=SC_CORES, num_subcores=SC_SUBCORES)

    @functools.partial(
        pl.kernel, mesh=mesh, out_type=jax.ShapeDtypeStruct((n_buf_rows, d), tok.dtype),
        scratch_types=[pltpu.VMEM((TOP_K, w), jnp.int32), pltpu.VMEM((w, d), tok.dtype), pltpu.SemaphoreType.DMA],
        name="moe_dispatch_sc")
    def scatter_rows(tok_hbm, dest_hbm, buf_hbm, idx_v, rows_v, sem):
        worker = lax.axis_index("subcore") * SC_CORES + lax.axis_index("core")

        @pl.loop(0, per_worker)
        def _(c):
            chunk = worker * per_worker + c
            pltpu.sync_copy(dest_hbm.at[chunk], idx_v)
            pltpu.sync_copy(tok_hbm.at[pl.ds(pl.multiple_of(chunk * w, w), w)], rows_v)
            copies = [pltpu.async_copy(rows_v, buf_hbm.at[idx_v.at[k]], sem) for k in range(TOP_K)]
            for cp in copies:
                cp.wait()

    return scatter_rows(tok, _sc_chunks(dest, w))


def _expert_kernel(layer, be_ref, valid_ref, nused_ref, next_ref, slot_ref, x_ref, w1_hbm, b1_ref, w2_hbm, b2_ref,
                   o_ref, w1f_ref, w2f_ref, w1b_ref, w2b_ref, sem):
    i = pl.program_id(0)
    used = valid_ref[i] > 0

    def fetch(expert, slot):
        return (pltpu.make_async_copy(w1_hbm.at[layer, expert], w1f_ref.at[slot], sem.at[0, slot]),
                pltpu.make_async_copy(w2_hbm.at[layer, expert], w2f_ref.at[slot], sem.at[1, slot]))

    @pl.when(used & ((i == 0) | (be_ref[i] != be_ref[jnp.maximum(i - 1, 0)])))
    def _():
        slot = slot_ref[i]

        @pl.when(i == 0)
        def _():
            for cp in fetch(be_ref[i], slot):
                cp.start()

        for cp in fetch(be_ref[i], slot):
            cp.wait()

        @pl.when(next_ref[i] >= 0)
        def _():
            for cp in fetch(next_ref[i], 1 - slot):
                cp.start()

        w1b_ref[...] = w1f_ref[slot].astype(BF16)
        w2b_ref[...] = w2f_ref[slot].astype(BF16)

    @pl.when(used)
    def _():
        f = w2b_ref.shape[0]
        row = lax.broadcasted_iota(jnp.int32, (x_ref.shape[0], 1), 0)
        x = jnp.where(row < valid_ref[i], x_ref[...], jnp.uint32(0))
        xa, xb = _unpack_halves(x)
        gu = _dot(jnp.concatenate([xa.astype(BF16), xb.astype(BF16)], axis=1), w1b_ref[...]) + b1_ref[0]
        gate = jnp.minimum(gu[:, :f], SWIGLU_LIMIT)
        up = jnp.clip(gu[:, f:], -SWIGLU_LIMIT, SWIGLU_LIMIT)
        hid = (up + 1.0) * (gate * jax.nn.sigmoid(gate * SWIGLU_ALPHA))
        o_ref[...] = _pack_halves(_dot(hid.astype(BF16), w2b_ref[...]) + b2_ref[0])

    @pl.when(jnp.logical_not(used))
    def _():
        o_ref[...] = jnp.zeros(o_ref.shape, o_ref.dtype)


def _experts(buf, block_expert, valid, nused, next_expert, slot, layer, w1, b1, w2, b2, tm):
    rows, dw = buf.shape
    _, n_exp, d, f2 = w1.shape
    f = w2.shape[2]
    in_rows = lambda i, be, va, nu, nx, sl: (jnp.minimum(i, nu[0] - 1), 0)
    by_expert = lambda i, be, va, nu, nx, sl: (be[i], 0, 0)
    any_spec = pl.BlockSpec(memory_space=pl.ANY)
    return pl.pallas_call(
        functools.partial(_expert_kernel, layer),
        grid_spec=pltpu.PrefetchScalarGridSpec(
            num_scalar_prefetch=5,
            grid=(rows // tm,),
            in_specs=[pl.BlockSpec((tm, dw), in_rows), any_spec, pl.BlockSpec((1, 1, f2), by_expert),
                      any_spec, pl.BlockSpec((1, 1, d), by_expert)],
            out_specs=pl.BlockSpec((tm, dw), lambda i, be, va, nu, nx, sl: (i, 0)),
            scratch_shapes=[pltpu.VMEM((2, d, f2), F32), pltpu.VMEM((2, f, d), F32),
                            pltpu.VMEM((d, f2), BF16), pltpu.VMEM((f, d), BF16), pltpu.SemaphoreType.DMA((2, 2))]),
        out_shape=jax.ShapeDtypeStruct((rows, dw), jnp.uint32),
        compiler_params=_params(("arbitrary",)),
        name="moe_experts",
    )(block_expert, valid, nused, next_expert, slot, buf, w1, b1.reshape(n_exp, 1, f2), w2, b2.reshape(n_exp, 1, d))


def _gather_rows(ybuf, dest):
    _, d = ybuf.shape
    n = dest.shape[1]
    w = SC_GATHER_CHUNK
    n_workers = SC_CORES * SC_SUBCORES
    per_worker = n // (w * n_workers)
    assert per_worker * w * n_workers == n and TOP_K % 2 == 0
    mesh = plsc.VectorSubcoreMesh(core_axis_name="core", subcore_axis_name="subcore",
                                  num_cores=SC_CORES, num_subcores=SC_SUBCORES)

    @functools.partial(
        pl.kernel, mesh=mesh, out_type=jax.ShapeDtypeStruct((TOP_K, n, d), ybuf.dtype),
        scratch_types=[pltpu.VMEM((TOP_K, w), jnp.int32), pltpu.VMEM((2, w, d), ybuf.dtype),
                       pltpu.SemaphoreType.DMA((2,)), pltpu.SemaphoreType.DMA((2,))],
        name="moe_gather_sc")
    def gather_rows(y_hbm, dest_hbm, out_hbm, idx_v, rows_v, sem_in, sem_out):
        worker = lax.axis_index("subcore") * SC_CORES + lax.axis_index("core")

        @pl.loop(0, per_worker)
        def _(c):
            chunk = worker * per_worker + c
            tokens = pl.ds(pl.multiple_of(chunk * w, w), w)
            pltpu.sync_copy(dest_hbm.at[chunk], idx_v)
            fetch = lambda k: pltpu.async_copy(y_hbm.at[idx_v.at[k]], rows_v.at[k % 2], sem_in.at[k % 2])
            store = lambda k: pltpu.async_copy(rows_v.at[k % 2], out_hbm.at[k, tokens], sem_out.at[k % 2])
            fetches = [fetch(0), fetch(1)]
            stores = []
            for k in range(TOP_K):
                fetches[k].wait()
                stores.append(store(k))
                if k + 2 < TOP_K:
                    stores[k].wait()
                    fetches.append(fetch(k + 2))
            for k in range(TOP_K - 2, TOP_K):
                stores[k].wait()

    return gather_rows(ybuf, _sc_chunks(dest, w))


def _combine_kernel(ntp, rows_ref, gate_ref, x_ref, g_ref, *outs):
    i = pl.program_id(0)
    r = x_ref.shape[0]
    gates = jnp.concatenate([gate_ref[...], jnp.zeros((LANES - TOP_K, r), F32)], axis=0)
    gates_t = jnp.transpose(gates)
    halves = [_unpack_halves(rows_ref[k]) for k in range(TOP_K)]
    f = []
    for side in range(2):
        acc = halves[0][side] * gates_t[:, 0:1]
        for k in range(1, TOP_K):
            acc = acc + halves[k][side] * gates_t[:, k:k + 1]
        f.append(acc)
    xo = x_ref[...] + jnp.concatenate(f, axis=1)
    hn = _rms(xo, g_ref[...])
    if ntp is None:
        xo_ref, hn_ref = outs
        xo_ref[...] = xo
        hn_ref[...] = hn.astype(hn_ref.dtype)
    else:
        hp_ref, hs_ref = outs

        @pl.when(i < ntp)
        def _():
            hp_ref[...] = hn

        @pl.when(i >= ntp)
        def _():
            hs_ref[...] = hn


def _combine(rows4, gates, x, g, n_p):
    n, d = x.shape
    r = COMBINE_TILE
    rows = pl.BlockSpec((r, d), lambda i: (i, 0))
    if n_p is None:
        ntp = None
        out_specs = [rows, rows]
        out_shape = [jax.ShapeDtypeStruct((n, d), F32), jax.ShapeDtypeStruct((n, d), BF16)]
    else:
        ntp = n_p // r
        out_specs = [pl.BlockSpec((r, d), lambda i: (jnp.minimum(i, ntp - 1), 0)),
                     pl.BlockSpec((r, d), lambda i: (jnp.maximum(i - ntp, 0), 0))]
        out_shape = [jax.ShapeDtypeStruct((n_p, d), F32), jax.ShapeDtypeStruct((n - n_p, d), F32)]
    return pl.pallas_call(
        functools.partial(_combine_kernel, ntp),
        grid=(n // r,),
        in_specs=[pl.BlockSpec((TOP_K, r, d // 2), lambda i: (0, i, 0)), pl.BlockSpec((TOP_K, r), lambda i: (0, i)),
                  rows, _full((1, d))],
        out_specs=out_specs,
        out_shape=out_shape,
        compiler_params=_params(("arbitrary",)),
        name="moe_combine",
    )(rows4, gates, x, g)


def _moe(tok, idx, gates, rank, cnt, x, g_next, n_p, layer, w1, b1, w2, b2):
    n = tok.shape[0]
    n_exp = w1.shape[1]
    pairs = n * TOP_K
    tm = 128
    for cand in (512, 256):
        if pairs >= 4 * cand * n_exp and pairs % cand == 0:
            tm = cand
            break
    n_blocks = -(-pairs // tm) + n_exp
    counts = cnt[:, 0].astype(jnp.int32)
    padded = (counts + tm - 1) // tm * tm
    pad_end = jnp.cumsum(padded)
    pad_start = pad_end - padded
    experts = jnp.arange(n_exp, dtype=jnp.int32)
    dest = rank + jnp.sum(jnp.where(idx[:, :, None] == experts, pad_start, 0), axis=-1)
    starts = jnp.arange(n_blocks, dtype=jnp.int32) * tm
    nused = (pad_end[-1:] // tm).astype(jnp.int32)
    block_expert = jnp.minimum(jnp.sum((pad_end[None, :] <= starts[:, None]).astype(jnp.int32), axis=1), n_exp - 1)
    tokens_end = jnp.sum(jnp.where(block_expert[:, None] == experts, pad_start + counts, 0), axis=-1)
    valid = jnp.clip(tokens_end - starts, 0, tm).astype(jnp.int32)
    has_tokens = counts > 0
    later = has_tokens[None, :] & (experts[None, :] > experts[:, None])
    successor = jnp.min(jnp.where(later, experts[None, :], n_exp), axis=1)
    successor = jnp.where(successor == n_exp, -1, successor)
    parity = (jnp.cumsum(has_tokens.astype(jnp.int32)) - 1) % 2
    of_block = lambda table: jnp.sum(jnp.where(block_expert[:, None] == experts, table, 0), axis=-1).astype(jnp.int32)
    buf = _dispatch(tok, dest, n_blocks * tm)
    ybuf = _experts(buf, block_expert, valid, nused, of_block(successor), of_block(parity), layer, w1, b1, w2, b2, tm)
    return _combine(_gather_rows(ybuf, dest), gates, x, g_next, n_p)


def _qkv_kernel(ntp, h_ref, wq_ref, wkt_ref, wvt_ref, q_ref, ktb_ref, vtb_ref, ktf_ref, vtf_ref,
                ksb_ref, vsb_ref, ksf_ref, vsf_ref):
    i = pl.program_id(0)
    r, d = h_ref.shape
    tk = ktb_ref.shape[-1]
    h = h_ref[...]
    q_ref[...] = (_dot(h, wq_ref[...]) * ((d // N_HEADS) ** -0.5)).astype(BF16)

    @pl.when(i < ntp)
    def _():
        for wt_ref, tb_ref, tf_ref in ((wkt_ref, ktb_ref, ktf_ref), (wvt_ref, vtb_ref, vtf_ref)):
            xt = _dot_nt(wt_ref[...], h)
            tf_ref[...] = xt
            for c in range(r // tk):
                tb_ref[0, c] = xt[:, c * tk:(c + 1) * tk].astype(BF16)

    @pl.when(i >= ntp)
    def _():
        for wt_ref, sb_ref, sf_ref in ((wkt_ref, ksb_ref, ksf_ref), (wvt_ref, vsb_ref, vsf_ref)):
            x = _dot_nt(h, wt_ref[...])
            sf_ref[...] = x
            sb_ref[...] = x.astype(BF16)


def _qkv(h, wq, wkt, wvt, n_streams, length):
    n, d = h.shape
    r = QKV_TILE
    tk = ATTN_TK
    n_p = n_streams * length
    nt = length // r
    ntp = n_p // r
    rows = pl.BlockSpec((r, d), lambda i: (i, 0))
    prompt_blk = lambda i: (jnp.minimum(i, ntp - 1) // nt, lax.rem(jnp.minimum(i, ntp - 1), nt))
    t_blocks = pl.BlockSpec((1, r // tk, d, tk), lambda i: prompt_blk(i) + (0, 0))
    t_full = pl.BlockSpec((d, r), prompt_blk)
    sample = pl.BlockSpec((r, d), lambda i: (jnp.maximum(i - ntp, 0), 0))
    w_spec = _full((d, d))
    return pl.pallas_call(
        functools.partial(_qkv_kernel, ntp),
        grid=(n // r,),
        in_specs=[rows, w_spec, w_spec, w_spec],
        out_specs=[rows, t_blocks, t_blocks, t_full, t_full, sample, sample, sample, sample],
        out_shape=[jax.ShapeDtypeStruct((n, d), BF16)]
        + [jax.ShapeDtypeStruct((n_streams, length // tk, d, tk), BF16)] * 2
        + [jax.ShapeDtypeStruct((n_streams * d, length), F32)] * 2
        + [jax.ShapeDtypeStruct((n - n_p, d), BF16)] * 2 + [jax.ShapeDtypeStruct((n - n_p, d), F32)] * 2,
        compiler_params=_params(("arbitrary",)),
        name="qkv",
    )(h, wq, wkt, wvt)


def _attn_setup(q_ref, qm_ref):
    tq, d = q_ref.shape
    lane = lax.broadcasted_iota(jnp.int32, (tq, LANES), 1)
    for p in range(d // LANES):
        qp = q_ref[:, p * LANES:(p + 1) * LANES]
        qm_ref[p, 0:tq, :] = jnp.where(lane < LANES // 2, qp, jnp.zeros_like(qp))
        qm_ref[p, tq:2 * tq, :] = jnp.where(lane >= LANES // 2, qp, jnp.zeros_like(qp))


def _suffix_sum_matrix(tk):
    src = lax.rem(lax.broadcasted_iota(jnp.int32, (2 * tk, 2 * tk), 0), tk)
    dst = lax.broadcasted_iota(jnp.int32, (2 * tk, 2 * tk), 1)
    return jnp.where((dst >= tk) | (src > dst), 1.0, 0.0).astype(BF16)


def _attn_block(qm_ref, carry_ref, acc_ref, keys, values, transposed, mask, sums):
    n_pairs, tq2, tk = carry_ref.shape
    tq = tq2 // 2
    lane = lax.broadcasted_iota(jnp.int32, (tq, LANES), 1)
    first = mask is not None
    visible = (lambda x: jnp.where(mask, x, 0.0)) if first else (lambda x: x)
    scores = _dot if transposed else _dot_nt
    mix = _dot_nt if transposed else _dot
    zs = [scores(qm_ref[p], keys[p]) for p in range(n_pairs)]
    log_beta, parts = [], []
    for z in zs:
        neg_z = -z
        minus_softplus = jnp.minimum(neg_z, 0.0) - jnp.log(1.0 + jnp.exp(jnp.minimum(z, neg_z)))
        log_keep = visible(minus_softplus)
        hi = log_keep.astype(BF16)
        lo = (log_keep - hi.astype(F32)).astype(BF16)
        parts.append(jnp.concatenate([hi, lo], axis=1))
        log_beta.append(z + minus_softplus)
    sums_out = [_dot(part, sums) for part in parts]
    weights = []
    top = jnp.full((tq2, tk), -jnp.inf, F32)
    for p in range(n_pairs):
        after, carry = sums_out[p][:, :tk], sums_out[p][:, tk:]
        if not first:
            after = after + carry_ref[p]
            carry = carry + carry_ref[p]
        weights.append(visible(jnp.exp(log_beta[p] + after)).astype(BF16))
        carry_ref[p] = carry
        if not first:
            top = jnp.maximum(top, carry)
    for p in range(n_pairs):
        out = mix(weights[p], values[p])
        out = jnp.where(lane < LANES // 2, out[:tq], out[tq:])
        cols = slice(p * LANES, (p + 1) * LANES)
        acc_ref[:, cols] = out if first else acc_ref[:, cols] + out
    return jnp.float32(0.0) if first else jnp.max(top)


def _keep_sweeping(state):
    j, top = state
    return (j >= 0) & (top > -ATTN_EXIT)


def _attn_prompt_kernel(q_ref, kt_ref, vt_ref, o_ref, qm_ref, carry_ref, acc_ref):
    qi = pl.program_id(1)
    tq, d = q_ref.shape
    tk = kt_ref.shape[-1]
    n_pairs = d // LANES
    _attn_setup(q_ref, qm_ref)
    row_pos = qi * tq + lax.rem(lax.broadcasted_iota(jnp.int32, (2 * tq, tk), 0), tq)
    col = lax.broadcasted_iota(jnp.int32, (2 * tq, tk), 1)
    sums = _suffix_sum_matrix(tk)

    def block(j, mask):
        keys = [kt_ref[0, j, p * LANES:(p + 1) * LANES, :] for p in range(n_pairs)]
        values = [vt_ref[0, j, p * LANES:(p + 1) * LANES, :] for p in range(n_pairs)]
        return _attn_block(qm_ref, carry_ref, acc_ref, keys, values, True, mask, sums)

    j0 = ((qi + 1) * tq - 2) // tk
    top = block(j0, (j0 * tk + col) < row_pos)
    lax.while_loop(_keep_sweeping, lambda state: (state[0] - 1, block(state[0], None)), (j0 - 1, top))
    o_ref[...] = acc_ref[...].astype(o_ref.dtype)


def _attention_prompt(q, kt, vt):
    n_streams, n_kblocks, d, tk = kt.shape
    length = n_kblocks * tk
    tq = min(length, ATTN_TQ)
    assert tk % tq == 0
    nq = length // tq
    kv = pl.BlockSpec((1, n_kblocks, d, tk), lambda b, i: (b, 0, 0, 0))
    rows = pl.BlockSpec((tq, d), lambda b, i: (b * nq + i, 0))
    return pl.pallas_call(
        _attn_prompt_kernel,
        grid=(n_streams, nq),
        in_specs=[rows, kv, kv],
        out_specs=rows,
        out_shape=jax.ShapeDtypeStruct((n_streams * length, d), BF16),
        scratch_shapes=[pltpu.VMEM((d // LANES, 2 * tq, LANES), BF16), pltpu.VMEM((d // LANES, 2 * tq, tk), F32),
                        pltpu.VMEM((tq, d), F32)],
        compiler_params=_params(("arbitrary", "arbitrary")),
        name="stick_breaking_prompt",
    )(q, kt, vt)


def _attn_sample_kernel(q_ref, kn_ref, vn_ref, ck_hbm, cv_hbm, o_ref, qm_ref, carry_ref, acc_ref, kbuf, vbuf, sem):
    b = pl.program_id(0)
    t, d = q_ref.shape
    tk = kbuf.shape[-1]
    n_pairs = d // LANES
    n_cache_blocks = ck_hbm.shape[1] // tk

    def fetch(j, slot):
        src = (pl.ds(pl.multiple_of(b * d, d), d), pl.ds(pl.multiple_of(j * tk, tk), tk))
        return (pltpu.make_async_copy(ck_hbm.at[src], kbuf.at[slot], sem.at[0, slot]),
                pltpu.make_async_copy(cv_hbm.at[src], vbuf.at[slot], sem.at[1, slot]))

    for cp in fetch(n_cache_blocks - 1, (n_cache_blocks - 1) % 2):
        cp.start()
    _attn_setup(q_ref, qm_ref)
    row = lax.rem(lax.broadcasted_iota(jnp.int32, (2 * t, tk), 0), t)
    col = lax.broadcasted_iota(jnp.int32, (2 * t, tk), 1)
    sums = _suffix_sum_matrix(tk)
    pad = jnp.zeros((tk - t, LANES), BF16)
    keys = [jnp.concatenate([kn_ref[:, p * LANES:(p + 1) * LANES], pad], axis=0) for p in range(n_pairs)]
    values = [jnp.concatenate([vn_ref[:, p * LANES:(p + 1) * LANES], pad], axis=0) for p in range(n_pairs)]
    top = _attn_block(qm_ref, carry_ref, acc_ref, keys, values, False, col < row, sums)

    def body(state):
        j, _ = state
        slot = lax.rem(j, 2)
        for cp in fetch(j, slot):
            cp.wait()

        @pl.when(j > 0)
        def _():
            for cp in fetch(j - 1, 1 - slot):
                cp.start()

        keys = [kbuf[slot, p * LANES:(p + 1) * LANES, :].astype(BF16) for p in range(n_pairs)]
        values = [vbuf[slot, p * LANES:(p + 1) * LANES, :].astype(BF16) for p in range(n_pairs)]
        return j - 1, _attn_block(qm_ref, carry_ref, acc_ref, keys, values, True, None, sums)

    j_end, _ = lax.while_loop(_keep_sweeping, body, (jnp.int32(n_cache_blocks - 1), top))

    @pl.when(j_end >= 0)
    def _():
        for cp in fetch(j_end, lax.rem(j_end, 2)):
            cp.wait()

    o_ref[...] = acc_ref[...].astype(o_ref.dtype)


def _attention_sample(q, k_new, v_new, cache_kt, cache_vt, t, q_row_off):
    d = q.shape[1]
    n_streams = k_new.shape[0] // t
    tk = ATTN_TK
    assert cache_kt.shape[1] % tk == 0 and t <= tk
    any_spec = pl.BlockSpec(memory_space=pl.ANY)
    new = pl.BlockSpec((t, d), lambda b: (b, 0))
    return pl.pallas_call(
        _attn_sample_kernel,
        grid=(n_streams,),
        in_specs=[pl.BlockSpec((t, d), lambda b: (q_row_off // t + b, 0)), new, new, any_spec, any_spec],
        out_specs=new,
        out_shape=jax.ShapeDtypeStruct((n_streams * t, d), BF16),
        scratch_shapes=[pltpu.VMEM((d // LANES, 2 * t, LANES), BF16), pltpu.VMEM((d // LANES, 2 * t, tk), F32),
                        pltpu.VMEM((t, d), F32), pltpu.VMEM((2, d, tk), F32), pltpu.VMEM((2, d, tk), F32),
                        pltpu.SemaphoreType.DMA((2, 2))],
        compiler_params=_params(("arbitrary",)),
        name="stick_breaking_sample",
    )(q, k_new, v_new, cache_kt, cache_vt)


def kernel(x_prompt, x_sample, state_pool, cache_k, cache_v, norm_mix, norm_ffn, pool_w, pool_scale, w_qkv, w_o, router_w, router_b, moe_w1, moe_b1, moe_w2, moe_b2, final_norm):
    b, s, d = x_prompt.shape
    db, t, _ = x_sample.shape
    past = cache_k.shape[2]
    n_exp = router_w.shape[2]
    hd = d // N_HEADS
    hist = POOL_HIST_ROWS
    n_p, n_s = b * s, db * t
    assert t >= hist and ROW_TILE % t == 0 and s % ROW_TILE == 0 and n_s % ROW_TILE == 0 and ROW_TILE % QKV_TILE == 0
    row = lambda a: a.reshape(1, -1)
    def split_router(w):
        w_t = w.T
        hi = w_t.astype(BF16)
        return jnp.concatenate([hi, (w_t - hi.astype(F32)).astype(BF16)], axis=0)

    wr = [split_router(router_w[i]) for i in range(2)]
    br = [router_b[i].reshape(n_exp, 1) for i in range(2)]

    hist_s = jnp.concatenate([jnp.zeros((db, 1, d), F32), state_pool[0]], axis=1)
    xn, tok, idx, gates, rank, hlast_p, hlast_s, cnt = _pool_layer(
        x_prompt, x_sample, hist_s, past, row(norm_mix[0]), pool_w[0].astype(BF16), row(pool_scale[0]),
        row(norm_ffn[0]), wr[0], br[0])
    x1, h1 = _moe(tok, idx, gates, rank, cnt, xn, row(norm_mix[1]), None, 0, moe_w1, moe_b1[0], moe_w2, moe_b2[0])

    wqkv = w_qkv[0].astype(BF16)
    q, ktb, vtb, ktf, vtf, ksb, vsb, ksf, vsf = _qkv(h1, wqkv[:, :d], wqkv[:, d:2 * d].T, wqkv[:, 2 * d:].T, b, s)
    o_p = _attention_prompt(q, ktb, vtb)
    transposed = lambda cache: cache.transpose(0, 1, 3, 4, 2).reshape(db * d, past)
    o_s = _attention_sample(q, ksb, vsb, transposed(cache_k), transposed(cache_v), t, n_p)
    xn, tok, idx, gates, rank, cnt = _proj_layer(o_p, o_s, x1, w_o[0].astype(BF16), row(norm_ffn[1]), wr[1], br[1])
    y_p, y_s = _moe(tok, idx, gates, rank, cnt, xn, row(final_norm), n_p, 1, moe_w1, moe_b1[1], moe_w2, moe_b2[1])

    frames_major = lambda xt: xt.reshape(1, b, N_HEADS, hd, s).transpose(0, 1, 4, 2, 3)
    heads = lambda a: a.reshape(1, db, t, N_HEADS, hd)
    return (y_p.reshape(b, s, d), y_s.reshape(db, t, d), hlast_p[None, :, 1:, :],
            frames_major(ktf), frames_major(vtf), hlast_s[None, :, 1:, :], heads(ksf), heads(vsf))
```

```python
import functools

import jax
import jax.numpy as jnp
from jax import lax
from jax.experimental import pallas as pl
from jax.experimental.pallas import tpu as pltpu
from jax.experimental.pallas import tpu_sc as plsc

EPS = 1e-5
POOL_WINDOWS = (2, 4, 8, 16)
POOL_HIST_ROWS = 16
N_HEADS = 16
TOP_K = 4
SWIGLU_LIMIT = 7.0
SWIGLU_ALPHA = 1.702
LANES = 128
ROW_TILE = 512
QKV_TILE = 512
ATTN_TQ = 128
ATTN_TK = 128
ATTN_EXIT = 88.0
VMEM_LIMIT = 56 * 1024 * 1024
SC_CORES = 2
SC_SUBCORES = 16
SC_CHUNK = 96
SC_GATHER_CHUNK = 96
COMBINE_TILE = 512
INPUT_BUFFERS = 3

F32 = jnp.float32
BF16 = jnp.bfloat16


def _rms(x, g):
    ms = jnp.mean(x * x, axis=-1, keepdims=True)
    return x * lax.rsqrt(ms + EPS) * g


def _dot(a, b):
    return jnp.dot(a, b, preferred_element_type=F32)


def _dot_nt(a, b, precision=None):
    return lax.dot_general(a, b, (((1,), (1,)), ((), ())), preferred_element_type=F32, precision=precision)


def _pack_halves(x):
    c = x.shape[1] // 2
    bits = lax.bitcast_convert_type(x.astype(BF16).astype(F32), jnp.uint32)
    return bits[:, :c] | (bits[:, c:] >> 16)


def _unpack_halves(w):
    return (lax.bitcast_convert_type(w & jnp.uint32(0xFFFF0000), F32), lax.bitcast_convert_type(w << 16, F32))


def _params(semantics):
    return pltpu.CompilerParams(dimension_semantics=semantics, vmem_limit_bytes=VMEM_LIMIT)


def _full(shape):
    return pl.BlockSpec(shape, lambda i, *_: (0,) * len(shape))


def _route_init(cnt_ref, tri_ref):
    r = tri_ref.shape[0]
    cnt_ref[...] = jnp.zeros(cnt_ref.shape, F32)
    tri_ref[...] = jnp.where(lax.broadcasted_iota(jnp.int32, (r, r), 0) < lax.broadcasted_iota(jnp.int32, (r, r), 1),
                             1.0, 0.0).astype(BF16)


def _route_tail(xn, gffn_ref, wr_ref, br_ref, cnt_ref, tri_ref, tok_ref, idx_ref, gate_ref, rank_ref):
    n_exp = br_ref.shape[0]
    tok = _rms(xn, gffn_ref[...])
    tok_ref[...] = _pack_halves(tok)
    tok_hi = tok.astype(BF16)
    tok_lo = (tok - tok_hi.astype(F32)).astype(BF16)
    by_hi = _dot_nt(wr_ref[...], tok_hi)
    logits = by_hi[:n_exp] + by_hi[n_exp:] + _dot_nt(wr_ref[:n_exp, :], tok_lo) + br_ref[...]
    eidx = lax.broadcasted_iota(jnp.int32, logits.shape, 0).astype(F32)
    vals, idxs = [], []
    l = logits
    for _ in range(TOP_K):
        m = jnp.max(l, axis=0, keepdims=True)
        i = jnp.min(jnp.where(l == m, eidx, float(n_exp)), axis=0, keepdims=True)
        vals.append(m)
        idxs.append(i)
        l = jnp.where(eidx == i, -jnp.inf, l)
    es = [jnp.exp(v - vals[0]) for v in vals]
    den = es[0]
    for e in es[1:]:
        den = den + e
    gate_ref[...] = jnp.concatenate([e / den for e in es], axis=0)
    idx_ref[...] = jnp.concatenate(idxs, axis=0).astype(jnp.int32)
    member = jnp.zeros(logits.shape, F32)
    for i in idxs:
        member = member + jnp.where(eidx == i, 1.0, 0.0)
    before = _dot(member.astype(BF16), tri_ref[...]) + cnt_ref[:, :1]
    ranks = [jnp.sum(jnp.where(eidx == i, before, 0.0), axis=0, keepdims=True) for i in idxs]
    rank_ref[...] = jnp.concatenate(ranks, axis=0).astype(jnp.int32)
    cnt_ref[...] = cnt_ref[...] + jnp.sum(member, axis=1, keepdims=True)


def _pool_mix(h, ext_ref, pos, pw_ref, ps_ref):
    ts, d = h.shape[-2:]
    hist = POOL_HIST_ROWS
    group = d // len(POOL_WINDOWS)
    pre = (slice(None),) * (h.ndim - 2)
    ys = []
    for g, win in enumerate(POOL_WINDOWS):
        cols = slice(g * group, (g + 1) * group)
        hg = h[pre + (slice(None), cols)]
        acc = hg
        for j in range(1, win):
            acc = acc + ext_ref[pre + (slice(hist - j, hist - j + ts), cols)]
        cnt = jnp.minimum(pos + 1, win).astype(F32)
        dg = acc / cnt - hg
        ys.append(_dot(dg.reshape(-1, group).astype(BF16), pw_ref[g]))
    return jnp.concatenate(ys, axis=-1) * ps_ref[...]


def _pool_mix_tiled(h, ext_ref, lvl_ref, pos, pw_ref, ps_ref):
    ts, d = h.shape
    hist = POOL_HIST_ROWS
    group = d // len(POOL_WINDOWS)
    prev_ref, prev_col0 = ext_ref, 0
    ys = []
    for g, win in enumerate(POOL_WINDOWS):
        assert win == 2 * (POOL_WINDOWS[g - 1] if g else 1)
        col0 = g * group
        cols = slice(col0 - prev_col0, d - prev_col0)
        level = prev_ref[hist:hist + ts, cols] + prev_ref[hist - win // 2:hist - win // 2 + ts, cols]
        if g + 1 < len(POOL_WINDOWS):
            lvl_ref[g, hist:hist + ts, 0:d - col0 - group] = level[:, group:]
            prev_ref, prev_col0 = lvl_ref.at[g], col0 + group
        cnt = jnp.minimum(pos + 1, win).astype(F32)
        dg = level[:, :group] / cnt - h[:, col0:col0 + group]
        ys.append(_dot(dg.astype(BF16), pw_ref[g]))
    return jnp.concatenate(ys, axis=-1) * ps_ref[...]


def _pool_kernel(ntp, nt, pos0_s, xp_ref, xs_ref, hist_ref, gmix_ref, pw_ref, ps_ref, gffn_ref, wr_ref, br_ref,
                 xn_ref, tok_ref, idx_ref, gate_ref, rank_ref, hlast_p_ref, hlast_s_ref, cnt_out_ref,
                 ext_p, lvl_p, ext_s, cnt_ref, tri_ref):
    i = pl.program_id(0)
    hist = POOL_HIST_ROWS

    @pl.when(i == 0)
    def _():
        _route_init(cnt_ref, tri_ref)

    @pl.when(i < ntp)
    def _():
        ts, d = xp_ref.shape
        t = lax.rem(i, nt)

        @pl.when(t == 0)
        def _():
            ext_p[0:hist, :] = jnp.zeros((hist, d), F32)
            lvl_p[:, 0:hist, :] = jnp.zeros((lvl_p.shape[0], hist, lvl_p.shape[2]), F32)

        x = xp_ref[...]
        h = _rms(x, gmix_ref[...])
        ext_p[hist:hist + ts, :] = h

        @pl.when(t == nt - 1)
        def _():
            hlast_p_ref[0] = h[ts - hist:, :]

        pos = t * ts + lax.broadcasted_iota(jnp.int32, (ts, 1), 0)
        xn_ref[...] = x + _pool_mix_tiled(h, ext_p, lvl_p, pos, pw_ref, ps_ref)
        ext_p[0:hist, :] = ext_p[ts:ts + hist, :]
        lvl_p[:, 0:hist, :] = lvl_p[:, ts:ts + hist, :]

    @pl.when(i >= ntp)
    def _():
        bb, ts, d = xs_ref.shape
        ext_s[:, 0:hist, :] = hist_ref[...]
        x = xs_ref[...]
        h = _rms(x, gmix_ref[...])
        ext_s[:, hist:hist + ts, :] = h
        hlast_s_ref[...] = h[:, ts - hist:, :]
        pos = pos0_s + lax.broadcasted_iota(jnp.int32, (1, ts, 1), 1)
        xn_ref[...] = x.reshape(bb * ts, d) + _pool_mix(h, ext_s, pos, pw_ref, ps_ref)

    _route_tail(xn_ref[...], gffn_ref, wr_ref, br_ref, cnt_ref, tri_ref, tok_ref, idx_ref, gate_ref, rank_ref)
    cnt_out_ref[...] = cnt_ref[...]


def _pool_layer(x_prompt, x_sample, hist_s, pos0_s, gmix, pw, ps, gffn, wr, br):
    b, s, d = x_prompt.shape
    db, t, _ = x_sample.shape
    n_exp = br.shape[0]
    r = ROW_TILE
    nt = s // r
    ntp = b * nt
    bb = r // t
    nts = db // bb
    n = b * s + db * t
    hist = POOL_HIST_ROWS
    rows = pl.BlockSpec((r, d), lambda i: (i, 0))
    lanes = pl.BlockSpec((TOP_K, r), lambda i: (0, i))
    sample_blk = lambda i: (jnp.maximum(i - ntp, 0), 0, 0)
    return pl.pallas_call(
        functools.partial(_pool_kernel, ntp, nt, pos0_s),
        grid=(ntp + nts,),
        in_specs=[pl.BlockSpec((r, d), lambda i: (jnp.minimum(i, ntp - 1), 0)),
                  pl.BlockSpec((bb, t, d), sample_blk), pl.BlockSpec((bb, hist, d), sample_blk),
                  _full((1, d)), _full(pw.shape), _full((1, d)), _full((1, d)), _full((2 * n_exp, d)), _full((n_exp, 1))],
        out_specs=[rows, pl.BlockSpec((r, d // 2), lambda i: (i, 0)), lanes, lanes, lanes,
                   pl.BlockSpec((1, hist, d), lambda i: (jnp.minimum(i // nt, b - 1), 0, 0)),
                   pl.BlockSpec((bb, hist, d), sample_blk), _full((n_exp, LANES))],
        out_shape=[jax.ShapeDtypeStruct((n, d), F32), jax.ShapeDtypeStruct((n, d // 2), jnp.uint32),
                   jax.ShapeDtypeStruct((TOP_K, n), jnp.int32), jax.ShapeDtypeStruct((TOP_K, n), F32),
                   jax.ShapeDtypeStruct((TOP_K, n), jnp.int32),
                   jax.ShapeDtypeStruct((b, hist, d), F32), jax.ShapeDtypeStruct((db, hist, d), F32),
                   jax.ShapeDtypeStruct((n_exp, LANES), F32)],
        scratch_shapes=[pltpu.VMEM((hist + r, d), F32),
                        pltpu.VMEM((len(POOL_WINDOWS) - 1, hist + r, d - d // len(POOL_WINDOWS)), F32),
                        pltpu.VMEM((bb, hist + t, d), F32), pltpu.VMEM((n_exp, LANES), F32), pltpu.VMEM((r, r), BF16)],
        compiler_params=_params(("arbitrary",)),
        name="pool_route",
    )(x_prompt.reshape(b * s, d), x_sample, hist_s, gmix, pw, ps, gffn, wr, br)


def _proj_kernel(ntp, op_ref, os_ref, x_ref, wo_ref, gffn_ref, wr_ref, br_ref,
                 xn_ref, tok_ref, idx_ref, gate_ref, rank_ref, cnt_out_ref, cnt_ref, tri_ref):
    i = pl.program_id(0)

    @pl.when(i == 0)
    def _():
        _route_init(cnt_ref, tri_ref)

    o = jnp.where(i < ntp, op_ref[...], os_ref[...])
    xn = x_ref[...] + _dot(o, wo_ref[...])
    xn_ref[...] = xn
    _route_tail(xn, gffn_ref, wr_ref, br_ref, cnt_ref, tri_ref, tok_ref, idx_ref, gate_ref, rank_ref)
    cnt_out_ref[...] = cnt_ref[...]


def _proj_layer(o_p, o_s, x, wo, gffn, wr, br):
    n, d = x.shape
    n_exp = br.shape[0]
    r = ROW_TILE
    ntp = o_p.shape[0] // r
    rows = pl.BlockSpec((r, d), lambda i: (i, 0))
    lanes = pl.BlockSpec((TOP_K, r), lambda i: (0, i))
    return pl.pallas_call(
        functools.partial(_proj_kernel, ntp),
        grid=(n // r,),
        in_specs=[pl.BlockSpec((r, d), lambda i: (jnp.minimum(i, ntp - 1), 0)),
                  pl.BlockSpec((r, d), lambda i: (jnp.maximum(i - ntp, 0), 0)),
                  rows, _full((d, d)), _full((1, d)), _full((2 * n_exp, d)), _full((n_exp, 1))],
        out_specs=[rows, pl.BlockSpec((r, d // 2), lambda i: (i, 0)), lanes, lanes, lanes, _full((n_exp, LANES))],
        out_shape=[jax.ShapeDtypeStruct((n, d), F32), jax.ShapeDtypeStruct((n, d // 2), jnp.uint32),
                   jax.ShapeDtypeStruct((TOP_K, n), jnp.int32), jax.ShapeDtypeStruct((TOP_K, n), F32),
                   jax.ShapeDtypeStruct((TOP_K, n), jnp.int32), jax.ShapeDtypeStruct((n_exp, LANES), F32)],
        scratch_shapes=[pltpu.VMEM((n_exp, LANES), F32), pltpu.VMEM((r, r), BF16)],
        compiler_params=_params(("arbitrary",)),
        name="proj_route",
    )(o_p, o_s, x, wo, gffn, wr, br)


def _sc_chunks(dest, w):
    n = dest.shape[1]
    return dest.reshape(TOP_K, n // w, w).transpose(1, 0, 2)


def _dispatch(tok, dest, n_buf_rows):
    n, d = tok.shape
    w = SC_CHUNK
    n_workers = SC_CORES * SC_SUBCORES
    per_worker = n // (w * n_workers)
    assert per_worker * w * n_workers == n
    mesh = plsc.VectorSubcoreMesh(core_axis_name="core", subcore_axis_name="subcore",
                                  num_cores=SC_CORES, num_subcores=SC_SUBCORES)

    @functools.partial(
        pl.kernel, mesh=mesh, out_type=jax.ShapeDtypeStruct((n_buf_rows, d), tok.dtype),
        scratch_types=[pltpu.VMEM((TOP_K, w), jnp.int32), pltpu.VMEM((w, d), tok.dtype), pltpu.SemaphoreType.DMA],
        name="moe_dispatch_sc")
    def scatter_rows(tok_hbm, dest_hbm, buf_hbm, idx_v, rows_v, sem):
        worker = lax.axis_index("subcore") * SC_CORES + lax.axis_index("core")

        @pl.loop(0, per_worker)
        def _(c):
            chunk = worker * per_worker + c
            pltpu.sync_copy(dest_hbm.at[chunk], idx_v)
            pltpu.sync_copy(tok_hbm.at[pl.ds(pl.multiple_of(chunk * w, w), w)], rows_v)
            copies = [pltpu.async_copy(rows_v, buf_hbm.at[idx_v.at[k]], sem) for k in range(TOP_K)]
            for cp in copies:
                cp.wait()

    return scatter_rows(tok, _sc_chunks(dest, w))


def _expert_kernel(layer, be_ref, valid_ref, nused_ref, next_ref, slot_ref, x_ref, w1_hbm, b1_ref, w2_hbm, b2_ref,
                   o_ref, w1f_ref, w2f_ref, w1b_ref, w2b_ref, sem):
    i = pl.program_id(0)
    used = valid_ref[i] > 0

    def fetch(expert, slot):
        return (pltpu.make_async_copy(w1_hbm.at[layer, expert], w1f_ref.at[slot], sem.at[0, slot]),
                pltpu.make_async_copy(w2_hbm.at[layer, expert], w2f_ref.at[slot], sem.at[1, slot]))

    @pl.when(used & ((i == 0) | (be_ref[i] != be_ref[jnp.maximum(i - 1, 0)])))
    def _():
        slot = slot_ref[i]

        @pl.when(i == 0)
        def _():
            for cp in fetch(be_ref[i], slot):
                cp.start()

        for cp in fetch(be_ref[i], slot):
            cp.wait()

        @pl.when(next_ref[i] >= 0)
        def _():
            for cp in fetch(next_ref[i], 1 - slot):
                cp.start()

        w1b_ref[...] = w1f_ref[slot].astype(BF16)
        w2b_ref[...] = w2f_ref[slot].astype(BF16)

    @pl.when(used)
    def _():
        f = w2b_ref.shape[0]
        row = lax.broadcasted_iota(jnp.int32, (x_ref.shape[0], 1), 0)
        x = jnp.where(row < valid_ref[i], x_ref[...], jnp.uint32(0))
        xa, xb = _unpack_halves(x)
        gu = _dot(jnp.concatenate([xa.astype(BF16), xb.astype(BF16)], axis=1), w1b_ref[...]) + b1_ref[0]
        gate = jnp.minimum(gu[:, :f], SWIGLU_LIMIT)
        up = jnp.clip(gu[:, f:], -SWIGLU_LIMIT, SWIGLU_LIMIT)
        hid = (up + 1.0) * (gate * jax.nn.sigmoid(gate * SWIGLU_ALPHA))
        o_ref[...] = _pack_halves(_dot(hid.astype(BF16), w2b_ref[...]) + b2_ref[0])

    @pl.when(jnp.logical_not(used))
    def _():
        o_ref[...] = jnp.zeros(o_ref.shape, o_ref.dtype)


def _experts(buf, block_expert, valid, nused, next_expert, slot, layer, w1, b1, w2, b2, tm):
    rows, dw = buf.shape
    _, n_exp, d, f2 = w1.shape
    f = w2.shape[2]
    in_rows = lambda i, be, va, nu, nx, sl: (jnp.minimum(i, nu[0] - 1), 0)
    by_expert = lambda i, be, va, nu, nx, sl: (be[i], 0, 0)
    any_spec = pl.BlockSpec(memory_space=pl.ANY)
    return pl.pallas_call(
        functools.partial(_expert_kernel, layer),
        grid_spec=pltpu.PrefetchScalarGridSpec(
            num_scalar_prefetch=5,
            grid=(rows // tm,),
            in_specs=[pl.BlockSpec((tm, dw), in_rows), any_spec, pl.BlockSpec((1, 1, f2), by_expert),
                      any_spec, pl.BlockSpec((1, 1, d), by_expert)],
            out_specs=pl.BlockSpec((tm, dw), lambda i, be, va, nu, nx, sl: (i, 0)),
            scratch_shapes=[pltpu.VMEM((2, d, f2), F32), pltpu.VMEM((2, f, d), F32),
                            pltpu.VMEM((d, f2), BF16), pltpu.VMEM((f, d), BF16), pltpu.SemaphoreType.DMA((2, 2))]),
        out_shape=jax.ShapeDtypeStruct((rows, dw), jnp.uint32),
        compiler_params=_params(("arbitrary",)),
        name="moe_experts",
    )(block_expert, valid, nused, next_expert, slot, buf, w1, b1.reshape(n_exp, 1, f2), w2, b2.reshape(n_exp, 1, d))


def _gather_rows(ybuf, dest):
    _, d = ybuf.shape
    n = dest.shape[1]
    w = SC_GATHER_CHUNK
    n_workers = SC_CORES * SC_SUBCORES
    per_worker = n // (w * n_workers)
    assert per_worker * w * n_workers == n and TOP_K % 2 == 0
    mesh = plsc.VectorSubcoreMesh(core_axis_name="core", subcore_axis_name="subcore",
                                  num_cores=SC_CORES, num_subcores=SC_SUBCORES)

    @functools.partial(
        pl.kernel, mesh=mesh, out_type=jax.ShapeDtypeStruct((TOP_K, n, d), ybuf.dtype),
        scratch_types=[pltpu.VMEM((TOP_K, w), jnp.int32), pltpu.VMEM((2, w, d), ybuf.dtype),
                       pltpu.SemaphoreType.DMA((2,)), pltpu.SemaphoreType.DMA((2,))],
        name="moe_gather_sc")
    def gather_rows(y_hbm, dest_hbm, out_hbm, idx_v, rows_v, sem_in, sem_out):
        worker = lax.axis_index("subcore") * SC_CORES + lax.axis_index("core")

        @pl.loop(0, per_worker)
        def _(c):
            chunk = worker * per_worker + c
            tokens = pl.ds(pl.multiple_of(chunk * w, w), w)
            pltpu.sync_copy(dest_hbm.at[chunk], idx_v)
            fetch = lambda k: pltpu.async_copy(y_hbm.at[idx_v.at[k]], rows_v.at[k % 2], sem_in.at[k % 2])
            store = lambda k: pltpu.async_copy(rows_v.at[k % 2], out_hbm.at[k, tokens], sem_out.at[k % 2])
            fetches = [fetch(0), fetch(1)]
            stores = []
            for k in range(TOP_K):
                fetches[k].wait()
                stores.append(store(k))
                if k + 2 < TOP_K:
                    stores[k].wait()
                    fetches.append(fetch(k + 2))
            for k in range(TOP_K - 2, TOP_K):
                stores[k].wait()

    return gather_rows(ybuf, _sc_chunks(dest, w))


def _combine_kernel(ntp, rows_hbm, gate_ref, x_ref, g_ref, *refs):
    outs, (ring_ref, sem) = refs[:-2], refs[-2:]
    i = pl.program_id(0)
    n_tiles = pl.num_programs(0)
    r = x_ref.shape[0]
    depth = ring_ref.shape[0]

    def tile_copy(tile):
        slot = lax.rem(tile, depth)
        rows = pl.ds(pl.multiple_of(tile * r, r), r)
        return pltpu.make_async_copy(rows_hbm.at[:, rows, :], ring_ref.at[slot], sem.at[slot])

    @pl.when(i == 0)
    def _():
        for ahead in range(depth - 1):
            @pl.when(ahead < n_tiles)
            def _():
                tile_copy(ahead).start()

    @pl.when(i + depth - 1 < n_tiles)
    def _():
        tile_copy(i + depth - 1).start()

    tile_copy(i).wait()
    rows_ref = ring_ref.at[lax.rem(i, depth)]
    gates = jnp.concatenate([gate_ref[...], jnp.zeros((LANES - TOP_K, r), F32)], axis=0)
    gates_t = jnp.transpose(gates)
    halves = [_unpack_halves(rows_ref[k]) for k in range(TOP_K)]
    f = []
    for side in range(2):
        acc = halves[0][side] * gates_t[:, 0:1]
        for k in range(1, TOP_K):
            acc = acc + halves[k][side] * gates_t[:, k:k + 1]
        f.append(acc)
    xo = x_ref[...] + jnp.concatenate(f, axis=1)
    hn = _rms(xo, g_ref[...])
    if ntp is None:
        xo_ref, hn_ref = outs
        xo_ref[...] = xo
        hn_ref[...] = hn.astype(hn_ref.dtype)
    else:
        hp_ref, hs_ref = outs

        @pl.when(i < ntp)
        def _():
            hp_ref[...] = hn

        @pl.when(i >= ntp)
        def _():
            hs_ref[...] = hn


def _combine(rows4, gates, x, g, n_p):
    n, d = x.shape
    r = COMBINE_TILE
    rows = pl.BlockSpec((r, d), lambda i: (i, 0))
    if n_p is None:
        ntp = None
        out_specs = [rows, rows]
        out_shape = [jax.ShapeDtypeStruct((n, d), F32), jax.ShapeDtypeStruct((n, d), BF16)]
    else:
        ntp = n_p // r
        out_specs = [pl.BlockSpec((r, d), lambda i: (jnp.minimum(i, ntp - 1), 0)),
                     pl.BlockSpec((r, d), lambda i: (jnp.maximum(i - ntp, 0), 0))]
        out_shape = [jax.ShapeDtypeStruct((n_p, d), F32), jax.ShapeDtypeStruct((n - n_p, d), F32)]
    return pl.pallas_call(
        functools.partial(_combine_kernel, ntp),
        grid=(n // r,),
        in_specs=[pl.BlockSpec(memory_space=pl.ANY), pl.BlockSpec((TOP_K, r), lambda i: (0, i)), rows, _full((1, d))],
        out_specs=out_specs,
        out_shape=out_shape,
        scratch_shapes=[pltpu.VMEM((INPUT_BUFFERS, TOP_K, r, d // 2), rows4.dtype),
                        pltpu.SemaphoreType.DMA((INPUT_BUFFERS,))],
        compiler_params=_params(("arbitrary",)),
        name="moe_combine",
    )(rows4, gates, x, g)


def _moe(tok, idx, gates, rank, cnt, x, g_next, n_p, layer, w1, b1, w2, b2):
    n = tok.shape[0]
    n_exp = w1.shape[1]
    pairs = n * TOP_K
    tm = 128
    for cand in (512, 256):
        if pairs >= 4 * cand * n_exp and pairs % cand == 0:
            tm = cand
            break
    n_blocks = -(-pairs // tm) + n_exp
    counts = cnt[:, 0].astype(jnp.int32)
    padded = (counts + tm - 1) // tm * tm
    pad_end = jnp.cumsum(padded)
    pad_start = pad_end - padded
    experts = jnp.arange(n_exp, dtype=jnp.int32)
    dest = rank + jnp.sum(jnp.where(idx[:, :, None] == experts, pad_start, 0), axis=-1)
    starts = jnp.arange(n_blocks, dtype=jnp.int32) * tm
    nused = (pad_end[-1:] // tm).astype(jnp.int32)
    block_expert = jnp.minimum(jnp.sum((pad_end[None, :] <= starts[:, None]).astype(jnp.int32), axis=1), n_exp - 1)
    tokens_end = jnp.sum(jnp.where(block_expert[:, None] == experts, pad_start + counts, 0), axis=-1)
    valid = jnp.clip(tokens_end - starts, 0, tm).astype(jnp.int32)
    has_tokens = counts > 0
    later = has_tokens[None, :] & (experts[None, :] > experts[:, None])
    successor = jnp.min(jnp.where(later, experts[None, :], n_exp), axis=1)
    successor = jnp.where(successor == n_exp, -1, successor)
    parity = (jnp.cumsum(has_tokens.astype(jnp.int32)) - 1) % 2
    of_block = lambda table: jnp.sum(jnp.where(block_expert[:, None] == experts, table, 0), axis=-1).astype(jnp.int32)
    buf = _dispatch(tok, dest, n_blocks * tm)
    ybuf = _experts(buf, block_expert, valid, nused, of_block(successor), of_block(parity), layer, w1, b1, w2, b2, tm)
    return _combine(_gather_rows(ybuf, dest), gates, x, g_next, n_p)


def _qkv_kernel(ntp, h_ref, wq_ref, wkt_ref, wvt_ref, q_ref, ktb_ref, vtb_ref, ktf_ref, vtf_ref,
                ksb_ref, vsb_ref, ksf_ref, vsf_ref):
    i = pl.program_id(0)
    r, d = h_ref.shape
    tk = ktb_ref.shape[-1]
    h = h_ref[...]
    q_ref[...] = (_dot(h, wq_ref[...]) * ((d // N_HEADS) ** -0.5)).astype(BF16)

    @pl.when(i < ntp)
    def _():
        for wt_ref, tb_ref, tf_ref in ((wkt_ref, ktb_ref, ktf_ref), (wvt_ref, vtb_ref, vtf_ref)):
            xt = _dot_nt(wt_ref[...], h)
            tf_ref[...] = xt
            for c in range(r // tk):
                tb_ref[0, c] = xt[:, c * tk:(c + 1) * tk].astype(BF16)

    @pl.when(i >= ntp)
    def _():
        for wt_ref, sb_ref, sf_ref in ((wkt_ref, ksb_ref, ksf_ref), (wvt_ref, vsb_ref, vsf_ref)):
            x = _dot_nt(h, wt_ref[...])
            sf_ref[...] = x
            sb_ref[...] = x.astype(BF16)


def _qkv(h, wq, wkt, wvt, n_streams, length):
    n, d = h.shape
    r = QKV_TILE
    tk = ATTN_TK
    n_p = n_streams * length
    nt = length // r
    ntp = n_p // r
    rows = pl.BlockSpec((r, d), lambda i: (i, 0))
    prompt_blk = lambda i: (jnp.minimum(i, ntp - 1) // nt, lax.rem(jnp.minimum(i, ntp - 1), nt))
    t_blocks = pl.BlockSpec((1, r // tk, d, tk), lambda i: prompt_blk(i) + (0, 0))
    t_full = pl.BlockSpec((d, r), prompt_blk)
    sample = pl.BlockSpec((r, d), lambda i: (jnp.maximum(i - ntp, 0), 0))
    w_spec = _full((d, d))
    return pl.pallas_call(
        functools.partial(_qkv_kernel, ntp),
        grid=(n // r,),
        in_specs=[rows, w_spec, w_spec, w_spec],
        out_specs=[rows, t_blocks, t_blocks, t_full, t_full, sample, sample, sample, sample],
        out_shape=[jax.ShapeDtypeStruct((n, d), BF16)]
        + [jax.ShapeDtypeStruct((n_streams, length // tk, d, tk), BF16)] * 2
        + [jax.ShapeDtypeStruct((n_streams * d, length), F32)] * 2
        + [jax.ShapeDtypeStruct((n - n_p, d), BF16)] * 2 + [jax.ShapeDtypeStruct((n - n_p, d), F32)] * 2,
        compiler_params=_params(("arbitrary",)),
        name="qkv",
    )(h, wq, wkt, wvt)


def _attn_setup(q_ref, qm_ref):
    tq, d = q_ref.shape
    lane = lax.broadcasted_iota(jnp.int32, (tq, LANES), 1)
    for p in range(d // LANES):
        qp = q_ref[:, p * LANES:(p + 1) * LANES]
        qm_ref[p, 0:tq, :] = jnp.where(lane < LANES // 2, qp, jnp.zeros_like(qp))
        qm_ref[p, tq:2 * tq, :] = jnp.where(lane >= LANES // 2, qp, jnp.zeros_like(qp))


def _suffix_sum_matrix(tk):
    src = lax.rem(lax.broadcasted_iota(jnp.int32, (2 * tk, 2 * tk), 0), tk)
    dst = lax.broadcasted_iota(jnp.int32, (2 * tk, 2 * tk), 1)
    return jnp.where((dst >= tk) | (src > dst), 1.0, 0.0).astype(BF16)


def _attn_block(qm_ref, carry_ref, acc_ref, keys, values, transposed, mask, sums):
    n_pairs, tq2, tk = carry_ref.shape
    tq = tq2 // 2
    lane = lax.broadcasted_iota(jnp.int32, (tq, LANES), 1)
    first = mask is not None
    visible = (lambda x: jnp.where(mask, x, 0.0)) if first else (lambda x: x)
    scores = _dot if transposed else _dot_nt
    mix = _dot_nt if transposed else _dot
    zs = [scores(qm_ref[p], keys[p]) for p in range(n_pairs)]
    log_beta, parts = [], []
    for z in zs:
        neg_z = -z
        minus_softplus = jnp.minimum(neg_z, 0.0) - jnp.log(1.0 + jnp.exp(jnp.minimum(z, neg_z)))
        log_keep = visible(minus_softplus)
        hi = log_keep.astype(BF16)
        lo = (log_keep - hi.astype(F32)).astype(BF16)
        parts.append(jnp.concatenate([hi, lo], axis=1))
        log_beta.append(z + minus_softplus)
    sums_out = [_dot(part, sums) for part in parts]
    weights = []
    top = jnp.full((tq2, tk), -jnp.inf, F32)
    for p in range(n_pairs):
        after, carry = sums_out[p][:, :tk], sums_out[p][:, tk:]
        if not first:
            after = after + carry_ref[p]
            carry = carry + carry_ref[p]
        weights.append(visible(jnp.exp(log_beta[p] + after)).astype(BF16))
        carry_ref[p] = carry
        if not first:
            top = jnp.maximum(top, carry)
    for p in range(n_pairs):
        out = mix(weights[p], values[p])
        out = jnp.where(lane < LANES // 2, out[:tq], out[tq:])
        cols = slice(p * LANES, (p + 1) * LANES)
        acc_ref[:, cols] = out if first else acc_ref[:, cols] + out
    return jnp.float32(0.0) if first else jnp.max(top)


def _keep_sweeping(state):
    j, top = state
    return (j >= 0) & (top > -ATTN_EXIT)


def _attn_prompt_kernel(q_ref, kt_ref, vt_ref, o_ref, qm_ref, carry_ref, acc_ref):
    qi = pl.program_id(1)
    tq, d = q_ref.shape
    tk = kt_ref.shape[-1]
    n_pairs = d // LANES
    _attn_setup(q_ref, qm_ref)
    row_pos = qi * tq + lax.rem(lax.broadcasted_iota(jnp.int32, (2 * tq, tk), 0), tq)
    col = lax.broadcasted_iota(jnp.int32, (2 * tq, tk), 1)
    sums = _suffix_sum_matrix(tk)

    def block(j, mask):
        keys = [kt_ref[0, j, p * LANES:(p + 1) * LANES, :] for p in range(n_pairs)]
        values = [vt_ref[0, j, p * LANES:(p + 1) * LANES, :] for p in range(n_pairs)]
        return _attn_block(qm_ref, carry_ref, acc_ref, keys, values, True, mask, sums)

    j0 = ((qi + 1) * tq - 2) // tk
    top = block(j0, (j0 * tk + col) < row_pos)
    lax.while_loop(_keep_sweeping, lambda state: (state[0] - 1, block(state[0], None)), (j0 - 1, top))
    o_ref[...] = acc_ref[...].astype(o_ref.dtype)


def _attention_prompt(q, kt, vt):
    n_streams, n_kblocks, d, tk = kt.shape
    length = n_kblocks * tk
    tq = min(length, ATTN_TQ)
    assert tk % tq == 0
    nq = length // tq
    kv = pl.BlockSpec((1, n_kblocks, d, tk), lambda b, i: (b, 0, 0, 0))
    rows = pl.BlockSpec((tq, d), lambda b, i: (b * nq + i, 0))
    return pl.pallas_call(
        _attn_prompt_kernel,
        grid=(n_streams, nq),
        in_specs=[rows, kv, kv],
        out_specs=rows,
        out_shape=jax.ShapeDtypeStruct((n_streams * length, d), BF16),
        scratch_shapes=[pltpu.VMEM((d // LANES, 2 * tq, LANES), BF16), pltpu.VMEM((d // LANES, 2 * tq, tk), F32),
                        pltpu.VMEM((tq, d), F32)],
        compiler_params=_params(("arbitrary", "arbitrary")),
        name="stick_breaking_prompt",
    )(q, kt, vt)


def _attn_sample_kernel(q_ref, kn_ref, vn_ref, ck_hbm, cv_hbm, o_ref, qm_ref, carry_ref, acc_ref, kbuf, vbuf, sem):
    b = pl.program_id(0)
    t, d = q_ref.shape
    tk = kbuf.shape[-1]
    n_pairs = d // LANES
    n_cache_blocks = ck_hbm.shape[1] // tk

    def fetch(j, slot):
        src = (pl.ds(pl.multiple_of(b * d, d), d), pl.ds(pl.multiple_of(j * tk, tk), tk))
        return (pltpu.make_async_copy(ck_hbm.at[src], kbuf.at[slot], sem.at[0, slot]),
                pltpu.make_async_copy(cv_hbm.at[src], vbuf.at[slot], sem.at[1, slot]))

    for cp in fetch(n_cache_blocks - 1, (n_cache_blocks - 1) % 2):
        cp.start()
    _attn_setup(q_ref, qm_ref)
    row = lax.rem(lax.broadcasted_iota(jnp.int32, (2 * t, tk), 0), t)
    col = lax.broadcasted_iota(jnp.int32, (2 * t, tk), 1)
    sums = _suffix_sum_matrix(tk)
    pad = jnp.zeros((tk - t, LANES), BF16)
    keys = [jnp.concatenate([kn_ref[:, p * LANES:(p + 1) * LANES], pad], axis=0) for p in range(n_pairs)]
    values = [jnp.concatenate([vn_ref[:, p * LANES:(p + 1) * LANES], pad], axis=0) for p in range(n_pairs)]
    top = _attn_block(qm_ref, carry_ref, acc_ref, keys, values, False, col < row, sums)

    def body(state):
        j, _ = state
        slot = lax.rem(j, 2)
        for cp in fetch(j, slot):
            cp.wait()

        @pl.when(j > 0)
        def _():
            for cp in fetch(j - 1, 1 - slot):
                cp.start()

        keys = [kbuf[slot, p * LANES:(p + 1) * LANES, :].astype(BF16) for p in range(n_pairs)]
        values = [vbuf[slot, p * LANES:(p + 1) * LANES, :].astype(BF16) for p in range(n_pairs)]
        return j - 1, _attn_block(qm_ref, carry_ref, acc_ref, keys, values, True, None, sums)

    j_end, _ = lax.while_loop(_keep_sweeping, body, (jnp.int32(n_cache_blocks - 1), top))

    @pl.when(j_end >= 0)
    def _():
        for cp in fetch(j_end, lax.rem(j_end, 2)):
            cp.wait()

    o_ref[...] = acc_ref[...].astype(o_ref.dtype)


def _attention_sample(q, k_new, v_new, cache_kt, cache_vt, t, q_row_off):
    d = q.shape[1]
    n_streams = k_new.shape[0] // t
    tk = ATTN_TK
    assert cache_kt.shape[1] % tk == 0 and t <= tk
    any_spec = pl.BlockSpec(memory_space=pl.ANY)
    new = pl.BlockSpec((t, d), lambda b: (b, 0))
    return pl.pallas_call(
        _attn_sample_kernel,
        grid=(n_streams,),
        in_specs=[pl.BlockSpec((t, d), lambda b: (q_row_off // t + b, 0)), new, new, any_spec, any_spec],
        out_specs=new,
        out_shape=jax.ShapeDtypeStruct((n_streams * t, d), BF16),
        scratch_shapes=[pltpu.VMEM((d // LANES, 2 * t, LANES), BF16), pltpu.VMEM((d // LANES, 2 * t, tk), F32),
                        pltpu.VMEM((t, d), F32), pltpu.VMEM((2, d, tk), F32), pltpu.VMEM((2, d, tk), F32),
                        pltpu.SemaphoreType.DMA((2, 2))],
        compiler_params=_params(("arbitrary",)),
        name="stick_breaking_sample",
    )(q, k_new, v_new, cache_kt, cache_vt)


def kernel(x_prompt, x_sample, state_pool, cache_k, cache_v, norm_mix, norm_ffn, pool_w, pool_scale, w_qkv, w_o, router_w, router_b, moe_w1, moe_b1, moe_w2, moe_b2, final_norm):
    b, s, d = x_prompt.shape
    db, t, _ = x_sample.shape
    past = cache_k.shape[2]
    n_exp = router_w.shape[2]
    hd = d // N_HEADS
    hist = POOL_HIST_ROWS
    n_p, n_s = b * s, db * t
    assert t >= hist and ROW_TILE % t == 0 and s % ROW_TILE == 0 and n_s % ROW_TILE == 0 and ROW_TILE % QKV_TILE == 0
    row = lambda a: a.reshape(1, -1)
    def split_router(w):
        w_t = w.T
        hi = w_t.astype(BF16)
        return jnp.concatenate([hi, (w_t - hi.astype(F32)).astype(BF16)], axis=0)

    wr = [split_router(router_w[i]) for i in range(2)]
    br = [router_b[i].reshape(n_exp, 1) for i in range(2)]

    hist_s = jnp.concatenate([jnp.zeros((db, 1, d), F32), state_pool[0]], axis=1)
    xn, tok, idx, gates, rank, hlast_p, hlast_s, cnt = _pool_layer(
        x_prompt, x_sample, hist_s, past, row(norm_mix[0]), pool_w[0].astype(BF16), row(pool_scale[0]),
        row(norm_ffn[0]), wr[0], br[0])
    x1, h1 = _moe(tok, idx, gates, rank, cnt, xn, row(norm_mix[1]), None, 0, moe_w1, moe_b1[0], moe_w2, moe_b2[0])

    wqkv = w_qkv[0].astype(BF16)
    q, ktb, vtb, ktf, vtf, ksb, vsb, ksf, vsf = _qkv(h1, wqkv[:, :d], wqkv[:, d:2 * d].T, wqkv[:, 2 * d:].T, b, s)
    o_p = _attention_prompt(q, ktb, vtb)
    transposed = lambda cache: cache.transpose(0, 1, 3, 4, 2).reshape(db * d, past)
    o_s = _attention_sample(q, ksb, vsb, transposed(cache_k), transposed(cache_v), t, n_p)
    xn, tok, idx, gates, rank, cnt = _proj_layer(o_p, o_s, x1, w_o[0].astype(BF16), row(norm_ffn[1]), wr[1], br[1])
    y_p, y_s = _moe(tok, idx, gates, rank, cnt, xn, row(final_norm), n_p, 1, moe_w1, moe_b1[1], moe_w2, moe_b2[1])

    frames_major = lambda xt: xt.reshape(1, b, N_HEADS, hd, s).transpose(0, 1, 4, 2, 3)
    heads = lambda a: a.reshape(1, db, t, N_HEADS, hd)
    return (y_p.reshape(b, s, d), y_s.reshape(db, t, d), hlast_p[None, :, 1:, :],
            frames_major(ktf), frames_major(vtf), hlast_s[None, :, 1:, :], heads(ksf), heads(vsf))
```

```python
import functools

import jax
import jax.numpy as jnp
from jax import lax
from jax.experimental import pallas as pl
from jax.experimental.pallas import tpu as pltpu
from jax.experimental.pallas import tpu_sc as plsc

EPS = 1e-5
POOL_WINDOWS = (2, 4, 8, 16)
POOL_HIST_ROWS = 16
N_HEADS = 16
TOP_K = 4
SWIGLU_LIMIT = 7.0
SWIGLU_ALPHA = 1.702
LANES = 128
ROW_TILE = 512
QKV_TILE = 512
ATTN_TQ = 128
ATTN_TK = 128
ATTN_EXIT = 88.0
VMEM_LIMIT = 56 * 1024 * 1024
SC_CORES = 2
SC_SUBCORES = 16
SC_CHUNK = 96
SC_GATHER_CHUNK = 96
COMBINE_TILE = 512
INPUT_BUFFERS = 3

F32 = jnp.float32
BF16 = jnp.bfloat16


def _rms(x, g):
    ms = jnp.mean(x * x, axis=-1, keepdims=True)
    return x * lax.rsqrt(ms + EPS) * g


def _dot(a, b):
    return jnp.dot(a, b, preferred_element_type=F32)


def _dot_nt(a, b, precision=None):
    return lax.dot_general(a, b, (((1,), (1,)), ((), ())), preferred_element_type=F32, precision=precision)


def _pack_halves(x):
    c = x.shape[1] // 2
    bits = lax.bitcast_convert_type(x.astype(BF16).astype(F32), jnp.uint32)
    return bits[:, :c] | (bits[:, c:] >> 16)


def _unpack_halves(w):
    return (lax.bitcast_convert_type(w & jnp.uint32(0xFFFF0000), F32), lax.bitcast_convert_type(w << 16, F32))


def _params(semantics):
    return pltpu.CompilerParams(dimension_semantics=semantics, vmem_limit_bytes=VMEM_LIMIT)


def _full(shape):
    return pl.BlockSpec(shape, lambda i, *_: (0,) * len(shape))


def _route_init(cnt_ref, tri_ref):
    r = tri_ref.shape[0]
    cnt_ref[...] = jnp.zeros(cnt_ref.shape, F32)
    tri_ref[...] = jnp.where(lax.broadcasted_iota(jnp.int32, (r, r), 0) < lax.broadcasted_iota(jnp.int32, (r, r), 1),
                             1.0, 0.0).astype(BF16)


def _route_tail(xn, gffn_ref, wr_ref, br_ref, cnt_ref, tri_ref, tok_ref, idx_ref, gate_ref, rank_ref):
    n_exp = br_ref.shape[0]
    tok = _rms(xn, gffn_ref[...])
    tok_ref[...] = _pack_halves(tok)
    tok_hi = tok.astype(BF16)
    tok_lo = (tok - tok_hi.astype(F32)).astype(BF16)
    by_hi = _dot_nt(wr_ref[...], tok_hi)
    logits = by_hi[:n_exp] + by_hi[n_exp:] + _dot_nt(wr_ref[:n_exp, :], tok_lo) + br_ref[...]
    eidx = lax.broadcasted_iota(jnp.int32, logits.shape, 0).astype(F32)
    vals, idxs = [], []
    l = logits
    for _ in range(TOP_K):
        m = jnp.max(l, axis=0, keepdims=True)
        i = jnp.min(jnp.where(l == m, eidx, float(n_exp)), axis=0, keepdims=True)
        vals.append(m)
        idxs.append(i)
        l = jnp.where(eidx == i, -jnp.inf, l)
    es = [jnp.exp(v - vals[0]) for v in vals]
    den = es[0]
    for e in es[1:]:
        den = den + e
    gate_ref[...] = jnp.concatenate([e / den for e in es], axis=0)
    idx_ref[...] = jnp.concatenate(idxs, axis=0).astype(jnp.int32)
    member = jnp.zeros(logits.shape, F32)
    for i in idxs:
        member = member + jnp.where(eidx == i, 1.0, 0.0)
    before = _dot(member.astype(BF16), tri_ref[...]) + cnt_ref[:, :1]
    ranks = [jnp.sum(jnp.where(eidx == i, before, 0.0), axis=0, keepdims=True) for i in idxs]
    rank_ref[...] = jnp.concatenate(ranks, axis=0).astype(jnp.int32)
    cnt_ref[...] = cnt_ref[...] + jnp.sum(member, axis=1, keepdims=True)


def _pool_mix(h, ext_ref, pos, pw_ref, ps_ref):
    ts, d = h.shape[-2:]
    hist = POOL_HIST_ROWS
    group = d // len(POOL_WINDOWS)
    pre = (slice(None),) * (h.ndim - 2)
    ys = []
    for g, win in enumerate(POOL_WINDOWS):
        cols = slice(g * group, (g + 1) * group)
        hg = h[pre + (slice(None), cols)]
        acc = hg
        for j in range(1, win):
            acc = acc + ext_ref[pre + (slice(hist - j, hist - j + ts), cols)]
        cnt = jnp.minimum(pos + 1, win).astype(F32)
        dg = acc / cnt - hg
        ys.append(_dot(dg.reshape(-1, group).astype(BF16), pw_ref[g]))
    return jnp.concatenate(ys, axis=-1) * ps_ref[...]


def _pool_mix_tiled(h, ext_ref, lvl_ref, pos, pw_ref, ps_ref):
    ts, d = h.shape
    hist = POOL_HIST_ROWS
    group = d // len(POOL_WINDOWS)
    prev_ref, prev_col0 = ext_ref, 0
    ys = []
    for g, win in enumerate(POOL_WINDOWS):
        assert win == 2 * (POOL_WINDOWS[g - 1] if g else 1)
        col0 = g * group
        cols = slice(col0 - prev_col0, d - prev_col0)
        level = prev_ref[hist:hist + ts, cols] + prev_ref[hist - win // 2:hist - win // 2 + ts, cols]
        if g + 1 < len(POOL_WINDOWS):
            lvl_ref[g, hist:hist + ts, 0:d - col0 - group] = level[:, group:]
            prev_ref, prev_col0 = lvl_ref.at[g], col0 + group
        cnt = jnp.minimum(pos + 1, win).astype(F32)
        dg = level[:, :group] / cnt - h[:, col0:col0 + group]
        ys.append(_dot(dg.astype(BF16), pw_ref[g]))
    return jnp.concatenate(ys, axis=-1) * ps_ref[...]


def _pool_kernel(ntp, nt, pos0_s, xp_ref, xs_ref, hist_ref, gmix_ref, pw_ref, ps_ref, gffn_ref, wr_ref, br_ref,
                 xn_ref, tok_ref, idx_ref, gate_ref, rank_ref, hlast_p_ref, hlast_s_ref, cnt_out_ref,
                 ext_p, lvl_p, ext_s, cnt_ref, tri_ref):
    i = pl.program_id(0)
    hist = POOL_HIST_ROWS

    @pl.when(i == 0)
    def _():
        _route_init(cnt_ref, tri_ref)

    @pl.when(i < ntp)
    def _():
        ts, d = xp_ref.shape
        t = lax.rem(i, nt)

        @pl.when(t == 0)
        def _():
            ext_p[0:hist, :] = jnp.zeros((hist, d), F32)
            lvl_p[:, 0:hist, :] = jnp.zeros((lvl_p.shape[0], hist, lvl_p.shape[2]), F32)

        x = xp_ref[...]
        h = _rms(x, gmix_ref[...])
        ext_p[hist:hist + ts, :] = h

        @pl.when(t == nt - 1)
        def _():
            hlast_p_ref[0] = h[ts - hist:, :]

        pos = t * ts + lax.broadcasted_iota(jnp.int32, (ts, 1), 0)
        xn_ref[...] = x + _pool_mix_tiled(h, ext_p, lvl_p, pos, pw_ref, ps_ref)
        ext_p[0:hist, :] = ext_p[ts:ts + hist, :]
        lvl_p[:, 0:hist, :] = lvl_p[:, ts:ts + hist, :]

    @pl.when(i >= ntp)
    def _():
        bb, ts, d = xs_ref.shape
        ext_s[:, 0:hist, :] = hist_ref[...]
        x = xs_ref[...]
        h = _rms(x, gmix_ref[...])
        ext_s[:, hist:hist + ts, :] = h
        hlast_s_ref[...] = h[:, ts - hist:, :]
        pos = pos0_s + lax.broadcasted_iota(jnp.int32, (1, ts, 1), 1)
        xn_ref[...] = x.reshape(bb * ts, d) + _pool_mix(h, ext_s, pos, pw_ref, ps_ref)

    _route_tail(xn_ref[...], gffn_ref, wr_ref, br_ref, cnt_ref, tri_ref, tok_ref, idx_ref, gate_ref, rank_ref)
    cnt_out_ref[...] = cnt_ref[...]


def _pool_layer(x_prompt, x_sample, hist_s, pos0_s, gmix, pw, ps, gffn, wr, br):
    b, s, d = x_prompt.shape
    db, t, _ = x_sample.shape
    n_exp = br.shape[0]
    r = ROW_TILE
    nt = s // r
    ntp = b * nt
    bb = r // t
    nts = db // bb
    n = b * s + db * t
    hist = POOL_HIST_ROWS
    rows = pl.BlockSpec((r, d), lambda i: (i, 0))
    lanes = pl.BlockSpec((TOP_K, r), lambda i: (0, i))
    sample_blk = lambda i: (jnp.maximum(i - ntp, 0), 0, 0)
    return pl.pallas_call(
        functools.partial(_pool_kernel, ntp, nt, pos0_s),
        grid=(ntp + nts,),
        in_specs=[pl.BlockSpec((r, d), lambda i: (jnp.minimum(i, ntp - 1), 0)),
                  pl.BlockSpec((bb, t, d), sample_blk), pl.BlockSpec((bb, hist, d), sample_blk),
                  _full((1, d)), _full(pw.shape), _full((1, d)), _full((1, d)), _full((2 * n_exp, d)), _full((n_exp, 1))],
        out_specs=[rows, pl.BlockSpec((r, d // 2), lambda i: (i, 0)), lanes, lanes, lanes,
                   pl.BlockSpec((1, hist, d), lambda i: (jnp.minimum(i // nt, b - 1), 0, 0)),
                   pl.BlockSpec((bb, hist, d), sample_blk), _full((n_exp, LANES))],
        out_shape=[jax.ShapeDtypeStruct((n, d), F32), jax.ShapeDtypeStruct((n, d // 2), jnp.uint32),
                   jax.ShapeDtypeStruct((TOP_K, n), jnp.int32), jax.ShapeDtypeStruct((TOP_K, n), F32),
                   jax.ShapeDtypeStruct((TOP_K, n), jnp.int32),
                   jax.ShapeDtypeStruct((b, hist, d), F32), jax.ShapeDtypeStruct((db, hist, d), F32),
                   jax.ShapeDtypeStruct((n_exp, LANES), F32)],
        scratch_shapes=[pltpu.VMEM((hist + r, d), F32),
                        pltpu.VMEM((len(POOL_WINDOWS) - 1, hist + r, d - d // len(POOL_WINDOWS)), F32),
                        pltpu.VMEM((bb, hist + t, d), F32), pltpu.VMEM((n_exp, LANES), F32), pltpu.VMEM((r, r), BF16)],
        compiler_params=_params(("arbitrary",)),
        name="pool_route",
    )(x_prompt.reshape(b * s, d), x_sample, hist_s, gmix, pw, ps, gffn, wr, br)


def _proj_kernel(ntp, op_ref, os_ref, x_ref, wo_ref, gffn_ref, wr_ref, br_ref,
                 xn_ref, tok_ref, idx_ref, gate_ref, rank_ref, cnt_out_ref, cnt_ref, tri_ref):
    i = pl.program_id(0)

    @pl.when(i == 0)
    def _():
        _route_init(cnt_ref, tri_ref)

    o = jnp.where(i < ntp, op_ref[...], os_ref[...])
    xn = x_ref[...] + _dot(o, wo_ref[...])
    xn_ref[...] = xn
    _route_tail(xn, gffn_ref, wr_ref, br_ref, cnt_ref, tri_ref, tok_ref, idx_ref, gate_ref, rank_ref)
    cnt_out_ref[...] = cnt_ref[...]


def _proj_layer(o_p, o_s, x, wo, gffn, wr, br):
    n, d = x.shape
    n_exp = br.shape[0]
    r = ROW_TILE
    ntp = o_p.shape[0] // r
    rows = pl.BlockSpec((r, d), lambda i: (i, 0))
    lanes = pl.BlockSpec((TOP_K, r), lambda i: (0, i))
    return pl.pallas_call(
        functools.partial(_proj_kernel, ntp),
        grid=(n // r,),
        in_specs=[pl.BlockSpec((r, d), lambda i: (jnp.minimum(i, ntp - 1), 0)),
                  pl.BlockSpec((r, d), lambda i: (jnp.maximum(i - ntp, 0), 0)),
                  rows, _full((d, d)), _full((1, d)), _full((2 * n_exp, d)), _full((n_exp, 1))],
        out_specs=[rows, pl.BlockSpec((r, d // 2), lambda i: (i, 0)), lanes, lanes, lanes, _full((n_exp, LANES))],
        out_shape=[jax.ShapeDtypeStruct((n, d), F32), jax.ShapeDtypeStruct((n, d // 2), jnp.uint32),
                   jax.ShapeDtypeStruct((TOP_K, n), jnp.int32), jax.ShapeDtypeStruct((TOP_K, n), F32),
                   jax.ShapeDtypeStruct((TOP_K, n), jnp.int32), jax.ShapeDtypeStruct((n_exp, LANES), F32)],
        scratch_shapes=[pltpu.VMEM((n_exp, LANES), F32), pltpu.VMEM((r, r), BF16)],
        compiler_params=_params(("arbitrary",)),
        name="proj_route",
    )(o_p, o_s, x, wo, gffn, wr, br)


def _sc_chunks(dest, w):
    n = dest.shape[1]
    return dest.reshape(TOP_K, n // w, w).transpose(1, 0, 2)


def _dispatch(tok, dest, n_buf_rows):
    n, d = tok.shape
    w = SC_CHUNK
    n_workers = SC_CORES * SC_SUBCORES
    per_worker = n // (w * n_workers)
    assert per_worker * w * n_workers == n
    mesh = plsc.VectorSubcoreMesh(core_axis_name="core", subcore_axis_name="subcore",
                                  num_cores=SC_CORES, num_subcores=SC_SUBCORES)

    @functools.partial(
        pl.kernel, mesh=mesh, out_type=jax.ShapeDtypeStruct((n_buf_rows, d), tok.dtype),
        scratch_types=[pltpu.VMEM((TOP_K, w), jnp.int32), pltpu.VMEM((w, d), tok.dtype), pltpu.SemaphoreType.DMA],
        name="moe_dispatch_sc")
    def scatter_rows(tok_hbm, dest_hbm, buf_hbm, idx_v, rows_v, sem):
        worker = lax.axis_index("subcore") * SC_CORES + lax.axis_index("core")

        @pl.loop(0, per_worker)
        def _(c):
            chunk = worker * per_worker + c
            pltpu.sync_copy(dest_hbm.at[chunk], idx_v)
            pltpu.sync_copy(tok_hbm.at[pl.ds(pl.multiple_of(chunk * w, w), w)], rows_v)
            copies = [pltpu.async_copy(rows_v, buf_hbm.at[idx_v.at[k]], sem) for k in range(TOP_K)]
            for cp in copies:
                cp.wait()

    return scatter_rows(tok, _sc_chunks(dest, w))


def _expert_kernel(layer, be_ref, valid_ref, nused_ref, next_ref, slot_ref, x_hbm, w1_hbm, b1_ref, w2_hbm, b2_ref,
                   o_ref, w1f_ref, w2f_ref, w1b_ref, w2b_ref, sem, x_ring, x_sem):
    i = pl.program_id(0)
    used = valid_ref[i] > 0
    tm = o_ref.shape[0]
    depth = x_ring.shape[0]

    def block_copy(blk):
        rows = pl.ds(pl.multiple_of(blk * tm, tm), tm)
        return pltpu.make_async_copy(x_hbm.at[rows, :], x_ring.at[lax.rem(blk, depth)], x_sem.at[lax.rem(blk, depth)])

    @pl.when(i == 0)
    def _():
        for ahead in range(depth - 1):
            @pl.when(ahead < nused_ref[0])
            def _():
                block_copy(ahead).start()

    @pl.when(i + depth - 1 < nused_ref[0])
    def _():
        block_copy(i + depth - 1).start()

    @pl.when(used)
    def _():
        block_copy(i).wait()

    x_ref = x_ring.at[lax.rem(i, depth)]

    def fetch(expert, slot):
        return (pltpu.make_async_copy(w1_hbm.at[layer, expert], w1f_ref.at[slot], sem.at[0, slot]),
                pltpu.make_async_copy(w2_hbm.at[layer, expert], w2f_ref.at[slot], sem.at[1, slot]))

    @pl.when(used & ((i == 0) | (be_ref[i] != be_ref[jnp.maximum(i - 1, 0)])))
    def _():
        slot = slot_ref[i]

        @pl.when(i == 0)
        def _():
            for cp in fetch(be_ref[i], slot):
                cp.start()

        for cp in fetch(be_ref[i], slot):
            cp.wait()

        @pl.when(next_ref[i] >= 0)
        def _():
            for cp in fetch(next_ref[i], 1 - slot):
                cp.start()

        w1b_ref[...] = w1f_ref[slot].astype(BF16)
        w2b_ref[...] = w2f_ref[slot].astype(BF16)

    @pl.when(used)
    def _():
        f = w2b_ref.shape[0]
        row = lax.broadcasted_iota(jnp.int32, (x_ref.shape[0], 1), 0)
        x = jnp.where(row < valid_ref[i], x_ref[...], jnp.uint32(0))
        xa, xb = _unpack_halves(x)
        gu = _dot(jnp.concatenate([xa.astype(BF16), xb.astype(BF16)], axis=1), w1b_ref[...]) + b1_ref[0]
        gate = jnp.minimum(gu[:, :f], SWIGLU_LIMIT)
        up = jnp.clip(gu[:, f:], -SWIGLU_LIMIT, SWIGLU_LIMIT)
        hid = (up + 1.0) * (gate * jax.nn.sigmoid(gate * SWIGLU_ALPHA))
        o_ref[...] = _pack_halves(_dot(hid.astype(BF16), w2b_ref[...]) + b2_ref[0])

    @pl.when(jnp.logical_not(used))
    def _():
        o_ref[...] = jnp.zeros(o_ref.shape, o_ref.dtype)


def _experts(buf, block_expert, valid, nused, next_expert, slot, layer, w1, b1, w2, b2, tm):
    rows, dw = buf.shape
    _, n_exp, d, f2 = w1.shape
    f = w2.shape[2]
    by_expert = lambda i, be, va, nu, nx, sl: (be[i], 0, 0)
    any_spec = pl.BlockSpec(memory_space=pl.ANY)
    return pl.pallas_call(
        functools.partial(_expert_kernel, layer),
        grid_spec=pltpu.PrefetchScalarGridSpec(
            num_scalar_prefetch=5,
            grid=(rows // tm,),
            in_specs=[any_spec, any_spec, pl.BlockSpec((1, 1, f2), by_expert),
                      any_spec, pl.BlockSpec((1, 1, d), by_expert)],
            out_specs=pl.BlockSpec((tm, dw), lambda i, be, va, nu, nx, sl: (i, 0)),
            scratch_shapes=[pltpu.VMEM((2, d, f2), F32), pltpu.VMEM((2, f, d), F32),
                            pltpu.VMEM((d, f2), BF16), pltpu.VMEM((f, d), BF16), pltpu.SemaphoreType.DMA((2, 2)),
                            pltpu.VMEM((INPUT_BUFFERS, tm, dw), buf.dtype), pltpu.SemaphoreType.DMA((INPUT_BUFFERS,))]),
        out_shape=jax.ShapeDtypeStruct((rows, dw), jnp.uint32),
        compiler_params=_params(("arbitrary",)),
        name="moe_experts",
    )(block_expert, valid, nused, next_expert, slot, buf, w1, b1.reshape(n_exp, 1, f2), w2, b2.reshape(n_exp, 1, d))


def _gather_rows(ybuf, dest):
    _, d = ybuf.shape
    n = dest.shape[1]
    w = SC_GATHER_CHUNK
    n_workers = SC_CORES * SC_SUBCORES
    per_worker = n // (w * n_workers)
    assert per_worker * w * n_workers == n and TOP_K % 2 == 0
    mesh = plsc.VectorSubcoreMesh(core_axis_name="core", subcore_axis_name="subcore",
                                  num_cores=SC_CORES, num_subcores=SC_SUBCORES)

    @functools.partial(
        pl.kernel, mesh=mesh, out_type=jax.ShapeDtypeStruct((TOP_K, n, d), ybuf.dtype),
        scratch_types=[pltpu.VMEM((TOP_K, w), jnp.int32), pltpu.VMEM((2, w, d), ybuf.dtype),
                       pltpu.SemaphoreType.DMA((2,)), pltpu.SemaphoreType.DMA((2,))],
        name="moe_gather_sc")
    def gather_rows(y_hbm, dest_hbm, out_hbm, idx_v, rows_v, sem_in, sem_out):
        worker = lax.axis_index("subcore") * SC_CORES + lax.axis_index("core")

        @pl.loop(0, per_worker)
        def _(c):
            chunk = worker * per_worker + c
            tokens = pl.ds(pl.multiple_of(chunk * w, w), w)
            pltpu.sync_copy(dest_hbm.at[chunk], idx_v)
            fetch = lambda k: pltpu.async_copy(y_hbm.at[idx_v.at[k]], rows_v.at[k % 2], sem_in.at[k % 2])
            store = lambda k: pltpu.async_copy(rows_v.at[k % 2], out_hbm.at[k, tokens], sem_out.at[k % 2])
            fetches = [fetch(0), fetch(1)]
            stores = []
            for k in range(TOP_K):
                fetches[k].wait()
                stores.append(store(k))
                if k + 2 < TOP_K:
                    stores[k].wait()
                    fetches.append(fetch(k + 2))
            for k in range(TOP_K - 2, TOP_K):
                stores[k].wait()

    return gather_rows(ybuf, _sc_chunks(dest, w))


def _combine_kernel(ntp, rows_hbm, gate_ref, x_ref, g_ref, *refs):
    outs, (ring_ref, sem) = refs[:-2], refs[-2:]
    i = pl.program_id(0)
    n_tiles = pl.num_programs(0)
    r = x_ref.shape[0]
    depth = ring_ref.shape[0]

    def tile_copy(tile):
        slot = lax.rem(tile, depth)
        rows = pl.ds(pl.multiple_of(tile * r, r), r)
        return pltpu.make_async_copy(rows_hbm.at[:, rows, :], ring_ref.at[slot], sem.at[slot])

    @pl.when(i == 0)
    def _():
        for ahead in range(depth - 1):
            @pl.when(ahead < n_tiles)
            def _():
                tile_copy(ahead).start()

    @pl.when(i + depth - 1 < n_tiles)
    def _():
        tile_copy(i + depth - 1).start()

    tile_copy(i).wait()
    rows_ref = ring_ref.at[lax.rem(i, depth)]
    gates = jnp.concatenate([gate_ref[...], jnp.zeros((LANES - TOP_K, r), F32)], axis=0)
    gates_t = jnp.transpose(gates)
    halves = [_unpack_halves(rows_ref[k]) for k in range(TOP_K)]
    f = []
    for side in range(2):
        acc = halves[0][side] * gates_t[:, 0:1]
        for k in range(1, TOP_K):
            acc = acc + halves[k][side] * gates_t[:, k:k + 1]
        f.append(acc)
    xo = x_ref[...] + jnp.concatenate(f, axis=1)
    hn = _rms(xo, g_ref[...])
    if ntp is None:
        xo_ref, hn_ref = outs
        xo_ref[...] = xo
        hn_ref[...] = hn.astype(hn_ref.dtype)
    else:
        hp_ref, hs_ref = outs

        @pl.when(i < ntp)
        def _():
            hp_ref[...] = hn

        @pl.when(i >= ntp)
        def _():
            hs_ref[...] = hn


def _combine(rows4, gates, x, g, n_p):
    n, d = x.shape
    r = COMBINE_TILE
    rows = pl.BlockSpec((r, d), lambda i: (i, 0))
    if n_p is None:
        ntp = None
        out_specs = [rows, rows]
        out_shape = [jax.ShapeDtypeStruct((n, d), F32), jax.ShapeDtypeStruct((n, d), BF16)]
    else:
        ntp = n_p // r
        out_specs = [pl.BlockSpec((r, d), lambda i: (jnp.minimum(i, ntp - 1), 0)),
                     pl.BlockSpec((r, d), lambda i: (jnp.maximum(i - ntp, 0), 0))]
        out_shape = [jax.ShapeDtypeStruct((n_p, d), F32), jax.ShapeDtypeStruct((n - n_p, d), F32)]
    return pl.pallas_call(
        functools.partial(_combine_kernel, ntp),
        grid=(n // r,),
        in_specs=[pl.BlockSpec(memory_space=pl.ANY), pl.BlockSpec((TOP_K, r), lambda i: (0, i)), rows, _full((1, d))],
        out_specs=out_specs,
        out_shape=out_shape,
        scratch_shapes=[pltpu.VMEM((INPUT_BUFFERS, TOP_K, r, d // 2), rows4.dtype),
                        pltpu.SemaphoreType.DMA((INPUT_BUFFERS,))],
        compiler_params=_params(("arbitrary",)),
        name="moe_combine",
    )(rows4, gates, x, g)


def _moe(tok, idx, gates, rank, cnt, x, g_next, n_p, layer, w1, b1, w2, b2):
    n = tok.shape[0]
    n_exp = w1.shape[1]
    pairs = n * TOP_K
    tm = 128
    for cand in (512, 256):
        if pairs >= 4 * cand * n_exp and pairs % cand == 0:
            tm = cand
            break
    n_blocks = -(-pairs // tm) + n_exp
    counts = cnt[:, 0].astype(jnp.int32)
    padded = (counts + tm - 1) // tm * tm
    pad_end = jnp.cumsum(padded)
    pad_start = pad_end - padded
    experts = jnp.arange(n_exp, dtype=jnp.int32)
    dest = rank + jnp.sum(jnp.where(idx[:, :, None] == experts, pad_start, 0), axis=-1)
    starts = jnp.arange(n_blocks, dtype=jnp.int32) * tm
    nused = (pad_end[-1:] // tm).astype(jnp.int32)
    block_expert = jnp.minimum(jnp.sum((pad_end[None, :] <= starts[:, None]).astype(jnp.int32), axis=1), n_exp - 1)
    tokens_end = jnp.sum(jnp.where(block_expert[:, None] == experts, pad_start + counts, 0), axis=-1)
    valid = jnp.clip(tokens_end - starts, 0, tm).astype(jnp.int32)
    has_tokens = counts > 0
    later = has_tokens[None, :] & (experts[None, :] > experts[:, None])
    successor = jnp.min(jnp.where(later, experts[None, :], n_exp), axis=1)
    successor = jnp.where(successor == n_exp, -1, successor)
    parity = (jnp.cumsum(has_tokens.astype(jnp.int32)) - 1) % 2
    of_block = lambda table: jnp.sum(jnp.where(block_expert[:, None] == experts, table, 0), axis=-1).astype(jnp.int32)
    buf = _dispatch(tok, dest, n_blocks * tm)
    ybuf = _experts(buf, block_expert, valid, nused, of_block(successor), of_block(parity), layer, w1, b1, w2, b2, tm)
    return _combine(_gather_rows(ybuf, dest), gates, x, g_next, n_p)


def _qkv_kernel(ntp, h_ref, wq_ref, wkt_ref, wvt_ref, q_ref, ktb_ref, vtb_ref, ktf_ref, vtf_ref,
                ksb_ref, vsb_ref, ksf_ref, vsf_ref):
    i = pl.program_id(0)
    r, d = h_ref.shape
    tk = ktb_ref.shape[-1]
    h = h_ref[...]
    q_ref[...] = (_dot(h, wq_ref[...]) * ((d // N_HEADS) ** -0.5)).astype(BF16)

    @pl.when(i < ntp)
    def _():
        for wt_ref, tb_ref, tf_ref in ((wkt_ref, ktb_ref, ktf_ref), (wvt_ref, vtb_ref, vtf_ref)):
            xt = _dot_nt(wt_ref[...], h)
            tf_ref[...] = xt
            for c in range(r // tk):
                tb_ref[0, c] = xt[:, c * tk:(c + 1) * tk].astype(BF16)

    @pl.when(i >= ntp)
    def _():
        for wt_ref, sb_ref, sf_ref in ((wkt_ref, ksb_ref, ksf_ref), (wvt_ref, vsb_ref, vsf_ref)):
            x = _dot_nt(h, wt_ref[...])
            sf_ref[...] = x
            sb_ref[...] = x.astype(BF16)


def _qkv(h, wq, wkt, wvt, n_streams, length):
    n, d = h.shape
    r = QKV_TILE
    tk = ATTN_TK
    n_p = n_streams * length
    nt = length // r
    ntp = n_p // r
    rows = pl.BlockSpec((r, d), lambda i: (i, 0))
    prompt_blk = lambda i: (jnp.minimum(i, ntp - 1) // nt, lax.rem(jnp.minimum(i, ntp - 1), nt))
    t_blocks = pl.BlockSpec((1, r // tk, d, tk), lambda i: prompt_blk(i) + (0, 0))
    t_full = pl.BlockSpec((d, r), prompt_blk)
    sample = pl.BlockSpec((r, d), lambda i: (jnp.maximum(i - ntp, 0), 0))
    w_spec = _full((d, d))
    return pl.pallas_call(
        functools.partial(_qkv_kernel, ntp),
        grid=(n // r,),
        in_specs=[rows, w_spec, w_spec, w_spec],
        out_specs=[rows, t_blocks, t_blocks, t_full, t_full, sample, sample, sample, sample],
        out_shape=[jax.ShapeDtypeStruct((n, d), BF16)]
        + [jax.ShapeDtypeStruct((n_streams, length // tk, d, tk), BF16)] * 2
        + [jax.ShapeDtypeStruct((n_streams * d, length), F32)] * 2
        + [jax.ShapeDtypeStruct((n - n_p, d), BF16)] * 2 + [jax.ShapeDtypeStruct((n - n_p, d), F32)] * 2,
        compiler_params=_params(("arbitrary",)),
        name="qkv",
    )(h, wq, wkt, wvt)


def _attn_setup(q_ref, qm_ref):
    tq, d = q_ref.shape
    lane = lax.broadcasted_iota(jnp.int32, (tq, LANES), 1)
    for p in range(d // LANES):
        qp = q_ref[:, p * LANES:(p + 1) * LANES]
        qm_ref[p, 0:tq, :] = jnp.where(lane < LANES // 2, qp, jnp.zeros_like(qp))
        qm_ref[p, tq:2 * tq, :] = jnp.where(lane >= LANES // 2, qp, jnp.zeros_like(qp))


def _suffix_sum_matrix(tk):
    src = lax.rem(lax.broadcasted_iota(jnp.int32, (2 * tk, 2 * tk), 0), tk)
    dst = lax.broadcasted_iota(jnp.int32, (2 * tk, 2 * tk), 1)
    return jnp.where((dst >= tk) | (src > dst), 1.0, 0.0).astype(BF16)


def _attn_block(qm_ref, carry_ref, acc_ref, keys, values, transposed, mask, sums):
    n_pairs, tq2, tk = carry_ref.shape
    tq = tq2 // 2
    lane = lax.broadcasted_iota(jnp.int32, (tq, LANES), 1)
    first = mask is not None
    visible = (lambda x: jnp.where(mask, x, 0.0)) if first else (lambda x: x)
    scores = _dot if transposed else _dot_nt
    mix = _dot_nt if transposed else _dot
    zs = [scores(qm_ref[p], keys[p]) for p in range(n_pairs)]
    log_beta, parts = [], []
    for z in zs:
        neg_z = -z
        minus_softplus = jnp.minimum(neg_z, 0.0) - jnp.log(1.0 + jnp.exp(jnp.minimum(z, neg_z)))
        log_keep = visible(minus_softplus)
        hi = log_keep.astype(BF16)
        lo = (log_keep - hi.astype(F32)).astype(BF16)
        parts.append(jnp.concatenate([hi, lo], axis=1))
        log_beta.append(z + minus_softplus)
    sums_out = [_dot(part, sums) for part in parts]
    weights = []
    top = jnp.full((tq2, tk), -jnp.inf, F32)
    for p in range(n_pairs):
        after, carry = sums_out[p][:, :tk], sums_out[p][:, tk:]
        if not first:
            after = after + carry_ref[p]
            carry = carry + carry_ref[p]
        weights.append(visible(jnp.exp(log_beta[p] + after)).astype(BF16))
        carry_ref[p] = carry
        if not first:
            top = jnp.maximum(top, carry)
    for p in range(n_pairs):
        out = mix(weights[p], values[p])
        out = jnp.where(lane < LANES // 2, out[:tq], out[tq:])
        cols = slice(p * LANES, (p + 1) * LANES)
        acc_ref[:, cols] = out if first else acc_ref[:, cols] + out
    return jnp.float32(0.0) if first else jnp.max(top)


def _keep_sweeping(state):
    j, top = state
    return (j >= 0) & (top > -ATTN_EXIT)


def _attn_prompt_kernel(q_ref, kt_ref, vt_ref, o_ref, qm_ref, carry_ref, acc_ref):
    qi = pl.program_id(1)
    tq, d = q_ref.shape
    tk = kt_ref.shape[-1]
    n_pairs = d // LANES
    _attn_setup(q_ref, qm_ref)
    row_pos = qi * tq + lax.rem(lax.broadcasted_iota(jnp.int32, (2 * tq, tk), 0), tq)
    col = lax.broadcasted_iota(jnp.int32, (2 * tq, tk), 1)
    sums = _suffix_sum_matrix(tk)

    def block(j, mask):
        keys = [kt_ref[0, j, p * LANES:(p + 1) * LANES, :] for p in range(n_pairs)]
        values = [vt_ref[0, j, p * LANES:(p + 1) * LANES, :] for p in range(n_pairs)]
        return _attn_block(qm_ref, carry_ref, acc_ref, keys, values, True, mask, sums)

    j0 = ((qi + 1) * tq - 2) // tk
    top = block(j0, (j0 * tk + col) < row_pos)
    lax.while_loop(_keep_sweeping, lambda state: (state[0] - 1, block(state[0], None)), (j0 - 1, top))
    o_ref[...] = acc_ref[...].astype(o_ref.dtype)


def _attention_prompt(q, kt, vt):
    n_streams, n_kblocks, d, tk = kt.shape
    length = n_kblocks * tk
    tq = min(length, ATTN_TQ)
    assert tk % tq == 0
    nq = length // tq
    kv = pl.BlockSpec((1, n_kblocks, d, tk), lambda b, i: (b, 0, 0, 0))
    rows = pl.BlockSpec((tq, d), lambda b, i: (b * nq + i, 0))
    return pl.pallas_call(
        _attn_prompt_kernel,
        grid=(n_streams, nq),
        in_specs=[rows, kv, kv],
        out_specs=rows,
        out_shape=jax.ShapeDtypeStruct((n_streams * length, d), BF16),
        scratch_shapes=[pltpu.VMEM((d // LANES, 2 * tq, LANES), BF16), pltpu.VMEM((d // LANES, 2 * tq, tk), F32),
                        pltpu.VMEM((tq, d), F32)],
        compiler_params=_params(("arbitrary", "arbitrary")),
        name="stick_breaking_prompt",
    )(q, kt, vt)


def _attn_sample_kernel(q_ref, kn_ref, vn_ref, ck_hbm, cv_hbm, o_ref, qm_ref, carry_ref, acc_ref, kbuf, vbuf, sem):
    b = pl.program_id(0)
    t, d = q_ref.shape
    tk = kbuf.shape[-1]
    n_pairs = d // LANES
    n_cache_blocks = ck_hbm.shape[1] // tk

    def fetch(j, slot):
        src = (pl.ds(pl.multiple_of(b * d, d), d), pl.ds(pl.multiple_of(j * tk, tk), tk))
        return (pltpu.make_async_copy(ck_hbm.at[src], kbuf.at[slot], sem.at[0, slot]),
                pltpu.make_async_copy(cv_hbm.at[src], vbuf.at[slot], sem.at[1, slot]))

    for cp in fetch(n_cache_blocks - 1, (n_cache_blocks - 1) % 2):
        cp.start()
    _attn_setup(q_ref, qm_ref)
    row = lax.rem(lax.broadcasted_iota(jnp.int32, (2 * t, tk), 0), t)
    col = lax.broadcasted_iota(jnp.int32, (2 * t, tk), 1)
    sums = _suffix_sum_matrix(tk)
    pad = jnp.zeros((tk - t, LANES), BF16)
    keys = [jnp.concatenate([kn_ref[:, p * LANES:(p + 1) * LANES], pad], axis=0) for p in range(n_pairs)]
    values = [jnp.concatenate([vn_ref[:, p * LANES:(p + 1) * LANES], pad], axis=0) for p in range(n_pairs)]
    top = _attn_block(qm_ref, carry_ref, acc_ref, keys, values, False, col < row, sums)

    def body(state):
        j, _ = state
        slot = lax.rem(j, 2)
        for cp in fetch(j, slot):
            cp.wait()

        @pl.when(j > 0)
        def _():
            for cp in fetch(j - 1, 1 - slot):
                cp.start()

        keys = [kbuf[slot, p * LANES:(p + 1) * LANES, :].astype(BF16) for p in range(n_pairs)]
        values = [vbuf[slot, p * LANES:(p + 1) * LANES, :].astype(BF16) for p in range(n_pairs)]
        return j - 1, _attn_block(qm_ref, carry_ref, acc_ref, keys, values, True, None, sums)

    j_end, _ = lax.while_loop(_keep_sweeping, body, (jnp.int32(n_cache_blocks - 1), top))

    @pl.when(j_end >= 0)
    def _():
        for cp in fetch(j_end, lax.rem(j_end, 2)):
            cp.wait()

    o_ref[...] = acc_ref[...].astype(o_ref.dtype)


def _attention_sample(q, k_new, v_new, cache_kt, cache_vt, t, q_row_off):
    d = q.shape[1]
    n_streams = k_new.shape[0] // t
    tk = ATTN_TK
    assert cache_kt.shape[1] % tk == 0 and t <= tk
    any_spec = pl.BlockSpec(memory_space=pl.ANY)
    new = pl.BlockSpec((t, d), lambda b: (b, 0))
    return pl.pallas_call(
        _attn_sample_kernel,
        grid=(n_streams,),
        in_specs=[pl.BlockSpec((t, d), lambda b: (q_row_off // t + b, 0)), new, new, any_spec, any_spec],
        out_specs=new,
        out_shape=jax.ShapeDtypeStruct((n_streams * t, d), BF16),
        scratch_shapes=[pltpu.VMEM((d // LANES, 2 * t, LANES), BF16), pltpu.VMEM((d // LANES, 2 * t, tk), F32),
                        pltpu.VMEM((t, d), F32), pltpu.VMEM((2, d, tk), F32), pltpu.VMEM((2, d, tk), F32),
                        pltpu.SemaphoreType.DMA((2, 2))],
        compiler_params=_params(("arbitrary",)),
        name="stick_breaking_sample",
    )(q, k_new, v_new, cache_kt, cache_vt)


def kernel(x_prompt, x_sample, state_pool, cache_k, cache_v, norm_mix, norm_ffn, pool_w, pool_scale, w_qkv, w_o, router_w, router_b, moe_w1, moe_b1, moe_w2, moe_b2, final_norm):
    b, s, d = x_prompt.shape
    db, t, _ = x_sample.shape
    past = cache_k.shape[2]
    n_exp = router_w.shape[2]
    hd = d // N_HEADS
    hist = POOL_HIST_ROWS
    n_p, n_s = b * s, db * t
    assert t >= hist and ROW_TILE % t == 0 and s % ROW_TILE == 0 and n_s % ROW_TILE == 0 and ROW_TILE % QKV_TILE == 0
    row = lambda a: a.reshape(1, -1)
    def split_router(w):
        w_t = w.T
        hi = w_t.astype(BF16)
        return jnp.concatenate([hi, (w_t - hi.astype(F32)).astype(BF16)], axis=0)

    wr = [split_router(router_w[i]) for i in range(2)]
    br = [router_b[i].reshape(n_exp, 1) for i in range(2)]

    hist_s = jnp.concatenate([jnp.zeros((db, 1, d), F32), state_pool[0]], axis=1)
    xn, tok, idx, gates, rank, hlast_p, hlast_s, cnt = _pool_layer(
        x_prompt, x_sample, hist_s, past, row(norm_mix[0]), pool_w[0].astype(BF16), row(pool_scale[0]),
        row(norm_ffn[0]), wr[0], br[0])
    x1, h1 = _moe(tok, idx, gates, rank, cnt, xn, row(norm_mix[1]), None, 0, moe_w1, moe_b1[0], moe_w2, moe_b2[0])

    wqkv = w_qkv[0].astype(BF16)
    q, ktb, vtb, ktf, vtf, ksb, vsb, ksf, vsf = _qkv(h1, wqkv[:, :d], wqkv[:, d:2 * d].T, wqkv[:, 2 * d:].T, b, s)
    o_p = _attention_prompt(q, ktb, vtb)
    transposed = lambda cache: cache.transpose(0, 1, 3, 4, 2).reshape(db * d, past)
    o_s = _attention_sample(q, ksb, vsb, transposed(cache_k), transposed(cache_v), t, n_p)
    xn, tok, idx, gates, rank, cnt = _proj_layer(o_p, o_s, x1, w_o[0].astype(BF16), row(norm_ffn[1]), wr[1], br[1])
    y_p, y_s = _moe(tok, idx, gates, rank, cnt, xn, row(final_norm), n_p, 1, moe_w1, moe_b1[1], moe_w2, moe_b2[1])

    frames_major = lambda xt: xt.reshape(1, b, N_HEADS, hd, s).transpose(0, 1, 4, 2, 3)
    heads = lambda a: a.reshape(1, db, t, N_HEADS, hd)
    return (y_p.reshape(b, s, d), y_s.reshape(db, t, d), hlast_p[None, :, 1:, :],
            frames_major(ktf), frames_major(vtf), hlast_s[None, :, 1:, :], heads(ksf), heads(vsf))
```
